```python
import jax, jax.numpy as jnp
from jax import lax
import numpy as np

D_MODEL = 2048
BATCH = 4
SEQ = 4096
DEPTH = 2

MLA_HEADS = D_MODEL // 256
MLA_NOPE_DIM = 128
MLA_ROPE_DIM = 64
MLA_V_DIM = 128
MLA_QK_DIM = MLA_NOPE_DIM + MLA_ROPE_DIM
MLA_Q_RANK = 3 * D_MODEL // 16
MLA_KV_RANK = D_MODEL // 8
MLA_WIDTH = MLA_HEADS * MLA_V_DIM
HGRN_HEADS = D_MODEL // 256
HGRN_K_DIM = 128
HGRN_V_DIM = 128
HGRN_WIDTH = HGRN_HEADS * HGRN_V_DIM
MIX_WIDTH = MLA_WIDTH + HGRN_WIDTH
IN_WIDTH = MLA_Q_RANK + MLA_KV_RANK + MLA_ROPE_DIM + 2 * HGRN_HEADS * HGRN_K_DIM + 2 * HGRN_WIDTH
CHUNK = 64
ATTN_BLOCK = 128
ROPE_THETA = 10000.0
LB_FLOOR = 1e-30
FFN_DIM = 7 * D_MODEL // 2
N_EXPERTS = 8
TOP_K = 2
N_DENSE = (DEPTH + 1) // 2
N_MOE = DEPTH // 2
PLE_DIM = 256
EPS = 1e-6

kernel_name = 'hybrid_mla_hgrn2_moe_sandwich_ple'


def rms_norm(x, gain):
    xf = x.astype(jnp.float32)
    y = xf * lax.rsqrt(jnp.mean(xf * xf, axis=-1, keepdims=True) + EPS)
    return (y * gain.astype(jnp.float32)).astype(x.dtype)


def rope_tables(positions):
    inv_freq = 1.0 / (ROPE_THETA ** (jnp.arange(0, MLA_ROPE_DIM, 2, dtype=jnp.float32) / MLA_ROPE_DIM))
    ang = positions.astype(jnp.float32)[..., None] * inv_freq
    return jnp.cos(ang), jnp.sin(ang)


def apply_rope(x, cos, sin):
    x1, x2 = jnp.split(x.astype(jnp.float32), 2, axis=-1)
    return jnp.concatenate([x1 * cos - x2 * sin, x2 * cos + x1 * sin], axis=-1).astype(x.dtype)


def mla_group(c_q, c_kv, k_rope, q_norm, kv_norm, w_uq, w_ukv, cos, sin):
    B, S, _ = c_q.shape
    H = MLA_HEADS
    q = (rms_norm(c_q, q_norm) @ w_uq).reshape(B, S, H, MLA_QK_DIM)
    q_nope, q_pe = jnp.split(q, [MLA_NOPE_DIM], axis=-1)
    q_pe = apply_rope(q_pe, cos[:, :, None, :], sin[:, :, None, :])
    kv = (rms_norm(c_kv, kv_norm) @ w_ukv).reshape(B, S, H, MLA_NOPE_DIM + MLA_V_DIM)
    k_nope, v = jnp.split(kv, [MLA_NOPE_DIM], axis=-1)
    k_pe = apply_rope(k_rope, cos, sin)
    q = jnp.concatenate([q_nope, q_pe], axis=-1)
    k = jnp.concatenate([k_nope, jnp.broadcast_to(k_pe[:, :, None, :], (B, S, H, MLA_ROPE_DIM))], axis=-1)
    scale = MLA_QK_DIM ** -0.5
    nb = S // ATTN_BLOCK
    q_blocks = q.reshape(B, nb, ATTN_BLOCK, H, MLA_QK_DIM).transpose(1, 0, 2, 3, 4)
    key_idx = jnp.arange(S)

    def one_block(args):
        q_blk, j = args
        s = jnp.einsum('bqhd,bkhd->bhqk', q_blk, k).astype(jnp.float32) * scale
        q_idx = j * ATTN_BLOCK + jnp.arange(ATTN_BLOCK)
        causal = key_idx[None, :] <= q_idx[:, None]
        s = jnp.where(causal, s, -jnp.inf)
        pr = jax.nn.softmax(s, axis=-1).astype(v.dtype)
        return jnp.einsum('bhqk,bkhv->bqhv', pr, v)

    out = lax.map(one_block, (q_blocks, jnp.arange(nb)))
    return out.transpose(1, 0, 2, 3, 4).reshape(B, S, H * MLA_V_DIM)


def hgrn2_group(q_raw, f_raw, i_raw, g_raw, lb, o_norm):
    B, S, _ = q_raw.shape
    H, K, V = HGRN_HEADS, HGRN_K_DIM, HGRN_V_DIM
    dtype = q_raw.dtype
    q = jax.nn.silu(q_raw.astype(jnp.float32)).reshape(B, S, H, K)
    z = f_raw.astype(jnp.float32).reshape(B, S, H, K)
    lbh = lb.astype(jnp.float32).reshape(H, K)
    log_f = jnp.logaddexp(jnp.log(jnp.maximum(lbh, LB_FLOOR)), jnp.log1p(-lbh) + jax.nn.log_sigmoid(z))
    k = (1.0 - lbh) * jax.nn.sigmoid(-z)
    v = i_raw.astype(jnp.float32).reshape(B, S, H, V)
    N = S // CHUNK

    def to_chunks(t):
        return t.reshape(B, N, CHUNK, H, t.shape[-1]).transpose(1, 0, 3, 2, 4)

    q, k, v, log_f = to_chunks(q), to_chunks(k), to_chunks(v), to_chunks(log_f)
    b = jnp.cumsum(log_f, axis=3)
    causal = jnp.tril(jnp.ones((CHUNK, CHUNK), dtype=bool))[:, :, None]

    def step(state, inp):
        q_c, k_c, v_c, b_c = inp
        diff = b_c[:, :, :, None, :] - b_c[:, :, None, :, :]
        decay = jnp.exp(jnp.where(causal, diff, -jnp.inf))
        a = jnp.einsum('bhtk,bhsk,bhtsk->bhts', q_c, k_c, decay)
        o_intra = jnp.einsum('bhts,bhsv->bhtv', a, v_c)
        o_inter = jnp.einsum('bhtk,bhkv->bhtv', q_c * jnp.exp(b_c), state)
        b_last = b_c[:, :, -1:, :]
        new_state = jnp.exp(b_last[:, :, 0, :])[..., None] * state + \
            jnp.einsum('bhsk,bhsv->bhkv', k_c * jnp.exp(b_last - b_c), v_c)
        return new_state, o_intra + o_inter

    _, o = lax.scan(step, jnp.zeros((B, H, K, V), jnp.float32), (q, k, v, b))
    o = o.transpose(1, 0, 3, 2, 4).reshape(B, S, H, V)
    o = rms_norm(o, o_norm) * jax.nn.silu(g_raw.astype(jnp.float32).reshape(B, S, H, V))
    return o.reshape(B, S, H * V).astype(dtype)


def token_mixer(a, w_in, q_norm, kv_norm, w_uq, w_ukv, lb, o_norm, w_out, cos, sin):
    z = a @ w_in
    hk = HGRN_HEADS * HGRN_K_DIM
    sizes = [MLA_Q_RANK, MLA_KV_RANK, MLA_ROPE_DIM, hk, hk, HGRN_WIDTH]
    offsets = [int(o) for o in np.cumsum(sizes)]
    c_q, c_kv, k_rope, hq, hf, hi, hg = jnp.split(z, offsets, axis=-1)
    y_mla = mla_group(c_q, c_kv, k_rope, q_norm, kv_norm, w_uq, w_ukv, cos, sin)
    y_hgrn = hgrn2_group(hq, hf, hi, hg, lb, o_norm)
    return jnp.concatenate([y_mla, y_hgrn], axis=-1) @ w_out


def swiglu(x, w_gu, w_down):
    g, u = jnp.split(x @ w_gu, 2, axis=-1)
    return (jax.nn.silu(g) * u) @ w_down


def moe_ffn(x, w_router, w_gu, w_down):
    B, S, D = x.shape
    xt = x.reshape(B * S, D)
    logits = (xt @ w_router).astype(jnp.float32)
    top_v, top_i = lax.top_k(logits, TOP_K)
    gates = jax.nn.softmax(top_v, axis=-1)
    combine = jnp.sum(jax.nn.one_hot(top_i, N_EXPERTS, dtype=jnp.float32) * gates[..., None], axis=1).astype(x.dtype)
    y = jnp.zeros_like(xt)
    for e in range(N_EXPERTS):
        y = y + combine[:, e:e + 1] * swiglu(xt, w_gu[e], w_down[e])
    return y.reshape(B, S, D)


def setup_inputs(seed: int = 0) -> dict:
    key = jax.random.key(seed)
    ks = jax.random.split(key, 24)
    f32 = jnp.float32
    hk = HGRN_HEADS * HGRN_K_DIM

    def nrm(k, shape, fan_in):
        return jax.random.normal(k, shape, f32) * (fan_in ** -0.5)

    def gain(k, shape):
        return 1.0 + 0.05 * jax.random.normal(k, shape, f32)

    x = jax.random.normal(ks[0], (BATCH, SEQ, D_MODEL), f32)
    p = jax.random.normal(ks[1], (DEPTH, BATCH, SEQ, PLE_DIM), f32)
    offs = jax.random.randint(ks[2], (BATCH, 1), 0, 1024, dtype=jnp.int32)
    positions = offs + jnp.arange(SEQ, dtype=jnp.int32)[None, :]
    return {
        'x': x,
        'p': p,
        'positions': positions,
        'sandwich_norms': gain(ks[3], (DEPTH, 4, D_MODEL)),
        'w_in': nrm(ks[4], (DEPTH, D_MODEL, IN_WIDTH), D_MODEL),
        'mla_q_norm': gain(ks[5], (DEPTH, MLA_Q_RANK)),
        'mla_kv_norm': gain(ks[6], (DEPTH, MLA_KV_RANK)),
        'w_uq': nrm(ks[7], (DEPTH, MLA_Q_RANK, MLA_HEADS * MLA_QK_DIM), MLA_Q_RANK),
        'w_ukv': nrm(ks[8], (DEPTH, MLA_KV_RANK, MLA_HEADS * (MLA_NOPE_DIM + MLA_V_DIM)), MLA_KV_RANK),
        'hgrn_lb_logits': 0.5 * jax.random.normal(ks[9], (DEPTH, hk), f32),
        'hgrn_out_norm': gain(ks[10], (DEPTH, HGRN_V_DIM)),
        'w_out': nrm(ks[11], (DEPTH, MIX_WIDTH, D_MODEL), MIX_WIDTH),
        'ffn_w_gu': nrm(ks[12], (N_DENSE, D_MODEL, 2 * FFN_DIM), D_MODEL),
        'ffn_w_down': nrm(ks[13], (N_DENSE, FFN_DIM, D_MODEL), FFN_DIM),
        'moe_w_router': nrm(ks[14], (N_MOE, D_MODEL, N_EXPERTS), D_MODEL),
        'moe_w_gu': nrm(ks[15], (N_MOE, N_EXPERTS, D_MODEL, 2 * FFN_DIM), D_MODEL),
        'moe_w_down': nrm(ks[16], (N_MOE, N_EXPERTS, FFN_DIM, D_MODEL), FFN_DIM),
        'ple_w_proj': nrm(ks[17], (DEPTH, PLE_DIM, D_MODEL), PLE_DIM),
        'ple_w_gate': nrm(ks[18], (DEPTH, D_MODEL, D_MODEL), D_MODEL),
        'ple_norm': gain(ks[19], (DEPTH, D_MODEL)),
    }


def reference(x, p, positions, sandwich_norms, w_in, mla_q_norm, mla_kv_norm, w_uq, w_ukv,
              hgrn_lb_logits, hgrn_out_norm, w_out, ffn_w_gu, ffn_w_down, moe_w_router,
              moe_w_gu, moe_w_down, ple_w_proj, ple_w_gate, ple_norm):
    cos, sin = rope_tables(positions)
    pr = jax.nn.softmax(hgrn_lb_logits.astype(jnp.float32), axis=0)
    lower_bounds = jnp.cumsum(pr, axis=0) - pr[0:1]
    h = x
    for l in range(DEPTH):
        a = rms_norm(h, sandwich_norms[l, 0])
        y = token_mixer(a, w_in[l], mla_q_norm[l], mla_kv_norm[l], w_uq[l], w_ukv[l],
                        lower_bounds[l], hgrn_out_norm[l], w_out[l], cos, sin)
        h = h + rms_norm(y, sandwich_norms[l, 1])
        c = rms_norm(h, sandwich_norms[l, 2])
        if l % 2 == 0:
            f = swiglu(c, ffn_w_gu[l // 2], ffn_w_down[l // 2])
        else:
            f = moe_ffn(c, moe_w_router[l // 2], moe_w_gu[l // 2], moe_w_down[l // 2])
        h = h + rms_norm(f, sandwich_norms[l, 3])
        gate = jax.nn.sigmoid((h @ ple_w_gate[l]).astype(jnp.float32)).astype(h.dtype)
        h = h + rms_norm(gate * (p[l] @ ple_w_proj[l]), ple_norm[l])
    return h
```

```python
import functools
import math

import numpy as np
import jax
import jax.numpy as jnp
from jax import lax
from jax.experimental import pallas as pl
from jax.experimental.pallas import tpu as pltpu

F32 = jnp.float32
BF16 = jnp.bfloat16

D_MODEL = 2048
DEPTH = 2
N_HEADS = 8
NOPE_DIM = 128
ROPE_DIM = 64
V_DIM = 128
QK_DIM = NOPE_DIM + ROPE_DIM
Q_RANK = 384
KV_RANK = 256
HGRN_K = 128
HGRN_V = 128
GROUP_WIDTH = N_HEADS * 128
FFN_DIM = 7168
N_EXPERTS = 8
PLE_DIM = 256
ROPE_THETA = 10000.0
LB_FLOOR = 1e-30
EPS = 1e-6

LANES = 128
QK_PAD = 256
Z_MLA = 1024
Z_WIDTH = 5120
HGRN_COL0 = 896
HGRN_CHUNK = 64
VMEM_LIMIT = 56 * 1024 * 1024


def _params(semantics, vmem=VMEM_LIMIT):
    return pltpu.CompilerParams(dimension_semantics=semantics, vmem_limit_bytes=vmem)


def _rms(x, gain_row):
    ms = jnp.mean(x * x, axis=-1, keepdims=True)
    return x * lax.rsqrt(ms + EPS) * gain_row


def _dot(a, b):
    return jnp.dot(a, b, preferred_element_type=F32)


def _dot_nt(a, b):
    return lax.dot_general(a, b, (((1,), (1,)), ((), ())), preferred_element_type=F32)


def _dot_tn(a, b):
    return lax.dot_general(a, b, (((0,), (0,)), ((), ())), preferred_element_type=F32)


def _sigmoid(x):
    return 1.0 / (1.0 + jnp.exp(-x))


def _rope_kernel(pos_ref, inv_ref, sgn_ref, ct_ref, st_ref):
    ang = pos_ref[...].astype(F32) * inv_ref[...]
    keep = jnp.abs(sgn_ref[...])
    ct_ref[...] = jnp.cos(ang) * keep
    st_ref[...] = jnp.sin(ang) * sgn_ref[...]


def _rope_tables(positions):
    T = positions.size
    tt = min(T, 1024)
    half = ROPE_DIM // 2
    inv_freq = 1.0 / (ROPE_THETA ** (jnp.arange(0, ROPE_DIM, 2, dtype=F32) / ROPE_DIM))
    inv_row = jnp.concatenate([inv_freq, inv_freq, jnp.zeros((LANES - ROPE_DIM,), F32)])[None, :]
    sgn = np.zeros((1, LANES), np.float32)
    sgn[0, :half] = -1.0
    sgn[0, half:ROPE_DIM] = 1.0
    return pl.pallas_call(
        _rope_kernel,
        grid=(T // tt,),
        in_specs=[pl.BlockSpec((tt, 1), lambda i: (i, 0)),
                  pl.BlockSpec((1, LANES), lambda i: (0, 0)),
                  pl.BlockSpec((1, LANES), lambda i: (0, 0))],
        out_specs=[pl.BlockSpec((tt, LANES), lambda i: (i, 0))] * 2,
        out_shape=[jax.ShapeDtypeStruct((T, LANES), F32)] * 2,
        compiler_params=_params(("parallel",)),
        name="rope_tables",
    )(positions.reshape(T, 1), inv_row, jnp.asarray(sgn))


def _norm_matmul_kernel(x_ref, g_ref, w_ref, o_ref, a_scr):
    @pl.when(pl.program_id(1) == 0)
    def _():
        a_scr[...] = _rms(x_ref[...], g_ref[...]).astype(BF16)

    o_ref[...] = _dot(a_scr[...], w_ref[...])


def _norm_matmul(x, gain, w):
    T, K = x.shape
    N = w.shape[1]
    tm = min(T, 1024)
    tn = 1024
    return pl.pallas_call(
        _norm_matmul_kernel,
        grid=(T // tm, N // tn),
        in_specs=[pl.BlockSpec((tm, K), lambda i, j: (i, 0)),
                  pl.BlockSpec((1, K), lambda i, j: (0, 0)),
                  pl.BlockSpec((K, tn), lambda i, j: (0, j))],
        out_specs=pl.BlockSpec((tm, tn), lambda i, j: (i, j)),
        out_shape=jax.ShapeDtypeStruct((T, N), F32),
        scratch_shapes=[pltpu.VMEM((tm, K), BF16)],
        compiler_params=_params(("parallel", "arbitrary")),
        name="in_proj",
    )(x, gain[None, :], w)


def _mla_proj_kernel(z_ref, ct_ref, st_ref, qn_ref, kvn_ref, wq_ref, wkv_ref,
                     q_ref, k_ref, v_ref):
    z = z_ref[...]
    ct = ct_ref[...]
    st = st_ref[...]
    aq = _rms(z[:, :Q_RANK], qn_ref[...]).astype(BF16)
    akv = _rms(z[:, Q_RANK:Q_RANK + KV_RANK], kvn_ref[...]).astype(BF16)
    k_a = z[:, 640:768]
    k_b = z[:, 768:896]
    k_pe = (k_a * ct + k_b * st).astype(BF16)
    q = _dot(aq, wq_ref[...])
    kv = _dot(akv, wkv_ref[...])
    for h in range(N_HEADS):
        lo, hi = h * LANES, (h + 1) * LANES
        q_ref[0, h, :, 0:LANES] = q[:, lo:hi].astype(BF16)
        q_pe = q[:, GROUP_WIDTH + lo:GROUP_WIDTH + hi] * ct \
            + q[:, 2 * GROUP_WIDTH + lo:2 * GROUP_WIDTH + hi] * st
        q_ref[0, h, :, LANES:QK_PAD] = q_pe.astype(BF16)
        k_ref[0, h, :, 0:LANES] = kv[:, lo:hi].astype(BF16)
        k_ref[0, h, :, LANES:QK_PAD] = k_pe
        v_ref[0, h, :, :] = kv[:, GROUP_WIDTH + lo:GROUP_WIDTH + hi].astype(BF16)


def _mla_proj(z, ct, st, q_norm, kv_norm, wq, wkv, B, S):
    ts = min(S, 512)
    ns = S // ts
    tok = lambda b, i: (b * ns + i, 0)
    const = lambda b, i: (0, 0)
    head_out = lambda b, i: (b, 0, i, 0)
    return pl.pallas_call(
        _mla_proj_kernel,
        grid=(B, ns),
        in_specs=[pl.BlockSpec((ts, Z_MLA), tok),
                  pl.BlockSpec((ts, LANES), tok),
                  pl.BlockSpec((ts, LANES), tok),
                  pl.BlockSpec((1, Q_RANK), const),
                  pl.BlockSpec((1, KV_RANK), const),
                  pl.BlockSpec(wq.shape, const),
                  pl.BlockSpec(wkv.shape, const)],
        out_specs=[pl.BlockSpec((1, N_HEADS, ts, QK_PAD), head_out),
                   pl.BlockSpec((1, N_HEADS, ts, QK_PAD), head_out),
                   pl.BlockSpec((1, N_HEADS, ts, V_DIM), head_out)],
        out_shape=[jax.ShapeDtypeStruct((B, N_HEADS, S, QK_PAD), BF16),
                   jax.ShapeDtypeStruct((B, N_HEADS, S, QK_PAD), BF16),
                   jax.ShapeDtypeStruct((B, N_HEADS, S, V_DIM), BF16)],
        compiler_params=_params(("parallel", "parallel")),
        name="mla_proj",
    )(z, ct, st, q_norm[None, :], kv_norm[None, :], wq, wkv)


def _attn_kernel(q_ref, k_ref, v_ref, o_ref, *, tq, scale):
    i = pl.program_id(2)
    q = q_ref[0, 0]

    def block(j, carry, masked):
        m, l, acc = carry
        start = pl.multiple_of(j * tq, tq)
        k = k_ref[0, 0, pl.ds(start, tq), :]
        v = v_ref[0, 0, pl.ds(start, tq), :]
        s = _dot_nt(q, k) * scale
        if masked:
            row = lax.broadcasted_iota(jnp.int32, (tq, tq), 0)
            col = lax.broadcasted_iota(jnp.int32, (tq, tq), 1)
            s = jnp.where(col <= row, s, -jnp.inf)
        m_new = jnp.maximum(m, jnp.max(s, axis=-1, keepdims=True))
        alpha = jnp.exp(m - m_new)
        p = jnp.exp(s - m_new)
        l = alpha * l + jnp.sum(p, axis=-1, keepdims=True)
        acc = alpha * acc + _dot(p.astype(BF16), v)
        return m_new, l, acc

    init = (jnp.full((tq, 1), -jnp.inf, F32), jnp.zeros((tq, 1), F32),
            jnp.zeros((tq, V_DIM), F32))
    carry = lax.fori_loop(0, i, lambda j, c: block(j, c, False), init)
    _, l, acc = block(i, carry, True)
    o_ref[0] = (acc / l).astype(BF16)


def _attention(q, k, v):
    B, H, S, _ = q.shape
    tq = min(S, 512)
    kern = functools.partial(_attn_kernel, tq=tq, scale=QK_DIM ** -0.5)
    return pl.pallas_call(
        kern,
        grid=(B, H, S // tq),
        in_specs=[pl.BlockSpec((1, 1, tq, QK_PAD), lambda b, h, i: (b, h, i, 0)),
                  pl.BlockSpec((1, 1, S, QK_PAD), lambda b, h, i: (b, h, 0, 0)),
                  pl.BlockSpec((1, 1, S, V_DIM), lambda b, h, i: (b, h, 0, 0))],
        out_specs=pl.BlockSpec((1, tq, V_DIM), lambda b, h, i: (b, i, h)),
        out_shape=jax.ShapeDtypeStruct((B, S, H * V_DIM), BF16),
        compiler_params=_params(("parallel", "parallel", "arbitrary")),
        name="mla_attention",
    )(q, k, v)


def _hgrn_consts(C):
    levels = [C >> (i + 1) for i in range(int(math.log2(C)))]
    t = np.arange(C)
    u = np.arange(C)
    mats = [(u[None, :] <= t[:, None]),
            (u[None, :] > t[:, None])]
    hi_rows, lo_rows = [], []
    pair_masks = [np.eye(C, dtype=bool)]
    for m in levels:
        ref = (t // (2 * m)) * (2 * m) + m - 1
        hi = ((t // m) % 2) == 1
        rng_hi = (u[None, :] > ref[:, None]) & (u[None, :] <= t[:, None])
        rng_lo = (u[None, :] > t[:, None]) & (u[None, :] <= ref[:, None])
        mats.append(np.where(hi[:, None], rng_hi, rng_lo))
        hi_rows.append(np.repeat(hi[:, None], LANES, axis=1))
        lo_rows.append(np.repeat(~hi[:, None], LANES, axis=1))
        same = (t[:, None] // (2 * m)) == (t[None, :] // (2 * m))
        pair_masks.append(hi[:, None] & (~hi[None, :]) & same)
    stack = np.concatenate(mats, axis=0).astype(np.float32)
    sums = jnp.asarray(np.concatenate([stack] * 3, axis=1), dtype=BF16)
    return (sums,
            jnp.asarray(np.stack(hi_rows).astype(np.float32)),
            jnp.asarray(np.stack(lo_rows).astype(np.float32)),
            jnp.asarray(np.stack(pair_masks).astype(np.float32)))


def _hgrn_kernel(zq_ref, zf_ref, zi_ref, zg_ref, lbl_ref, on_ref,
                 sums_ref, hi_ref, lo_ref, pm_ref, o_ref, st_scr, *, layer, ts):
    C = HGRN_CHUNK
    n_levels = hi_ref.shape[0]

    @pl.when(pl.program_id(2) == 0)
    def _():
        st_scr[...] = jnp.zeros_like(st_scr)

    lg = lbl_ref[...]
    ex = jnp.exp(lg - jnp.max(lg, axis=0, keepdims=True))
    pr = ex / jnp.sum(ex, axis=0, keepdims=True)
    cum = pr[0:1, :]
    for r in range(1, layer + 1):
        cum = cum + pr[r:r + 1, :]
    lb = cum - pr[0:1, :]
    lb_floor = jnp.maximum(lb, LB_FLOOR)
    one_m_lb = 1.0 - lb
    gain = on_ref[...]

    for c in range(ts // C):
        rows = pl.ds(c * C, C)
        q_raw = zq_ref[rows, :]
        zf = zf_ref[rows, :]
        v = zi_ref[rows, :].astype(BF16)
        g_raw = zg_ref[rows, :]
        q = q_raw * _sigmoid(q_raw)
        e = jnp.exp(-jnp.abs(zf))
        r = 1.0 / (1.0 + e)
        er = e * r
        pos = zf >= 0
        log_f = jnp.log(lb_floor + one_m_lb * jnp.where(pos, r, er))
        kk = one_m_lb * jnp.where(pos, er, r)
        p0 = log_f.astype(BF16)
        r1 = log_f - p0.astype(F32)
        p1 = r1.astype(BF16)
        p2 = (r1 - p1.astype(F32)).astype(BF16)
        sums = _dot(sums_ref[...], jnp.concatenate([p0, p1, p2], axis=0))
        b = sums[0:C]
        b_rest = sums[C:2 * C]
        a = pm_ref[0] * _dot_nt(q.astype(BF16), kk.astype(BF16))
        for lv in range(n_levels):
            dec = jnp.exp(sums[(2 + lv) * C:(3 + lv) * C])
            qt = (q * dec * hi_ref[lv]).astype(BF16)
            kt = (kk * dec * lo_ref[lv]).astype(BF16)
            a = a + pm_ref[lv + 1] * _dot_nt(qt, kt)
        st = st_scr[...]
        o = _dot(a.astype(BF16), v) + _dot_nt((q * jnp.exp(b)).astype(BF16), st.astype(BF16))
        k_end = (kk * jnp.exp(b_rest)).astype(BF16)
        st_scr[...] = st * jnp.exp(b[C - 1:C, :]) + _dot_tn(v, k_end)
        o_ref[rows, :] = (_rms(o, gain) * (g_raw * _sigmoid(g_raw))).astype(BF16)


def _hgrn(z, lb_logits, out_norm, consts, layer, B, S):
    T = B * S
    ts = min(S, 256)
    ns = S // ts
    sums, hi_rows, lo_rows, pair_masks = consts
    col0 = HGRN_COL0 // LANES

    def zspec(part):
        return pl.BlockSpec((ts, LANES),
                            lambda b, h, i: (b * ns + i, col0 + part * N_HEADS + h))

    full = lambda arr: pl.BlockSpec(arr.shape, lambda b, h, i: (0,) * arr.ndim)
    kern = functools.partial(_hgrn_kernel, layer=layer, ts=ts)
    return pl.pallas_call(
        kern,
        grid=(B, N_HEADS, ns),
        in_specs=[zspec(0), zspec(1), zspec(2), zspec(3),
                  pl.BlockSpec((DEPTH, LANES), lambda b, h, i: (0, h)),
                  pl.BlockSpec((1, HGRN_V), lambda b, h, i: (0, 0)),
                  full(sums), full(hi_rows), full(lo_rows), full(pair_masks)],
        out_specs=pl.BlockSpec((ts, LANES), lambda b, h, i: (b * ns + i, h)),
        out_shape=jax.ShapeDtypeStruct((T, GROUP_WIDTH), BF16),
        scratch_shapes=[pltpu.VMEM((HGRN_V, HGRN_K), F32)],
        compiler_params=_params(("parallel", "parallel", "arbitrary")),
        name="hgrn2",
    )(z, z, z, z, lb_logits, out_norm[None, :], sums, hi_rows, lo_rows, pair_masks)


def _out_proj_kernel(om_ref, oh_ref, h_ref, w_ref, g1_ref, g2_ref, h1_ref, c_ref):
    y = _dot(om_ref[...], w_ref[0:GROUP_WIDTH, :]) + _dot(oh_ref[...], w_ref[GROUP_WIDTH:, :])
    h1 = h_ref[...] + _rms(y, g1_ref[...])
    h1_ref[...] = h1
    c_ref[...] = _rms(h1, g2_ref[...]).astype(c_ref.dtype)


def _out_proj(om, oh, h, w, g1, g2, c_dtype):
    T = h.shape[0]
    tm = min(T, 256)
    row = lambda i: (i, 0)
    const = lambda i: (0, 0)
    return pl.pallas_call(
        _out_proj_kernel,
        grid=(T // tm,),
        in_specs=[pl.BlockSpec((tm, GROUP_WIDTH), row),
                  pl.BlockSpec((tm, GROUP_WIDTH), row),
                  pl.BlockSpec((tm, D_MODEL), row),
                  pl.BlockSpec((D_MODEL, D_MODEL), const),
                  pl.BlockSpec((1, D_MODEL), const),
                  pl.BlockSpec((1, D_MODEL), const)],
        out_specs=[pl.BlockSpec((tm, D_MODEL), row), pl.BlockSpec((tm, D_MODEL), row)],
        out_shape=[jax.ShapeDtypeStruct((T, D_MODEL), F32),
                   jax.ShapeDtypeStruct((T, D_MODEL), c_dtype)],
        compiler_params=_params(("parallel",)),
        name="out_proj",
    )(om, oh, h, w, g1[None, :], g2[None, :])


def _gate_up_kernel(te_ref, nv_ref, x_ref, wg_ref, wu_ref, o_ref):
    valid = pl.program_id(1) < nv_ref[0]

    @pl.when(valid)
    def _():
        x = x_ref[...].astype(BF16)
        g = _dot(x, wg_ref[0])
        u = _dot(x, wu_ref[0])
        o_ref[...] = (g * _sigmoid(g) * u).astype(BF16)

    @pl.when(jnp.logical_not(valid))
    def _():
        o_ref[...] = jnp.zeros_like(o_ref)


def _gate_up(x, w_gu, tile_expert, n_valid, tm):
    R = x.shape[0]
    n_tiles = R // tm
    tf = 512
    nf = FFN_DIM // tf

    def tile(i, nv):
        return jnp.minimum(i, nv[0] - 1)

    return pl.pallas_call(
        _gate_up_kernel,
        grid_spec=pltpu.PrefetchScalarGridSpec(
            num_scalar_prefetch=2,
            grid=(nf, n_tiles),
            in_specs=[pl.BlockSpec((tm, D_MODEL), lambda j, i, te, nv: (tile(i, nv), 0)),
                      pl.BlockSpec((1, D_MODEL, tf), lambda j, i, te, nv: (te[tile(i, nv)], 0, j)),
                      pl.BlockSpec((1, D_MODEL, tf),
                                   lambda j, i, te, nv: (te[tile(i, nv)], 0, j + nf))],
            out_specs=pl.BlockSpec((tm, tf), lambda j, i, te, nv: (i, j)),
        ),
        out_shape=jax.ShapeDtypeStruct((R, FFN_DIM), BF16),
        compiler_params=_params(("arbitrary", "arbitrary")),
        name="ffn_gate_up",
    )(tile_expert, n_valid, x, w_gu, w_gu)


def _down_kernel(te_ref, nv_ref, a_ref, w_ref, o_ref):
    valid = pl.program_id(1) < nv_ref[0]

    @pl.when(valid)
    def _():
        o_ref[...] = _dot(a_ref[...], w_ref[0])

    @pl.when(jnp.logical_not(valid))
    def _():
        o_ref[...] = jnp.zeros_like(o_ref)


def _down(act, w_down, tile_expert, n_valid, tm):
    R = act.shape[0]
    n_tiles = R // tm
    tn = 512

    def tile(i, nv):
        return jnp.minimum(i, nv[0] - 1)

    return pl.pallas_call(
        _down_kernel,
        grid_spec=pltpu.PrefetchScalarGridSpec(
            num_scalar_prefetch=2,
            grid=(D_MODEL // tn, n_tiles),
            in_specs=[pl.BlockSpec((tm, FFN_DIM), lambda j, i, te, nv: (tile(i, nv), 0)),
                      pl.BlockSpec((1, FFN_DIM, tn), lambda j, i, te, nv: (te[tile(i, nv)], 0, j))],
            out_specs=pl.BlockSpec((tm, tn), lambda j, i, te, nv: (i, j)),
        ),
        out_shape=jax.ShapeDtypeStruct((R, D_MODEL), F32),
        compiler_params=_params(("arbitrary", "arbitrary")),
        name="ffn_down",
    )(tile_expert, n_valid, act, w_down)


def _ffn_epilogue(f, h1, p, wg, wp, g3, gp):
    h2 = h1 + _rms(f, g3)
    gate = _sigmoid(_dot(h2.astype(BF16), wg))
    proj = _dot(p.astype(BF16), wp)
    return h2 + _rms(gate * proj, gp)


def _post_ffn_kernel(f_ref, h1_ref, p_ref, wg_ref, wp_ref, g3_ref, gp_ref, o_ref):
    o_ref[...] = _ffn_epilogue(f_ref[...], h1_ref[...], p_ref[...], wg_ref[...], wp_ref[...],
                               g3_ref[...], gp_ref[...])


def _post_ffn(f, h1, p, wg, wp, g3, gp):
    T = h1.shape[0]
    tm = min(T, 256)
    row = lambda i: (i, 0)
    const = lambda i: (0, 0)
    return pl.pallas_call(
        _post_ffn_kernel,
        grid=(T // tm,),
        in_specs=[pl.BlockSpec((tm, D_MODEL), row),
                  pl.BlockSpec((tm, D_MODEL), row),
                  pl.BlockSpec((tm, PLE_DIM), row),
                  pl.BlockSpec((D_MODEL, D_MODEL), const),
                  pl.BlockSpec((PLE_DIM, D_MODEL), const),
                  pl.BlockSpec((1, D_MODEL), const),
                  pl.BlockSpec((1, D_MODEL), const)],
        out_specs=pl.BlockSpec((tm, D_MODEL), row),
        out_shape=jax.ShapeDtypeStruct((T, D_MODEL), F32),
        compiler_params=_params(("parallel",)),
        name="post_ffn",
    )(f, h1, p, wg, wp, g3[None, :], gp[None, :])


def _router_kernel(c_ref, wr_ref, tri_ref, route_ref, cnt_ref, carry):
    @pl.when(pl.program_id(0) == 0)
    def _():
        carry[...] = jnp.zeros_like(carry)

    logits = _dot(c_ref[...].astype(BF16), wr_ref[...])
    lane = lax.broadcasted_iota(jnp.int32, logits.shape, 1)
    lg = jnp.where(lane < N_EXPERTS, logits, -jnp.inf)
    m1 = jnp.max(lg, axis=-1, keepdims=True)
    i1 = jnp.min(jnp.where(lg == m1, lane, LANES), axis=-1, keepdims=True)
    lg2 = jnp.where(lane == i1, -jnp.inf, lg)
    m2 = jnp.max(lg2, axis=-1, keepdims=True)
    i2 = jnp.min(jnp.where(lg2 == m2, lane, LANES), axis=-1, keepdims=True)
    e = jnp.exp(m2 - m1)
    g1 = 1.0 / (1.0 + e)
    g2 = e / (1.0 + e)
    onehot = jnp.where((lane == i1) | (lane == i2), 1.0, 0.0)
    before = _dot(tri_ref[...], onehot.astype(BF16)) + carry[0:1, :]
    rank1 = jnp.sum(jnp.where(lane == i1, before, 0.0), axis=-1, keepdims=True)
    rank2 = jnp.sum(jnp.where(lane == i2, before, 0.0), axis=-1, keepdims=True)
    total = carry[0:1, :] + jnp.sum(onehot, axis=0, keepdims=True)
    carry[...] = jnp.broadcast_to(total, carry.shape)
    cnt_ref[...] = jnp.broadcast_to(total, cnt_ref.shape)
    out = jnp.where(lane == 0, i1.astype(F32), 0.0)
    out = jnp.where(lane == 1, i2.astype(F32), out)
    out = jnp.where(lane == 2, g1, out)
    out = jnp.where(lane == 3, g2, out)
    out = jnp.where(lane == 4, rank1, out)
    out = jnp.where(lane == 5, rank2, out)
    route_ref[...] = out


def _router(c, w_router):
    T = c.shape[0]
    tr = min(T, 512)
    wr = jnp.zeros((D_MODEL, LANES), BF16).at[:, :N_EXPERTS].set(w_router.astype(BF16))
    tri = jnp.asarray(np.tril(np.ones((tr, tr), np.float32), -1), dtype=BF16)
    return pl.pallas_call(
        _router_kernel,
        grid=(T // tr,),
        in_specs=[pl.BlockSpec((tr, D_MODEL), lambda i: (i, 0)),
                  pl.BlockSpec((D_MODEL, LANES), lambda i: (0, 0)),
                  pl.BlockSpec((tr, tr), lambda i: (0, 0))],
        out_specs=[pl.BlockSpec((tr, LANES), lambda i: (i, 0)),
                   pl.BlockSpec((8, LANES), lambda i: (0, 0))],
        out_shape=[jax.ShapeDtypeStruct((T, LANES), F32),
                   jax.ShapeDtypeStruct((8, LANES), F32)],
        scratch_shapes=[pltpu.VMEM((8, LANES), F32)],
        compiler_params=_params(("arbitrary",)),
        name="moe_router",
    )(c, wr, tri)


def _dispatch_kernel(p1_ref, p2_ref, c_ref, xs_in_ref, xs_ref, sem, *, tm):
    del xs_in_ref
    base = pl.program_id(0) * tm

    def row_copy(r, dst):
        return pltpu.make_async_copy(c_ref.at[pl.ds(r, 1)], xs_ref.at[pl.ds(dst, 1)], sem)

    def issue(r, _):
        row_copy(r, p1_ref[base + r]).start()
        row_copy(r, p2_ref[base + r]).start()
        return 0

    lax.fori_loop(0, tm, issue, 0)

    def drain(r, _):
        row_copy(0, 0).wait()
        row_copy(0, 0).wait()
        return 0

    lax.fori_loop(0, tm, drain, 0)


def _dispatch(c, pos1, pos2, n_rows):
    T = c.shape[0]
    tm = min(T, 256)
    zeros = jnp.zeros((n_rows, D_MODEL), c.dtype)
    return pl.pallas_call(
        functools.partial(_dispatch_kernel, tm=tm),
        grid_spec=pltpu.PrefetchScalarGridSpec(
            num_scalar_prefetch=2,
            grid=(T // tm,),
            in_specs=[pl.BlockSpec((tm, D_MODEL), lambda i, p1, p2: (i, 0)),
                      pl.BlockSpec(memory_space=pl.ANY)],
            out_specs=pl.BlockSpec(memory_space=pl.ANY),
            scratch_shapes=[pltpu.SemaphoreType.DMA(())],
        ),
        out_shape=jax.ShapeDtypeStruct((n_rows, D_MODEL), c.dtype),
        input_output_aliases={3: 0},
        compiler_params=_params(("arbitrary",)),
        name="moe_dispatch",
    )(pos1, pos2, c, zeros)


def _combine_kernel(p1_ref, p2_ref, ys_ref, route_ref, h1_ref, p_ref, wg_ref, wp_ref,
                    g3_ref, gp_ref, o_ref, buf, sem, *, tm):
    base = pl.program_id(0) * tm

    def row_copy(src, k, r):
        return pltpu.make_async_copy(ys_ref.at[pl.ds(src, 1)], buf.at[k, pl.ds(r, 1)], sem)

    def issue(r, _):
        row_copy(p1_ref[base + r], 0, r).start()
        row_copy(p2_ref[base + r], 1, r).start()
        return 0

    lax.fori_loop(0, tm, issue, 0)

    def drain(r, _):
        row_copy(0, 0, 0).wait()
        row_copy(0, 1, 0).wait()
        return 0

    lax.fori_loop(0, tm, drain, 0)
    route = route_ref[...]
    f = route[:, 2:3] * buf[0] + route[:, 3:4] * buf[1]
    o_ref[...] = _ffn_epilogue(f, h1_ref[...], p_ref[...], wg_ref[...], wp_ref[...],
                               g3_ref[...], gp_ref[...])


def _combine(ys, pos1, pos2, route, h1, p, wg, wp, g3, gp):
    T = h1.shape[0]
    tm = min(T, 256)
    row = lambda i, p1, p2: (i, 0)
    const = lambda i, p1, p2: (0, 0)
    return pl.pallas_call(
        functools.partial(_combine_kernel, tm=tm),
        grid_spec=pltpu.PrefetchScalarGridSpec(
            num_scalar_prefetch=2,
            grid=(T // tm,),
            in_specs=[pl.BlockSpec(memory_space=pl.ANY),
                      pl.BlockSpec((tm, LANES), row),
                      pl.BlockSpec((tm, D_MODEL), row),
                      pl.BlockSpec((tm, PLE_DIM), row),
                      pl.BlockSpec((D_MODEL, D_MODEL), const),
                      pl.BlockSpec((PLE_DIM, D_MODEL), const),
                      pl.BlockSpec((1, D_MODEL), const),
                      pl.BlockSpec((1, D_MODEL), const)],
            out_specs=pl.BlockSpec((tm, D_MODEL), row),
            scratch_shapes=[pltpu.VMEM((2, tm, D_MODEL), F32),
                            pltpu.SemaphoreType.DMA(())],
        ),
        out_shape=jax.ShapeDtypeStruct((T, D_MODEL), F32),
        compiler_params=_params(("arbitrary",)),
        name="moe_combine",
    )(pos1, pos2, ys, route, h1, p, wg, wp, g3[None, :], gp[None, :])


def _swap_halves(w):
    half = w.shape[-1] // 2
    return jnp.concatenate([w[..., half:], w[..., :half]], axis=-1)


def _prep_w_in(w):
    k_rope = w[:, 640:704]
    pad64 = jnp.zeros((D_MODEL, 64), w.dtype)
    cols = [w[:, :640], k_rope, pad64, _swap_halves(k_rope), pad64, w[:, 704:],
            jnp.zeros((D_MODEL, Z_WIDTH - HGRN_COL0 - 4 * GROUP_WIDTH), w.dtype)]
    return jnp.concatenate(cols, axis=1).astype(BF16)


def _prep_w_uq(w):
    w = w.reshape(Q_RANK, N_HEADS, QK_DIM)
    nope = w[:, :, :NOPE_DIM]
    pe = w[:, :, NOPE_DIM:]
    pad = jnp.zeros((Q_RANK, N_HEADS, LANES - ROPE_DIM), w.dtype)
    pe_pad = jnp.concatenate([pe, pad], axis=-1)
    pe_swap = jnp.concatenate([_swap_halves(pe), pad], axis=-1)
    parts = [x.reshape(Q_RANK, GROUP_WIDTH) for x in (nope, pe_pad, pe_swap)]
    return jnp.concatenate(parts, axis=1).astype(BF16)


def _prep_w_ukv(w):
    w = w.reshape(KV_RANK, N_HEADS, NOPE_DIM + V_DIM)
    k = w[:, :, :NOPE_DIM].reshape(KV_RANK, GROUP_WIDTH)
    v = w[:, :, NOPE_DIM:].reshape(KV_RANK, GROUP_WIDTH)
    return jnp.concatenate([k, v], axis=1).astype(BF16)


def _moe_plan(route, counts, tm, n_tiles):
    e1 = route[:, 0].astype(jnp.int32)
    e2 = route[:, 1].astype(jnp.int32)
    rank1 = route[:, 4].astype(jnp.int32)
    rank2 = route[:, 5].astype(jnp.int32)
    cnt = counts[0, :N_EXPERTS].astype(jnp.int32)
    padded = ((cnt + tm - 1) // tm) * tm
    seg_end = jnp.cumsum(padded)
    seg_start = seg_end - padded
    pos1 = seg_start[e1] + rank1
    pos2 = seg_start[e2] + rank2
    tile_start = jnp.arange(n_tiles, dtype=jnp.int32) * tm
    tile_expert = jnp.sum(tile_start[:, None] >= seg_end[None, :], axis=1).astype(jnp.int32)
    tile_expert = jnp.minimum(tile_expert, N_EXPERTS - 1)
    n_valid = (seg_end[-1] // tm).astype(jnp.int32)[None]
    return pos1, pos2, tile_expert, n_valid


def kernel(x, p, positions, sandwich_norms, w_in, mla_q_norm, mla_kv_norm, w_uq, w_ukv,
           hgrn_lb_logits, hgrn_out_norm, w_out, ffn_w_gu, ffn_w_down, moe_w_router,
           moe_w_gu, moe_w_down, ple_w_proj, ple_w_gate, ple_norm):
    B, S, _ = x.shape
    T = B * S
    h = x.reshape(T, D_MODEL)
    ct, st = _rope_tables(positions)
    hgrn_consts = _hgrn_consts(HGRN_CHUNK)
    tm = min(T, 512)
    for l in range(DEPTH):
        z = _norm_matmul(h, sandwich_norms[l, 0], _prep_w_in(w_in[l]))
        q, k, v = _mla_proj(z, ct, st, mla_q_norm[l], mla_kv_norm[l],
                            _prep_w_uq(w_uq[l]), _prep_w_ukv(w_ukv[l]), B, S)
        o_mla = _attention(q, k, v).reshape(T, GROUP_WIDTH)
        o_hgrn = _hgrn(z, hgrn_lb_logits, hgrn_out_norm[l], hgrn_consts, l, B, S)
        moe = l % 2 == 1
        h1, c = _out_proj(o_mla, o_hgrn, h, w_out[l].astype(BF16),
                          sandwich_norms[l, 1], sandwich_norms[l, 2], F32 if moe else BF16)
        wg = ple_w_gate[l].astype(BF16)
        wp = ple_w_proj[l].astype(BF16)
        p_l = p[l].reshape(T, PLE_DIM)
        if not moe:
            n_tiles = T // tm
            tile_expert = jnp.zeros((n_tiles,), jnp.int32)
            n_valid = jnp.full((1,), n_tiles, jnp.int32)
            act = _gate_up(c, ffn_w_gu[l // 2][None].astype(BF16), tile_expert, n_valid, tm)
            f = _down(act, ffn_w_down[l // 2][None].astype(BF16), tile_expert, n_valid, tm)
            h = _post_ffn(f, h1, p_l, wg, wp, sandwich_norms[l, 3], ple_norm[l])
        else:
            n_tiles = (2 * T) // tm + N_EXPERTS
            route, counts = _router(c, moe_w_router[l // 2])
            pos1, pos2, tile_expert, n_valid = _moe_plan(route, counts, tm, n_tiles)
            xs = _dispatch(c, pos1, pos2, n_tiles * tm)
            act = _gate_up(xs, moe_w_gu[l // 2].astype(BF16), tile_expert, n_valid, tm)
            ys = _down(act, moe_w_down[l // 2].astype(BF16), tile_expert, n_valid, tm)
            h = _combine(ys, pos1, pos2, route, h1, p_l, wg, wp,
                         sandwich_norms[l, 3], ple_norm[l])
    return h.reshape(B, S, D_MODEL)
```

```python
import functools
import math

import numpy as np
import jax
import jax.numpy as jnp
from jax import lax
from jax.experimental import pallas as pl
from jax.experimental.pallas import tpu as pltpu

F32 = jnp.float32
BF16 = jnp.bfloat16

D_MODEL = 2048
DEPTH = 2
N_HEADS = 8
NOPE_DIM = 128
ROPE_DIM = 64
V_DIM = 128
QK_DIM = NOPE_DIM + ROPE_DIM
Q_RANK = 384
KV_RANK = 256
HGRN_K = 128
HGRN_V = 128
GROUP_WIDTH = N_HEADS * 128
FFN_DIM = 7168
N_EXPERTS = 8
PLE_DIM = 256
ROPE_THETA = 10000.0
LB_FLOOR = 1e-30
EPS = 1e-6

LANES = 128
QK_PAD = 256
Q_SCALE = QK_DIM ** -0.5 * math.log2(math.e)
Z_MLA = 1024
Z_WIDTH = 5120
HGRN_COL0 = 1024
HGRN_CHUNK = 64
HGRN_HEADS_PER_STEP = 2
VMEM_LIMIT = 56 * 1024 * 1024


def _params(semantics, vmem=VMEM_LIMIT):
    return pltpu.CompilerParams(dimension_semantics=semantics, vmem_limit_bytes=vmem)


def _rms(x, gain_row):
    ms = jnp.mean(x * x, axis=-1, keepdims=True)
    return x * lax.rsqrt(ms + EPS) * gain_row


def _dot(a, b):
    return jnp.dot(a, b, preferred_element_type=F32)


def _dot_nt(a, b):
    return lax.dot_general(a, b, (((1,), (1,)), ((), ())), preferred_element_type=F32)


def _dot_tn(a, b):
    return lax.dot_general(a, b, (((0,), (0,)), ((), ())), preferred_element_type=F32)


def _sigmoid(x):
    return 1.0 / (1.0 + jnp.exp(-x))


def _fold_lanes(x, op):
    parts = [x[:, c:c + LANES] for c in range(0, x.shape[1], LANES)]
    while len(parts) > 1:
        parts = [op(a, b) for a, b in zip(parts[0::2], parts[1::2])] + parts[len(parts) & ~1:]
    return parts[0]


def _rope_kernel(pos_ref, inv_ref, sgn_ref, ct_ref, st_ref):
    ang = pos_ref[...].astype(F32) * inv_ref[...]
    keep = jnp.abs(sgn_ref[...])
    ct_ref[...] = jnp.cos(ang) * keep
    st_ref[...] = jnp.sin(ang) * sgn_ref[...]


def _rope_tables(positions):
    T = positions.size
    tt = min(T, 1024)
    half = ROPE_DIM // 2
    inv_freq = 1.0 / (ROPE_THETA ** (jnp.arange(0, ROPE_DIM, 2, dtype=F32) / ROPE_DIM))
    inv_row = jnp.concatenate([inv_freq, inv_freq, jnp.zeros((LANES - ROPE_DIM,), F32)])[None, :]
    sgn = np.zeros((1, LANES), np.float32)
    sgn[0, :half] = -1.0
    sgn[0, half:ROPE_DIM] = 1.0
    return pl.pallas_call(
        _rope_kernel,
        grid=(T // tt,),
        in_specs=[pl.BlockSpec((tt, 1), lambda i: (i, 0)),
                  pl.BlockSpec((1, LANES), lambda i: (0, 0)),
                  pl.BlockSpec((1, LANES), lambda i: (0, 0))],
        out_specs=[pl.BlockSpec((tt, LANES), lambda i: (i, 0))] * 2,
        out_shape=[jax.ShapeDtypeStruct((T, LANES), F32)] * 2,
        compiler_params=_params(("parallel",)),
        name="rope_tables",
    )(positions.reshape(T, 1), inv_row, jnp.asarray(sgn))


def _norm_matmul_kernel(x_ref, g_ref, w_ref, o_ref, a_scr):
    @pl.when(pl.program_id(1) == 0)
    def _():
        a_scr[...] = _rms(x_ref[...], g_ref[...]).astype(BF16)

    o_ref[...] = _dot(a_scr[...], w_ref[...])


def _norm_matmul(x, gain, w):
    T, K = x.shape
    N = w.shape[1]
    tm = min(T, 1024)
    tn = 1024
    return pl.pallas_call(
        _norm_matmul_kernel,
        grid=(T // tm, N // tn),
        in_specs=[pl.BlockSpec((tm, K), lambda i, j: (i, 0)),
                  pl.BlockSpec((1, K), lambda i, j: (0, 0)),
                  pl.BlockSpec((K, tn), lambda i, j: (0, j))],
        out_specs=pl.BlockSpec((tm, tn), lambda i, j: (i, j)),
        out_shape=jax.ShapeDtypeStruct((T, N), F32),
        scratch_shapes=[pltpu.VMEM((tm, K), BF16)],
        compiler_params=_params(("parallel", "arbitrary")),
        name="in_proj",
    )(x, gain[None, :], w)


def _mla_proj_kernel(z_ref, ct_ref, st_ref, qn_ref, kvn_ref, wq_ref, wkv_ref,
                     q_ref, k_ref, v_ref):
    z = z_ref[...]
    ct = ct_ref[...]
    st = st_ref[...]
    aq = _rms(z[:, :Q_RANK], qn_ref[...]).astype(BF16)
    akv = _rms(z[:, Q_RANK:Q_RANK + KV_RANK], kvn_ref[...]).astype(BF16)
    k_a = z[:, 640:768]
    k_b = z[:, 768:896]
    k_pe = (k_a * ct + k_b * st).astype(BF16)
    q = _dot(aq, wq_ref[...])
    kv = _dot(akv, wkv_ref[...])
    for h in range(N_HEADS):
        lo, hi = h * LANES, (h + 1) * LANES
        q_ref[0, h, :, 0:LANES] = (q[:, lo:hi] * Q_SCALE).astype(BF16)
        q_pe = q[:, GROUP_WIDTH + lo:GROUP_WIDTH + hi] * ct \
            + q[:, 2 * GROUP_WIDTH + lo:2 * GROUP_WIDTH + hi] * st
        q_ref[0, h, :, LANES:QK_PAD] = (q_pe * Q_SCALE).astype(BF16)
        k_ref[0, h, :, 0:LANES] = kv[:, lo:hi].astype(BF16)
        k_ref[0, h, :, LANES:QK_PAD] = k_pe
        v_ref[0, h, :, :] = kv[:, GROUP_WIDTH + lo:GROUP_WIDTH + hi].astype(BF16)


def _mla_proj(z, ct, st, q_norm, kv_norm, wq, wkv, B, S):
    ts = min(S, 512)
    ns = S // ts
    tok = lambda b, i: (b * ns + i, 0)
    const = lambda b, i: (0, 0)
    head_out = lambda b, i: (b, 0, i, 0)
    return pl.pallas_call(
        _mla_proj_kernel,
        grid=(B, ns),
        in_specs=[pl.BlockSpec((ts, Z_MLA), tok),
                  pl.BlockSpec((ts, LANES), tok),
                  pl.BlockSpec((ts, LANES), tok),
                  pl.BlockSpec((1, Q_RANK), const),
                  pl.BlockSpec((1, KV_RANK), const),
                  pl.BlockSpec(wq.shape, const),
                  pl.BlockSpec(wkv.shape, const)],
        out_specs=[pl.BlockSpec((1, N_HEADS, ts, QK_PAD), head_out),
                   pl.BlockSpec((1, N_HEADS, ts, QK_PAD), head_out),
                   pl.BlockSpec((1, N_HEADS, ts, V_DIM), head_out)],
        out_shape=[jax.ShapeDtypeStruct((B, N_HEADS, S, QK_PAD), BF16),
                   jax.ShapeDtypeStruct((B, N_HEADS, S, QK_PAD), BF16),
                   jax.ShapeDtypeStruct((B, N_HEADS, S, V_DIM), BF16)],
        compiler_params=_params(("parallel", "parallel")),
        name="mla_proj",
    )(z, ct, st, q_norm[None, :], kv_norm[None, :], wq, wkv)


def _attn_kernel(q_ref, k_ref, v_ref, o_ref, s_a, s_b, *, tk):
    i = pl.program_id(2)
    top_rows = pl.ds(0, tk)
    bot_rows = pl.ds(tk, tk)

    def kv_rows(u):
        return pl.ds(pl.multiple_of(u * tk, tk), tk)

    def scores(u):
        return _dot_nt(q_ref[0, 0], k_ref[0, 0, kv_rows(u), :])

    def update(state, s, v, masked):
        m, l, acc = state
        if masked:
            row = lax.broadcasted_iota(jnp.int32, (tk, tk), 0)
            col = lax.broadcasted_iota(jnp.int32, (tk, tk), 1)
            s = jnp.where(col <= row, s, -jnp.inf)
        m_new = jnp.maximum(m, jnp.max(_fold_lanes(s, jnp.maximum), axis=-1, keepdims=True))
        alpha = jnp.exp2(m - m_new)
        p = jnp.exp2(s - m_new)
        l = alpha * l + jnp.sum(_fold_lanes(p, jnp.add), axis=-1, keepdims=True)
        acc = alpha * acc + _dot(p.astype(BF16), v)
        return m_new, l, acc

    def both(top, bot, s_ref, u):
        v = v_ref[0, 0, kv_rows(u), :]
        return (update(top, s_ref[top_rows, :], v, False),
                update(bot, s_ref[bot_rows, :], v, False))

    s_a[...] = scores(0)

    def body(t, state):
        top, bot = state
        s_b[...] = scores(2 * t + 1)
        top, bot = both(top, bot, s_a, 2 * t)
        s_a[...] = scores(2 * t + 2)
        top, bot = both(top, bot, s_b, 2 * t + 1)
        return top, bot

    init = (jnp.full((tk, 1), -jnp.inf, F32), jnp.zeros((tk, 1), F32),
            jnp.zeros((tk, V_DIM), F32))
    top, bot = lax.fori_loop(0, i, body, (init, init))
    s_b[bot_rows, :] = _dot_nt(q_ref[0, 0, bot_rows, :], k_ref[0, 0, kv_rows(2 * i + 1), :])
    v0 = v_ref[0, 0, kv_rows(2 * i), :]
    top = update(top, s_a[top_rows, :], v0, True)
    bot = update(bot, s_a[bot_rows, :], v0, False)
    bot = update(bot, s_b[bot_rows, :], v_ref[0, 0, kv_rows(2 * i + 1), :], True)
    o_ref[0, top_rows, :] = (top[2] / top[1]).astype(BF16)
    o_ref[0, bot_rows, :] = (bot[2] / bot[1]).astype(BF16)


def _attention(q, k, v):
    B, H, S, _ = q.shape
    tq = min(S, 1024)
    tk = tq // 2
    return pl.pallas_call(
        functools.partial(_attn_kernel, tk=tk),
        grid=(B, H, S // tq),
        in_specs=[pl.BlockSpec((1, 1, tq, QK_PAD), lambda b, h, i: (b, h, i, 0)),
                  pl.BlockSpec((1, 1, S, QK_PAD), lambda b, h, i: (b, h, 0, 0)),
                  pl.BlockSpec((1, 1, S, V_DIM), lambda b, h, i: (b, h, 0, 0))],
        out_specs=pl.BlockSpec((1, tq, V_DIM), lambda b, h, i: (b, i, h)),
        out_shape=jax.ShapeDtypeStruct((B, S, H * V_DIM), BF16),
        scratch_shapes=[pltpu.VMEM((tq, tk), F32), pltpu.VMEM((tq, tk), F32)],
        compiler_params=_params(("parallel", "parallel", "arbitrary")),
        name="mla_attention",
    )(q, k, v)


def _hgrn_consts(C):
    levels = [C >> (i + 1) for i in range(int(math.log2(C)))]
    t = np.arange(C)
    u = np.arange(C)
    mats = [(u[None, :] <= t[:, None]),
            (u[None, :] > t[:, None])]
    hi_rows, lo_rows = [], []
    pair_masks = [np.eye(C, dtype=bool)]
    for m in levels:
        ref = (t // (2 * m)) * (2 * m) + m - 1
        hi = ((t // m) % 2) == 1
        rng_hi = (u[None, :] > ref[:, None]) & (u[None, :] <= t[:, None])
        rng_lo = (u[None, :] > t[:, None]) & (u[None, :] <= ref[:, None])
        mats.append(np.where(hi[:, None], rng_hi, rng_lo))
        hi_rows.append(np.repeat(hi[:, None], LANES, axis=1))
        lo_rows.append(np.repeat(~hi[:, None], LANES, axis=1))
        same = (t[:, None] // (2 * m)) == (t[None, :] // (2 * m))
        pair_masks.append(hi[:, None] & (~hi[None, :]) & same)
    stack = np.concatenate(mats, axis=0).astype(np.float32)
    sums = jnp.asarray(np.concatenate([stack] * 3, axis=1), dtype=BF16)
    return (sums,
            jnp.asarray(np.stack(hi_rows).astype(np.float32)),
            jnp.asarray(np.stack(lo_rows).astype(np.float32)),
            jnp.asarray(np.stack(pair_masks).astype(np.float32)))


def _hgrn_kernel(zq_ref, zf_ref, zi_ref, zg_ref, lbl_ref, on_ref,
                 sums_ref, hi_ref, lo_ref, pm_ref, o_ref, st_scr, *, layer, ts):
    C = HGRN_CHUNK
    n_levels = hi_ref.shape[0]

    @pl.when(pl.program_id(2) == 0)
    def _():
        st_scr[...] = jnp.zeros_like(st_scr)

    lg = lbl_ref[...]
    ex = jnp.exp(lg - jnp.max(lg, axis=0, keepdims=True))
    pr = ex / jnp.sum(ex, axis=0, keepdims=True)
    cum = pr[0:1, :]
    for r in range(1, layer + 1):
        cum = cum + pr[r:r + 1, :]
    lb = cum - pr[0:1, :]
    lb_floor_all = jnp.maximum(lb, LB_FLOOR)
    one_m_lb_all = 1.0 - lb
    gain = on_ref[...]

    for c, hh in [(c, hh) for c in range(ts // C) for hh in range(HGRN_HEADS_PER_STEP)]:
        rows = pl.ds(c * C, C)
        lanes = pl.ds(hh * LANES, LANES)
        lb_floor = lb_floor_all[:, hh * LANES:(hh + 1) * LANES]
        one_m_lb = one_m_lb_all[:, hh * LANES:(hh + 1) * LANES]
        q_raw = zq_ref[rows, lanes]
        zf = zf_ref[rows, lanes]
        v = zi_ref[rows, lanes].astype(BF16)
        g_raw = zg_ref[rows, lanes]
        q = q_raw * _sigmoid(q_raw)
        e = jnp.exp(-jnp.abs(zf))
        r = 1.0 / (1.0 + e)
        er = e * r
        pos = zf >= 0
        log_f = jnp.log(lb_floor + one_m_lb * jnp.where(pos, r, er))
        kk = one_m_lb * jnp.where(pos, er, r)
        p0 = log_f.astype(BF16)
        r1 = log_f - p0.astype(F32)
        p1 = r1.astype(BF16)
        p2 = (r1 - p1.astype(F32)).astype(BF16)
        sums = _dot(sums_ref[...], jnp.concatenate([p0, p1, p2], axis=0))
        b = sums[0:C]
        b_rest = sums[C:2 * C]
        a = pm_ref[0] * _dot_nt(q.astype(BF16), kk.astype(BF16))
        for lv in range(n_levels):
            dec = jnp.exp(sums[(2 + lv) * C:(3 + lv) * C])
            qt = (q * dec * hi_ref[lv]).astype(BF16)
            kt = (kk * dec * lo_ref[lv]).astype(BF16)
            a = a + pm_ref[lv + 1] * _dot_nt(qt, kt)
        st = st_scr[hh]
        o = _dot(a.astype(BF16), v) + _dot_nt((q * jnp.exp(b)).astype(BF16), st.astype(BF16))
        k_end = (kk * jnp.exp(b_rest)).astype(BF16)
        st_scr[hh] = st * jnp.exp(b[C - 1:C, :]) + _dot_tn(v, k_end)
        o_ref[rows, lanes] = (_rms(o, gain) * (g_raw * _sigmoid(g_raw))).astype(BF16)


def _hgrn(z, lb_logits, out_norm, consts, layer, B, S):
    T = B * S
    ts = min(S, 256)
    ns = S // ts
    sums, hi_rows, lo_rows, pair_masks = consts
    hp = HGRN_HEADS_PER_STEP
    width = hp * LANES
    col0 = HGRN_COL0 // width
    groups = N_HEADS // hp

    def zspec(part):
        return pl.BlockSpec((ts, width),
                            lambda b, h, i: (b * ns + i, col0 + part * groups + h))

    full = lambda arr: pl.BlockSpec(arr.shape, lambda b, h, i: (0,) * arr.ndim)
    kern = functools.partial(_hgrn_kernel, layer=layer, ts=ts)
    return pl.pallas_call(
        kern,
        grid=(B, groups, ns),
        in_specs=[zspec(0), zspec(1), zspec(2), zspec(3),
                  pl.BlockSpec((DEPTH, width), lambda b, h, i: (0, h)),
                  pl.BlockSpec((1, HGRN_V), lambda b, h, i: (0, 0)),
                  full(sums), full(hi_rows), full(lo_rows), full(pair_masks)],
        out_specs=pl.BlockSpec((ts, width), lambda b, h, i: (b * ns + i, h)),
        out_shape=jax.ShapeDtypeStruct((T, GROUP_WIDTH), BF16),
        scratch_shapes=[pltpu.VMEM((hp, HGRN_V, HGRN_K), F32)],
        compiler_params=_params(("parallel", "parallel", "arbitrary")),
        name="hgrn2",
    )(z, z, z, z, lb_logits, out_norm[None, :], sums, hi_rows, lo_rows, pair_masks)


def _out_proj_kernel(om_ref, oh_ref, h_ref, w_ref, g1_ref, g2_ref, h1_ref, c_ref):
    y = _dot(om_ref[...], w_ref[0:GROUP_WIDTH, :]) + _dot(oh_ref[...], w_ref[GROUP_WIDTH:, :])
    h1 = h_ref[...] + _rms(y, g1_ref[...])
    h1_ref[...] = h1
    c_ref[...] = _rms(h1, g2_ref[...]).astype(c_ref.dtype)


def _out_proj(om, oh, h, w, g1, g2, c_dtype):
    T = h.shape[0]
    tm = min(T, 256)
    row = lambda i: (i, 0)
    const = lambda i: (0, 0)
    return pl.pallas_call(
        _out_proj_kernel,
        grid=(T // tm,),
        in_specs=[pl.BlockSpec((tm, GROUP_WIDTH), row),
                  pl.BlockSpec((tm, GROUP_WIDTH), row),
                  pl.BlockSpec((tm, D_MODEL), row),
                  pl.BlockSpec((D_MODEL, D_MODEL), const),
                  pl.BlockSpec((1, D_MODEL), const),
                  pl.BlockSpec((1, D_MODEL), const)],
        out_specs=[pl.BlockSpec((tm, D_MODEL), row), pl.BlockSpec((tm, D_MODEL), row)],
        out_shape=[jax.ShapeDtypeStruct((T, D_MODEL), F32),
                   jax.ShapeDtypeStruct((T, D_MODEL), c_dtype)],
        compiler_params=_params(("parallel",)),
        name="out_proj",
    )(om, oh, h, w, g1[None, :], g2[None, :])


def _gate_up_kernel(te_ref, nv_ref, x_ref, wg_ref, wu_ref, o_ref, wg_bf, wu_bf):
    i = pl.program_id(1)
    valid = i < nv_ref[0]
    new_weights = jnp.logical_or(i == 0, te_ref[i] != te_ref[jnp.maximum(i - 1, 0)])

    @pl.when(jnp.logical_and(valid, new_weights))
    def _():
        wg_bf[...] = wg_ref[0].astype(BF16)
        wu_bf[...] = wu_ref[0].astype(BF16)

    @pl.when(valid)
    def _():
        x = x_ref[...].astype(BF16)
        g = _dot(x, wg_bf[...])
        u = _dot(x, wu_bf[...])
        o_ref[...] = (g * _sigmoid(g) * u).astype(BF16)

    @pl.when(jnp.logical_not(valid))
    def _():
        o_ref[...] = jnp.zeros_like(o_ref)


def _gate_up(x, w_gu, tile_expert, n_valid, tm):
    R = x.shape[0]
    n_tiles = R // tm
    tf = 512
    nf = FFN_DIM // tf

    def tile(i, nv):
        return jnp.minimum(i, nv[0] - 1)

    return pl.pallas_call(
        _gate_up_kernel,
        grid_spec=pltpu.PrefetchScalarGridSpec(
            num_scalar_prefetch=2,
            grid=(nf, n_tiles),
            in_specs=[pl.BlockSpec((tm, D_MODEL), lambda j, i, te, nv: (tile(i, nv), 0)),
                      pl.BlockSpec((1, D_MODEL, tf), lambda j, i, te, nv: (te[tile(i, nv)], 0, j)),
                      pl.BlockSpec((1, D_MODEL, tf),
                                   lambda j, i, te, nv: (te[tile(i, nv)], 0, j + nf))],
            out_specs=pl.BlockSpec((tm, tf), lambda j, i, te, nv: (i, j)),
            scratch_shapes=[pltpu.VMEM((D_MODEL, tf), BF16), pltpu.VMEM((D_MODEL, tf), BF16)],
        ),
        out_shape=jax.ShapeDtypeStruct((R, FFN_DIM), BF16),
        compiler_params=_params(("arbitrary", "arbitrary")),
        name="ffn_gate_up",
    )(tile_expert, n_valid, x, w_gu, w_gu)


def _down_kernel(te_ref, nv_ref, a_ref, w_ref, o_ref):
    valid = pl.program_id(1) < nv_ref[0]

    @pl.when(valid)
    def _():
        o_ref[...] = _dot(a_ref[...], w_ref[0])

    @pl.when(jnp.logical_not(valid))
    def _():
        o_ref[...] = jnp.zeros_like(o_ref)


def _down(act, w_down, tile_expert, n_valid, tm):
    R = act.shape[0]
    n_tiles = R // tm
    tn = 512

    def tile(i, nv):
        return jnp.minimum(i, nv[0] - 1)

    return pl.pallas_call(
        _down_kernel,
        grid_spec=pltpu.PrefetchScalarGridSpec(
            num_scalar_prefetch=2,
            grid=(D_MODEL // tn, n_tiles),
            in_specs=[pl.BlockSpec((tm, FFN_DIM), lambda j, i, te, nv: (tile(i, nv), 0)),
                      pl.BlockSpec((1, FFN_DIM, tn), lambda j, i, te, nv: (te[tile(i, nv)], 0, j))],
            out_specs=pl.BlockSpec((tm, tn), lambda j, i, te, nv: (i, j)),
        ),
        out_shape=jax.ShapeDtypeStruct((R, D_MODEL), F32),
        compiler_params=_params(("arbitrary", "arbitrary")),
        name="ffn_down",
    )(tile_expert, n_valid, act, w_down)


def _ffn_epilogue(f, h1, p, wg, wp, g3, gp):
    h2 = h1 + _rms(f, g3)
    gate = _sigmoid(_dot(h2.astype(BF16), wg))
    proj = _dot(p.astype(BF16), wp)
    return h2 + _rms(gate * proj, gp)


def _post_ffn_kernel(f_ref, h1_ref, p_ref, wg_ref, wp_ref, g3_ref, gp_ref, o_ref):
    o_ref[...] = _ffn_epilogue(f_ref[...], h1_ref[...], p_ref[...], wg_ref[...], wp_ref[...],
                               g3_ref[...], gp_ref[...])


def _post_ffn(f, h1, p, wg, wp, g3, gp):
    T = h1.shape[0]
    tm = min(T, 256)
    row = lambda i: (i, 0)
    const = lambda i: (0, 0)
    return pl.pallas_call(
        _post_ffn_kernel,
        grid=(T // tm,),
        in_specs=[pl.BlockSpec((tm, D_MODEL), row),
                  pl.BlockSpec((tm, D_MODEL), row),
                  pl.BlockSpec((tm, PLE_DIM), row),
                  pl.BlockSpec((D_MODEL, D_MODEL), const),
                  pl.BlockSpec((PLE_DIM, D_MODEL), const),
                  pl.BlockSpec((1, D_MODEL), const),
                  pl.BlockSpec((1, D_MODEL), const)],
        out_specs=pl.BlockSpec((tm, D_MODEL), row),
        out_shape=jax.ShapeDtypeStruct((T, D_MODEL), F32),
        compiler_params=_params(("parallel",)),
        name="post_ffn",
    )(f, h1, p, wg, wp, g3[None, :], gp[None, :])


def _router_kernel(c_ref, wr_ref, tri_ref, route_ref, cnt_ref, carry):
    @pl.when(pl.program_id(0) == 0)
    def _():
        carry[...] = jnp.zeros_like(carry)

    logits = _dot(c_ref[...].astype(BF16), wr_ref[...])
    lane = lax.broadcasted_iota(jnp.int32, logits.shape, 1)
    lg = jnp.where(lane < N_EXPERTS, logits, -jnp.inf)
    m1 = jnp.max(lg, axis=-1, keepdims=True)
    i1 = jnp.min(jnp.where(lg == m1, lane, LANES), axis=-1, keepdims=True)
    lg2 = jnp.where(lane == i1, -jnp.inf, lg)
    m2 = jnp.max(lg2, axis=-1, keepdims=True)
    i2 = jnp.min(jnp.where(lg2 == m2, lane, LANES), axis=-1, keepdims=True)
    e = jnp.exp(m2 - m1)
    g1 = 1.0 / (1.0 + e)
    g2 = e / (1.0 + e)
    onehot = jnp.where((lane == i1) | (lane == i2), 1.0, 0.0)
    before = _dot(tri_ref[...], onehot.astype(BF16)) + carry[0:1, :]
    rank1 = jnp.sum(jnp.where(lane == i1, before, 0.0), axis=-1, keepdims=True)
    rank2 = jnp.sum(jnp.where(lane == i2, before, 0.0), axis=-1, keepdims=True)
    total = carry[0:1, :] + jnp.sum(onehot, axis=0, keepdims=True)
    carry[...] = jnp.broadcast_to(total, carry.shape)
    cnt_ref[...] = jnp.broadcast_to(total, cnt_ref.shape)
    out = jnp.where(lane == 0, i1.astype(F32), 0.0)
    out = jnp.where(lane == 1, i2.astype(F32), out)
    out = jnp.where(lane == 2, g1, out)
    out = jnp.where(lane == 3, g2, out)
    out = jnp.where(lane == 4, rank1, out)
    out = jnp.where(lane == 5, rank2, out)
    route_ref[...] = out


def _router(c, w_router):
    T = c.shape[0]
    tr = min(T, 512)
    wr = jnp.zeros((D_MODEL, LANES), BF16).at[:, :N_EXPERTS].set(w_router.astype(BF16))
    tri = jnp.asarray(np.tril(np.ones((tr, tr), np.float32), -1), dtype=BF16)
    return pl.pallas_call(
        _router_kernel,
        grid=(T // tr,),
        in_specs=[pl.BlockSpec((tr, D_MODEL), lambda i: (i, 0)),
                  pl.BlockSpec((D_MODEL, LANES), lambda i: (0, 0)),
                  pl.BlockSpec((tr, tr), lambda i: (0, 0))],
        out_specs=[pl.BlockSpec((tr, LANES), lambda i: (i, 0)),
                   pl.BlockSpec((8, LANES), lambda i: (0, 0))],
        out_shape=[jax.ShapeDtypeStruct((T, LANES), F32),
                   jax.ShapeDtypeStruct((8, LANES), F32)],
        scratch_shapes=[pltpu.VMEM((8, LANES), F32)],
        compiler_params=_params(("arbitrary",)),
        name="moe_router",
    )(c, wr, tri)


def _dispatch_kernel(p1_ref, p2_ref, c_ref, xs_in_ref, xs_ref, sem, *, tm):
    del xs_in_ref
    base = pl.program_id(0) * tm

    def row_copy(r, dst):
        return pltpu.make_async_copy(c_ref.at[pl.ds(r, 1)], xs_ref.at[pl.ds(dst, 1)], sem)

    def issue(r, _):
        row_copy(r, p1_ref[base + r]).start()
        row_copy(r, p2_ref[base + r]).start()
        return 0

    lax.fori_loop(0, tm, issue, 0)

    def drain(r, _):
        row_copy(0, 0).wait()
        row_copy(0, 0).wait()
        return 0

    lax.fori_loop(0, tm, drain, 0)


def _dispatch(c, pos1, pos2, n_rows):
    T = c.shape[0]
    tm = min(T, 256)
    zeros = jnp.zeros((n_rows, D_MODEL), c.dtype)
    return pl.pallas_call(
        functools.partial(_dispatch_kernel, tm=tm),
        grid_spec=pltpu.PrefetchScalarGridSpec(
            num_scalar_prefetch=2,
            grid=(T // tm,),
            in_specs=[pl.BlockSpec((tm, D_MODEL), lambda i, p1, p2: (i, 0)),
                      pl.BlockSpec(memory_space=pl.ANY)],
            out_specs=pl.BlockSpec(memory_space=pl.ANY),
            scratch_shapes=[pltpu.SemaphoreType.DMA(())],
        ),
        out_shape=jax.ShapeDtypeStruct((n_rows, D_MODEL), c.dtype),
        input_output_aliases={3: 0},
        compiler_params=_params(("arbitrary",)),
        name="moe_dispatch",
    )(pos1, pos2, c, zeros)


def _combine_kernel(p1_ref, p2_ref, ys_ref, route_ref, h1_ref, p_ref, wg_ref, wp_ref,
                    g3_ref, gp_ref, o_ref, buf, sem, *, tm):
    base = pl.program_id(0) * tm

    def row_copy(src, k, r):
        return pltpu.make_async_copy(ys_ref.at[pl.ds(src, 1)], buf.at[k, pl.ds(r, 1)], sem)

    def issue(r, _):
        row_copy(p1_ref[base + r], 0, r).start()
        row_copy(p2_ref[base + r], 1, r).start()
        return 0

    lax.fori_loop(0, tm, issue, 0)

    def drain(r, _):
        row_copy(0, 0, 0).wait()
        row_copy(0, 1, 0).wait()
        return 0

    lax.fori_loop(0, tm, drain, 0)
    route = route_ref[...]
    f = route[:, 2:3] * buf[0] + route[:, 3:4] * buf[1]
    o_ref[...] = _ffn_epilogue(f, h1_ref[...], p_ref[...], wg_ref[...], wp_ref[...],
                               g3_ref[...], gp_ref[...])


def _combine(ys, pos1, pos2, route, h1, p, wg, wp, g3, gp):
    T = h1.shape[0]
    tm = min(T, 256)
    row = lambda i, p1, p2: (i, 0)
    const = lambda i, p1, p2: (0, 0)
    return pl.pallas_call(
        functools.partial(_combine_kernel, tm=tm),
        grid_spec=pltpu.PrefetchScalarGridSpec(
            num_scalar_prefetch=2,
            grid=(T // tm,),
            in_specs=[pl.BlockSpec(memory_space=pl.ANY),
                      pl.BlockSpec((tm, LANES), row),
                      pl.BlockSpec((tm, D_MODEL), row),
                      pl.BlockSpec((tm, PLE_DIM), row),
                      pl.BlockSpec((D_MODEL, D_MODEL), const),
                      pl.BlockSpec((PLE_DIM, D_MODEL), const),
                      pl.BlockSpec((1, D_MODEL), const),
                      pl.BlockSpec((1, D_MODEL), const)],
            out_specs=pl.BlockSpec((tm, D_MODEL), row),
            scratch_shapes=[pltpu.VMEM((2, tm, D_MODEL), F32),
                            pltpu.SemaphoreType.DMA(())],
        ),
        out_shape=jax.ShapeDtypeStruct((T, D_MODEL), F32),
        compiler_params=_params(("arbitrary",)),
        name="moe_combine",
    )(pos1, pos2, ys, route, h1, p, wg, wp, g3[None, :], gp[None, :])


def _swap_halves(w):
    half = w.shape[-1] // 2
    return jnp.concatenate([w[..., half:], w[..., :half]], axis=-1)


def _prep_w_in(w):
    k_rope = w[:, 640:704]
    pad64 = jnp.zeros((D_MODEL, 64), w.dtype)
    cols = [w[:, :640], k_rope, pad64, _swap_halves(k_rope), pad64,
            jnp.zeros((D_MODEL, HGRN_COL0 - 896), w.dtype), w[:, 704:]]
    return jnp.concatenate(cols, axis=1).astype(BF16)


def _prep_w_uq(w):
    w = w.reshape(Q_RANK, N_HEADS, QK_DIM)
    nope = w[:, :, :NOPE_DIM]
    pe = w[:, :, NOPE_DIM:]
    pad = jnp.zeros((Q_RANK, N_HEADS, LANES - ROPE_DIM), w.dtype)
    pe_pad = jnp.concatenate([pe, pad], axis=-1)
    pe_swap = jnp.concatenate([_swap_halves(pe), pad], axis=-1)
    parts = [x.reshape(Q_RANK, GROUP_WIDTH) for x in (nope, pe_pad, pe_swap)]
    return jnp.concatenate(parts, axis=1).astype(BF16)


def _prep_w_ukv(w):
    w = w.reshape(KV_RANK, N_HEADS, NOPE_DIM + V_DIM)
    k = w[:, :, :NOPE_DIM].reshape(KV_RANK, GROUP_WIDTH)
    v = w[:, :, NOPE_DIM:].reshape(KV_RANK, GROUP_WIDTH)
    return jnp.concatenate([k, v], axis=1).astype(BF16)


def _moe_plan(route, counts, tm, n_tiles):
    e1 = route[:, 0].astype(jnp.int32)
    e2 = route[:, 1].astype(jnp.int32)
    rank1 = route[:, 4].astype(jnp.int32)
    rank2 = route[:, 5].astype(jnp.int32)
    cnt = counts[0, :N_EXPERTS].astype(jnp.int32)
    padded = ((cnt + tm - 1) // tm) * tm
    seg_end = jnp.cumsum(padded)
    seg_start = seg_end - padded
    pos1 = seg_start[e1] + rank1
    pos2 = seg_start[e2] + rank2
    tile_start = jnp.arange(n_tiles, dtype=jnp.int32) * tm
    tile_expert = jnp.sum(tile_start[:, None] >= seg_end[None, :], axis=1).astype(jnp.int32)
    tile_expert = jnp.minimum(tile_expert, N_EXPERTS - 1)
    n_valid = (seg_end[-1] // tm).astype(jnp.int32)[None]
    return pos1, pos2, tile_expert, n_valid


def kernel(x, p, positions, sandwich_norms, w_in, mla_q_norm, mla_kv_norm, w_uq, w_ukv,
           hgrn_lb_logits, hgrn_out_norm, w_out, ffn_w_gu, ffn_w_down, moe_w_router,
           moe_w_gu, moe_w_down, ple_w_proj, ple_w_gate, ple_norm):
    B, S, _ = x.shape
    T = B * S
    h = x.reshape(T, D_MODEL)
    ct, st = _rope_tables(positions)
    hgrn_consts = _hgrn_consts(HGRN_CHUNK)
    tm = min(T, 512)
    for l in range(DEPTH):
        z = _norm_matmul(h, sandwich_norms[l, 0], _prep_w_in(w_in[l]))
        q, k, v = _mla_proj(z, ct, st, mla_q_norm[l], mla_kv_norm[l],
                            _prep_w_uq(w_uq[l]), _prep_w_ukv(w_ukv[l]), B, S)
        o_mla = _attention(q, k, v).reshape(T, GROUP_WIDTH)
        o_hgrn = _hgrn(z, hgrn_lb_logits, hgrn_out_norm[l], hgrn_consts, l, B, S)
        moe = l % 2 == 1
        h1, c = _out_proj(o_mla, o_hgrn, h, w_out[l].astype(BF16),
                          sandwich_norms[l, 1], sandwich_norms[l, 2], F32 if moe else BF16)
        wg = ple_w_gate[l].astype(BF16)
        wp = ple_w_proj[l].astype(BF16)
        p_l = p[l].reshape(T, PLE_DIM)
        if not moe:
            n_tiles = T // tm
            tile_expert = jnp.zeros((n_tiles,), jnp.int32)
            n_valid = jnp.full((1,), n_tiles, jnp.int32)
            act = _gate_up(c, ffn_w_gu[l // 2][None], tile_expert, n_valid, tm)
            f = _down(act, ffn_w_down[l // 2][None].astype(BF16), tile_expert, n_valid, tm)
            h = _post_ffn(f, h1, p_l, wg, wp, sandwich_norms[l, 3], ple_norm[l])
        else:
            n_tiles = (2 * T) // tm + N_EXPERTS
            route, counts = _router(c, moe_w_router[l // 2])
            pos1, pos2, tile_expert, n_valid = _moe_plan(route, counts, tm, n_tiles)
            xs = _dispatch(c, pos1, pos2, n_tiles * tm)
            act = _gate_up(xs, moe_w_gu[l // 2], tile_expert, n_valid, tm)
            ys = _down(act, moe_w_down[l // 2].astype(BF16), tile_expert, n_valid, tm)
            h = _combine(ys, pos1, pos2, route, h1, p_l, wg, wp,
                         sandwich_norms[l, 3], ple_norm[l])
    return h.reshape(B, S, D_MODEL)
```

```python
import functools
import math

import numpy as np
import jax
import jax.numpy as jnp
from jax import lax
from jax.experimental import pallas as pl
from jax.experimental.pallas import tpu as pltpu

F32 = jnp.float32
BF16 = jnp.bfloat16

D_MODEL = 2048
DEPTH = 2
N_HEADS = 8
NOPE_DIM = 128
ROPE_DIM = 64
V_DIM = 128
QK_DIM = NOPE_DIM + ROPE_DIM
Q_RANK = 384
KV_RANK = 256
HGRN_K = 128
HGRN_V = 128
GROUP_WIDTH = N_HEADS * 128
FFN_DIM = 7168
N_EXPERTS = 8
PLE_DIM = 256
ROPE_THETA = 10000.0
LB_FLOOR = 1e-30
EPS = 1e-6

LANES = 128
QK_PAD = 256
Q_SCALE = QK_DIM ** -0.5 * math.log2(math.e)
Z_MLA = 1024
Z_WIDTH = 5120
HGRN_COL0 = 1024
HGRN_CHUNK = 64
HGRN_HEADS_PER_STEP = 4
ROW_UNROLL = 8
VMEM_LIMIT = 56 * 1024 * 1024


def _params(semantics, vmem=VMEM_LIMIT):
    return pltpu.CompilerParams(dimension_semantics=semantics, vmem_limit_bytes=vmem)


def _rms(x, gain_row):
    ms = jnp.mean(x * x, axis=-1, keepdims=True)
    return x * lax.rsqrt(ms + EPS) * gain_row


def _dot(a, b):
    return jnp.dot(a, b, preferred_element_type=F32)


def _dot_nt(a, b):
    return lax.dot_general(a, b, (((1,), (1,)), ((), ())), preferred_element_type=F32)


def _dot_tn(a, b):
    return lax.dot_general(a, b, (((0,), (0,)), ((), ())), preferred_element_type=F32)


def _sigmoid(x):
    return 1.0 / (1.0 + jnp.exp(-x))


def _fold_lanes(x, op):
    parts = [x[:, c:c + LANES] for c in range(0, x.shape[1], LANES)]
    while len(parts) > 1:
        parts = [op(a, b) for a, b in zip(parts[0::2], parts[1::2])] + parts[len(parts) & ~1:]
    return parts[0]


def _rope_kernel(pos_ref, inv_ref, sgn_ref, ct_ref, st_ref):
    ang = pos_ref[...].astype(F32) * inv_ref[...]
    keep = jnp.abs(sgn_ref[...])
    ct_ref[...] = jnp.cos(ang) * keep
    st_ref[...] = jnp.sin(ang) * sgn_ref[...]


def _rope_tables(positions):
    T = positions.size
    tt = min(T, 1024)
    half = ROPE_DIM // 2
    inv_freq = 1.0 / (ROPE_THETA ** (jnp.arange(0, ROPE_DIM, 2, dtype=F32) / ROPE_DIM))
    inv_row = jnp.concatenate([inv_freq, inv_freq, jnp.zeros((LANES - ROPE_DIM,), F32)])[None, :]
    sgn = np.zeros((1, LANES), np.float32)
    sgn[0, :half] = -1.0
    sgn[0, half:ROPE_DIM] = 1.0
    return pl.pallas_call(
        _rope_kernel,
        grid=(T // tt,),
        in_specs=[pl.BlockSpec((tt, 1), lambda i: (i, 0)),
                  pl.BlockSpec((1, LANES), lambda i: (0, 0)),
                  pl.BlockSpec((1, LANES), lambda i: (0, 0))],
        out_specs=[pl.BlockSpec((tt, LANES), lambda i: (i, 0))] * 2,
        out_shape=[jax.ShapeDtypeStruct((T, LANES), F32)] * 2,
        compiler_params=_params(("parallel",)),
        name="rope_tables",
    )(positions.reshape(T, 1), inv_row, jnp.asarray(sgn))


def _norm_matmul_kernel(x_ref, g_ref, w_ref, o_ref, a_scr):
    @pl.when(pl.program_id(1) == 0)
    def _():
        a_scr[...] = _rms(x_ref[...], g_ref[...]).astype(BF16)

    o_ref[...] = _dot(a_scr[...], w_ref[...])


def _norm_matmul(x, gain, w):
    T, K = x.shape
    N = w.shape[1]
    tm = min(T, 1024)
    tn = 1024
    return pl.pallas_call(
        _norm_matmul_kernel,
        grid=(T // tm, N // tn),
        in_specs=[pl.BlockSpec((tm, K), lambda i, j: (i, 0)),
                  pl.BlockSpec((1, K), lambda i, j: (0, 0)),
                  pl.BlockSpec((K, tn), lambda i, j: (0, j))],
        out_specs=pl.BlockSpec((tm, tn), lambda i, j: (i, j)),
        out_shape=jax.ShapeDtypeStruct((T, N), F32),
        scratch_shapes=[pltpu.VMEM((tm, K), BF16)],
        compiler_params=_params(("parallel", "arbitrary")),
        name="in_proj",
    )(x, gain[None, :], w)


def _mla_proj_kernel(z_ref, ct_ref, st_ref, qn_ref, kvn_ref, wq_ref, wkv_ref,
                     q_ref, k_ref, v_ref):
    z = z_ref[...]
    ct = ct_ref[...]
    st = st_ref[...]
    aq = _rms(z[:, :Q_RANK], qn_ref[...]).astype(BF16)
    akv = _rms(z[:, Q_RANK:Q_RANK + KV_RANK], kvn_ref[...]).astype(BF16)
    k_a = z[:, 640:768]
    k_b = z[:, 768:896]
    k_pe = (k_a * ct + k_b * st).astype(BF16)
    q = _dot(aq, wq_ref[...])
    kv = _dot(akv, wkv_ref[...])
    for h in range(N_HEADS):
        lo, hi = h * LANES, (h + 1) * LANES
        q_ref[0, h, :, 0:LANES] = (q[:, lo:hi] * Q_SCALE).astype(BF16)
        q_pe = q[:, GROUP_WIDTH + lo:GROUP_WIDTH + hi] * ct \
            + q[:, 2 * GROUP_WIDTH + lo:2 * GROUP_WIDTH + hi] * st
        q_ref[0, h, :, LANES:QK_PAD] = (q_pe * Q_SCALE).astype(BF16)
        k_ref[0, h, :, 0:LANES] = kv[:, lo:hi].astype(BF16)
        k_ref[0, h, :, LANES:QK_PAD] = k_pe
        v_ref[0, h, :, :] = kv[:, GROUP_WIDTH + lo:GROUP_WIDTH + hi].astype(BF16)


def _mla_proj(z, ct, st, q_norm, kv_norm, wq, wkv, B, S):
    ts = min(S, 512)
    ns = S // ts
    tok = lambda b, i: (b * ns + i, 0)
    const = lambda b, i: (0, 0)
    head_out = lambda b, i: (b, 0, i, 0)
    return pl.pallas_call(
        _mla_proj_kernel,
        grid=(B, ns),
        in_specs=[pl.BlockSpec((ts, Z_MLA), tok),
                  pl.BlockSpec((ts, LANES), tok),
                  pl.BlockSpec((ts, LANES), tok),
                  pl.BlockSpec((1, Q_RANK), const),
                  pl.BlockSpec((1, KV_RANK), const),
                  pl.BlockSpec(wq.shape, const),
                  pl.BlockSpec(wkv.shape, const)],
        out_specs=[pl.BlockSpec((1, N_HEADS, ts, QK_PAD), head_out),
                   pl.BlockSpec((1, N_HEADS, ts, QK_PAD), head_out),
                   pl.BlockSpec((1, N_HEADS, ts, V_DIM), head_out)],
        out_shape=[jax.ShapeDtypeStruct((B, N_HEADS, S, QK_PAD), BF16),
                   jax.ShapeDtypeStruct((B, N_HEADS, S, QK_PAD), BF16),
                   jax.ShapeDtypeStruct((B, N_HEADS, S, V_DIM), BF16)],
        compiler_params=_params(("parallel", "parallel")),
        name="mla_proj",
    )(z, ct, st, q_norm[None, :], kv_norm[None, :], wq, wkv)


def _attn_kernel(q_ref, k_ref, v_ref, o_ref, s_a, s_b, *, tk):
    i = pl.program_id(2)
    top_rows = pl.ds(0, tk)
    bot_rows = pl.ds(tk, tk)

    def kv_rows(u):
        return pl.ds(pl.multiple_of(u * tk, tk), tk)

    def scores(u):
        return _dot_nt(q_ref[0, 0], k_ref[0, 0, kv_rows(u), :])

    def update(state, s, v, masked):
        m, l, acc = state
        if masked:
            row = lax.broadcasted_iota(jnp.int32, (tk, tk), 0)
            col = lax.broadcasted_iota(jnp.int32, (tk, tk), 1)
            s = jnp.where(col <= row, s, -jnp.inf)
        m_new = jnp.maximum(m, jnp.max(_fold_lanes(s, jnp.maximum), axis=-1, keepdims=True))
        alpha = jnp.exp2(m - m_new)
        p = jnp.exp2(s - m_new)
        l = alpha * l + jnp.sum(_fold_lanes(p, jnp.add), axis=-1, keepdims=True)
        acc = alpha * acc + _dot(p.astype(BF16), v)
        return m_new, l, acc

    def both(top, bot, s_ref, u):
        v = v_ref[0, 0, kv_rows(u), :]
        return (update(top, s_ref[top_rows, :], v, False),
                update(bot, s_ref[bot_rows, :], v, False))

    s_a[...] = scores(0)

    def body(t, state):
        top, bot = state
        s_b[...] = scores(2 * t + 1)
        top, bot = both(top, bot, s_a, 2 * t)
        s_a[...] = scores(2 * t + 2)
        top, bot = both(top, bot, s_b, 2 * t + 1)
        return top, bot

    init = (jnp.full((tk, 1), -jnp.inf, F32), jnp.zeros((tk, 1), F32),
            jnp.zeros((tk, V_DIM), F32))
    top, bot = lax.fori_loop(0, i, body, (init, init))
    s_b[bot_rows, :] = _dot_nt(q_ref[0, 0, bot_rows, :], k_ref[0, 0, kv_rows(2 * i + 1), :])
    v0 = v_ref[0, 0, kv_rows(2 * i), :]
    top = update(top, s_a[top_rows, :], v0, True)
    bot = update(bot, s_a[bot_rows, :], v0, False)
    bot = update(bot, s_b[bot_rows, :], v_ref[0, 0, kv_rows(2 * i + 1), :], True)
    o_ref[0, top_rows, :] = (top[2] / top[1]).astype(BF16)
    o_ref[0, bot_rows, :] = (bot[2] / bot[1]).astype(BF16)


def _attention(q, k, v):
    B, H, S, _ = q.shape
    tq = min(S, 1024)
    tk = tq // 2
    return pl.pallas_call(
        functools.partial(_attn_kernel, tk=tk),
        grid=(B, H, S // tq),
        in_specs=[pl.BlockSpec((1, 1, tq, QK_PAD), lambda b, h, i: (b, h, i, 0)),
                  pl.BlockSpec((1, 1, S, QK_PAD), lambda b, h, i: (b, h, 0, 0)),
                  pl.BlockSpec((1, 1, S, V_DIM), lambda b, h, i: (b, h, 0, 0))],
        out_specs=pl.BlockSpec((1, tq, V_DIM), lambda b, h, i: (b, i, h)),
        out_shape=jax.ShapeDtypeStruct((B, S, H * V_DIM), BF16),
        scratch_shapes=[pltpu.VMEM((tq, tk), F32), pltpu.VMEM((tq, tk), F32)],
        compiler_params=_params(("parallel", "parallel", "arbitrary")),
        name="mla_attention",
    )(q, k, v)


def _hgrn_consts(C):
    levels = [C >> (i + 1) for i in range(int(math.log2(C)))]
    t = np.arange(C)
    u = np.arange(C)
    mats = [(u[None, :] <= t[:, None]),
            (u[None, :] > t[:, None])]
    pair_masks = [np.eye(C, dtype=bool)]
    for m in levels:
        ref = (t // (2 * m)) * (2 * m) + m - 1
        hi = ((t // m) % 2) == 1
        rng_hi = (u[None, :] > ref[:, None]) & (u[None, :] <= t[:, None])
        rng_lo = (u[None, :] > t[:, None]) & (u[None, :] <= ref[:, None])
        mats.append(np.where(hi[:, None], rng_hi, rng_lo))
        same = (t[:, None] // (2 * m)) == (t[None, :] // (2 * m))
        pair_masks.append(hi[:, None] & (~hi[None, :]) & same)
    stack = np.concatenate(mats, axis=0).astype(np.float32)
    sums = jnp.asarray(np.concatenate([stack] * 3, axis=1), dtype=BF16)
    return sums, jnp.asarray(np.stack(pair_masks).astype(np.float32))


def _hgrn_kernel(zq_ref, zf_ref, zi_ref, zg_ref, lbl_ref, on_ref,
                 sums_ref, pm_ref, o_ref, st_scr, *, layer, ts):
    C = HGRN_CHUNK
    n_levels = pm_ref.shape[0] - 1

    @pl.when(pl.program_id(2) == 0)
    def _():
        st_scr[...] = jnp.zeros_like(st_scr)

    lg = lbl_ref[...]
    ex = jnp.exp(lg - jnp.max(lg, axis=0, keepdims=True))
    pr = ex / jnp.sum(ex, axis=0, keepdims=True)
    cum = pr[0:1, :]
    for r in range(1, layer + 1):
        cum = cum + pr[r:r + 1, :]
    lb = cum - pr[0:1, :]
    lb_floor_all = jnp.maximum(lb, LB_FLOOR)
    one_m_lb_all = 1.0 - lb
    gain = on_ref[...]

    for c, hh in [(c, hh) for c in range(ts // C) for hh in range(HGRN_HEADS_PER_STEP)]:
        rows = pl.ds(c * C, C)
        lanes = pl.ds(hh * LANES, LANES)
        lb_floor = lb_floor_all[:, hh * LANES:(hh + 1) * LANES]
        one_m_lb = one_m_lb_all[:, hh * LANES:(hh + 1) * LANES]
        q_raw = zq_ref[rows, lanes]
        zf = zf_ref[rows, lanes]
        v = zi_ref[rows, lanes].astype(BF16)
        g_raw = zg_ref[rows, lanes]
        q = q_raw * _sigmoid(q_raw)
        e = jnp.exp(-jnp.abs(zf))
        r = 1.0 / (1.0 + e)
        er = e * r
        pos = zf >= 0
        log_f = jnp.log(lb_floor + one_m_lb * jnp.where(pos, r, er))
        kk = one_m_lb * jnp.where(pos, er, r)
        p0 = log_f.astype(BF16)
        r1 = log_f - p0.astype(F32)
        p1 = r1.astype(BF16)
        p2 = (r1 - p1.astype(F32)).astype(BF16)
        sums = _dot(sums_ref[...], jnp.concatenate([p0, p1, p2], axis=0))
        b = sums[0:C]
        b_rest = sums[C:2 * C]
        a = pm_ref[0] * _dot_nt(q.astype(BF16), kk.astype(BF16))
        for lv in range(n_levels):
            dec = jnp.exp(sums[(2 + lv) * C:(3 + lv) * C])
            qt = (q * dec).astype(BF16)
            kt = (kk * dec).astype(BF16)
            a = a + pm_ref[lv + 1] * _dot_nt(qt, kt)
        st = st_scr[hh]
        o = _dot(a.astype(BF16), v) + _dot_nt((q * jnp.exp(b)).astype(BF16), st.astype(BF16))
        k_end = (kk * jnp.exp(b_rest)).astype(BF16)
        st_scr[hh] = st * jnp.exp(b[C - 1:C, :]) + _dot_tn(v, k_end)
        o_ref[rows, lanes] = (_rms(o, gain) * (g_raw * _sigmoid(g_raw))).astype(BF16)


def _hgrn(z, lb_logits, out_norm, consts, layer, B, S):
    T = B * S
    ts = min(S, 256)
    ns = S // ts
    sums, pair_masks = consts
    hp = HGRN_HEADS_PER_STEP
    width = hp * LANES
    col0 = HGRN_COL0 // width
    groups = N_HEADS // hp

    def zspec(part):
        return pl.BlockSpec((ts, width),
                            lambda b, h, i: (b * ns + i, col0 + part * groups + h))

    full = lambda arr: pl.BlockSpec(arr.shape, lambda b, h, i: (0,) * arr.ndim)
    kern = functools.partial(_hgrn_kernel, layer=layer, ts=ts)
    return pl.pallas_call(
        kern,
        grid=(B, groups, ns),
        in_specs=[zspec(0), zspec(1), zspec(2), zspec(3),
                  pl.BlockSpec((DEPTH, width), lambda b, h, i: (0, h)),
                  pl.BlockSpec((1, HGRN_V), lambda b, h, i: (0, 0)),
                  full(sums), full(pair_masks)],
        out_specs=pl.BlockSpec((ts, width), lambda b, h, i: (b * ns + i, h)),
        out_shape=jax.ShapeDtypeStruct((T, GROUP_WIDTH), BF16),
        scratch_shapes=[pltpu.VMEM((hp, HGRN_V, HGRN_K), F32)],
        compiler_params=_params(("parallel", "parallel", "arbitrary")),
        name="hgrn2",
    )(z, z, z, z, lb_logits, out_norm[None, :], sums, pair_masks)


def _out_proj_kernel(om_ref, oh_ref, h_ref, w_ref, g1_ref, g2_ref, h1_ref, c_ref):
    y = _dot(om_ref[...], w_ref[0:GROUP_WIDTH, :]) + _dot(oh_ref[...], w_ref[GROUP_WIDTH:, :])
    h1 = h_ref[...] + _rms(y, g1_ref[...])
    h1_ref[...] = h1
    c_ref[...] = _rms(h1, g2_ref[...]).astype(c_ref.dtype)


def _out_proj(om, oh, h, w, g1, g2, c_dtype):
    T = h.shape[0]
    tm = min(T, 256)
    row = lambda i: (i, 0)
    const = lambda i: (0, 0)
    return pl.pallas_call(
        _out_proj_kernel,
        grid=(T // tm,),
        in_specs=[pl.BlockSpec((tm, GROUP_WIDTH), row),
                  pl.BlockSpec((tm, GROUP_WIDTH), row),
                  pl.BlockSpec((tm, D_MODEL), row),
                  pl.BlockSpec((D_MODEL, D_MODEL), const),
                  pl.BlockSpec((1, D_MODEL), const),
                  pl.BlockSpec((1, D_MODEL), const)],
        out_specs=[pl.BlockSpec((tm, D_MODEL), row), pl.BlockSpec((tm, D_MODEL), row)],
        out_shape=[jax.ShapeDtypeStruct((T, D_MODEL), F32),
                   jax.ShapeDtypeStruct((T, D_MODEL), c_dtype)],
        compiler_params=_params(("parallel",)),
        name="out_proj",
    )(om, oh, h, w, g1[None, :], g2[None, :])


def _gate_up_kernel(te_ref, nv_ref, x_ref, wg_ref, wu_ref, o_ref, wg_bf, wu_bf):
    i = pl.program_id(1)
    valid = i < nv_ref[0]
    new_weights = jnp.logical_or(i == 0, te_ref[i] != te_ref[jnp.maximum(i - 1, 0)])

    @pl.when(jnp.logical_and(valid, new_weights))
    def _():
        wg_bf[...] = wg_ref[0].astype(BF16)
        wu_bf[...] = wu_ref[0].astype(BF16)

    @pl.when(valid)
    def _():
        x = x_ref[...].astype(BF16)
        g = _dot(x, wg_bf[...])
        u = _dot(x, wu_bf[...])
        o_ref[...] = (g * _sigmoid(g) * u).astype(BF16)

    @pl.when(jnp.logical_not(valid))
    def _():
        o_ref[...] = jnp.zeros_like(o_ref)


def _gate_up(x, w_gu, tile_expert, n_valid, tm):
    R = x.shape[0]
    n_tiles = R // tm
    tf = 512
    nf = FFN_DIM // tf

    def tile(i, nv):
        return jnp.minimum(i, nv[0] - 1)

    return pl.pallas_call(
        _gate_up_kernel,
        grid_spec=pltpu.PrefetchScalarGridSpec(
            num_scalar_prefetch=2,
            grid=(nf, n_tiles),
            in_specs=[pl.BlockSpec((tm, D_MODEL), lambda j, i, te, nv: (tile(i, nv), 0)),
                      pl.BlockSpec((1, D_MODEL, tf), lambda j, i, te, nv: (te[tile(i, nv)], 0, j)),
                      pl.BlockSpec((1, D_MODEL, tf),
                                   lambda j, i, te, nv: (te[tile(i, nv)], 0, j + nf))],
            out_specs=pl.BlockSpec((tm, tf), lambda j, i, te, nv: (i, j)),
            scratch_shapes=[pltpu.VMEM((D_MODEL, tf), BF16), pltpu.VMEM((D_MODEL, tf), BF16)],
        ),
        out_shape=jax.ShapeDtypeStruct((R, FFN_DIM), BF16),
        compiler_params=_params(("arbitrary", "arbitrary")),
        name="ffn_gate_up",
    )(tile_expert, n_valid, x, w_gu, w_gu)


def _down_kernel(te_ref, nv_ref, a_ref, w_ref, o_ref):
    valid = pl.program_id(1) < nv_ref[0]

    @pl.when(valid)
    def _():
        o_ref[...] = _dot(a_ref[...], w_ref[0])

    @pl.when(jnp.logical_not(valid))
    def _():
        o_ref[...] = jnp.zeros_like(o_ref)


def _down(act, w_down, tile_expert, n_valid, tm):
    R = act.shape[0]
    n_tiles = R // tm
    tn = 512

    def tile(i, nv):
        return jnp.minimum(i, nv[0] - 1)

    return pl.pallas_call(
        _down_kernel,
        grid_spec=pltpu.PrefetchScalarGridSpec(
            num_scalar_prefetch=2,
            grid=(D_MODEL // tn, n_tiles),
            in_specs=[pl.BlockSpec((tm, FFN_DIM), lambda j, i, te, nv: (tile(i, nv), 0)),
                      pl.BlockSpec((1, FFN_DIM, tn), lambda j, i, te, nv: (te[tile(i, nv)], 0, j))],
            out_specs=pl.BlockSpec((tm, tn), lambda j, i, te, nv: (i, j)),
        ),
        out_shape=jax.ShapeDtypeStruct((R, D_MODEL), F32),
        compiler_params=_params(("arbitrary", "arbitrary")),
        name="ffn_down",
    )(tile_expert, n_valid, act, w_down)


def _ffn_epilogue(f, h1, p, wg, wp, g3, gp):
    h2 = h1 + _rms(f, g3)
    gate = _sigmoid(_dot(h2.astype(BF16), wg))
    proj = _dot(p.astype(BF16), wp)
    return h2 + _rms(gate * proj, gp)


def _post_ffn_kernel(f_ref, h1_ref, p_ref, wg_ref, wp_ref, g3_ref, gp_ref, o_ref):
    o_ref[...] = _ffn_epilogue(f_ref[...], h1_ref[...], p_ref[...], wg_ref[...], wp_ref[...],
                               g3_ref[...], gp_ref[...])


def _post_ffn(f, h1, p, wg, wp, g3, gp):
    T = h1.shape[0]
    tm = min(T, 256)
    row = lambda i: (i, 0)
    const = lambda i: (0, 0)
    return pl.pallas_call(
        _post_ffn_kernel,
        grid=(T // tm,),
        in_specs=[pl.BlockSpec((tm, D_MODEL), row),
                  pl.BlockSpec((tm, D_MODEL), row),
                  pl.BlockSpec((tm, PLE_DIM), row),
                  pl.BlockSpec((D_MODEL, D_MODEL), const),
                  pl.BlockSpec((PLE_DIM, D_MODEL), const),
                  pl.BlockSpec((1, D_MODEL), const),
                  pl.BlockSpec((1, D_MODEL), const)],
        out_specs=pl.BlockSpec((tm, D_MODEL), row),
        out_shape=jax.ShapeDtypeStruct((T, D_MODEL), F32),
        compiler_params=_params(("parallel",)),
        name="post_ffn",
    )(f, h1, p, wg, wp, g3[None, :], gp[None, :])


def _router_kernel(c_ref, wr_ref, tri_ref, route_ref, cnt_ref, carry):
    @pl.when(pl.program_id(0) == 0)
    def _():
        carry[...] = jnp.zeros_like(carry)

    logits = _dot(c_ref[...].astype(BF16), wr_ref[...])
    lane = lax.broadcasted_iota(jnp.int32, logits.shape, 1)
    lg = jnp.where(lane < N_EXPERTS, logits, -jnp.inf)
    m1 = jnp.max(lg, axis=-1, keepdims=True)
    i1 = jnp.min(jnp.where(lg == m1, lane, LANES), axis=-1, keepdims=True)
    lg2 = jnp.where(lane == i1, -jnp.inf, lg)
    m2 = jnp.max(lg2, axis=-1, keepdims=True)
    i2 = jnp.min(jnp.where(lg2 == m2, lane, LANES), axis=-1, keepdims=True)
    e = jnp.exp(m2 - m1)
    g1 = 1.0 / (1.0 + e)
    g2 = e / (1.0 + e)
    onehot = jnp.where((lane == i1) | (lane == i2), 1.0, 0.0)
    before = _dot(tri_ref[...], onehot.astype(BF16)) + carry[0:1, :]
    rank1 = jnp.sum(jnp.where(lane == i1, before, 0.0), axis=-1, keepdims=True)
    rank2 = jnp.sum(jnp.where(lane == i2, before, 0.0), axis=-1, keepdims=True)
    total = carry[0:1, :] + jnp.sum(onehot, axis=0, keepdims=True)
    carry[...] = jnp.broadcast_to(total, carry.shape)
    cnt_ref[...] = jnp.broadcast_to(total, cnt_ref.shape)
    out = jnp.where(lane == 0, i1.astype(F32), 0.0)
    out = jnp.where(lane == 1, i2.astype(F32), out)
    out = jnp.where(lane == 2, g1, out)
    out = jnp.where(lane == 3, g2, out)
    out = jnp.where(lane == 4, rank1, out)
    out = jnp.where(lane == 5, rank2, out)
    route_ref[...] = out


def _router(c, w_router):
    T = c.shape[0]
    tr = min(T, 512)
    wr = jnp.zeros((D_MODEL, LANES), BF16).at[:, :N_EXPERTS].set(w_router.astype(BF16))
    tri = jnp.asarray(np.tril(np.ones((tr, tr), np.float32), -1), dtype=BF16)
    return pl.pallas_call(
        _router_kernel,
        grid=(T // tr,),
        in_specs=[pl.BlockSpec((tr, D_MODEL), lambda i: (i, 0)),
                  pl.BlockSpec((D_MODEL, LANES), lambda i: (0, 0)),
                  pl.BlockSpec((tr, tr), lambda i: (0, 0))],
        out_specs=[pl.BlockSpec((tr, LANES), lambda i: (i, 0)),
                   pl.BlockSpec((8, LANES), lambda i: (0, 0))],
        out_shape=[jax.ShapeDtypeStruct((T, LANES), F32),
                   jax.ShapeDtypeStruct((8, LANES), F32)],
        scratch_shapes=[pltpu.VMEM((8, LANES), F32)],
        compiler_params=_params(("arbitrary",)),
        name="moe_router",
    )(c, wr, tri)


def _dispatch_kernel(p1_ref, p2_ref, pad_ref, nv_ref, c_ref, xs_ref, zbuf, sem, zsem, *,
                     tm, seg_tile):
    i = pl.program_id(0)
    base = i * tm
    zr = zbuf.shape[0]

    @pl.when(i == 0)
    def _():
        zbuf[...] = jnp.zeros_like(zbuf)

        def zero_tile(start):
            first = pl.multiple_of(start, zr)
            copies = [pltpu.make_async_copy(zbuf, xs_ref.at[pl.ds(first + k * zr, zr)], zsem)
                      for k in range(seg_tile // zr)]
            for cp in copies:
                cp.start()
            for cp in copies:
                cp.wait()

        def zero_padding(e, _):
            @pl.when(pad_ref[e] >= 0)
            def _():
                zero_tile(pad_ref[e])
            return 0

        def zero_unused(t, _):
            zero_tile(t * seg_tile)
            return 0

        lax.fori_loop(0, N_EXPERTS, zero_padding, 0)
        lax.fori_loop(nv_ref[0], xs_ref.shape[0] // seg_tile, zero_unused, 0)

    def row_copy(r, dst):
        return pltpu.make_async_copy(c_ref.at[pl.ds(r, 1)], xs_ref.at[pl.ds(dst, 1)], sem)

    def issue(g, _):
        for u in range(ROW_UNROLL):
            r = g * ROW_UNROLL + u
            row_copy(r, p1_ref[base + r]).start(priority=0)
            row_copy(r, p2_ref[base + r]).start(priority=1)
        return 0

    lax.fori_loop(0, tm // ROW_UNROLL, issue, 0)

    def drain(g, _):
        for _u in range(2 * ROW_UNROLL):
            row_copy(0, 0).wait()
        return 0

    lax.fori_loop(0, tm // ROW_UNROLL, drain, 0)


def _dispatch(c, pos1, pos2, pad_tile_row, n_valid, n_rows, seg_tile):
    T = c.shape[0]
    tm = min(T, 256)
    return pl.pallas_call(
        functools.partial(_dispatch_kernel, tm=tm, seg_tile=seg_tile),
        grid_spec=pltpu.PrefetchScalarGridSpec(
            num_scalar_prefetch=4,
            grid=(T // tm,),
            in_specs=[pl.BlockSpec((tm, D_MODEL), lambda i, p1, p2, pt, nv: (i, 0))],
            out_specs=pl.BlockSpec(memory_space=pl.ANY),
            scratch_shapes=[pltpu.VMEM((min(seg_tile, 64), D_MODEL), c.dtype),
                            pltpu.SemaphoreType.DMA(()),
                            pltpu.SemaphoreType.DMA(())],
        ),
        out_shape=jax.ShapeDtypeStruct((n_rows, D_MODEL), c.dtype),
        compiler_params=_params(("arbitrary",)),
        name="moe_dispatch",
    )(pos1, pos2, pad_tile_row, n_valid, c)


def _combine_kernel(p1_ref, p2_ref, ys_ref, route_ref, h1_ref, p_ref, wg_ref, wp_ref,
                    g3_ref, gp_ref, o_ref, buf, sems, *, tm):
    i = pl.program_id(0)
    slot = i % 2

    def row_copy(src, s, k, r):
        return pltpu.make_async_copy(ys_ref.at[pl.ds(src, 1)], buf.at[s, k, pl.ds(r, 1)],
                                     sems.at[s])

    def gather_tile(tile, s):
        base = tile * tm

        def issue(g, _):
            for u in range(ROW_UNROLL):
                r = g * ROW_UNROLL + u
                row_copy(p1_ref[base + r], s, 0, r).start(priority=0)
                row_copy(p2_ref[base + r], s, 1, r).start(priority=1)
            return 0

        lax.fori_loop(0, tm // ROW_UNROLL, issue, 0)

    @pl.when(i == 0)
    def _():
        gather_tile(0, 0)

    @pl.when(i + 1 < pl.num_programs(0))
    def _():
        gather_tile(i + 1, 1 - slot)

    def drain(g, _):
        for _u in range(2 * ROW_UNROLL):
            row_copy(0, slot, 0, 0).wait()
        return 0

    lax.fori_loop(0, tm // ROW_UNROLL, drain, 0)
    route = route_ref[...]
    f = route[:, 2:3] * buf[slot, 0] + route[:, 3:4] * buf[slot, 1]
    o_ref[...] = _ffn_epilogue(f, h1_ref[...], p_ref[...], wg_ref[...], wp_ref[...],
                               g3_ref[...], gp_ref[...])


def _combine(ys, pos1, pos2, route, h1, p, wg, wp, g3, gp):
    T = h1.shape[0]
    tm = min(T, 256)
    row = lambda i, p1, p2: (i, 0)
    const = lambda i, p1, p2: (0, 0)
    return pl.pallas_call(
        functools.partial(_combine_kernel, tm=tm),
        grid_spec=pltpu.PrefetchScalarGridSpec(
            num_scalar_prefetch=2,
            grid=(T // tm,),
            in_specs=[pl.BlockSpec(memory_space=pl.ANY),
                      pl.BlockSpec((tm, LANES), row),
                      pl.BlockSpec((tm, D_MODEL), row),
                      pl.BlockSpec((tm, PLE_DIM), row),
                      pl.BlockSpec((D_MODEL, D_MODEL), const),
                      pl.BlockSpec((PLE_DIM, D_MODEL), const),
                      pl.BlockSpec((1, D_MODEL), const),
                      pl.BlockSpec((1, D_MODEL), const)],
            out_specs=pl.BlockSpec((tm, D_MODEL), row),
            scratch_shapes=[pltpu.VMEM((2, 2, tm, D_MODEL), F32),
                            pltpu.SemaphoreType.DMA((2,))],
        ),
        out_shape=jax.ShapeDtypeStruct((T, D_MODEL), F32),
        compiler_params=_params(("arbitrary",)),
        name="moe_combine",
    )(pos1, pos2, ys, route, h1, p, wg, wp, g3[None, :], gp[None, :])


def _swap_halves(w):
    half = w.shape[-1] // 2
    return jnp.concatenate([w[..., half:], w[..., :half]], axis=-1)


def _prep_w_in(w):
    k_rope = w[:, 640:704]
    pad64 = jnp.zeros((D_MODEL, 64), w.dtype)
    cols = [w[:, :640], k_rope, pad64, _swap_halves(k_rope), pad64,
            jnp.zeros((D_MODEL, HGRN_COL0 - 896), w.dtype), w[:, 704:]]
    return jnp.concatenate(cols, axis=1).astype(BF16)


def _prep_w_uq(w):
    w = w.reshape(Q_RANK, N_HEADS, QK_DIM)
    nope = w[:, :, :NOPE_DIM]
    pe = w[:, :, NOPE_DIM:]
    pad = jnp.zeros((Q_RANK, N_HEADS, LANES - ROPE_DIM), w.dtype)
    pe_pad = jnp.concatenate([pe, pad], axis=-1)
    pe_swap = jnp.concatenate([_swap_halves(pe), pad], axis=-1)
    parts = [x.reshape(Q_RANK, GROUP_WIDTH) for x in (nope, pe_pad, pe_swap)]
    return jnp.concatenate(parts, axis=1).astype(BF16)


def _prep_w_ukv(w):
    w = w.reshape(KV_RANK, N_HEADS, NOPE_DIM + V_DIM)
    k = w[:, :, :NOPE_DIM].reshape(KV_RANK, GROUP_WIDTH)
    v = w[:, :, NOPE_DIM:].reshape(KV_RANK, GROUP_WIDTH)
    return jnp.concatenate([k, v], axis=1).astype(BF16)


def _moe_plan(route, counts, tm, n_tiles):
    e1 = route[:, 0].astype(jnp.int32)
    e2 = route[:, 1].astype(jnp.int32)
    rank1 = route[:, 4].astype(jnp.int32)
    rank2 = route[:, 5].astype(jnp.int32)
    cnt = counts[0, :N_EXPERTS].astype(jnp.int32)
    padded = ((cnt + tm - 1) // tm) * tm
    seg_end = jnp.cumsum(padded)
    seg_start = seg_end - padded
    pos1 = seg_start[e1] + rank1
    pos2 = seg_start[e2] + rank2
    tile_start = jnp.arange(n_tiles, dtype=jnp.int32) * tm
    tile_expert = jnp.sum(tile_start[:, None] >= seg_end[None, :], axis=1).astype(jnp.int32)
    tile_expert = jnp.minimum(tile_expert, N_EXPERTS - 1)
    n_valid = (seg_end[-1] // tm).astype(jnp.int32)[None]
    pad_tile_row = jnp.where(padded > 0, seg_end - tm, -1).astype(jnp.int32)
    return pos1, pos2, tile_expert, n_valid, pad_tile_row


def kernel(x, p, positions, sandwich_norms, w_in, mla_q_norm, mla_kv_norm, w_uq, w_ukv,
           hgrn_lb_logits, hgrn_out_norm, w_out, ffn_w_gu, ffn_w_down, moe_w_router,
           moe_w_gu, moe_w_down, ple_w_proj, ple_w_gate, ple_norm):
    B, S, _ = x.shape
    T = B * S
    h = x.reshape(T, D_MODEL)
    ct, st = _rope_tables(positions)
    hgrn_consts = _hgrn_consts(HGRN_CHUNK)
    tm = min(T, 512)
    for l in range(DEPTH):
        z = _norm_matmul(h, sandwich_norms[l, 0], _prep_w_in(w_in[l]))
        q, k, v = _mla_proj(z, ct, st, mla_q_norm[l], mla_kv_norm[l],
                            _prep_w_uq(w_uq[l]), _prep_w_ukv(w_ukv[l]), B, S)
        o_mla = _attention(q, k, v).reshape(T, GROUP_WIDTH)
        o_hgrn = _hgrn(z, hgrn_lb_logits, hgrn_out_norm[l], hgrn_consts, l, B, S)
        moe = l % 2 == 1
        h1, c = _out_proj(o_mla, o_hgrn, h, w_out[l].astype(BF16),
                          sandwich_norms[l, 1], sandwich_norms[l, 2], F32 if moe else BF16)
        wg = ple_w_gate[l].astype(BF16)
        wp = ple_w_proj[l].astype(BF16)
        p_l = p[l].reshape(T, PLE_DIM)
        if not moe:
            def one_expert(rows):
                return jnp.zeros((T // rows,), jnp.int32), jnp.full((1,), T // rows, jnp.int32)

            tm_dense = min(T, 1024)
            act = _gate_up(c, ffn_w_gu[l // 2][None], *one_expert(tm_dense), tm_dense)
            f = _down(act, ffn_w_down[l // 2][None].astype(BF16), *one_expert(tm), tm)
            h = _post_ffn(f, h1, p_l, wg, wp, sandwich_norms[l, 3], ple_norm[l])
        else:
            n_tiles = (2 * T) // tm + N_EXPERTS
            route, counts = _router(c, moe_w_router[l // 2])
            pos1, pos2, tile_expert, n_valid, pad_tile_row = _moe_plan(route, counts, tm, n_tiles)
            xs = _dispatch(c, pos1, pos2, pad_tile_row, n_valid, n_tiles * tm, tm)
            act = _gate_up(xs, moe_w_gu[l // 2], tile_expert, n_valid, tm)
            ys = _down(act, moe_w_down[l // 2].astype(BF16), tile_expert, n_valid, tm)
            h = _combine(ys, pos1, pos2, route, h1, p_l, wg, wp,
                         sandwich_norms[l, 3], ple_norm[l])
    return h.reshape(B, S, D_MODEL)
```

```python
import functools
import math

import numpy as np
import jax
import jax.numpy as jnp
from jax import lax
from jax.experimental import pallas as pl
from jax.experimental.pallas import tpu as pltpu

F32 = jnp.float32
BF16 = jnp.bfloat16

D_MODEL = 2048
DEPTH = 2
N_HEADS = 8
NOPE_DIM = 128
ROPE_DIM = 64
V_DIM = 128
QK_DIM = NOPE_DIM + ROPE_DIM
Q_RANK = 384
KV_RANK = 256
HGRN_K = 128
HGRN_V = 128
GROUP_WIDTH = N_HEADS * 128
FFN_DIM = 7168
N_EXPERTS = 8
PLE_DIM = 256
ROPE_THETA = 10000.0
LB_FLOOR = 1e-30
EPS = 1e-6

LANES = 128
QK_PAD = 256
Q_SCALE = QK_DIM ** -0.5 * math.log2(math.e)
Z_MLA = 1024
Z_WIDTH = 5120
HGRN_COL0 = 1024
HGRN_CHUNK = 64
HGRN_HEADS_PER_STEP = 4
ROW_UNROLL = 8
GU_SUB = 512
IN_PROJ_SUB = 1024
VMEM_LIMIT = 56 * 1024 * 1024


def _params(semantics, vmem=VMEM_LIMIT):
    return pltpu.CompilerParams(dimension_semantics=semantics, vmem_limit_bytes=vmem)


def _rms(x, gain_row):
    ms = jnp.mean(x * x, axis=-1, keepdims=True)
    return x * lax.rsqrt(ms + EPS) * gain_row


def _dot(a, b):
    return jnp.dot(a, b, preferred_element_type=F32)


def _dot_nt(a, b):
    return lax.dot_general(a, b, (((1,), (1,)), ((), ())), preferred_element_type=F32)


def _dot_tn(a, b):
    return lax.dot_general(a, b, (((0,), (0,)), ((), ())), preferred_element_type=F32)


def _sigmoid(x):
    return 1.0 / (1.0 + jnp.exp(-x))


def _pack_bf16_pairs(x):
    n = x.shape[1] // 2
    bits = pltpu.bitcast(x.astype(BF16).astype(F32), jnp.uint32)
    return (bits[:, :n] >> 16) | (bits[:, n:] & jnp.uint32(0xFFFF0000))


def _unpack_bf16_pairs(w):
    lo = pltpu.bitcast(w << 16, F32)
    hi = pltpu.bitcast(w & jnp.uint32(0xFFFF0000), F32)
    return jnp.concatenate([lo, hi], axis=1).astype(BF16)


def _fold_lanes(x, op):
    parts = [x[:, c:c + LANES] for c in range(0, x.shape[1], LANES)]
    while len(parts) > 1:
        parts = [op(a, b) for a, b in zip(parts[0::2], parts[1::2])] + parts[len(parts) & ~1:]
    return parts[0]


def _rope_kernel(pos_ref, inv_ref, sgn_ref, ct_ref, st_ref):
    ang = pos_ref[...].astype(F32) * inv_ref[...]
    keep = jnp.abs(sgn_ref[...])
    ct_ref[...] = jnp.cos(ang) * keep
    st_ref[...] = jnp.sin(ang) * sgn_ref[...]


def _rope_tables(positions):
    T = positions.size
    tt = min(T, 1024)
    half = ROPE_DIM // 2
    inv_freq = 1.0 / (ROPE_THETA ** (jnp.arange(0, ROPE_DIM, 2, dtype=F32) / ROPE_DIM))
    inv_row = jnp.concatenate([inv_freq, inv_freq, jnp.zeros((LANES - ROPE_DIM,), F32)])[None, :]
    sgn = np.zeros((1, LANES), np.float32)
    sgn[0, :half] = -1.0
    sgn[0, half:ROPE_DIM] = 1.0
    return pl.pallas_call(
        _rope_kernel,
        grid=(T // tt,),
        in_specs=[pl.BlockSpec((tt, 1), lambda i: (i, 0)),
                  pl.BlockSpec((1, LANES), lambda i: (0, 0)),
                  pl.BlockSpec((1, LANES), lambda i: (0, 0))],
        out_specs=[pl.BlockSpec((tt, LANES), lambda i: (i, 0))] * 2,
        out_shape=[jax.ShapeDtypeStruct((T, LANES), F32)] * 2,
        compiler_params=_params(("parallel",)),
        name="rope_tables",
    )(positions.reshape(T, 1), inv_row, jnp.asarray(sgn))


def _norm_matmul_kernel(x_ref, g_ref, w_ref, o_ref):
    a = _rms(x_ref[...], g_ref[...]).astype(BF16)
    for c in range(0, o_ref.shape[1], IN_PROJ_SUB):
        o_ref[:, c:c + IN_PROJ_SUB] = _dot(a, w_ref[:, c:c + IN_PROJ_SUB])


def _norm_matmul(x, gain, w):
    T, K = x.shape
    N = w.shape[1]
    tm = min(T, 256)
    return pl.pallas_call(
        _norm_matmul_kernel,
        grid=(T // tm,),
        in_specs=[pl.BlockSpec((tm, K), lambda i: (i, 0)),
                  pl.BlockSpec((1, K), lambda i: (0, 0)),
                  pl.BlockSpec((K, N), lambda i: (0, 0), pipeline_mode=pl.Buffered(1))],
        out_specs=pl.BlockSpec((tm, N), lambda i: (i, 0)),
        out_shape=jax.ShapeDtypeStruct((T, N), F32),
        compiler_params=_params(("parallel",)),
        name="in_proj",
    )(x, gain[None, :], w)


def _mla_proj_kernel(z_ref, ct_ref, st_ref, qn_ref, kvn_ref, wq_ref, wkv_ref,
                     q_ref, k_ref, v_ref):
    z = z_ref[...]
    ct = ct_ref[...]
    st = st_ref[...]
    aq = _rms(z[:, :Q_RANK], qn_ref[...]).astype(BF16)
    akv = _rms(z[:, Q_RANK:Q_RANK + KV_RANK], kvn_ref[...]).astype(BF16)
    k_a = z[:, 640:768]
    k_b = z[:, 768:896]
    k_pe = (k_a * ct + k_b * st).astype(BF16)
    q = _dot(aq, wq_ref[...])
    kv = _dot(akv, wkv_ref[...])
    for h in range(N_HEADS):
        lo, hi = h * LANES, (h + 1) * LANES
        q_ref[0, h, :, 0:LANES] = (q[:, lo:hi] * Q_SCALE).astype(BF16)
        q_pe = q[:, GROUP_WIDTH + lo:GROUP_WIDTH + hi] * ct \
            + q[:, 2 * GROUP_WIDTH + lo:2 * GROUP_WIDTH + hi] * st
        q_ref[0, h, :, LANES:QK_PAD] = (q_pe * Q_SCALE).astype(BF16)
        k_ref[0, h, :, 0:LANES] = kv[:, lo:hi].astype(BF16)
        k_ref[0, h, :, LANES:QK_PAD] = k_pe
        v_ref[0, h, :, :] = kv[:, GROUP_WIDTH + lo:GROUP_WIDTH + hi].astype(BF16)


def _mla_proj(z, ct, st, q_norm, kv_norm, wq, wkv, B, S):
    ts = min(S, 512)
    ns = S // ts
    tok = lambda b, i: (b * ns + i, 0)
    const = lambda b, i: (0, 0)
    head_out = lambda b, i: (b, 0, i, 0)
    return pl.pallas_call(
        _mla_proj_kernel,
        grid=(B, ns),
        in_specs=[pl.BlockSpec((ts, Z_MLA), tok),
                  pl.BlockSpec((ts, LANES), tok),
                  pl.BlockSpec((ts, LANES), tok),
                  pl.BlockSpec((1, Q_RANK), const),
                  pl.BlockSpec((1, KV_RANK), const),
                  pl.BlockSpec(wq.shape, const),
                  pl.BlockSpec(wkv.shape, const)],
        out_specs=[pl.BlockSpec((1, N_HEADS, ts, QK_PAD), head_out),
                   pl.BlockSpec((1, N_HEADS, ts, QK_PAD), head_out),
                   pl.BlockSpec((1, N_HEADS, ts, V_DIM), head_out)],
        out_shape=[jax.ShapeDtypeStruct((B, N_HEADS, S, QK_PAD), BF16),
                   jax.ShapeDtypeStruct((B, N_HEADS, S, QK_PAD), BF16),
                   jax.ShapeDtypeStruct((B, N_HEADS, S, V_DIM), BF16)],
        compiler_params=_params(("parallel", "parallel")),
        name="mla_proj",
    )(z, ct, st, q_norm[None, :], kv_norm[None, :], wq, wkv)


def _attn_kernel(q_ref, k_ref, v_ref, o_ref, s_a, s_b, *, tk):
    i = pl.program_id(2)
    top_rows = pl.ds(0, tk)
    bot_rows = pl.ds(tk, tk)

    def kv_rows(u):
        return pl.ds(pl.multiple_of(u * tk, tk), tk)

    def scores(u):
        return _dot_nt(q_ref[0, 0], k_ref[0, 0, kv_rows(u), :])

    def update(state, s, v, masked):
        m, l, acc = state
        if masked:
            row = lax.broadcasted_iota(jnp.int32, (tk, tk), 0)
            col = lax.broadcasted_iota(jnp.int32, (tk, tk), 1)
            s = jnp.where(col <= row, s, -jnp.inf)
        m_new = jnp.maximum(m, jnp.max(_fold_lanes(s, jnp.maximum), axis=-1, keepdims=True))
        alpha = jnp.exp2(m - m_new)
        p = jnp.exp2(s - m_new)
        l = alpha * l + jnp.sum(_fold_lanes(p, jnp.add), axis=-1, keepdims=True)
        acc = alpha * acc + _dot(p.astype(BF16), v)
        return m_new, l, acc

    def both(top, bot, s_ref, u):
        v = v_ref[0, 0, kv_rows(u), :]
        return (update(top, s_ref[top_rows, :], v, False),
                update(bot, s_ref[bot_rows, :], v, False))

    s_a[...] = scores(0)

    def body(t, state):
        top, bot = state
        s_b[...] = scores(2 * t + 1)
        top, bot = both(top, bot, s_a, 2 * t)
        s_a[...] = scores(2 * t + 2)
        top, bot = both(top, bot, s_b, 2 * t + 1)
        return top, bot

    init = (jnp.full((tk, 1), -jnp.inf, F32), jnp.zeros((tk, 1), F32),
            jnp.zeros((tk, V_DIM), F32))
    top, bot = lax.fori_loop(0, i, body, (init, init))
    s_b[bot_rows, :] = _dot_nt(q_ref[0, 0, bot_rows, :], k_ref[0, 0, kv_rows(2 * i + 1), :])
    v0 = v_ref[0, 0, kv_rows(2 * i), :]
    top = update(top, s_a[top_rows, :], v0, True)
    bot = update(bot, s_a[bot_rows, :], v0, False)
    bot = update(bot, s_b[bot_rows, :], v_ref[0, 0, kv_rows(2 * i + 1), :], True)
    o_ref[0, top_rows, :] = (top[2] / top[1]).astype(BF16)
    o_ref[0, bot_rows, :] = (bot[2] / bot[1]).astype(BF16)


def _attention(q, k, v):
    B, H, S, _ = q.shape
    tq = min(S, 1024)
    tk = tq // 2
    return pl.pallas_call(
        functools.partial(_attn_kernel, tk=tk),
        grid=(B, H, S // tq),
        in_specs=[pl.BlockSpec((1, 1, tq, QK_PAD), lambda b, h, i: (b, h, i, 0)),
                  pl.BlockSpec((1, 1, S, QK_PAD), lambda b, h, i: (b, h, 0, 0)),
                  pl.BlockSpec((1, 1, S, V_DIM), lambda b, h, i: (b, h, 0, 0))],
        out_specs=pl.BlockSpec((1, tq, V_DIM), lambda b, h, i: (b, i, h)),
        out_shape=jax.ShapeDtypeStruct((B, S, H * V_DIM), BF16),
        scratch_shapes=[pltpu.VMEM((tq, tk), F32), pltpu.VMEM((tq, tk), F32)],
        compiler_params=_params(("parallel", "parallel", "arbitrary")),
        name="mla_attention",
    )(q, k, v)


def _hgrn_consts(C):
    levels = [C >> (i + 1) for i in range(int(math.log2(C)))]
    t = np.arange(C)
    u = np.arange(C)
    mats = [(u[None, :] <= t[:, None]),
            (u[None, :] > t[:, None])]
    pair_masks = [np.eye(C, dtype=bool)]
    for m in levels:
        ref = (t // (2 * m)) * (2 * m) + m - 1
        hi = ((t // m) % 2) == 1
        rng_hi = (u[None, :] > ref[:, None]) & (u[None, :] <= t[:, None])
        rng_lo = (u[None, :] > t[:, None]) & (u[None, :] <= ref[:, None])
        mats.append(np.where(hi[:, None], rng_hi, rng_lo))
        same = (t[:, None] // (2 * m)) == (t[None, :] // (2 * m))
        pair_masks.append(hi[:, None] & (~hi[None, :]) & same)
    stack = np.concatenate(mats, axis=0).astype(np.float32)
    sums = jnp.asarray(np.concatenate([stack] * 3, axis=1), dtype=BF16)
    return sums, jnp.asarray(np.stack(pair_masks).astype(np.float32))


def _hgrn_kernel(zq_ref, zf_ref, zi_ref, zg_ref, lbl_ref, on_ref,
                 sums_ref, pm_ref, o_ref, st_scr, *, layer, ts):
    C = HGRN_CHUNK
    n_levels = pm_ref.shape[0] - 1

    @pl.when(pl.program_id(2) == 0)
    def _():
        st_scr[...] = jnp.zeros_like(st_scr)

    lg = lbl_ref[...]
    ex = jnp.exp(lg - jnp.max(lg, axis=0, keepdims=True))
    pr = ex / jnp.sum(ex, axis=0, keepdims=True)
    cum = pr[0:1, :]
    for r in range(1, layer + 1):
        cum = cum + pr[r:r + 1, :]
    lb = cum - pr[0:1, :]
    lb_floor_all = jnp.maximum(lb, LB_FLOOR)
    one_m_lb_all = 1.0 - lb
    gain = on_ref[...]

    for c, hh in [(c, hh) for c in range(ts // C) for hh in range(HGRN_HEADS_PER_STEP)]:
        rows = pl.ds(c * C, C)
        lanes = pl.ds(hh * LANES, LANES)
        lb_floor = lb_floor_all[:, hh * LANES:(hh + 1) * LANES]
        one_m_lb = one_m_lb_all[:, hh * LANES:(hh + 1) * LANES]
        q_raw = zq_ref[rows, lanes]
        zf = zf_ref[rows, lanes]
        v = zi_ref[rows, lanes].astype(BF16)
        g_raw = zg_ref[rows, lanes]
        q = q_raw * _sigmoid(q_raw)
        e = jnp.exp(-jnp.abs(zf))
        r = 1.0 / (1.0 + e)
        er = e * r
        pos = zf >= 0
        log_f = jnp.log(lb_floor + one_m_lb * jnp.where(pos, r, er))
        kk = one_m_lb * jnp.where(pos, er, r)
        p0 = log_f.astype(BF16)
        r1 = log_f - p0.astype(F32)
        p1 = r1.astype(BF16)
        p2 = (r1 - p1.astype(F32)).astype(BF16)
        sums = _dot(sums_ref[...], jnp.concatenate([p0, p1, p2], axis=0))
        b = sums[0:C]
        b_rest = sums[C:2 * C]
        a = pm_ref[0] * _dot_nt(q.astype(BF16), kk.astype(BF16))
        for lv in range(n_levels):
            dec = jnp.exp(sums[(2 + lv) * C:(3 + lv) * C])
            qt = (q * dec).astype(BF16)
            kt = (kk * dec).astype(BF16)
            a = a + pm_ref[lv + 1] * _dot_nt(qt, kt)
        st = st_scr[hh]
        o = _dot(a.astype(BF16), v) + _dot_nt((q * jnp.exp(b)).astype(BF16), st.astype(BF16))
        k_end = (kk * jnp.exp(b_rest)).astype(BF16)
        st_scr[hh] = st * jnp.exp(b[C - 1:C, :]) + _dot_tn(v, k_end)
        o_ref[rows, lanes] = (_rms(o, gain) * (g_raw * _sigmoid(g_raw))).astype(BF16)


def _hgrn(z, lb_logits, out_norm, consts, layer, B, S):
    T = B * S
    ts = min(S, 256)
    ns = S // ts
    sums, pair_masks = consts
    hp = HGRN_HEADS_PER_STEP
    width = hp * LANES
    col0 = HGRN_COL0 // width
    groups = N_HEADS // hp

    def zspec(part):
        return pl.BlockSpec((ts, width),
                            lambda b, h, i: (b * ns + i, col0 + part * groups + h))

    full = lambda arr: pl.BlockSpec(arr.shape, lambda b, h, i: (0,) * arr.ndim)
    kern = functools.partial(_hgrn_kernel, layer=layer, ts=ts)
    return pl.pallas_call(
        kern,
        grid=(B, groups, ns),
        in_specs=[zspec(0), zspec(1), zspec(2), zspec(3),
                  pl.BlockSpec((DEPTH, width), lambda b, h, i: (0, h)),
                  pl.BlockSpec((1, HGRN_V), lambda b, h, i: (0, 0)),
                  full(sums), full(pair_masks)],
        out_specs=pl.BlockSpec((ts, width), lambda b, h, i: (b * ns + i, h)),
        out_shape=jax.ShapeDtypeStruct((T, GROUP_WIDTH), BF16),
        scratch_shapes=[pltpu.VMEM((hp, HGRN_V, HGRN_K), F32)],
        compiler_params=_params(("parallel", "parallel", "arbitrary")),
        name="hgrn2",
    )(z, z, z, z, lb_logits, out_norm[None, :], sums, pair_masks)


def _out_proj_kernel(om_ref, oh_ref, h_ref, w_ref, g1_ref, g2_ref, h1_ref, c_ref):
    y = _dot(om_ref[...], w_ref[0:GROUP_WIDTH, :]) + _dot(oh_ref[...], w_ref[GROUP_WIDTH:, :])
    h1 = h_ref[...] + _rms(y, g1_ref[...])
    h1_ref[...] = h1
    c = _rms(h1, g2_ref[...])
    if c_ref.dtype == jnp.uint32:
        c_ref[...] = _pack_bf16_pairs(c)
    else:
        c_ref[...] = c.astype(c_ref.dtype)


def _out_proj(om, oh, h, w, g1, g2, c_dtype):
    T = h.shape[0]
    tm = min(T, 256)
    c_width = D_MODEL // 2 if c_dtype == jnp.uint32 else D_MODEL
    row = lambda i: (i, 0)
    const = lambda i: (0, 0)
    return pl.pallas_call(
        _out_proj_kernel,
        grid=(T // tm,),
        in_specs=[pl.BlockSpec((tm, GROUP_WIDTH), row),
                  pl.BlockSpec((tm, GROUP_WIDTH), row),
                  pl.BlockSpec((tm, D_MODEL), row),
                  pl.BlockSpec((D_MODEL, D_MODEL), const),
                  pl.BlockSpec((1, D_MODEL), const),
                  pl.BlockSpec((1, D_MODEL), const)],
        out_specs=[pl.BlockSpec((tm, D_MODEL), row), pl.BlockSpec((tm, c_width), row)],
        out_shape=[jax.ShapeDtypeStruct((T, D_MODEL), F32),
                   jax.ShapeDtypeStruct((T, c_width), c_dtype)],
        compiler_params=_params(("parallel",)),
        name="out_proj",
    )(om, oh, h, w, g1[None, :], g2[None, :])


def _gate_up_kernel(te_ref, nv_ref, x_ref, wg_ref, wu_ref, o_ref, wg_bf, wu_bf):
    i = pl.program_id(1)
    valid = i < nv_ref[0]
    new_weights = jnp.logical_or(i == 0, te_ref[i] != te_ref[jnp.maximum(i - 1, 0)])

    @pl.when(jnp.logical_and(valid, new_weights))
    def _():
        wg_bf[...] = wg_ref[0].astype(BF16)
        wu_bf[...] = wu_ref[0].astype(BF16)

    @pl.when(valid)
    def _():
        if x_ref.dtype == jnp.uint32:
            x = _unpack_bf16_pairs(x_ref[...])
        else:
            x = x_ref[...]
        for c in range(0, o_ref.shape[1], GU_SUB):
            g = _dot(x, wg_bf[:, c:c + GU_SUB])
            u = _dot(x, wu_bf[:, c:c + GU_SUB])
            o_ref[:, c:c + GU_SUB] = (g * _sigmoid(g) * u).astype(BF16)

    @pl.when(jnp.logical_not(valid))
    def _():
        o_ref[...] = jnp.zeros_like(o_ref)


def _gate_up(x, w_gu, tile_expert, n_valid, tm, tf):
    R, xw = x.shape
    n_tiles = R // tm
    nf = FFN_DIM // tf

    def tile(i, nv):
        return jnp.minimum(i, nv[0] - 1)

    return pl.pallas_call(
        _gate_up_kernel,
        grid_spec=pltpu.PrefetchScalarGridSpec(
            num_scalar_prefetch=2,
            grid=(nf, n_tiles),
            in_specs=[pl.BlockSpec((tm, xw), lambda j, i, te, nv: (tile(i, nv), 0)),
                      pl.BlockSpec((1, D_MODEL, tf), lambda j, i, te, nv: (te[tile(i, nv)], 0, j)),
                      pl.BlockSpec((1, D_MODEL, tf),
                                   lambda j, i, te, nv: (te[tile(i, nv)], 0, j + nf))],
            out_specs=pl.BlockSpec((tm, tf), lambda j, i, te, nv: (i, j)),
            scratch_shapes=[pltpu.VMEM((D_MODEL, tf), BF16), pltpu.VMEM((D_MODEL, tf), BF16)],
        ),
        out_shape=jax.ShapeDtypeStruct((R, FFN_DIM), BF16),
        compiler_params=_params(("arbitrary", "arbitrary")),
        name="ffn_gate_up",
    )(tile_expert, n_valid, x, w_gu, w_gu)


def _down_kernel(te_ref, nv_ref, a_ref, w_ref, o_ref):
    valid = pl.program_id(1) < nv_ref[0]

    @pl.when(valid)
    def _():
        o_ref[...] = _dot(a_ref[...], w_ref[0])

    @pl.when(jnp.logical_not(valid))
    def _():
        o_ref[...] = jnp.zeros_like(o_ref)


def _down(act, w_down, tile_expert, n_valid, tm):
    R = act.shape[0]
    n_tiles = R // tm
    tn = 512

    def tile(i, nv):
        return jnp.minimum(i, nv[0] - 1)

    return pl.pallas_call(
        _down_kernel,
        grid_spec=pltpu.PrefetchScalarGridSpec(
            num_scalar_prefetch=2,
            grid=(D_MODEL // tn, n_tiles),
            in_specs=[pl.BlockSpec((tm, FFN_DIM), lambda j, i, te, nv: (tile(i, nv), 0)),
                      pl.BlockSpec((1, FFN_DIM, tn), lambda j, i, te, nv: (te[tile(i, nv)], 0, j))],
            out_specs=pl.BlockSpec((tm, tn), lambda j, i, te, nv: (i, j)),
        ),
        out_shape=jax.ShapeDtypeStruct((R, D_MODEL), F32),
        compiler_params=_params(("arbitrary", "arbitrary")),
        name="ffn_down",
    )(tile_expert, n_valid, act, w_down)


def _ffn_epilogue(f, h1, p, wg, wp, g3, gp):
    h2 = h1 + _rms(f, g3)
    gate = _sigmoid(_dot(h2.astype(BF16), wg))
    proj = _dot(p.astype(BF16), wp)
    return h2 + _rms(gate * proj, gp)


def _post_ffn_kernel(f_ref, h1_ref, p_ref, wg_ref, wp_ref, g3_ref, gp_ref, o_ref):
    o_ref[...] = _ffn_epilogue(f_ref[...], h1_ref[...], p_ref[...], wg_ref[...], wp_ref[...],
                               g3_ref[...], gp_ref[...])


def _post_ffn(f, h1, p, wg, wp, g3, gp):
    T = h1.shape[0]
    tm = min(T, 256)
    row = lambda i: (i, 0)
    const = lambda i: (0, 0)
    return pl.pallas_call(
        _post_ffn_kernel,
        grid=(T // tm,),
        in_specs=[pl.BlockSpec((tm, D_MODEL), row),
                  pl.BlockSpec((tm, D_MODEL), row),
                  pl.BlockSpec((tm, PLE_DIM), row),
                  pl.BlockSpec((D_MODEL, D_MODEL), const),
                  pl.BlockSpec((PLE_DIM, D_MODEL), const),
                  pl.BlockSpec((1, D_MODEL), const),
                  pl.BlockSpec((1, D_MODEL), const)],
        out_specs=pl.BlockSpec((tm, D_MODEL), row),
        out_shape=jax.ShapeDtypeStruct((T, D_MODEL), F32),
        compiler_params=_params(("parallel",)),
        name="post_ffn",
    )(f, h1, p, wg, wp, g3[None, :], gp[None, :])


def _router_kernel(c_ref, wr_ref, tri_ref, route_ref, cnt_ref, carry):
    @pl.when(pl.program_id(0) == 0)
    def _():
        carry[...] = jnp.zeros_like(carry)

    logits = _dot(_unpack_bf16_pairs(c_ref[...]), wr_ref[...])
    lane = lax.broadcasted_iota(jnp.int32, logits.shape, 1)
    lg = jnp.where(lane < N_EXPERTS, logits, -jnp.inf)
    m1 = jnp.max(lg, axis=-1, keepdims=True)
    i1 = jnp.min(jnp.where(lg == m1, lane, LANES), axis=-1, keepdims=True)
    lg2 = jnp.where(lane == i1, -jnp.inf, lg)
    m2 = jnp.max(lg2, axis=-1, keepdims=True)
    i2 = jnp.min(jnp.where(lg2 == m2, lane, LANES), axis=-1, keepdims=True)
    e = jnp.exp(m2 - m1)
    g1 = 1.0 / (1.0 + e)
    g2 = e / (1.0 + e)
    onehot = jnp.where((lane == i1) | (lane == i2), 1.0, 0.0)
    before = _dot(tri_ref[...], onehot.astype(BF16)) + carry[0:1, :]
    rank1 = jnp.sum(jnp.where(lane == i1, before, 0.0), axis=-1, keepdims=True)
    rank2 = jnp.sum(jnp.where(lane == i2, before, 0.0), axis=-1, keepdims=True)
    total = carry[0:1, :] + jnp.sum(onehot, axis=0, keepdims=True)
    carry[...] = jnp.broadcast_to(total, carry.shape)
    cnt_ref[...] = jnp.broadcast_to(total, cnt_ref.shape)
    out = jnp.where(lane == 0, i1.astype(F32), 0.0)
    out = jnp.where(lane == 1, i2.astype(F32), out)
    out = jnp.where(lane == 2, g1, out)
    out = jnp.where(lane == 3, g2, out)
    out = jnp.where(lane == 4, rank1, out)
    out = jnp.where(lane == 5, rank2, out)
    route_ref[...] = out


def _router(c, w_router):
    T = c.shape[0]
    tr = min(T, 512)
    wr = jnp.zeros((D_MODEL, LANES), BF16).at[:, :N_EXPERTS].set(w_router.astype(BF16))
    tri = jnp.asarray(np.tril(np.ones((tr, tr), np.float32), -1), dtype=BF16)
    return pl.pallas_call(
        _router_kernel,
        grid=(T // tr,),
        in_specs=[pl.BlockSpec((tr, c.shape[1]), lambda i: (i, 0)),
                  pl.BlockSpec((D_MODEL, LANES), lambda i: (0, 0)),
                  pl.BlockSpec((tr, tr), lambda i: (0, 0))],
        out_specs=[pl.BlockSpec((tr, LANES), lambda i: (i, 0)),
                   pl.BlockSpec((8, LANES), lambda i: (0, 0))],
        out_shape=[jax.ShapeDtypeStruct((T, LANES), F32),
                   jax.ShapeDtypeStruct((8, LANES), F32)],
        scratch_shapes=[pltpu.VMEM((8, LANES), F32)],
        compiler_params=_params(("arbitrary",)),
        name="moe_router",
    )(c, wr, tri)


def _dispatch_kernel(p1_ref, p2_ref, pad_ref, nv_ref, c_ref, xs_ref, zbuf, sem, zsem, *,
                     tm, seg_tile):
    i = pl.program_id(0)
    base = i * tm
    zr = zbuf.shape[0]

    @pl.when(i == 0)
    def _():
        zbuf[...] = jnp.zeros_like(zbuf)

        def zero_tile(start):
            first = pl.multiple_of(start, zr)
            copies = [pltpu.make_async_copy(zbuf, xs_ref.at[pl.ds(first + k * zr, zr)], zsem)
                      for k in range(seg_tile // zr)]
            for cp in copies:
                cp.start()
            for cp in copies:
                cp.wait()

        def zero_padding(e, _):
            @pl.when(pad_ref[e] >= 0)
            def _():
                zero_tile(pad_ref[e])
            return 0

        def zero_unused(t, _):
            zero_tile(t * seg_tile)
            return 0

        lax.fori_loop(0, N_EXPERTS, zero_padding, 0)
        lax.fori_loop(nv_ref[0], xs_ref.shape[0] // seg_tile, zero_unused, 0)

    def row_copy(r, dst):
        return pltpu.make_async_copy(c_ref.at[pl.ds(r, 1)], xs_ref.at[pl.ds(dst, 1)], sem)

    def issue(g, _):
        for u in range(ROW_UNROLL):
            r = g * ROW_UNROLL + u
            row_copy(r, p1_ref[base + r]).start(priority=0)
            row_copy(r, p2_ref[base + r]).start(priority=1)
        return 0

    lax.fori_loop(0, tm // ROW_UNROLL, issue, 0)

    def drain(g, _):
        for _u in range(2 * ROW_UNROLL):
            row_copy(0, 0).wait()
        return 0

    lax.fori_loop(0, tm // ROW_UNROLL, drain, 0)


def _dispatch(c, pos1, pos2, pad_tile_row, n_valid, n_rows, seg_tile):
    T, width = c.shape
    tm = min(T, 256)
    return pl.pallas_call(
        functools.partial(_dispatch_kernel, tm=tm, seg_tile=seg_tile),
        grid_spec=pltpu.PrefetchScalarGridSpec(
            num_scalar_prefetch=4,
            grid=(T // tm,),
            in_specs=[pl.BlockSpec((tm, width), lambda i, p1, p2, pt, nv: (i, 0))],
            out_specs=pl.BlockSpec(memory_space=pl.ANY),
            scratch_shapes=[pltpu.VMEM((min(seg_tile, 64), width), c.dtype),
                            pltpu.SemaphoreType.DMA(()),
                            pltpu.SemaphoreType.DMA(())],
        ),
        out_shape=jax.ShapeDtypeStruct((n_rows, width), c.dtype),
        compiler_params=_params(("arbitrary",)),
        name="moe_dispatch",
    )(pos1, pos2, pad_tile_row, n_valid, c)


def _combine_kernel(p1_ref, p2_ref, ys_ref, route_ref, h1_ref, p_ref, wg_ref, wp_ref,
                    g3_ref, gp_ref, o_ref, buf, sems, *, tm):
    i = pl.program_id(0)
    slot = i % 2

    def row_copy(src, s, k, r):
        return pltpu.make_async_copy(ys_ref.at[pl.ds(src, 1)], buf.at[s, k, pl.ds(r, 1)],
                                     sems.at[s])

    def gather_tile(tile, s):
        base = tile * tm

        def issue(g, _):
            for u in range(ROW_UNROLL):
                r = g * ROW_UNROLL + u
                row_copy(p1_ref[base + r], s, 0, r).start(priority=0)
                row_copy(p2_ref[base + r], s, 1, r).start(priority=1)
            return 0

        lax.fori_loop(0, tm // ROW_UNROLL, issue, 0)

    @pl.when(i == 0)
    def _():
        gather_tile(0, 0)

    @pl.when(i + 1 < pl.num_programs(0))
    def _():
        gather_tile(i + 1, 1 - slot)

    def drain(g, _):
        for _u in range(2 * ROW_UNROLL):
            row_copy(0, slot, 0, 0).wait()
        return 0

    lax.fori_loop(0, tm // ROW_UNROLL, drain, 0)
    route = route_ref[...]
    f = route[:, 2:3] * buf[slot, 0] + route[:, 3:4] * buf[slot, 1]
    o_ref[...] = _ffn_epilogue(f, h1_ref[...], p_ref[...], wg_ref[...], wp_ref[...],
                               g3_ref[...], gp_ref[...])


def _combine(ys, pos1, pos2, route, h1, p, wg, wp, g3, gp):
    T = h1.shape[0]
    tm = min(T, 256)
    row = lambda i, p1, p2: (i, 0)
    const = lambda i, p1, p2: (0, 0)
    return pl.pallas_call(
        functools.partial(_combine_kernel, tm=tm),
        grid_spec=pltpu.PrefetchScalarGridSpec(
            num_scalar_prefetch=2,
            grid=(T // tm,),
            in_specs=[pl.BlockSpec(memory_space=pl.ANY),
                      pl.BlockSpec((tm, LANES), row),
                      pl.BlockSpec((tm, D_MODEL), row),
                      pl.BlockSpec((tm, PLE_DIM), row),
                      pl.BlockSpec((D_MODEL, D_MODEL), const),
                      pl.BlockSpec((PLE_DIM, D_MODEL), const),
                      pl.BlockSpec((1, D_MODEL), const),
                      pl.BlockSpec((1, D_MODEL), const)],
            out_specs=pl.BlockSpec((tm, D_MODEL), row),
            scratch_shapes=[pltpu.VMEM((2, 2, tm, D_MODEL), F32),
                            pltpu.SemaphoreType.DMA((2,))],
        ),
        out_shape=jax.ShapeDtypeStruct((T, D_MODEL), F32),
        compiler_params=_params(("arbitrary",)),
        name="moe_combine",
    )(pos1, pos2, ys, route, h1, p, wg, wp, g3[None, :], gp[None, :])


def _swap_halves(w):
    half = w.shape[-1] // 2
    return jnp.concatenate([w[..., half:], w[..., :half]], axis=-1)


def _prep_w_in(w):
    k_rope = w[:, 640:704]
    pad64 = jnp.zeros((D_MODEL, 64), w.dtype)
    cols = [w[:, :640], k_rope, pad64, _swap_halves(k_rope), pad64,
            jnp.zeros((D_MODEL, HGRN_COL0 - 896), w.dtype), w[:, 704:]]
    return jnp.concatenate(cols, axis=1).astype(BF16)


def _prep_w_uq(w):
    w = w.reshape(Q_RANK, N_HEADS, QK_DIM)
    nope = w[:, :, :NOPE_DIM]
    pe = w[:, :, NOPE_DIM:]
    pad = jnp.zeros((Q_RANK, N_HEADS, LANES - ROPE_DIM), w.dtype)
    pe_pad = jnp.concatenate([pe, pad], axis=-1)
    pe_swap = jnp.concatenate([_swap_halves(pe), pad], axis=-1)
    parts = [x.reshape(Q_RANK, GROUP_WIDTH) for x in (nope, pe_pad, pe_swap)]
    return jnp.concatenate(parts, axis=1).astype(BF16)


def _prep_w_ukv(w):
    w = w.reshape(KV_RANK, N_HEADS, NOPE_DIM + V_DIM)
    k = w[:, :, :NOPE_DIM].reshape(KV_RANK, GROUP_WIDTH)
    v = w[:, :, NOPE_DIM:].reshape(KV_RANK, GROUP_WIDTH)
    return jnp.concatenate([k, v], axis=1).astype(BF16)


def _moe_plan(route, counts, tm, n_tiles):
    e1 = route[:, 0].astype(jnp.int32)
    e2 = route[:, 1].astype(jnp.int32)
    rank1 = route[:, 4].astype(jnp.int32)
    rank2 = route[:, 5].astype(jnp.int32)
    cnt = counts[0, :N_EXPERTS].astype(jnp.int32)
    padded = ((cnt + tm - 1) // tm) * tm
    seg_end = jnp.cumsum(padded)
    seg_start = seg_end - padded
    pos1 = seg_start[e1] + rank1
    pos2 = seg_start[e2] + rank2
    tile_start = jnp.arange(n_tiles, dtype=jnp.int32) * tm
    tile_expert = jnp.sum(tile_start[:, None] >= seg_end[None, :], axis=1).astype(jnp.int32)
    tile_expert = jnp.minimum(tile_expert, N_EXPERTS - 1)
    n_valid = (seg_end[-1] // tm).astype(jnp.int32)[None]
    pad_tile_row = jnp.where(padded > 0, seg_end - tm, -1).astype(jnp.int32)
    return pos1, pos2, tile_expert, n_valid, pad_tile_row


def kernel(x, p, positions, sandwich_norms, w_in, mla_q_norm, mla_kv_norm, w_uq, w_ukv,
           hgrn_lb_logits, hgrn_out_norm, w_out, ffn_w_gu, ffn_w_down, moe_w_router,
           moe_w_gu, moe_w_down, ple_w_proj, ple_w_gate, ple_norm):
    B, S, _ = x.shape
    T = B * S
    h = x.reshape(T, D_MODEL)
    ct, st = _rope_tables(positions)
    hgrn_consts = _hgrn_consts(HGRN_CHUNK)
    tm = min(T, 512)
    for l in range(DEPTH):
        z = _norm_matmul(h, sandwich_norms[l, 0], _prep_w_in(w_in[l]))
        q, k, v = _mla_proj(z, ct, st, mla_q_norm[l], mla_kv_norm[l],
                            _prep_w_uq(w_uq[l]), _prep_w_ukv(w_ukv[l]), B, S)
        o_mla = _attention(q, k, v).reshape(T, GROUP_WIDTH)
        o_hgrn = _hgrn(z, hgrn_lb_logits, hgrn_out_norm[l], hgrn_consts, l, B, S)
        moe = l % 2 == 1
        h1, c = _out_proj(o_mla, o_hgrn, h, w_out[l].astype(BF16),
                          sandwich_norms[l, 1], sandwich_norms[l, 2],
                          jnp.uint32 if moe else BF16)
        wg = ple_w_gate[l].astype(BF16)
        wp = ple_w_proj[l].astype(BF16)
        p_l = p[l].reshape(T, PLE_DIM)
        if not moe:
            def one_expert(rows):
                return jnp.zeros((T // rows,), jnp.int32), jnp.full((1,), T // rows, jnp.int32)

            tm_dense = min(T, 1024)
            act = _gate_up(c, ffn_w_gu[l // 2][None], *one_expert(tm_dense), tm_dense, 512)
            f = _down(act, ffn_w_down[l // 2][None].astype(BF16), *one_expert(tm), tm)
            h = _post_ffn(f, h1, p_l, wg, wp, sandwich_norms[l, 3], ple_norm[l])
        else:
            n_tiles = (2 * T) // tm + N_EXPERTS
            route, counts = _router(c, moe_w_router[l // 2])
            pos1, pos2, tile_expert, n_valid, pad_tile_row = _moe_plan(route, counts, tm, n_tiles)
            xs = _dispatch(c, pos1, pos2, pad_tile_row, n_valid, n_tiles * tm, tm)
            act = _gate_up(xs, moe_w_gu[l // 2], tile_expert, n_valid, tm, 1024)
            ys = _down(act, moe_w_down[l // 2].astype(BF16), tile_expert, n_valid, tm)
            h = _combine(ys, pos1, pos2, route, h1, p_l, wg, wp,
                         sandwich_norms[l, 3], ple_norm[l])
    return h.reshape(B, S, D_MODEL)
```

```python
import functools
import math

import numpy as np
import jax
import jax.numpy as jnp
from jax import lax
from jax.experimental import pallas as pl
from jax.experimental.pallas import tpu as pltpu

F32 = jnp.float32
BF16 = jnp.bfloat16

D_MODEL = 2048
DEPTH = 2
N_HEADS = 8
NOPE_DIM = 128
ROPE_DIM = 64
V_DIM = 128
QK_DIM = NOPE_DIM + ROPE_DIM
Q_RANK = 384
KV_RANK = 256
HGRN_K = 128
HGRN_V = 128
GROUP_WIDTH = N_HEADS * 128
FFN_DIM = 7168
N_EXPERTS = 8
PLE_DIM = 256
ROPE_THETA = 10000.0
LB_FLOOR = 1e-30
EPS = 1e-6

LANES = 128
QK_PAD = 256
Q_SCALE = QK_DIM ** -0.5 * math.log2(math.e)
Z_MLA = 1024
Z_WIDTH = 5120
HGRN_COL0 = 1024
HGRN_CHUNK = 64
HGRN_HEADS_PER_STEP = 4
ROW_UNROLL = 8
GU_SUB = 512
IN_PROJ_SUB = 1024
VMEM_LIMIT = 56 * 1024 * 1024


def _params(semantics, vmem=VMEM_LIMIT):
    return pltpu.CompilerParams(dimension_semantics=semantics, vmem_limit_bytes=vmem)


def _rms(x, gain_row):
    ms = jnp.mean(x * x, axis=-1, keepdims=True)
    return x * lax.rsqrt(ms + EPS) * gain_row


def _dot(a, b):
    return jnp.dot(a, b, preferred_element_type=F32)


def _dot_nt(a, b):
    return lax.dot_general(a, b, (((1,), (1,)), ((), ())), preferred_element_type=F32)


def _dot_tn(a, b):
    return lax.dot_general(a, b, (((0,), (0,)), ((), ())), preferred_element_type=F32)


def _sigmoid(x):
    return 1.0 / (1.0 + jnp.exp(-x))


def _pack_bf16_pairs(x):
    n = x.shape[1] // 2
    bits = pltpu.bitcast(x.astype(BF16).astype(F32), jnp.uint32)
    return (bits[:, :n] >> 16) | (bits[:, n:] & jnp.uint32(0xFFFF0000))


def _unpack_bf16_pairs(w):
    lo = pltpu.bitcast(w << 16, F32)
    hi = pltpu.bitcast(w & jnp.uint32(0xFFFF0000), F32)
    return jnp.concatenate([lo, hi], axis=1).astype(BF16)


def _fold_lanes(x, op):
    parts = [x[:, c:c + LANES] for c in range(0, x.shape[1], LANES)]
    while len(parts) > 1:
        parts = [op(a, b) for a, b in zip(parts[0::2], parts[1::2])] + parts[len(parts) & ~1:]
    return parts[0]


def _rope_kernel(pos_ref, inv_ref, sgn_ref, ct_ref, st_ref):
    ang = pos_ref[...].astype(F32) * inv_ref[...]
    keep = jnp.abs(sgn_ref[...])
    ct_ref[...] = jnp.cos(ang) * keep
    st_ref[...] = jnp.sin(ang) * sgn_ref[...]


def _rope_tables(positions):
    T = positions.size
    tt = min(T, 1024)
    half = ROPE_DIM // 2
    inv_freq = 1.0 / (ROPE_THETA ** (jnp.arange(0, ROPE_DIM, 2, dtype=F32) / ROPE_DIM))
    inv_row = jnp.concatenate([inv_freq, inv_freq, jnp.zeros((LANES - ROPE_DIM,), F32)])[None, :]
    sgn = np.zeros((1, LANES), np.float32)
    sgn[0, :half] = -1.0
    sgn[0, half:ROPE_DIM] = 1.0
    return pl.pallas_call(
        _rope_kernel,
        grid=(T // tt,),
        in_specs=[pl.BlockSpec((tt, 1), lambda i: (i, 0)),
                  pl.BlockSpec((1, LANES), lambda i: (0, 0)),
                  pl.BlockSpec((1, LANES), lambda i: (0, 0))],
        out_specs=[pl.BlockSpec((tt, LANES), lambda i: (i, 0))] * 2,
        out_shape=[jax.ShapeDtypeStruct((T, LANES), F32)] * 2,
        compiler_params=_params(("parallel",)),
        name="rope_tables",
    )(positions.reshape(T, 1), inv_row, jnp.asarray(sgn))


def _norm_matmul_kernel(x_ref, g_ref, w_ref, o_ref):
    a = _rms(x_ref[...], g_ref[...]).astype(BF16)
    for c in range(0, o_ref.shape[1], IN_PROJ_SUB):
        o_ref[:, c:c + IN_PROJ_SUB] = _dot(a, w_ref[:, c:c + IN_PROJ_SUB])


def _norm_matmul(x, gain, w):
    T, K = x.shape
    N = w.shape[1]
    tm = min(T, 256)
    return pl.pallas_call(
        _norm_matmul_kernel,
        grid=(T // tm,),
        in_specs=[pl.BlockSpec((tm, K), lambda i: (i, 0)),
                  pl.BlockSpec((1, K), lambda i: (0, 0)),
                  pl.BlockSpec((K, N), lambda i: (0, 0), pipeline_mode=pl.Buffered(1))],
        out_specs=pl.BlockSpec((tm, N), lambda i: (i, 0)),
        out_shape=jax.ShapeDtypeStruct((T, N), F32),
        compiler_params=_params(("parallel",)),
        name="in_proj",
    )(x, gain[None, :], w)


def _mla_proj_kernel(z_ref, ct_ref, st_ref, qn_ref, kvn_ref, wq_ref, wkv_ref,
                     q_ref, k_ref, v_ref):
    z = z_ref[...]
    ct = ct_ref[...]
    st = st_ref[...]
    aq = _rms(z[:, :Q_RANK], qn_ref[...]).astype(BF16)
    akv = _rms(z[:, Q_RANK:Q_RANK + KV_RANK], kvn_ref[...]).astype(BF16)
    k_a = z[:, 640:768]
    k_b = z[:, 768:896]
    k_pe = (k_a * ct + k_b * st).astype(BF16)
    q = _dot(aq, wq_ref[...])
    kv = _dot(akv, wkv_ref[...])
    for h in range(N_HEADS):
        lo, hi = h * LANES, (h + 1) * LANES
        q_ref[0, h, :, 0:LANES] = (q[:, lo:hi] * Q_SCALE).astype(BF16)
        q_pe = q[:, GROUP_WIDTH + lo:GROUP_WIDTH + hi] * ct \
            + q[:, 2 * GROUP_WIDTH + lo:2 * GROUP_WIDTH + hi] * st
        q_ref[0, h, :, LANES:QK_PAD] = (q_pe * Q_SCALE).astype(BF16)
        k_ref[0, h, :, 0:LANES] = kv[:, lo:hi].astype(BF16)
        k_ref[0, h, :, LANES:QK_PAD] = k_pe
        v_ref[0, h, :, :] = kv[:, GROUP_WIDTH + lo:GROUP_WIDTH + hi].astype(BF16)


def _mla_proj(z, ct, st, q_norm, kv_norm, wq, wkv, B, S):
    ts = min(S, 512)
    ns = S // ts
    tok = lambda b, i: (b * ns + i, 0)
    const = lambda b, i: (0, 0)
    head_out = lambda b, i: (b, 0, i, 0)
    return pl.pallas_call(
        _mla_proj_kernel,
        grid=(B, ns),
        in_specs=[pl.BlockSpec((ts, Z_MLA), tok),
                  pl.BlockSpec((ts, LANES), tok),
                  pl.BlockSpec((ts, LANES), tok),
                  pl.BlockSpec((1, Q_RANK), const),
                  pl.BlockSpec((1, KV_RANK), const),
                  pl.BlockSpec(wq.shape, const),
                  pl.BlockSpec(wkv.shape, const)],
        out_specs=[pl.BlockSpec((1, N_HEADS, ts, QK_PAD), head_out),
                   pl.BlockSpec((1, N_HEADS, ts, QK_PAD), head_out),
                   pl.BlockSpec((1, N_HEADS, ts, V_DIM), head_out)],
        out_shape=[jax.ShapeDtypeStruct((B, N_HEADS, S, QK_PAD), BF16),
                   jax.ShapeDtypeStruct((B, N_HEADS, S, QK_PAD), BF16),
                   jax.ShapeDtypeStruct((B, N_HEADS, S, V_DIM), BF16)],
        compiler_params=_params(("parallel", "parallel")),
        name="mla_proj",
    )(z, ct, st, q_norm[None, :], kv_norm[None, :], wq, wkv)


def _attn_kernel(q_ref, k_ref, v_ref, o_ref, s_a, s_b, *, tk):
    i = pl.program_id(2)
    top_rows = pl.ds(0, tk)
    bot_rows = pl.ds(tk, tk)

    def kv_rows(u):
        return pl.ds(pl.multiple_of(u * tk, tk), tk)

    def scores(u):
        return _dot_nt(q_ref[0, 0], k_ref[0, 0, kv_rows(u), :])

    lane = lax.broadcasted_iota(jnp.int32, (tk, LANES), 1)
    ones_col = jnp.where(lane == 0, 1.0, 0.0).astype(BF16)

    def values(u):
        return jnp.concatenate([v_ref[0, 0, kv_rows(u), :], ones_col], axis=1)

    def update(state, s, v, masked):
        m, acc = state
        if masked:
            row = lax.broadcasted_iota(jnp.int32, (tk, tk), 0)
            col = lax.broadcasted_iota(jnp.int32, (tk, tk), 1)
            s = jnp.where(col <= row, s, -jnp.inf)
        m_new = jnp.maximum(m, jnp.max(_fold_lanes(s, jnp.maximum), axis=-1, keepdims=True))
        alpha = jnp.exp2(m - m_new)
        p = jnp.exp2((s - m_new).astype(BF16))
        acc = alpha * acc + _dot(p, v)
        return m_new, acc

    def both(top, bot, s_ref, u):
        v = values(u)
        return (update(top, s_ref[top_rows, :], v, False),
                update(bot, s_ref[bot_rows, :], v, False))

    s_a[...] = scores(0)

    def body(t, state):
        top, bot = state
        s_b[...] = scores(2 * t + 1)
        top, bot = both(top, bot, s_a, 2 * t)
        s_a[...] = scores(2 * t + 2)
        top, bot = both(top, bot, s_b, 2 * t + 1)
        return top, bot

    init = (jnp.full((tk, 1), -jnp.inf, F32), jnp.zeros((tk, 2 * LANES), F32))
    top, bot = lax.fori_loop(0, i, body, (init, init))
    s_b[bot_rows, :] = _dot_nt(q_ref[0, 0, bot_rows, :], k_ref[0, 0, kv_rows(2 * i + 1), :])
    v0 = values(2 * i)
    top = update(top, s_a[top_rows, :], v0, True)
    bot = update(bot, s_a[bot_rows, :], v0, False)
    bot = update(bot, s_b[bot_rows, :], values(2 * i + 1), True)
    for rows, (_, acc) in ((top_rows, top), (bot_rows, bot)):
        o_ref[0, rows, :] = (acc[:, :V_DIM] / acc[:, V_DIM:V_DIM + 1]).astype(BF16)


def _attention(q, k, v):
    B, H, S, _ = q.shape
    tq = min(S, 1024)
    tk = tq // 2
    return pl.pallas_call(
        functools.partial(_attn_kernel, tk=tk),
        grid=(B, H, S // tq),
        in_specs=[pl.BlockSpec((1, 1, tq, QK_PAD), lambda b, h, i: (b, h, i, 0)),
                  pl.BlockSpec((1, 1, S, QK_PAD), lambda b, h, i: (b, h, 0, 0)),
                  pl.BlockSpec((1, 1, S, V_DIM), lambda b, h, i: (b, h, 0, 0))],
        out_specs=pl.BlockSpec((1, tq, V_DIM), lambda b, h, i: (b, i, h)),
        out_shape=jax.ShapeDtypeStruct((B, S, H * V_DIM), BF16),
        scratch_shapes=[pltpu.VMEM((tq, tk), F32), pltpu.VMEM((tq, tk), F32)],
        compiler_params=_params(("parallel", "parallel", "arbitrary")),
        name="mla_attention",
    )(q, k, v)


def _hgrn_consts(C):
    levels = [C >> (i + 1) for i in range(int(math.log2(C)))]
    t = np.arange(C)
    u = np.arange(C)
    mats = [(u[None, :] <= t[:, None]),
            (u[None, :] > t[:, None])]
    pair_masks = [np.eye(C, dtype=bool)]
    for m in levels:
        ref = (t // (2 * m)) * (2 * m) + m - 1
        hi = ((t // m) % 2) == 1
        rng_hi = (u[None, :] > ref[:, None]) & (u[None, :] <= t[:, None])
        rng_lo = (u[None, :] > t[:, None]) & (u[None, :] <= ref[:, None])
        mats.append(np.where(hi[:, None], rng_hi, rng_lo))
        same = (t[:, None] // (2 * m)) == (t[None, :] // (2 * m))
        pair_masks.append(hi[:, None] & (~hi[None, :]) & same)
    stack = np.concatenate(mats, axis=0).astype(np.float32)
    sums = jnp.asarray(np.concatenate([stack] * 3, axis=1), dtype=BF16)
    return sums, jnp.asarray(np.stack(pair_masks).astype(np.float32))


def _hgrn_kernel(zq_ref, zf_ref, zi_ref, zg_ref, lbl_ref, on_ref,
                 sums_ref, pm_ref, o_ref, st_scr, *, layer, ts):
    C = HGRN_CHUNK
    n_levels = pm_ref.shape[0] - 1

    @pl.when(pl.program_id(2) == 0)
    def _():
        st_scr[...] = jnp.zeros_like(st_scr)

    lg = lbl_ref[...]
    ex = jnp.exp(lg - jnp.max(lg, axis=0, keepdims=True))
    pr = ex / jnp.sum(ex, axis=0, keepdims=True)
    cum = pr[0:1, :]
    for r in range(1, layer + 1):
        cum = cum + pr[r:r + 1, :]
    lb = cum - pr[0:1, :]
    lb_floor = jnp.maximum(lb, LB_FLOOR)
    one_m_lb = 1.0 - lb
    gain = on_ref[...]

    for c in range(ts // C):
        rows = pl.ds(c * C, C)
        q_raw = zq_ref[rows, :]
        zf = zf_ref[rows, :]
        g_raw = zg_ref[rows, :]
        v_all = zi_ref[rows, :].astype(BF16)
        q_all = q_raw * _sigmoid(q_raw)
        e = jnp.exp(-jnp.abs(zf))
        r = 1.0 / (1.0 + e)
        er = e * r
        pos = zf >= 0
        log_f = jnp.log(lb_floor + one_m_lb * jnp.where(pos, r, er))
        kk_all = one_m_lb * jnp.where(pos, er, r)
        gate_all = g_raw * _sigmoid(g_raw)
        p0 = log_f.astype(BF16)
        r1 = log_f - p0.astype(F32)
        p1 = r1.astype(BF16)
        p2 = (r1 - p1.astype(F32)).astype(BF16)
        sums_all = _dot(sums_ref[...], jnp.concatenate([p0, p1, p2], axis=0))
        for hh in range(HGRN_HEADS_PER_STEP):
            lanes = slice(hh * LANES, (hh + 1) * LANES)
            q, kk, v, sums = q_all[:, lanes], kk_all[:, lanes], v_all[:, lanes], sums_all[:, lanes]
            b = sums[0:C]
            b_rest = sums[C:2 * C]
            a = pm_ref[0] * _dot_nt(q.astype(BF16), kk.astype(BF16))
            for lv in range(n_levels):
                dec = jnp.exp(sums[(2 + lv) * C:(3 + lv) * C])
                qt = (q * dec).astype(BF16)
                kt = (kk * dec).astype(BF16)
                a = a + pm_ref[lv + 1] * _dot_nt(qt, kt)
            st = st_scr[hh]
            o = _dot(a.astype(BF16), v) \
                + _dot_nt((q * jnp.exp(b)).astype(BF16), st.astype(BF16))
            k_end = (kk * jnp.exp(b_rest)).astype(BF16)
            st_scr[hh] = st * jnp.exp(b[C - 1:C, :]) + _dot_tn(v, k_end)
            o_ref[rows, pl.ds(hh * LANES, LANES)] = \
                (_rms(o, gain) * gate_all[:, lanes]).astype(BF16)


def _hgrn(z, lb_logits, out_norm, consts, layer, B, S):
    T = B * S
    ts = min(S, 256)
    ns = S // ts
    sums, pair_masks = consts
    hp = HGRN_HEADS_PER_STEP
    width = hp * LANES
    col0 = HGRN_COL0 // width
    groups = N_HEADS // hp

    def zspec(part):
        return pl.BlockSpec((ts, width),
                            lambda b, h, i: (b * ns + i, col0 + part * groups + h))

    full = lambda arr: pl.BlockSpec(arr.shape, lambda b, h, i: (0,) * arr.ndim)
    kern = functools.partial(_hgrn_kernel, layer=layer, ts=ts)
    return pl.pallas_call(
        kern,
        grid=(B, groups, ns),
        in_specs=[zspec(0), zspec(1), zspec(2), zspec(3),
                  pl.BlockSpec((DEPTH, width), lambda b, h, i: (0, h)),
                  pl.BlockSpec((1, HGRN_V), lambda b, h, i: (0, 0)),
                  full(sums), full(pair_masks)],
        out_specs=pl.BlockSpec((ts, width), lambda b, h, i: (b * ns + i, h)),
        out_shape=jax.ShapeDtypeStruct((T, GROUP_WIDTH), BF16),
        scratch_shapes=[pltpu.VMEM((hp, HGRN_V, HGRN_K), F32)],
        compiler_params=_params(("parallel", "parallel", "arbitrary")),
        name="hgrn2",
    )(z, z, z, z, lb_logits, out_norm[None, :], sums, pair_masks)


def _out_proj_kernel(om_ref, oh_ref, h_ref, w_ref, g1_ref, g2_ref, h1_ref, c_ref):
    y = _dot(om_ref[...], w_ref[0:GROUP_WIDTH, :]) + _dot(oh_ref[...], w_ref[GROUP_WIDTH:, :])
    h1 = h_ref[...] + _rms(y, g1_ref[...])
    h1_ref[...] = h1
    c = _rms(h1, g2_ref[...])
    if c_ref.dtype == jnp.uint32:
        c_ref[...] = _pack_bf16_pairs(c)
    else:
        c_ref[...] = c.astype(c_ref.dtype)


def _out_proj(om, oh, h, w, g1, g2, c_dtype):
    T = h.shape[0]
    tm = min(T, 512)
    c_width = D_MODEL // 2 if c_dtype == jnp.uint32 else D_MODEL
    row = lambda i: (i, 0)
    const = lambda i: (0, 0)
    return pl.pallas_call(
        _out_proj_kernel,
        grid=(T // tm,),
        in_specs=[pl.BlockSpec((tm, GROUP_WIDTH), row),
                  pl.BlockSpec((tm, GROUP_WIDTH), row),
                  pl.BlockSpec((tm, D_MODEL), row),
                  pl.BlockSpec((D_MODEL, D_MODEL), const, pipeline_mode=pl.Buffered(1)),
                  pl.BlockSpec((1, D_MODEL), const),
                  pl.BlockSpec((1, D_MODEL), const)],
        out_specs=[pl.BlockSpec((tm, D_MODEL), row), pl.BlockSpec((tm, c_width), row)],
        out_shape=[jax.ShapeDtypeStruct((T, D_MODEL), F32),
                   jax.ShapeDtypeStruct((T, c_width), c_dtype)],
        compiler_params=_params(("parallel",)),
        name="out_proj",
    )(om, oh, h, w, g1[None, :], g2[None, :])


def _gate_up_kernel(te_ref, nv_ref, x_ref, wg_ref, wu_ref, o_ref, wg_bf, wu_bf):
    i = pl.program_id(1)
    valid = i < nv_ref[0]
    new_weights = jnp.logical_or(i == 0, te_ref[i] != te_ref[jnp.maximum(i - 1, 0)])

    @pl.when(jnp.logical_and(valid, new_weights))
    def _():
        wg_bf[...] = wg_ref[0].astype(BF16)
        wu_bf[...] = wu_ref[0].astype(BF16)

    @pl.when(valid)
    def _():
        if x_ref.dtype == jnp.uint32:
            x = _unpack_bf16_pairs(x_ref[...])
        else:
            x = x_ref[...]
        for c in range(0, o_ref.shape[1], GU_SUB):
            g = _dot(x, wg_bf[:, c:c + GU_SUB])
            u = _dot(x, wu_bf[:, c:c + GU_SUB])
            o_ref[:, c:c + GU_SUB] = (g * _sigmoid(g) * u).astype(BF16)

    @pl.when(jnp.logical_not(valid))
    def _():
        o_ref[...] = jnp.zeros_like(o_ref)


def _gate_up(x, w_gu, tile_expert, n_valid, tm, tf):
    R, xw = x.shape
    n_tiles = R // tm
    nf = FFN_DIM // tf

    def tile(i, nv):
        return jnp.minimum(i, nv[0] - 1)

    return pl.pallas_call(
        _gate_up_kernel,
        grid_spec=pltpu.PrefetchScalarGridSpec(
            num_scalar_prefetch=2,
            grid=(nf, n_tiles),
            in_specs=[pl.BlockSpec((tm, xw), lambda j, i, te, nv: (tile(i, nv), 0)),
                      pl.BlockSpec((1, D_MODEL, tf), lambda j, i, te, nv: (te[tile(i, nv)], 0, j)),
                      pl.BlockSpec((1, D_MODEL, tf),
                                   lambda j, i, te, nv: (te[tile(i, nv)], 0, j + nf))],
            out_specs=pl.BlockSpec((tm, tf), lambda j, i, te, nv: (i, j)),
            scratch_shapes=[pltpu.VMEM((D_MODEL, tf), BF16), pltpu.VMEM((D_MODEL, tf), BF16)],
        ),
        out_shape=jax.ShapeDtypeStruct((R, FFN_DIM), BF16),
        compiler_params=_params(("arbitrary", "arbitrary")),
        name="ffn_gate_up",
    )(tile_expert, n_valid, x, w_gu, w_gu)


def _down_kernel(te_ref, nv_ref, a_ref, w_ref, o_ref):
    valid = pl.program_id(1) < nv_ref[0]

    @pl.when(valid)
    def _():
        o_ref[...] = _dot(a_ref[...], w_ref[0])

    @pl.when(jnp.logical_not(valid))
    def _():
        o_ref[...] = jnp.zeros_like(o_ref)


def _down(act, w_down, tile_expert, n_valid, tm):
    R = act.shape[0]
    n_tiles = R // tm
    tn = 512

    def tile(i, nv):
        return jnp.minimum(i, nv[0] - 1)

    return pl.pallas_call(
        _down_kernel,
        grid_spec=pltpu.PrefetchScalarGridSpec(
            num_scalar_prefetch=2,
            grid=(D_MODEL // tn, n_tiles),
            in_specs=[pl.BlockSpec((tm, FFN_DIM), lambda j, i, te, nv: (tile(i, nv), 0)),
                      pl.BlockSpec((1, FFN_DIM, tn), lambda j, i, te, nv: (te[tile(i, nv)], 0, j))],
            out_specs=pl.BlockSpec((tm, tn), lambda j, i, te, nv: (i, j)),
        ),
        out_shape=jax.ShapeDtypeStruct((R, D_MODEL), F32),
        compiler_params=_params(("arbitrary", "arbitrary")),
        name="ffn_down",
    )(tile_expert, n_valid, act, w_down)


def _ffn_epilogue(f, h1, p, wg, wp, g3, gp):
    h2 = h1 + _rms(f, g3)
    gate = _sigmoid(_dot(h2.astype(BF16), wg))
    proj = _dot(p.astype(BF16), wp)
    return h2 + _rms(gate * proj, gp)


def _post_ffn_kernel(f_ref, h1_ref, p_ref, wg_ref, wp_ref, g3_ref, gp_ref, o_ref):
    o_ref[...] = _ffn_epilogue(f_ref[...], h1_ref[...], p_ref[...], wg_ref[...], wp_ref[...],
                               g3_ref[...], gp_ref[...])


def _post_ffn(f, h1, p, wg, wp, g3, gp):
    T = h1.shape[0]
    tm = min(T, 512)
    row = lambda i: (i, 0)
    const = lambda i: (0, 0)
    return pl.pallas_call(
        _post_ffn_kernel,
        grid=(T // tm,),
        in_specs=[pl.BlockSpec((tm, D_MODEL), row),
                  pl.BlockSpec((tm, D_MODEL), row),
                  pl.BlockSpec((tm, PLE_DIM), row),
                  pl.BlockSpec((D_MODEL, D_MODEL), const, pipeline_mode=pl.Buffered(1)),
                  pl.BlockSpec((PLE_DIM, D_MODEL), const, pipeline_mode=pl.Buffered(1)),
                  pl.BlockSpec((1, D_MODEL), const),
                  pl.BlockSpec((1, D_MODEL), const)],
        out_specs=pl.BlockSpec((tm, D_MODEL), row),
        out_shape=jax.ShapeDtypeStruct((T, D_MODEL), F32),
        compiler_params=_params(("parallel",)),
        name="post_ffn",
    )(f, h1, p, wg, wp, g3[None, :], gp[None, :])


def _router_kernel(c_ref, wr_ref, tri_ref, route_ref, cnt_ref, carry):
    @pl.when(pl.program_id(0) == 0)
    def _():
        carry[...] = jnp.zeros_like(carry)

    logits = _dot(_unpack_bf16_pairs(c_ref[...]), wr_ref[...])
    lane = lax.broadcasted_iota(jnp.int32, logits.shape, 1)
    lg = jnp.where(lane < N_EXPERTS, logits, -jnp.inf)
    m1 = jnp.max(lg, axis=-1, keepdims=True)
    i1 = jnp.min(jnp.where(lg == m1, lane, LANES), axis=-1, keepdims=True)
    lg2 = jnp.where(lane == i1, -jnp.inf, lg)
    m2 = jnp.max(lg2, axis=-1, keepdims=True)
    i2 = jnp.min(jnp.where(lg2 == m2, lane, LANES), axis=-1, keepdims=True)
    e = jnp.exp(m2 - m1)
    g1 = 1.0 / (1.0 + e)
    g2 = e / (1.0 + e)
    onehot = jnp.where((lane == i1) | (lane == i2), 1.0, 0.0)
    before = _dot(tri_ref[...], onehot.astype(BF16)) + carry[0:1, :]
    rank1 = jnp.sum(jnp.where(lane == i1, before, 0.0), axis=-1, keepdims=True)
    rank2 = jnp.sum(jnp.where(lane == i2, before, 0.0), axis=-1, keepdims=True)
    total = carry[0:1, :] + jnp.sum(onehot, axis=0, keepdims=True)
    carry[...] = jnp.broadcast_to(total, carry.shape)
    cnt_ref[...] = jnp.broadcast_to(total, cnt_ref.shape)
    out = jnp.where(lane == 0, i1.astype(F32), 0.0)
    out = jnp.where(lane == 1, i2.astype(F32), out)
    out = jnp.where(lane == 2, g1, out)
    out = jnp.where(lane == 3, g2, out)
    out = jnp.where(lane == 4, rank1, out)
    out = jnp.where(lane == 5, rank2, out)
    route_ref[...] = out


def _router(c, w_router):
    T = c.shape[0]
    tr = min(T, 512)
    wr = jnp.zeros((D_MODEL, LANES), BF16).at[:, :N_EXPERTS].set(w_router.astype(BF16))
    tri = jnp.asarray(np.tril(np.ones((tr, tr), np.float32), -1), dtype=BF16)
    return pl.pallas_call(
        _router_kernel,
        grid=(T // tr,),
        in_specs=[pl.BlockSpec((tr, c.shape[1]), lambda i: (i, 0)),
                  pl.BlockSpec((D_MODEL, LANES), lambda i: (0, 0)),
                  pl.BlockSpec((tr, tr), lambda i: (0, 0))],
        out_specs=[pl.BlockSpec((tr, LANES), lambda i: (i, 0)),
                   pl.BlockSpec((8, LANES), lambda i: (0, 0))],
        out_shape=[jax.ShapeDtypeStruct((T, LANES), F32),
                   jax.ShapeDtypeStruct((8, LANES), F32)],
        scratch_shapes=[pltpu.VMEM((8, LANES), F32)],
        compiler_params=_params(("arbitrary",)),
        name="moe_router",
    )(c, wr, tri)


def _dispatch_kernel(p1_ref, p2_ref, pad_ref, nv_ref, c_ref, xs_ref, zbuf, sem, zsem, *,
                     tm, seg_tile):
    i = pl.program_id(0)
    base = i * tm
    zr = zbuf.shape[0]

    @pl.when(i == 0)
    def _():
        zbuf[...] = jnp.zeros_like(zbuf)

        def zero_tile(start):
            first = pl.multiple_of(start, zr)
            copies = [pltpu.make_async_copy(zbuf, xs_ref.at[pl.ds(first + k * zr, zr)], zsem)
                      for k in range(seg_tile // zr)]
            for cp in copies:
                cp.start()
            for cp in copies:
                cp.wait()

        def zero_padding(e, _):
            @pl.when(pad_ref[e] >= 0)
            def _():
                zero_tile(pad_ref[e])
            return 0

        def zero_unused(t, _):
            zero_tile(t * seg_tile)
            return 0

        lax.fori_loop(0, N_EXPERTS, zero_padding, 0)
        lax.fori_loop(nv_ref[0], xs_ref.shape[0] // seg_tile, zero_unused, 0)

    def row_copy(r, dst):
        return pltpu.make_async_copy(c_ref.at[pl.ds(r, 1)], xs_ref.at[pl.ds(dst, 1)], sem)

    def issue(g, _):
        for u in range(ROW_UNROLL):
            r = g * ROW_UNROLL + u
            row_copy(r, p1_ref[base + r]).start(priority=0)
            row_copy(r, p2_ref[base + r]).start(priority=1)
        return 0

    lax.fori_loop(0, tm // ROW_UNROLL, issue, 0)

    def drain(g, _):
        for _u in range(2 * ROW_UNROLL):
            row_copy(0, 0).wait()
        return 0

    lax.fori_loop(0, tm // ROW_UNROLL, drain, 0)


def _dispatch(c, pos1, pos2, pad_tile_row, n_valid, n_rows, seg_tile):
    T, width = c.shape
    tm = min(T, 256)
    return pl.pallas_call(
        functools.partial(_dispatch_kernel, tm=tm, seg_tile=seg_tile),
        grid_spec=pltpu.PrefetchScalarGridSpec(
            num_scalar_prefetch=4,
            grid=(T // tm,),
            in_specs=[pl.BlockSpec((tm, width), lambda i, p1, p2, pt, nv: (i, 0))],
            out_specs=pl.BlockSpec(memory_space=pl.ANY),
            scratch_shapes=[pltpu.VMEM((min(seg_tile, 64), width), c.dtype),
                            pltpu.SemaphoreType.DMA(()),
                            pltpu.SemaphoreType.DMA(())],
        ),
        out_shape=jax.ShapeDtypeStruct((n_rows, width), c.dtype),
        compiler_params=_params(("arbitrary",)),
        name="moe_dispatch",
    )(pos1, pos2, pad_tile_row, n_valid, c)


def _combine_kernel(p1_ref, p2_ref, ys_ref, route_ref, h1_ref, p_ref, wg_ref, wp_ref,
                    g3_ref, gp_ref, o_ref, buf, sems, *, tm):
    i = pl.program_id(0)
    slot = i % 2

    def row_copy(src, s, k, r):
        return pltpu.make_async_copy(ys_ref.at[pl.ds(src, 1)], buf.at[s, k, pl.ds(r, 1)],
                                     sems.at[s])

    def gather_tile(tile, s):
        base = tile * tm

        def issue(g, _):
            for u in range(ROW_UNROLL):
                r = g * ROW_UNROLL + u
                row_copy(p1_ref[base + r], s, 0, r).start(priority=0)
                row_copy(p2_ref[base + r], s, 1, r).start(priority=1)
            return 0

        lax.fori_loop(0, tm // ROW_UNROLL, issue, 0)

    @pl.when(i == 0)
    def _():
        gather_tile(0, 0)

    @pl.when(i + 1 < pl.num_programs(0))
    def _():
        gather_tile(i + 1, 1 - slot)

    def drain(g, _):
        for _u in range(2 * ROW_UNROLL):
            row_copy(0, slot, 0, 0).wait()
        return 0

    lax.fori_loop(0, tm // ROW_UNROLL, drain, 0)
    route = route_ref[...]
    f = route[:, 2:3] * buf[slot, 0] + route[:, 3:4] * buf[slot, 1]
    o_ref[...] = _ffn_epilogue(f, h1_ref[...], p_ref[...], wg_ref[...], wp_ref[...],
                               g3_ref[...], gp_ref[...])


def _combine(ys, pos1, pos2, route, h1, p, wg, wp, g3, gp):
    T = h1.shape[0]
    tm = min(T, 256)
    row = lambda i, p1, p2: (i, 0)
    const = lambda i, p1, p2: (0, 0)
    return pl.pallas_call(
        functools.partial(_combine_kernel, tm=tm),
        grid_spec=pltpu.PrefetchScalarGridSpec(
            num_scalar_prefetch=2,
            grid=(T // tm,),
            in_specs=[pl.BlockSpec(memory_space=pl.ANY),
                      pl.BlockSpec((tm, LANES), row),
                      pl.BlockSpec((tm, D_MODEL), row),
                      pl.BlockSpec((tm, PLE_DIM), row),
                      pl.BlockSpec((D_MODEL, D_MODEL), const, pipeline_mode=pl.Buffered(1)),
                      pl.BlockSpec((PLE_DIM, D_MODEL), const, pipeline_mode=pl.Buffered(1)),
                      pl.BlockSpec((1, D_MODEL), const),
                      pl.BlockSpec((1, D_MODEL), const)],
            out_specs=pl.BlockSpec((tm, D_MODEL), row),
            scratch_shapes=[pltpu.VMEM((2, 2, tm, D_MODEL), F32),
                            pltpu.SemaphoreType.DMA((2,))],
        ),
        out_shape=jax.ShapeDtypeStruct((T, D_MODEL), F32),
        compiler_params=_params(("arbitrary",)),
        name="moe_combine",
    )(pos1, pos2, ys, route, h1, p, wg, wp, g3[None, :], gp[None, :])


def _swap_halves(w):
    half = w.shape[-1] // 2
    return jnp.concatenate([w[..., half:], w[..., :half]], axis=-1)


def _prep_w_in(w):
    k_rope = w[:, 640:704]
    pad64 = jnp.zeros((D_MODEL, 64), w.dtype)
    cols = [w[:, :640], k_rope, pad64, _swap_halves(k_rope), pad64,
            jnp.zeros((D_MODEL, HGRN_COL0 - 896), w.dtype), w[:, 704:]]
    return jnp.concatenate(cols, axis=1).astype(BF16)


def _prep_w_uq(w):
    w = w.reshape(Q_RANK, N_HEADS, QK_DIM)
    nope = w[:, :, :NOPE_DIM]
    pe = w[:, :, NOPE_DIM:]
    pad = jnp.zeros((Q_RANK, N_HEADS, LANES - ROPE_DIM), w.dtype)
    pe_pad = jnp.concatenate([pe, pad], axis=-1)
    pe_swap = jnp.concatenate([_swap_halves(pe), pad], axis=-1)
    parts = [x.reshape(Q_RANK, GROUP_WIDTH) for x in (nope, pe_pad, pe_swap)]
    return jnp.concatenate(parts, axis=1).astype(BF16)


def _prep_w_ukv(w):
    w = w.reshape(KV_RANK, N_HEADS, NOPE_DIM + V_DIM)
    k = w[:, :, :NOPE_DIM].reshape(KV_RANK, GROUP_WIDTH)
    v = w[:, :, NOPE_DIM:].reshape(KV_RANK, GROUP_WIDTH)
    return jnp.concatenate([k, v], axis=1).astype(BF16)


def _moe_plan(route, counts, tm, n_tiles):
    e1 = route[:, 0].astype(jnp.int32)
    e2 = route[:, 1].astype(jnp.int32)
    rank1 = route[:, 4].astype(jnp.int32)
    rank2 = route[:, 5].astype(jnp.int32)
    cnt = counts[0, :N_EXPERTS].astype(jnp.int32)
    padded = ((cnt + tm - 1) // tm) * tm
    seg_end = jnp.cumsum(padded)
    seg_start = seg_end - padded
    pos1 = seg_start[e1] + rank1
    pos2 = seg_start[e2] + rank2
    tile_start = jnp.arange(n_tiles, dtype=jnp.int32) * tm
    tile_expert = jnp.sum(tile_start[:, None] >= seg_end[None, :], axis=1).astype(jnp.int32)
    tile_expert = jnp.minimum(tile_expert, N_EXPERTS - 1)
    n_valid = (seg_end[-1] // tm).astype(jnp.int32)[None]
    pad_tile_row = jnp.where(padded > 0, seg_end - tm, -1).astype(jnp.int32)
    return pos1, pos2, tile_expert, n_valid, pad_tile_row


def kernel(x, p, positions, sandwich_norms, w_in, mla_q_norm, mla_kv_norm, w_uq, w_ukv,
           hgrn_lb_logits, hgrn_out_norm, w_out, ffn_w_gu, ffn_w_down, moe_w_router,
           moe_w_gu, moe_w_down, ple_w_proj, ple_w_gate, ple_norm):
    B, S, _ = x.shape
    T = B * S
    h = x.reshape(T, D_MODEL)
    ct, st = _rope_tables(positions)
    hgrn_consts = _hgrn_consts(HGRN_CHUNK)
    tm = min(T, 512)
    for l in range(DEPTH):
        z = _norm_matmul(h, sandwich_norms[l, 0], _prep_w_in(w_in[l]))
        q, k, v = _mla_proj(z, ct, st, mla_q_norm[l], mla_kv_norm[l],
                            _prep_w_uq(w_uq[l]), _prep_w_ukv(w_ukv[l]), B, S)
        o_mla = _attention(q, k, v).reshape(T, GROUP_WIDTH)
        o_hgrn = _hgrn(z, hgrn_lb_logits, hgrn_out_norm[l], hgrn_consts, l, B, S)
        moe = l % 2 == 1
        h1, c = _out_proj(o_mla, o_hgrn, h, w_out[l].astype(BF16),
                          sandwich_norms[l, 1], sandwich_norms[l, 2],
                          jnp.uint32 if moe else BF16)
        wg = ple_w_gate[l].astype(BF16)
        wp = ple_w_proj[l].astype(BF16)
        p_l = p[l].reshape(T, PLE_DIM)
        if not moe:
            def one_expert(rows):
                return jnp.zeros((T // rows,), jnp.int32), jnp.full((1,), T // rows, jnp.int32)

            tm_dense = min(T, 1024)
            act = _gate_up(c, ffn_w_gu[l // 2][None], *one_expert(tm_dense), tm_dense, 512)
            f = _down(act, ffn_w_down[l // 2][None].astype(BF16), *one_expert(tm), tm)
            h = _post_ffn(f, h1, p_l, wg, wp, sandwich_norms[l, 3], ple_norm[l])
        else:
            n_tiles = (2 * T) // tm + N_EXPERTS
            route, counts = _router(c, moe_w_router[l // 2])
            pos1, pos2, tile_expert, n_valid, pad_tile_row = _moe_plan(route, counts, tm, n_tiles)
            xs = _dispatch(c, pos1, pos2, pad_tile_row, n_valid, n_tiles * tm, tm)
            act = _gate_up(xs, moe_w_gu[l // 2], tile_expert, n_valid, tm, 1024)
            ys = _down(act, moe_w_down[l // 2].astype(BF16), tile_expert, n_valid, tm)
            h = _combine(ys, pos1, pos2, route, h1, p_l, wg, wp,
                         sandwich_norms[l, 3], ple_norm[l])
    return h.reshape(B, S, D_MODEL)
```

```python
import functools
import math

import numpy as np
import jax
import jax.numpy as jnp
from jax import lax
from jax.experimental import pallas as pl
from jax.experimental.pallas import tpu as pltpu

F32 = jnp.float32
BF16 = jnp.bfloat16

D_MODEL = 2048
DEPTH = 2
N_HEADS = 8
NOPE_DIM = 128
ROPE_DIM = 64
V_DIM = 128
QK_DIM = NOPE_DIM + ROPE_DIM
Q_RANK = 384
KV_RANK = 256
HGRN_K = 128
HGRN_V = 128
GROUP_WIDTH = N_HEADS * 128
FFN_DIM = 7168
N_EXPERTS = 8
PLE_DIM = 256
ROPE_THETA = 10000.0
LB_FLOOR = 1e-30
EPS = 1e-6

LANES = 128
QK_PAD = 256
Q_SCALE = QK_DIM ** -0.5 * math.log2(math.e)
Z_MLA = 1024
Z_WIDTH = 5120
HGRN_COL0 = 1024
HGRN_CHUNK = 64
HGRN_HEADS_PER_STEP = 4
ROW_UNROLL = 8
GU_SUB = 512
IN_PROJ_SUB = 1024
VMEM_LIMIT = 56 * 1024 * 1024


def _params(semantics, vmem=VMEM_LIMIT):
    return pltpu.CompilerParams(dimension_semantics=semantics, vmem_limit_bytes=vmem)


def _rms(x, gain_row):
    ms = jnp.mean(x * x, axis=-1, keepdims=True)
    return x * lax.rsqrt(ms + EPS) * gain_row


def _dot(a, b):
    return jnp.dot(a, b, preferred_element_type=F32)


def _dot_nt(a, b):
    return lax.dot_general(a, b, (((1,), (1,)), ((), ())), preferred_element_type=F32)


def _dot_tn(a, b):
    return lax.dot_general(a, b, (((0,), (0,)), ((), ())), preferred_element_type=F32)


def _sigmoid(x):
    return 1.0 / (1.0 + jnp.exp(-x))


def _pack_bf16_pairs(x):
    n = x.shape[1] // 2
    bits = pltpu.bitcast(x.astype(BF16).astype(F32), jnp.uint32)
    return (bits[:, :n] >> 16) | (bits[:, n:] & jnp.uint32(0xFFFF0000))


def _unpack_bf16_pairs(w):
    lo = pltpu.bitcast(w << 16, F32)
    hi = pltpu.bitcast(w & jnp.uint32(0xFFFF0000), F32)
    return jnp.concatenate([lo, hi], axis=1).astype(BF16)


def _fold_lanes(x, op):
    parts = [x[:, c:c + LANES] for c in range(0, x.shape[1], LANES)]
    while len(parts) > 1:
        parts = [op(a, b) for a, b in zip(parts[0::2], parts[1::2])] + parts[len(parts) & ~1:]
    return parts[0]


def _rope_kernel(pos_ref, inv_ref, sgn_ref, ct_ref, st_ref):
    ang = pos_ref[...].astype(F32) * inv_ref[...]
    keep = jnp.abs(sgn_ref[...])
    ct_ref[...] = jnp.cos(ang) * keep
    st_ref[...] = jnp.sin(ang) * sgn_ref[...]


def _rope_tables(positions):
    T = positions.size
    tt = min(T, 1024)
    half = ROPE_DIM // 2
    inv_freq = 1.0 / (ROPE_THETA ** (jnp.arange(0, ROPE_DIM, 2, dtype=F32) / ROPE_DIM))
    inv_row = jnp.concatenate([inv_freq, inv_freq, jnp.zeros((LANES - ROPE_DIM,), F32)])[None, :]
    sgn = np.zeros((1, LANES), np.float32)
    sgn[0, :half] = -1.0
    sgn[0, half:ROPE_DIM] = 1.0
    return pl.pallas_call(
        _rope_kernel,
        grid=(T // tt,),
        in_specs=[pl.BlockSpec((tt, 1), lambda i: (i, 0)),
                  pl.BlockSpec((1, LANES), lambda i: (0, 0)),
                  pl.BlockSpec((1, LANES), lambda i: (0, 0))],
        out_specs=[pl.BlockSpec((tt, LANES), lambda i: (i, 0))] * 2,
        out_shape=[jax.ShapeDtypeStruct((T, LANES), F32)] * 2,
        compiler_params=_params(("parallel",)),
        name="rope_tables",
    )(positions.reshape(T, 1), inv_row, jnp.asarray(sgn))


def _norm_matmul_kernel(x_ref, g_ref, w_ref, o_ref):
    a = _rms(x_ref[...], g_ref[...]).astype(BF16)
    for c in range(0, o_ref.shape[1], IN_PROJ_SUB):
        o_ref[:, c:c + IN_PROJ_SUB] = _dot(a, w_ref[:, c:c + IN_PROJ_SUB])


def _norm_matmul(x, gain, w):
    T, K = x.shape
    N = w.shape[1]
    tm = min(T, 256)
    return pl.pallas_call(
        _norm_matmul_kernel,
        grid=(T // tm,),
        in_specs=[pl.BlockSpec((tm, K), lambda i: (i, 0)),
                  pl.BlockSpec((1, K), lambda i: (0, 0)),
                  pl.BlockSpec((K, N), lambda i: (0, 0), pipeline_mode=pl.Buffered(1))],
        out_specs=pl.BlockSpec((tm, N), lambda i: (i, 0)),
        out_shape=jax.ShapeDtypeStruct((T, N), F32),
        compiler_params=_params(("parallel",)),
        name="in_proj",
    )(x, gain[None, :], w)


def _mla_proj_kernel(z_ref, ct_ref, st_ref, qn_ref, kvn_ref, wq_ref, wkv_ref,
                     q_ref, k_ref, v_ref):
    z = z_ref[...]
    ct = ct_ref[...]
    st = st_ref[...]
    aq = _rms(z[:, :Q_RANK], qn_ref[...]).astype(BF16)
    akv = _rms(z[:, Q_RANK:Q_RANK + KV_RANK], kvn_ref[...]).astype(BF16)
    k_a = z[:, 640:768]
    k_b = z[:, 768:896]
    k_pe = (k_a * ct + k_b * st).astype(BF16)
    q = _dot(aq, wq_ref[...])
    kv = _dot(akv, wkv_ref[...])
    for h in range(N_HEADS):
        lo, hi = h * LANES, (h + 1) * LANES
        q_ref[0, h, :, 0:LANES] = (q[:, lo:hi] * Q_SCALE).astype(BF16)
        q_pe = q[:, GROUP_WIDTH + lo:GROUP_WIDTH + hi] * ct \
            + q[:, 2 * GROUP_WIDTH + lo:2 * GROUP_WIDTH + hi] * st
        q_ref[0, h, :, LANES:QK_PAD] = (q_pe * Q_SCALE).astype(BF16)
        k_ref[0, h, :, 0:LANES] = kv[:, lo:hi].astype(BF16)
        k_ref[0, h, :, LANES:QK_PAD] = k_pe
        v_ref[0, h, :, :] = kv[:, GROUP_WIDTH + lo:GROUP_WIDTH + hi].astype(BF16)


def _mla_proj(z, ct, st, q_norm, kv_norm, wq, wkv, B, S):
    ts = min(S, 512)
    ns = S // ts
    tok = lambda b, i: (b * ns + i, 0)
    const = lambda b, i: (0, 0)
    head_out = lambda b, i: (b, 0, i, 0)
    return pl.pallas_call(
        _mla_proj_kernel,
        grid=(B, ns),
        in_specs=[pl.BlockSpec((ts, Z_MLA), tok),
                  pl.BlockSpec((ts, LANES), tok),
                  pl.BlockSpec((ts, LANES), tok),
                  pl.BlockSpec((1, Q_RANK), const),
                  pl.BlockSpec((1, KV_RANK), const),
                  pl.BlockSpec(wq.shape, const),
                  pl.BlockSpec(wkv.shape, const)],
        out_specs=[pl.BlockSpec((1, N_HEADS, ts, QK_PAD), head_out),
                   pl.BlockSpec((1, N_HEADS, ts, QK_PAD), head_out),
                   pl.BlockSpec((1, N_HEADS, ts, V_DIM), head_out)],
        out_shape=[jax.ShapeDtypeStruct((B, N_HEADS, S, QK_PAD), BF16),
                   jax.ShapeDtypeStruct((B, N_HEADS, S, QK_PAD), BF16),
                   jax.ShapeDtypeStruct((B, N_HEADS, S, V_DIM), BF16)],
        compiler_params=_params(("parallel", "parallel")),
        name="mla_proj",
    )(z, ct, st, q_norm[None, :], kv_norm[None, :], wq, wkv)


def _attn_kernel(q_ref, k_ref, v_ref, o_ref, s_a, s_b, *, tk):
    i = pl.program_id(2)
    top_rows = pl.ds(0, tk)
    bot_rows = pl.ds(tk, tk)

    def kv_rows(u):
        return pl.ds(pl.multiple_of(u * tk, tk), tk)

    def scores(u):
        return _dot_nt(q_ref[0, 0], k_ref[0, 0, kv_rows(u), :])

    lane = lax.broadcasted_iota(jnp.int32, (tk, LANES), 1)
    ones_col = jnp.where(lane == 0, 1.0, 0.0).astype(BF16)

    def values(u):
        return jnp.concatenate([v_ref[0, 0, kv_rows(u), :], ones_col], axis=1)

    def update(state, s, v, masked):
        m, acc = state
        if masked:
            row = lax.broadcasted_iota(jnp.int32, (tk, tk), 0)
            col = lax.broadcasted_iota(jnp.int32, (tk, tk), 1)
            s = jnp.where(col <= row, s, -jnp.inf)
        m_new = jnp.maximum(m, jnp.max(_fold_lanes(s, jnp.maximum), axis=-1, keepdims=True))
        alpha = jnp.exp2(m - m_new)
        p = jnp.exp2((s - m_new).astype(BF16))
        acc = alpha * acc + _dot(p, v)
        return m_new, acc

    def both(top, bot, s_ref, u):
        v = values(u)
        return (update(top, s_ref[top_rows, :], v, False),
                update(bot, s_ref[bot_rows, :], v, False))

    s_a[...] = scores(0)

    def body(t, state):
        top, bot = state
        s_b[...] = scores(2 * t + 1)
        top, bot = both(top, bot, s_a, 2 * t)
        s_a[...] = scores(2 * t + 2)
        top, bot = both(top, bot, s_b, 2 * t + 1)
        return top, bot

    init = (jnp.full((tk, 1), -jnp.inf, F32), jnp.zeros((tk, 2 * LANES), F32))
    top, bot = lax.fori_loop(0, i, body, (init, init))
    s_b[bot_rows, :] = _dot_nt(q_ref[0, 0, bot_rows, :], k_ref[0, 0, kv_rows(2 * i + 1), :])
    v0 = values(2 * i)
    top = update(top, s_a[top_rows, :], v0, True)
    bot = update(bot, s_a[bot_rows, :], v0, False)
    bot = update(bot, s_b[bot_rows, :], values(2 * i + 1), True)
    for rows, (_, acc) in ((top_rows, top), (bot_rows, bot)):
        o_ref[0, rows, :] = (acc[:, :V_DIM] / acc[:, V_DIM:V_DIM + 1]).astype(BF16)


def _attention(q, k, v):
    B, H, S, _ = q.shape
    tq = min(S, 1024)
    tk = tq // 2
    return pl.pallas_call(
        functools.partial(_attn_kernel, tk=tk),
        grid=(B, H, S // tq),
        in_specs=[pl.BlockSpec((1, 1, tq, QK_PAD), lambda b, h, i: (b, h, i, 0)),
                  pl.BlockSpec((1, 1, S, QK_PAD), lambda b, h, i: (b, h, 0, 0)),
                  pl.BlockSpec((1, 1, S, V_DIM), lambda b, h, i: (b, h, 0, 0))],
        out_specs=pl.BlockSpec((1, tq, V_DIM), lambda b, h, i: (b, i, h)),
        out_shape=jax.ShapeDtypeStruct((B, S, H * V_DIM), BF16),
        scratch_shapes=[pltpu.VMEM((tq, tk), F32), pltpu.VMEM((tq, tk), F32)],
        compiler_params=_params(("parallel", "parallel", "arbitrary")),
        name="mla_attention",
    )(q, k, v)


def _hgrn_consts(C):
    levels = [C >> (i + 1) for i in range(int(math.log2(C)))]
    t = np.arange(C)
    u = np.arange(C)
    mats = [(u[None, :] <= t[:, None]),
            (u[None, :] > t[:, None])]
    pair_masks = [np.eye(C, dtype=bool)]
    for m in levels:
        ref = (t // (2 * m)) * (2 * m) + m - 1
        hi = ((t // m) % 2) == 1
        rng_hi = (u[None, :] > ref[:, None]) & (u[None, :] <= t[:, None])
        rng_lo = (u[None, :] > t[:, None]) & (u[None, :] <= ref[:, None])
        mats.append(np.where(hi[:, None], rng_hi, rng_lo))
        same = (t[:, None] // (2 * m)) == (t[None, :] // (2 * m))
        pair_masks.append(hi[:, None] & (~hi[None, :]) & same)
    stack = np.concatenate(mats, axis=0).astype(np.float32)
    sums = jnp.asarray(np.concatenate([stack] * 3, axis=1), dtype=BF16)
    return sums, jnp.asarray(np.stack(pair_masks).astype(np.float32))


def _hgrn_kernel(zq_ref, zf_ref, zi_ref, zg_ref, lbl_ref, on_ref,
                 sums_ref, pm_ref, o_ref, st_scr, *, layer, ts):
    C = HGRN_CHUNK
    n_levels = pm_ref.shape[0] - 1

    @pl.when(pl.program_id(2) == 0)
    def _():
        st_scr[...] = jnp.zeros_like(st_scr)

    lg = lbl_ref[...]
    ex = jnp.exp(lg - jnp.max(lg, axis=0, keepdims=True))
    pr = ex / jnp.sum(ex, axis=0, keepdims=True)
    cum = pr[0:1, :]
    for r in range(1, layer + 1):
        cum = cum + pr[r:r + 1, :]
    lb = cum - pr[0:1, :]
    lb_floor = jnp.maximum(lb, LB_FLOOR)
    one_m_lb = 1.0 - lb
    gain = on_ref[...]

    for c in range(ts // C):
        rows = pl.ds(c * C, C)
        q_raw = zq_ref[rows, :]
        zf = zf_ref[rows, :]
        g_raw = zg_ref[rows, :]
        v_all = zi_ref[rows, :].astype(BF16)
        q_all = q_raw * _sigmoid(q_raw)
        e = jnp.exp(-jnp.abs(zf))
        r = 1.0 / (1.0 + e)
        er = e * r
        pos = zf >= 0
        log_f = jnp.log(lb_floor + one_m_lb * jnp.where(pos, r, er))
        kk_all = one_m_lb * jnp.where(pos, er, r)
        gate_all = g_raw * _sigmoid(g_raw)
        p0 = log_f.astype(BF16)
        r1 = log_f - p0.astype(F32)
        p1 = r1.astype(BF16)
        p2 = (r1 - p1.astype(F32)).astype(BF16)
        sums_all = _dot(sums_ref[...], jnp.concatenate([p0, p1, p2], axis=0))
        for hh in range(HGRN_HEADS_PER_STEP):
            lanes = slice(hh * LANES, (hh + 1) * LANES)
            q, kk, v, sums = q_all[:, lanes], kk_all[:, lanes], v_all[:, lanes], sums_all[:, lanes]
            b = sums[0:C]
            b_rest = sums[C:2 * C]
            a = pm_ref[0] * _dot_nt(q.astype(BF16), kk.astype(BF16))
            for lv in range(n_levels):
                dec = jnp.exp(sums[(2 + lv) * C:(3 + lv) * C])
                qt = (q * dec).astype(BF16)
                kt = (kk * dec).astype(BF16)
                a = a + pm_ref[lv + 1] * _dot_nt(qt, kt)
            st = st_scr[hh]
            o = _dot(a.astype(BF16), v) \
                + _dot_nt((q * jnp.exp(b)).astype(BF16), st.astype(BF16))
            k_end = (kk * jnp.exp(b_rest)).astype(BF16)
            st_scr[hh] = st * jnp.exp(b[C - 1:C, :]) + _dot_tn(v, k_end)
            o_ref[rows, pl.ds(hh * LANES, LANES)] = \
                (_rms(o, gain) * gate_all[:, lanes]).astype(BF16)


def _hgrn(z, lb_logits, out_norm, consts, layer, B, S):
    T = B * S
    ts = min(S, 256)
    ns = S // ts
    sums, pair_masks = consts
    hp = HGRN_HEADS_PER_STEP
    width = hp * LANES
    col0 = HGRN_COL0 // width
    groups = N_HEADS // hp

    def zspec(part):
        return pl.BlockSpec((ts, width),
                            lambda b, h, i: (b * ns + i, col0 + part * groups + h))

    full = lambda arr: pl.BlockSpec(arr.shape, lambda b, h, i: (0,) * arr.ndim)
    kern = functools.partial(_hgrn_kernel, layer=layer, ts=ts)
    return pl.pallas_call(
        kern,
        grid=(B, groups, ns),
        in_specs=[zspec(0), zspec(1), zspec(2), zspec(3),
                  pl.BlockSpec((DEPTH, width), lambda b, h, i: (0, h)),
                  pl.BlockSpec((1, HGRN_V), lambda b, h, i: (0, 0)),
                  full(sums), full(pair_masks)],
        out_specs=pl.BlockSpec((ts, width), lambda b, h, i: (b * ns + i, h)),
        out_shape=jax.ShapeDtypeStruct((T, GROUP_WIDTH), BF16),
        scratch_shapes=[pltpu.VMEM((hp, HGRN_V, HGRN_K), F32)],
        compiler_params=_params(("parallel", "parallel", "arbitrary")),
        name="hgrn2",
    )(z, z, z, z, lb_logits, out_norm[None, :], sums, pair_masks)


def _out_proj_kernel(om_ref, oh_ref, h_ref, w_ref, g1_ref, g2_ref, h1_ref, c_ref):
    y = _dot(om_ref[...], w_ref[0:GROUP_WIDTH, :]) + _dot(oh_ref[...], w_ref[GROUP_WIDTH:, :])
    h1 = h_ref[...] + _rms(y, g1_ref[...])
    h1_ref[...] = h1
    c = _rms(h1, g2_ref[...])
    if c_ref.dtype == jnp.uint32:
        c_ref[...] = _pack_bf16_pairs(c)
    else:
        c_ref[...] = c.astype(c_ref.dtype)


def _out_proj(om, oh, h, w, g1, g2, c_dtype):
    T = h.shape[0]
    tm = min(T, 512)
    c_width = D_MODEL // 2 if c_dtype == jnp.uint32 else D_MODEL
    row = lambda i: (i, 0)
    const = lambda i: (0, 0)
    return pl.pallas_call(
        _out_proj_kernel,
        grid=(T // tm,),
        in_specs=[pl.BlockSpec((tm, GROUP_WIDTH), row),
                  pl.BlockSpec((tm, GROUP_WIDTH), row),
                  pl.BlockSpec((tm, D_MODEL), row),
                  pl.BlockSpec((D_MODEL, D_MODEL), const, pipeline_mode=pl.Buffered(1)),
                  pl.BlockSpec((1, D_MODEL), const),
                  pl.BlockSpec((1, D_MODEL), const)],
        out_specs=[pl.BlockSpec((tm, D_MODEL), row), pl.BlockSpec((tm, c_width), row)],
        out_shape=[jax.ShapeDtypeStruct((T, D_MODEL), F32),
                   jax.ShapeDtypeStruct((T, c_width), c_dtype)],
        compiler_params=_params(("parallel",)),
        name="out_proj",
    )(om, oh, h, w, g1[None, :], g2[None, :])


def _gate_up_kernel(te_ref, nv_ref, x_ref, wg_ref, wu_ref, o_ref, wg_bf, wu_bf):
    i = pl.program_id(1)
    valid = i < nv_ref[0]
    new_weights = jnp.logical_or(i == 0, te_ref[i] != te_ref[jnp.maximum(i - 1, 0)])

    def step(refresh):
        if x_ref.dtype == jnp.uint32:
            x = _unpack_bf16_pairs(x_ref[...])
        else:
            x = x_ref[...]
        for c in range(0, o_ref.shape[1], GU_SUB):
            cols = slice(c, c + GU_SUB)
            if refresh:
                wg_bf[:, cols] = wg_ref[0, :, cols].astype(BF16)
                wu_bf[:, cols] = wu_ref[0, :, cols].astype(BF16)
            g = _dot(x, wg_bf[:, cols])
            u = _dot(x, wu_bf[:, cols])
            o_ref[:, cols] = (g * _sigmoid(g) * u).astype(BF16)

    pl.when(jnp.logical_and(valid, new_weights))(functools.partial(step, True))
    pl.when(jnp.logical_and(valid, jnp.logical_not(new_weights)))(functools.partial(step, False))

    @pl.when(jnp.logical_not(valid))
    def _():
        o_ref[...] = jnp.zeros_like(o_ref)


def _gate_up(x, w_gu, tile_expert, n_valid, tm, tf):
    R, xw = x.shape
    n_tiles = R // tm
    nf = FFN_DIM // tf

    def tile(i, nv):
        return jnp.minimum(i, nv[0] - 1)

    return pl.pallas_call(
        _gate_up_kernel,
        grid_spec=pltpu.PrefetchScalarGridSpec(
            num_scalar_prefetch=2,
            grid=(nf, n_tiles),
            in_specs=[pl.BlockSpec((tm, xw), lambda j, i, te, nv: (tile(i, nv), 0)),
                      pl.BlockSpec((1, D_MODEL, tf), lambda j, i, te, nv: (te[tile(i, nv)], 0, j)),
                      pl.BlockSpec((1, D_MODEL, tf),
                                   lambda j, i, te, nv: (te[tile(i, nv)], 0, j + nf))],
            out_specs=pl.BlockSpec((tm, tf), lambda j, i, te, nv: (i, j)),
            scratch_shapes=[pltpu.VMEM((D_MODEL, tf), BF16), pltpu.VMEM((D_MODEL, tf), BF16)],
        ),
        out_shape=jax.ShapeDtypeStruct((R, FFN_DIM), BF16),
        compiler_params=_params(("arbitrary", "arbitrary")),
        name="ffn_gate_up",
    )(tile_expert, n_valid, x, w_gu, w_gu)


def _down_kernel(te_ref, nv_ref, a_ref, w_ref, o_ref):
    valid = pl.program_id(1) < nv_ref[0]

    @pl.when(valid)
    def _():
        a = a_ref[...]
        for c in range(0, o_ref.shape[1], GU_SUB):
            o_ref[:, c:c + GU_SUB] = _dot(a, w_ref[0, :, c:c + GU_SUB])

    @pl.when(jnp.logical_not(valid))
    def _():
        o_ref[...] = jnp.zeros_like(o_ref)


def _down(act, w_down, tile_expert, n_valid, tm):
    R = act.shape[0]
    n_tiles = R // tm
    tn = 1024

    def tile(i, nv):
        return jnp.minimum(i, nv[0] - 1)

    return pl.pallas_call(
        _down_kernel,
        grid_spec=pltpu.PrefetchScalarGridSpec(
            num_scalar_prefetch=2,
            grid=(D_MODEL // tn, n_tiles),
            in_specs=[pl.BlockSpec((tm, FFN_DIM), lambda j, i, te, nv: (tile(i, nv), 0)),
                      pl.BlockSpec((1, FFN_DIM, tn), lambda j, i, te, nv: (te[tile(i, nv)], 0, j))],
            out_specs=pl.BlockSpec((tm, tn), lambda j, i, te, nv: (i, j)),
        ),
        out_shape=jax.ShapeDtypeStruct((R, D_MODEL), F32),
        compiler_params=_params(("arbitrary", "arbitrary")),
        name="ffn_down",
    )(tile_expert, n_valid, act, w_down)


def _ffn_epilogue(f, h1, p, wg, wp, g3, gp):
    h2 = h1 + _rms(f, g3)
    gate = _sigmoid(_dot(h2.astype(BF16), wg))
    proj = _dot(p.astype(BF16), wp)
    return h2 + _rms(gate * proj, gp)


def _post_ffn_kernel(f_ref, h1_ref, p_ref, wg_ref, wp_ref, g3_ref, gp_ref, o_ref):
    o_ref[...] = _ffn_epilogue(f_ref[...], h1_ref[...], p_ref[...], wg_ref[...], wp_ref[...],
                               g3_ref[...], gp_ref[...])


def _post_ffn(f, h1, p, layer, wg, wp, g3, gp):
    T = h1.shape[0]
    tm = min(T, 512)
    row = lambda i: (i, 0)
    const = lambda i: (0, 0)
    return pl.pallas_call(
        _post_ffn_kernel,
        grid=(T // tm,),
        in_specs=[pl.BlockSpec((tm, D_MODEL), row),
                  pl.BlockSpec((tm, D_MODEL), row),
                  pl.BlockSpec((None, tm, PLE_DIM), lambda i: (layer, i, 0)),
                  pl.BlockSpec((D_MODEL, D_MODEL), const, pipeline_mode=pl.Buffered(1)),
                  pl.BlockSpec((PLE_DIM, D_MODEL), const, pipeline_mode=pl.Buffered(1)),
                  pl.BlockSpec((1, D_MODEL), const),
                  pl.BlockSpec((1, D_MODEL), const)],
        out_specs=pl.BlockSpec((tm, D_MODEL), row),
        out_shape=jax.ShapeDtypeStruct((T, D_MODEL), F32),
        compiler_params=_params(("parallel",)),
        name="post_ffn",
    )(f, h1, p, wg, wp, g3[None, :], gp[None, :])


def _router_kernel(c_ref, wr_ref, tri_ref, route_ref, cnt_ref, carry):
    @pl.when(pl.program_id(0) == 0)
    def _():
        carry[...] = jnp.zeros_like(carry)

    logits = _dot(_unpack_bf16_pairs(c_ref[...]), wr_ref[...])
    lane = lax.broadcasted_iota(jnp.int32, logits.shape, 1)
    lg = jnp.where(lane < N_EXPERTS, logits, -jnp.inf)
    m1 = jnp.max(lg, axis=-1, keepdims=True)
    i1 = jnp.min(jnp.where(lg == m1, lane, LANES), axis=-1, keepdims=True)
    lg2 = jnp.where(lane == i1, -jnp.inf, lg)
    m2 = jnp.max(lg2, axis=-1, keepdims=True)
    i2 = jnp.min(jnp.where(lg2 == m2, lane, LANES), axis=-1, keepdims=True)
    e = jnp.exp(m2 - m1)
    g1 = 1.0 / (1.0 + e)
    g2 = e / (1.0 + e)
    onehot = jnp.where((lane == i1) | (lane == i2), 1.0, 0.0)
    before = _dot(tri_ref[...], onehot.astype(BF16)) + carry[0:1, :]
    rank1 = jnp.sum(jnp.where(lane == i1, before, 0.0), axis=-1, keepdims=True)
    rank2 = jnp.sum(jnp.where(lane == i2, before, 0.0), axis=-1, keepdims=True)
    total = carry[0:1, :] + jnp.sum(onehot, axis=0, keepdims=True)
    carry[...] = jnp.broadcast_to(total, carry.shape)
    cnt_ref[...] = jnp.broadcast_to(total, cnt_ref.shape)
    out = jnp.where(lane == 0, i1.astype(F32), 0.0)
    out = jnp.where(lane == 1, i2.astype(F32), out)
    out = jnp.where(lane == 2, g1, out)
    out = jnp.where(lane == 3, g2, out)
    out = jnp.where(lane == 4, rank1, out)
    out = jnp.where(lane == 5, rank2, out)
    route_ref[...] = out


def _router(c, w_router):
    T = c.shape[0]
    tr = min(T, 512)
    wr = jnp.zeros((D_MODEL, LANES), BF16).at[:, :N_EXPERTS].set(w_router.astype(BF16))
    tri = jnp.asarray(np.tril(np.ones((tr, tr), np.float32), -1), dtype=BF16)
    return pl.pallas_call(
        _router_kernel,
        grid=(T // tr,),
        in_specs=[pl.BlockSpec((tr, c.shape[1]), lambda i: (i, 0)),
                  pl.BlockSpec((D_MODEL, LANES), lambda i: (0, 0)),
                  pl.BlockSpec((tr, tr), lambda i: (0, 0))],
        out_specs=[pl.BlockSpec((tr, LANES), lambda i: (i, 0)),
                   pl.BlockSpec((8, LANES), lambda i: (0, 0))],
        out_shape=[jax.ShapeDtypeStruct((T, LANES), F32),
                   jax.ShapeDtypeStruct((8, LANES), F32)],
        scratch_shapes=[pltpu.VMEM((8, LANES), F32)],
        compiler_params=_params(("arbitrary",)),
        name="moe_router",
    )(c, wr, tri)


def _dispatch_kernel(p1_ref, p2_ref, pad_ref, nv_ref, c_ref, xs_ref, zbuf, sem, zsem, *,
                     tm, seg_tile):
    i = pl.program_id(0)
    base = i * tm
    zr = zbuf.shape[0]

    @pl.when(i == 0)
    def _():
        zbuf[...] = jnp.zeros_like(zbuf)

        def zero_tile(start):
            first = pl.multiple_of(start, zr)
            copies = [pltpu.make_async_copy(zbuf, xs_ref.at[pl.ds(first + k * zr, zr)], zsem)
                      for k in range(seg_tile // zr)]
            for cp in copies:
                cp.start()
            for cp in copies:
                cp.wait()

        def zero_padding(e, _):
            @pl.when(pad_ref[e] >= 0)
            def _():
                zero_tile(pad_ref[e])
            return 0

        def zero_unused(t, _):
            zero_tile(t * seg_tile)
            return 0

        lax.fori_loop(0, N_EXPERTS, zero_padding, 0)
        lax.fori_loop(nv_ref[0], xs_ref.shape[0] // seg_tile, zero_unused, 0)

    def row_copy(r, dst):
        return pltpu.make_async_copy(c_ref.at[pl.ds(r, 1)], xs_ref.at[pl.ds(dst, 1)], sem)

    def issue(g, _):
        for u in range(ROW_UNROLL):
            r = g * ROW_UNROLL + u
            row_copy(r, p1_ref[base + r]).start(priority=0)
            row_copy(r, p2_ref[base + r]).start(priority=1)
        return 0

    lax.fori_loop(0, tm // ROW_UNROLL, issue, 0)

    def drain(g, _):
        for _u in range(2 * ROW_UNROLL):
            row_copy(0, 0).wait()
        return 0

    lax.fori_loop(0, tm // ROW_UNROLL, drain, 0)


def _dispatch(c, pos1, pos2, pad_tile_row, n_valid, n_rows, seg_tile):
    T, width = c.shape
    tm = min(T, 256)
    return pl.pallas_call(
        functools.partial(_dispatch_kernel, tm=tm, seg_tile=seg_tile),
        grid_spec=pltpu.PrefetchScalarGridSpec(
            num_scalar_prefetch=4,
            grid=(T // tm,),
            in_specs=[pl.BlockSpec((tm, width), lambda i, p1, p2, pt, nv: (i, 0))],
            out_specs=pl.BlockSpec(memory_space=pl.ANY),
            scratch_shapes=[pltpu.VMEM((min(seg_tile, 64), width), c.dtype),
                            pltpu.SemaphoreType.DMA(()),
                            pltpu.SemaphoreType.DMA(())],
        ),
        out_shape=jax.ShapeDtypeStruct((n_rows, width), c.dtype),
        compiler_params=_params(("arbitrary",)),
        name="moe_dispatch",
    )(pos1, pos2, pad_tile_row, n_valid, c)


def _combine_kernel(p1_ref, p2_ref, ys_ref, route_ref, h1_ref, p_ref, wg_ref, wp_ref,
                    g3_ref, gp_ref, o_ref, buf, sems, *, tm):
    i = pl.program_id(0)
    slot = i % 2

    def row_copy(src, s, k, r):
        return pltpu.make_async_copy(ys_ref.at[pl.ds(src, 1)], buf.at[s, k, pl.ds(r, 1)],
                                     sems.at[s])

    def gather_tile(tile, s):
        base = tile * tm

        def issue(g, _):
            for u in range(ROW_UNROLL):
                r = g * ROW_UNROLL + u
                row_copy(p1_ref[base + r], s, 0, r).start(priority=0)
                row_copy(p2_ref[base + r], s, 1, r).start(priority=1)
            return 0

        lax.fori_loop(0, tm // ROW_UNROLL, issue, 0)

    @pl.when(i == 0)
    def _():
        gather_tile(0, 0)

    @pl.when(i + 1 < pl.num_programs(0))
    def _():
        gather_tile(i + 1, 1 - slot)

    def drain(g, _):
        for _u in range(2 * ROW_UNROLL):
            row_copy(0, slot, 0, 0).wait()
        return 0

    lax.fori_loop(0, tm // ROW_UNROLL, drain, 0)
    route = route_ref[...]
    f = route[:, 2:3] * buf[slot, 0] + route[:, 3:4] * buf[slot, 1]
    o_ref[...] = _ffn_epilogue(f, h1_ref[...], p_ref[...], wg_ref[...], wp_ref[...],
                               g3_ref[...], gp_ref[...])


def _combine(ys, pos1, pos2, route, h1, p, layer, wg, wp, g3, gp):
    T = h1.shape[0]
    tm = min(T, 256)
    row = lambda i, p1, p2: (i, 0)
    const = lambda i, p1, p2: (0, 0)
    return pl.pallas_call(
        functools.partial(_combine_kernel, tm=tm),
        grid_spec=pltpu.PrefetchScalarGridSpec(
            num_scalar_prefetch=2,
            grid=(T // tm,),
            in_specs=[pl.BlockSpec(memory_space=pl.ANY),
                      pl.BlockSpec((tm, LANES), row),
                      pl.BlockSpec((tm, D_MODEL), row),
                      pl.BlockSpec((None, tm, PLE_DIM), lambda i, p1, p2: (layer, i, 0)),
                      pl.BlockSpec((D_MODEL, D_MODEL), const, pipeline_mode=pl.Buffered(1)),
                      pl.BlockSpec((PLE_DIM, D_MODEL), const, pipeline_mode=pl.Buffered(1)),
                      pl.BlockSpec((1, D_MODEL), const),
                      pl.BlockSpec((1, D_MODEL), const)],
            out_specs=pl.BlockSpec((tm, D_MODEL), row),
            scratch_shapes=[pltpu.VMEM((2, 2, tm, D_MODEL), F32),
                            pltpu.SemaphoreType.DMA((2,))],
        ),
        out_shape=jax.ShapeDtypeStruct((T, D_MODEL), F32),
        compiler_params=_params(("arbitrary",)),
        name="moe_combine",
    )(pos1, pos2, ys, route, h1, p, wg, wp, g3[None, :], gp[None, :])


def _swap_halves(w):
    half = w.shape[-1] // 2
    return jnp.concatenate([w[..., half:], w[..., :half]], axis=-1)


def _prep_w_in(w):
    w = w.astype(BF16)
    k_rope = w[:, 640:704]
    pad64 = jnp.zeros((D_MODEL, 64), w.dtype)
    cols = [w[:, :640], k_rope, pad64, _swap_halves(k_rope), pad64,
            jnp.zeros((D_MODEL, HGRN_COL0 - 896), w.dtype), w[:, 704:]]
    return jnp.concatenate(cols, axis=1).astype(BF16)


def _prep_w_uq(w):
    w = w.reshape(Q_RANK, N_HEADS, QK_DIM)
    nope = w[:, :, :NOPE_DIM]
    pe = w[:, :, NOPE_DIM:]
    pad = jnp.zeros((Q_RANK, N_HEADS, LANES - ROPE_DIM), w.dtype)
    pe_pad = jnp.concatenate([pe, pad], axis=-1)
    pe_swap = jnp.concatenate([_swap_halves(pe), pad], axis=-1)
    parts = [x.reshape(Q_RANK, GROUP_WIDTH) for x in (nope, pe_pad, pe_swap)]
    return jnp.concatenate(parts, axis=1).astype(BF16)


def _prep_w_ukv(w):
    w = w.reshape(KV_RANK, N_HEADS, NOPE_DIM + V_DIM)
    k = w[:, :, :NOPE_DIM].reshape(KV_RANK, GROUP_WIDTH)
    v = w[:, :, NOPE_DIM:].reshape(KV_RANK, GROUP_WIDTH)
    return jnp.concatenate([k, v], axis=1).astype(BF16)


def _moe_plan(route, counts, tm, n_tiles):
    e1 = route[:, 0].astype(jnp.int32)
    e2 = route[:, 1].astype(jnp.int32)
    rank1 = route[:, 4].astype(jnp.int32)
    rank2 = route[:, 5].astype(jnp.int32)
    cnt = counts[0, :N_EXPERTS].astype(jnp.int32)
    padded = ((cnt + tm - 1) // tm) * tm
    seg_end = jnp.cumsum(padded)
    seg_start = seg_end - padded
    pos1 = seg_start[e1] + rank1
    pos2 = seg_start[e2] + rank2
    tile_start = jnp.arange(n_tiles, dtype=jnp.int32) * tm
    tile_expert = jnp.sum(tile_start[:, None] >= seg_end[None, :], axis=1).astype(jnp.int32)
    tile_expert = jnp.minimum(tile_expert, N_EXPERTS - 1)
    n_valid = (seg_end[-1] // tm).astype(jnp.int32)[None]
    pad_tile_row = jnp.where(padded > 0, seg_end - tm, -1).astype(jnp.int32)
    return pos1, pos2, tile_expert, n_valid, pad_tile_row


def kernel(x, p, positions, sandwich_norms, w_in, mla_q_norm, mla_kv_norm, w_uq, w_ukv,
           hgrn_lb_logits, hgrn_out_norm, w_out, ffn_w_gu, ffn_w_down, moe_w_router,
           moe_w_gu, moe_w_down, ple_w_proj, ple_w_gate, ple_norm):
    B, S, _ = x.shape
    T = B * S
    h = x.reshape(T, D_MODEL)
    ct, st = _rope_tables(positions)
    hgrn_consts = _hgrn_consts(HGRN_CHUNK)
    tm = min(T, 512)
    for l in range(DEPTH):
        z = _norm_matmul(h, sandwich_norms[l, 0], _prep_w_in(w_in[l]))
        q, k, v = _mla_proj(z, ct, st, mla_q_norm[l], mla_kv_norm[l],
                            _prep_w_uq(w_uq[l]), _prep_w_ukv(w_ukv[l]), B, S)
        o_mla = _attention(q, k, v).reshape(T, GROUP_WIDTH)
        o_hgrn = _hgrn(z, hgrn_lb_logits, hgrn_out_norm[l], hgrn_consts, l, B, S)
        moe = l % 2 == 1
        h1, c = _out_proj(o_mla, o_hgrn, h, w_out[l].astype(BF16),
                          sandwich_norms[l, 1], sandwich_norms[l, 2],
                          jnp.uint32 if moe else BF16)
        wg = ple_w_gate[l].astype(BF16)
        wp = ple_w_proj[l].astype(BF16)
        p_all = p.reshape(DEPTH, T, PLE_DIM)
        if not moe:
            def one_expert(rows):
                return jnp.zeros((T // rows,), jnp.int32), jnp.full((1,), T // rows, jnp.int32)

            tm_dense = min(T, 1024)
            act = _gate_up(c, ffn_w_gu[l // 2][None], *one_expert(tm_dense), tm_dense, 512)
            f = _down(act, ffn_w_down[l // 2][None].astype(BF16), *one_expert(tm), tm)
            h = _post_ffn(f, h1, p_all, l, wg, wp, sandwich_norms[l, 3], ple_norm[l])
        else:
            n_tiles = (2 * T) // tm + N_EXPERTS
            route, counts = _router(c, moe_w_router[l // 2])
            pos1, pos2, tile_expert, n_valid, pad_tile_row = _moe_plan(route, counts, tm, n_tiles)
            xs = _dispatch(c, pos1, pos2, pad_tile_row, n_valid, n_tiles * tm, tm)
            act = _gate_up(xs, moe_w_gu[l // 2], tile_expert, n_valid, tm, 1024)
            ys = _down(act, moe_w_down[l // 2].astype(BF16), tile_expert, n_valid, tm)
            h = _combine(ys, pos1, pos2, route, h1, p_all, l, wg, wp,
                         sandwich_norms[l, 3], ple_norm[l])
    return h.reshape(B, S, D_MODEL)
```

```python
import functools
import math

import numpy as np
import jax
import jax.numpy as jnp
from jax import lax
from jax.experimental import pallas as pl
from jax.experimental.pallas import tpu as pltpu

F32 = jnp.float32
BF16 = jnp.bfloat16

D_MODEL = 2048
DEPTH = 2
N_HEADS = 8
NOPE_DIM = 128
ROPE_DIM = 64
V_DIM = 128
QK_DIM = NOPE_DIM + ROPE_DIM
Q_RANK = 384
KV_RANK = 256
HGRN_K = 128
HGRN_V = 128
GROUP_WIDTH = N_HEADS * 128
FFN_DIM = 7168
N_EXPERTS = 8
PLE_DIM = 256
ROPE_THETA = 10000.0
LB_FLOOR = 1e-30
EPS = 1e-6

LANES = 128
QK_PAD = 256
Q_SCALE = QK_DIM ** -0.5 * math.log2(math.e)
Z_MLA = 1024
Z_WIDTH = 5120
HGRN_COL0 = 1024
HGRN_CHUNK = 64
HGRN_HEADS_PER_STEP = 4
ROW_UNROLL = 8
GU_SUB = 512
IN_PROJ_SUB = 1024
CAST_ROWS = 896
VMEM_LIMIT = 56 * 1024 * 1024


def _params(semantics, vmem=VMEM_LIMIT):
    return pltpu.CompilerParams(dimension_semantics=semantics, vmem_limit_bytes=vmem)


def _rms(x, gain_row):
    ms = jnp.mean(x * x, axis=-1, keepdims=True)
    return x * lax.rsqrt(ms + EPS) * gain_row


def _dot(a, b):
    return jnp.dot(a, b, preferred_element_type=F32)


def _dot_nt(a, b):
    return lax.dot_general(a, b, (((1,), (1,)), ((), ())), preferred_element_type=F32)


def _dot_tn(a, b):
    return lax.dot_general(a, b, (((0,), (0,)), ((), ())), preferred_element_type=F32)


def _sigmoid(x):
    return 1.0 / (1.0 + jnp.exp(-x))


def _pack_bf16_pairs(x):
    n = x.shape[1] // 2
    bits = pltpu.bitcast(x.astype(BF16).astype(F32), jnp.uint32)
    return (bits[:, :n] >> 16) | (bits[:, n:] & jnp.uint32(0xFFFF0000))


def _unpack_bf16_pairs(w):
    lo = pltpu.bitcast(w << 16, F32)
    hi = pltpu.bitcast(w & jnp.uint32(0xFFFF0000), F32)
    return jnp.concatenate([lo, hi], axis=1).astype(BF16)


def _fold_lanes(x, op):
    parts = [x[:, c:c + LANES] for c in range(0, x.shape[1], LANES)]
    while len(parts) > 1:
        parts = [op(a, b) for a, b in zip(parts[0::2], parts[1::2])] + parts[len(parts) & ~1:]
    return parts[0]


def _rope_kernel(pos_ref, inv_ref, sgn_ref, ct_ref, st_ref):
    ang = pos_ref[...].astype(F32) * inv_ref[...]
    keep = jnp.abs(sgn_ref[...])
    ct_ref[...] = jnp.cos(ang) * keep
    st_ref[...] = jnp.sin(ang) * sgn_ref[...]


def _rope_tables(positions):
    T = positions.size
    tt = min(T, 1024)
    half = ROPE_DIM // 2
    inv_freq = 1.0 / (ROPE_THETA ** (jnp.arange(0, ROPE_DIM, 2, dtype=F32) / ROPE_DIM))
    inv_row = jnp.concatenate([inv_freq, inv_freq, jnp.zeros((LANES - ROPE_DIM,), F32)])[None, :]
    sgn = np.zeros((1, LANES), np.float32)
    sgn[0, :half] = -1.0
    sgn[0, half:ROPE_DIM] = 1.0
    return pl.pallas_call(
        _rope_kernel,
        grid=(T // tt,),
        in_specs=[pl.BlockSpec((tt, 1), lambda i: (i, 0)),
                  pl.BlockSpec((1, LANES), lambda i: (0, 0)),
                  pl.BlockSpec((1, LANES), lambda i: (0, 0))],
        out_specs=[pl.BlockSpec((tt, LANES), lambda i: (i, 0))] * 2,
        out_shape=[jax.ShapeDtypeStruct((T, LANES), F32)] * 2,
        compiler_params=_params(("parallel",)),
        name="rope_tables",
    )(positions.reshape(T, 1), inv_row, jnp.asarray(sgn))


def _norm_matmul_kernel(x_ref, g_ref, w_ref, o_ref):
    a = _rms(x_ref[...], g_ref[...]).astype(BF16)
    for c in range(0, o_ref.shape[1], IN_PROJ_SUB):
        o_ref[:, c:c + IN_PROJ_SUB] = _dot(a, w_ref[:, c:c + IN_PROJ_SUB])


def _norm_matmul(x, gain, w):
    T, K = x.shape
    N = w.shape[1]
    tm = min(T, 256)
    return pl.pallas_call(
        _norm_matmul_kernel,
        grid=(T // tm,),
        in_specs=[pl.BlockSpec((tm, K), lambda i: (i, 0)),
                  pl.BlockSpec((1, K), lambda i: (0, 0)),
                  pl.BlockSpec((K, N), lambda i: (0, 0), pipeline_mode=pl.Buffered(1))],
        out_specs=pl.BlockSpec((tm, N), lambda i: (i, 0)),
        out_shape=jax.ShapeDtypeStruct((T, N), F32),
        compiler_params=_params(("parallel",)),
        name="in_proj",
    )(x, gain[None, :], w)


def _mla_proj_kernel(z_ref, ct_ref, st_ref, qn_ref, kvn_ref, wq_ref, wkv_ref,
                     q_ref, k_ref, v_ref):
    z = z_ref[...]
    ct = ct_ref[...]
    st = st_ref[...]
    aq = _rms(z[:, :Q_RANK], qn_ref[...]).astype(BF16)
    akv = _rms(z[:, Q_RANK:Q_RANK + KV_RANK], kvn_ref[...]).astype(BF16)
    k_a = z[:, 640:768]
    k_b = z[:, 768:896]
    k_pe = (k_a * ct + k_b * st).astype(BF16)
    q = _dot(aq, wq_ref[...])
    kv = _dot(akv, wkv_ref[...])
    for h in range(N_HEADS):
        lo, hi = h * LANES, (h + 1) * LANES
        q_ref[0, h, :, 0:LANES] = (q[:, lo:hi] * Q_SCALE).astype(BF16)
        q_pe = q[:, GROUP_WIDTH + lo:GROUP_WIDTH + hi] * ct \
            + q[:, 2 * GROUP_WIDTH + lo:2 * GROUP_WIDTH + hi] * st
        q_ref[0, h, :, LANES:QK_PAD] = (q_pe * Q_SCALE).astype(BF16)
        k_ref[0, h, :, 0:LANES] = kv[:, lo:hi].astype(BF16)
        k_ref[0, h, :, LANES:QK_PAD] = k_pe
        v_ref[0, h, :, :] = kv[:, GROUP_WIDTH + lo:GROUP_WIDTH + hi].astype(BF16)


def _mla_proj(z, ct, st, q_norm, kv_norm, wq, wkv, B, S):
    ts = min(S, 512)
    ns = S // ts
    tok = lambda b, i: (b * ns + i, 0)
    const = lambda b, i: (0, 0)
    head_out = lambda b, i: (b, 0, i, 0)
    return pl.pallas_call(
        _mla_proj_kernel,
        grid=(B, ns),
        in_specs=[pl.BlockSpec((ts, Z_MLA), tok),
                  pl.BlockSpec((ts, LANES), tok),
                  pl.BlockSpec((ts, LANES), tok),
                  pl.BlockSpec((1, Q_RANK), const),
                  pl.BlockSpec((1, KV_RANK), const),
                  pl.BlockSpec(wq.shape, const),
                  pl.BlockSpec(wkv.shape, const)],
        out_specs=[pl.BlockSpec((1, N_HEADS, ts, QK_PAD), head_out),
                   pl.BlockSpec((1, N_HEADS, ts, QK_PAD), head_out),
                   pl.BlockSpec((1, N_HEADS, ts, V_DIM), head_out)],
        out_shape=[jax.ShapeDtypeStruct((B, N_HEADS, S, QK_PAD), BF16),
                   jax.ShapeDtypeStruct((B, N_HEADS, S, QK_PAD), BF16),
                   jax.ShapeDtypeStruct((B, N_HEADS, S, V_DIM), BF16)],
        compiler_params=_params(("parallel", "parallel")),
        name="mla_proj",
    )(z, ct, st, q_norm[None, :], kv_norm[None, :], wq, wkv)


def _attn_kernel(q_ref, k_ref, v_ref, o_ref, s_a, s_b, *, tk):
    i = pl.program_id(2)
    top_rows = pl.ds(0, tk)
    bot_rows = pl.ds(tk, tk)

    def kv_rows(u):
        return pl.ds(pl.multiple_of(u * tk, tk), tk)

    def scores(u):
        return _dot_nt(q_ref[0, 0], k_ref[0, 0, kv_rows(u), :])

    lane = lax.broadcasted_iota(jnp.int32, (tk, LANES), 1)
    ones_col = jnp.where(lane == 0, 1.0, 0.0).astype(BF16)

    def values(u):
        return jnp.concatenate([v_ref[0, 0, kv_rows(u), :], ones_col], axis=1)

    def update(state, s, v, masked):
        m, acc = state
        if masked:
            row = lax.broadcasted_iota(jnp.int32, (tk, tk), 0)
            col = lax.broadcasted_iota(jnp.int32, (tk, tk), 1)
            s = jnp.where(col <= row, s, -jnp.inf)
        m_new = jnp.maximum(m, jnp.max(_fold_lanes(s, jnp.maximum), axis=-1, keepdims=True))
        alpha = jnp.exp2(m - m_new)
        p = jnp.exp2((s - m_new).astype(BF16))
        acc = alpha * acc + _dot(p, v)
        return m_new, acc

    def both(top, bot, s_ref, u):
        v = values(u)
        return (update(top, s_ref[top_rows, :], v, False),
                update(bot, s_ref[bot_rows, :], v, False))

    s_a[...] = scores(0)

    def body(t, state):
        top, bot = state
        s_b[...] = scores(2 * t + 1)
        top, bot = both(top, bot, s_a, 2 * t)
        s_a[...] = scores(2 * t + 2)
        top, bot = both(top, bot, s_b, 2 * t + 1)
        return top, bot

    init = (jnp.full((tk, 1), -jnp.inf, F32), jnp.zeros((tk, 2 * LANES), F32))
    top, bot = lax.fori_loop(0, i, body, (init, init))
    s_b[bot_rows, :] = _dot_nt(q_ref[0, 0, bot_rows, :], k_ref[0, 0, kv_rows(2 * i + 1), :])
    v0 = values(2 * i)
    top = update(top, s_a[top_rows, :], v0, True)
    bot = update(bot, s_a[bot_rows, :], v0, False)
    bot = update(bot, s_b[bot_rows, :], values(2 * i + 1), True)
    for rows, (_, acc) in ((top_rows, top), (bot_rows, bot)):
        o_ref[0, rows, :] = (acc[:, :V_DIM] / acc[:, V_DIM:V_DIM + 1]).astype(BF16)


def _attention(q, k, v):
    B, H, S, _ = q.shape
    tq = min(S, 1024)
    tk = tq // 2
    return pl.pallas_call(
        functools.partial(_attn_kernel, tk=tk),
        grid=(B, H, S // tq),
        in_specs=[pl.BlockSpec((1, 1, tq, QK_PAD), lambda b, h, i: (b, h, i, 0)),
                  pl.BlockSpec((1, 1, S, QK_PAD), lambda b, h, i: (b, h, 0, 0)),
                  pl.BlockSpec((1, 1, S, V_DIM), lambda b, h, i: (b, h, 0, 0))],
        out_specs=pl.BlockSpec((1, tq, V_DIM), lambda b, h, i: (b, i, h)),
        out_shape=jax.ShapeDtypeStruct((B, S, H * V_DIM), BF16),
        scratch_shapes=[pltpu.VMEM((tq, tk), F32), pltpu.VMEM((tq, tk), F32)],
        compiler_params=_params(("parallel", "parallel", "arbitrary")),
        name="mla_attention",
    )(q, k, v)


def _hgrn_consts(C):
    levels = [C >> (i + 1) for i in range(int(math.log2(C)))]
    t = np.arange(C)
    u = np.arange(C)
    mats = [(u[None, :] <= t[:, None]),
            (u[None, :] > t[:, None])]
    pair_masks = [np.eye(C, dtype=bool)]
    for m in levels:
        ref = (t // (2 * m)) * (2 * m) + m - 1
        hi = ((t // m) % 2) == 1
        rng_hi = (u[None, :] > ref[:, None]) & (u[None, :] <= t[:, None])
        rng_lo = (u[None, :] > t[:, None]) & (u[None, :] <= ref[:, None])
        mats.append(np.where(hi[:, None], rng_hi, rng_lo))
        same = (t[:, None] // (2 * m)) == (t[None, :] // (2 * m))
        pair_masks.append(hi[:, None] & (~hi[None, :]) & same)
    stack = np.concatenate(mats, axis=0).astype(np.float32)
    sums = jnp.asarray(np.concatenate([stack] * 3, axis=1), dtype=BF16)
    heads = np.eye(HGRN_HEADS_PER_STEP, dtype=bool)
    stacked = np.stack([np.kron(heads, m) for m in pair_masks])
    return sums, jnp.asarray(stacked.astype(np.float32))


def _hgrn_kernel(zq_ref, zf_ref, zi_ref, zg_ref, lbl_ref, on_ref,
                 sums_ref, pm_ref, o_ref, st_scr, *, layer, ts):
    C = HGRN_CHUNK
    n_levels = pm_ref.shape[0] - 1

    @pl.when(pl.program_id(2) == 0)
    def _():
        st_scr[...] = jnp.zeros_like(st_scr)

    lg = lbl_ref[...]
    ex = jnp.exp(lg - jnp.max(lg, axis=0, keepdims=True))
    pr = ex / jnp.sum(ex, axis=0, keepdims=True)
    cum = pr[0:1, :]
    for r in range(1, layer + 1):
        cum = cum + pr[r:r + 1, :]
    lb = cum - pr[0:1, :]
    lb_floor = jnp.maximum(lb, LB_FLOOR)
    one_m_lb = 1.0 - lb
    gain = on_ref[...]

    for c in range(ts // C):
        rows = pl.ds(c * C, C)
        q_raw = zq_ref[rows, :]
        zf = zf_ref[rows, :]
        g_raw = zg_ref[rows, :]
        v_all = zi_ref[rows, :].astype(BF16)
        q_all = q_raw * _sigmoid(q_raw)
        e = jnp.exp(-jnp.abs(zf))
        r = 1.0 / (1.0 + e)
        er = e * r
        pos = zf >= 0
        log_f = jnp.log(lb_floor + one_m_lb * jnp.where(pos, r, er))
        kk_all = one_m_lb * jnp.where(pos, er, r)
        gate_all = g_raw * _sigmoid(g_raw)
        p0 = log_f.astype(BF16)
        r1 = log_f - p0.astype(F32)
        p1 = r1.astype(BF16)
        p2 = (r1 - p1.astype(F32)).astype(BF16)
        sums_all = _dot(sums_ref[...], jnp.concatenate([p0, p1, p2], axis=0))

        def stack(x):
            return jnp.concatenate([x[:, h * LANES:(h + 1) * LANES]
                                    for h in range(HGRN_HEADS_PER_STEP)], axis=0)

        q_st = stack(q_all).astype(BF16)
        kk_st = stack(kk_all).astype(BF16)
        a = pm_ref[0] * _dot_nt(q_st, kk_st)
        for lv in range(n_levels):
            dec = jnp.exp(stack(sums_all[(2 + lv) * C:(3 + lv) * C])).astype(BF16)
            a = a + pm_ref[lv + 1] * _dot_nt(q_st * dec, kk_st * dec)
        o_intra = _dot(a.astype(BF16), stack(v_all))
        b_all = sums_all[0:C]
        st_all = st_scr[...]
        o_inter = _dot_nt(stack(q_all * jnp.exp(b_all)).astype(BF16), st_all.astype(BF16))
        k_end = (kk_all * jnp.exp(sums_all[C:2 * C])).astype(BF16)
        decay = jnp.exp(b_all[C - 1:C, :])
        for hh in range(HGRN_HEADS_PER_STEP):
            lanes = slice(hh * LANES, (hh + 1) * LANES)
            srows = slice(hh * HGRN_V, (hh + 1) * HGRN_V)
            o = o_intra[hh * C:(hh + 1) * C] + o_inter[hh * C:(hh + 1) * C, lanes]
            st_scr[srows, :] = st_all[srows] * decay[:, lanes] \
                + _dot_tn(v_all[:, lanes], k_end[:, lanes])
            o_ref[rows, pl.ds(hh * LANES, LANES)] = \
                (_rms(o, gain) * gate_all[:, lanes]).astype(BF16)


def _hgrn(z, lb_logits, out_norm, consts, layer, B, S):
    T = B * S
    ts = min(S, 256)
    ns = S // ts
    sums, pair_masks = consts
    hp = HGRN_HEADS_PER_STEP
    width = hp * LANES
    col0 = HGRN_COL0 // width
    groups = N_HEADS // hp

    def zspec(part):
        return pl.BlockSpec((ts, width),
                            lambda b, h, i: (b * ns + i, col0 + part * groups + h))

    full = lambda arr: pl.BlockSpec(arr.shape, lambda b, h, i: (0,) * arr.ndim)
    kern = functools.partial(_hgrn_kernel, layer=layer, ts=ts)
    return pl.pallas_call(
        kern,
        grid=(B, groups, ns),
        in_specs=[zspec(0), zspec(1), zspec(2), zspec(3),
                  pl.BlockSpec((DEPTH, width), lambda b, h, i: (0, h)),
                  pl.BlockSpec((1, HGRN_V), lambda b, h, i: (0, 0)),
                  full(sums), full(pair_masks)],
        out_specs=pl.BlockSpec((ts, width), lambda b, h, i: (b * ns + i, h)),
        out_shape=jax.ShapeDtypeStruct((T, GROUP_WIDTH), BF16),
        scratch_shapes=[pltpu.VMEM((hp * HGRN_V, HGRN_K), F32)],
        compiler_params=_params(("parallel", "parallel", "arbitrary")),
        name="hgrn2",
    )(z, z, z, z, lb_logits, out_norm[None, :], sums, pair_masks)


def _out_proj_kernel(om_ref, oh_ref, h_ref, w_ref, g1_ref, g2_ref, h1_ref, c_ref):
    y = _dot(om_ref[...], w_ref[0:GROUP_WIDTH, :]) + _dot(oh_ref[...], w_ref[GROUP_WIDTH:, :])
    h1 = h_ref[...] + _rms(y, g1_ref[...])
    h1_ref[...] = h1
    c = _rms(h1, g2_ref[...])
    if c_ref.dtype == jnp.uint32:
        c_ref[...] = _pack_bf16_pairs(c)
    else:
        c_ref[...] = c.astype(c_ref.dtype)


def _out_proj(om, oh, h, w, g1, g2, c_dtype):
    T = h.shape[0]
    tm = min(T, 512)
    c_width = D_MODEL // 2 if c_dtype == jnp.uint32 else D_MODEL
    row = lambda i: (i, 0)
    const = lambda i: (0, 0)
    return pl.pallas_call(
        _out_proj_kernel,
        grid=(T // tm,),
        in_specs=[pl.BlockSpec((tm, GROUP_WIDTH), row),
                  pl.BlockSpec((tm, GROUP_WIDTH), row),
                  pl.BlockSpec((tm, D_MODEL), row),
                  pl.BlockSpec((D_MODEL, D_MODEL), const, pipeline_mode=pl.Buffered(1)),
                  pl.BlockSpec((1, D_MODEL), const),
                  pl.BlockSpec((1, D_MODEL), const)],
        out_specs=[pl.BlockSpec((tm, D_MODEL), row), pl.BlockSpec((tm, c_width), row)],
        out_shape=[jax.ShapeDtypeStruct((T, D_MODEL), F32),
                   jax.ShapeDtypeStruct((T, c_width), c_dtype)],
        compiler_params=_params(("parallel",)),
        name="out_proj",
    )(om, oh, h, w, g1[None, :], g2[None, :])


def _gate_up_kernel(te_ref, nv_ref, x_ref, wg_ref, wu_ref, o_ref, wg_bf, wu_bf):
    i = pl.program_id(1)
    valid = i < nv_ref[0]
    new_weights = jnp.logical_or(i == 0, te_ref[i] != te_ref[jnp.maximum(i - 1, 0)])

    def step(refresh):
        if x_ref.dtype == jnp.uint32:
            x = _unpack_bf16_pairs(x_ref[...])
        else:
            x = x_ref[...]
        for c in range(0, o_ref.shape[1], GU_SUB):
            cols = slice(c, c + GU_SUB)
            if refresh:
                wg_bf[:, cols] = wg_ref[0, :, cols].astype(BF16)
                wu_bf[:, cols] = wu_ref[0, :, cols].astype(BF16)
            g = _dot(x, wg_bf[:, cols])
            u = _dot(x, wu_bf[:, cols])
            o_ref[:, cols] = (g * _sigmoid(g) * u).astype(BF16)

    pl.when(jnp.logical_and(valid, new_weights))(functools.partial(step, True))
    pl.when(jnp.logical_and(valid, jnp.logical_not(new_weights)))(functools.partial(step, False))

    @pl.when(jnp.logical_not(valid))
    def _():
        o_ref[...] = jnp.zeros_like(o_ref)


def _gate_up(x, w_gu, tile_expert, n_valid, tm, tf):
    R, xw = x.shape
    n_tiles = R // tm
    nf = FFN_DIM // tf

    def tile(i, nv):
        return jnp.minimum(i, nv[0] - 1)

    return pl.pallas_call(
        _gate_up_kernel,
        grid_spec=pltpu.PrefetchScalarGridSpec(
            num_scalar_prefetch=2,
            grid=(nf, n_tiles),
            in_specs=[pl.BlockSpec((tm, xw), lambda j, i, te, nv: (tile(i, nv), 0)),
                      pl.BlockSpec((1, D_MODEL, tf), lambda j, i, te, nv: (te[tile(i, nv)], 0, j)),
                      pl.BlockSpec((1, D_MODEL, tf),
                                   lambda j, i, te, nv: (te[tile(i, nv)], 0, j + nf))],
            out_specs=pl.BlockSpec((tm, tf), lambda j, i, te, nv: (i, j)),
            scratch_shapes=[pltpu.VMEM((D_MODEL, tf), BF16), pltpu.VMEM((D_MODEL, tf), BF16)],
        ),
        out_shape=jax.ShapeDtypeStruct((R, FFN_DIM), BF16),
        compiler_params=_params(("arbitrary", "arbitrary")),
        name="ffn_gate_up",
    )(tile_expert, n_valid, x, w_gu, w_gu)


def _down_kernel(te_ref, nv_ref, a_ref, w_ref, o_ref):
    valid = pl.program_id(1) < nv_ref[0]

    @pl.when(valid)
    def _():
        a = a_ref[...]
        for c in range(0, o_ref.shape[1], GU_SUB):
            o_ref[:, c:c + GU_SUB] = _dot(a, w_ref[0, :, c:c + GU_SUB])

    @pl.when(jnp.logical_not(valid))
    def _():
        o_ref[...] = jnp.zeros_like(o_ref)


def _down(act, w_down, tile_expert, n_valid, tm):
    R = act.shape[0]
    n_tiles = R // tm
    tn = 1024

    def tile(i, nv):
        return jnp.minimum(i, nv[0] - 1)

    return pl.pallas_call(
        _down_kernel,
        grid_spec=pltpu.PrefetchScalarGridSpec(
            num_scalar_prefetch=2,
            grid=(D_MODEL // tn, n_tiles),
            in_specs=[pl.BlockSpec((tm, FFN_DIM), lambda j, i, te, nv: (tile(i, nv), 0)),
                      pl.BlockSpec((1, FFN_DIM, tn), lambda j, i, te, nv: (te[tile(i, nv)], 0, j))],
            out_specs=pl.BlockSpec((tm, tn), lambda j, i, te, nv: (i, j)),
        ),
        out_shape=jax.ShapeDtypeStruct((R, D_MODEL), F32),
        compiler_params=_params(("arbitrary", "arbitrary")),
        name="ffn_down",
    )(tile_expert, n_valid, act, w_down)


def _ffn_epilogue(f, h1, p, wg, wp, g3, gp):
    h2 = h1 + _rms(f, g3)
    gate = _sigmoid(_dot(h2.astype(BF16), wg))
    proj = _dot(p.astype(BF16), wp)
    return h2 + _rms(gate * proj, gp)


def _post_ffn_kernel(f_ref, h1_ref, p_ref, wg_ref, wp_ref, g3_ref, gp_ref, o_ref):
    o_ref[...] = _ffn_epilogue(f_ref[...], h1_ref[...], p_ref[...], wg_ref[...], wp_ref[...],
                               g3_ref[...], gp_ref[...])


def _post_ffn(f, h1, p, layer, wg, wp, g3, gp):
    T = h1.shape[0]
    tm = min(T, 512)
    row = lambda i: (i, 0)
    const = lambda i: (0, 0)
    return pl.pallas_call(
        _post_ffn_kernel,
        grid=(T // tm,),
        in_specs=[pl.BlockSpec((tm, D_MODEL), row),
                  pl.BlockSpec((tm, D_MODEL), row),
                  pl.BlockSpec((None, tm, PLE_DIM), lambda i: (layer, i, 0)),
                  pl.BlockSpec((D_MODEL, D_MODEL), const, pipeline_mode=pl.Buffered(1)),
                  pl.BlockSpec((PLE_DIM, D_MODEL), const, pipeline_mode=pl.Buffered(1)),
                  pl.BlockSpec((1, D_MODEL), const),
                  pl.BlockSpec((1, D_MODEL), const)],
        out_specs=pl.BlockSpec((tm, D_MODEL), row),
        out_shape=jax.ShapeDtypeStruct((T, D_MODEL), F32),
        compiler_params=_params(("parallel",)),
        name="post_ffn",
    )(f, h1, p, wg, wp, g3[None, :], gp[None, :])


def _router_kernel(c_ref, wr_ref, tri_ref, route_ref, cnt_ref, carry):
    @pl.when(pl.program_id(0) == 0)
    def _():
        carry[...] = jnp.zeros_like(carry)

    logits = _dot(_unpack_bf16_pairs(c_ref[...]), wr_ref[...])
    lane = lax.broadcasted_iota(jnp.int32, logits.shape, 1)
    lg = jnp.where(lane < N_EXPERTS, logits, -jnp.inf)
    m1 = jnp.max(lg, axis=-1, keepdims=True)
    i1 = jnp.min(jnp.where(lg == m1, lane, LANES), axis=-1, keepdims=True)
    lg2 = jnp.where(lane == i1, -jnp.inf, lg)
    m2 = jnp.max(lg2, axis=-1, keepdims=True)
    i2 = jnp.min(jnp.where(lg2 == m2, lane, LANES), axis=-1, keepdims=True)
    e = jnp.exp(m2 - m1)
    g1 = 1.0 / (1.0 + e)
    g2 = e / (1.0 + e)
    onehot = jnp.where((lane == i1) | (lane == i2), 1.0, 0.0)
    before = _dot(tri_ref[...], onehot.astype(BF16)) + carry[0:1, :]
    rank1 = jnp.sum(jnp.where(lane == i1, before, 0.0), axis=-1, keepdims=True)
    rank2 = jnp.sum(jnp.where(lane == i2, before, 0.0), axis=-1, keepdims=True)
    total = carry[0:1, :] + jnp.sum(onehot, axis=0, keepdims=True)
    carry[...] = jnp.broadcast_to(total, carry.shape)
    cnt_ref[...] = jnp.broadcast_to(total, cnt_ref.shape)
    out = jnp.where(lane == 0, i1.astype(F32), 0.0)
    out = jnp.where(lane == 1, i2.astype(F32), out)
    out = jnp.where(lane == 2, g1, out)
    out = jnp.where(lane == 3, g2, out)
    out = jnp.where(lane == 4, rank1, out)
    out = jnp.where(lane == 5, rank2, out)
    route_ref[...] = out


def _router(c, w_router):
    T = c.shape[0]
    tr = min(T, 512)
    wr = jnp.zeros((D_MODEL, LANES), BF16).at[:, :N_EXPERTS].set(w_router.astype(BF16))
    tri = jnp.asarray(np.tril(np.ones((tr, tr), np.float32), -1), dtype=BF16)
    return pl.pallas_call(
        _router_kernel,
        grid=(T // tr,),
        in_specs=[pl.BlockSpec((tr, c.shape[1]), lambda i: (i, 0)),
                  pl.BlockSpec((D_MODEL, LANES), lambda i: (0, 0)),
                  pl.BlockSpec((tr, tr), lambda i: (0, 0))],
        out_specs=[pl.BlockSpec((tr, LANES), lambda i: (i, 0)),
                   pl.BlockSpec((8, LANES), lambda i: (0, 0))],
        out_shape=[jax.ShapeDtypeStruct((T, LANES), F32),
                   jax.ShapeDtypeStruct((8, LANES), F32)],
        scratch_shapes=[pltpu.VMEM((8, LANES), F32)],
        compiler_params=_params(("arbitrary",)),
        name="moe_router",
    )(c, wr, tri)


def _dispatch_kernel(p1_ref, p2_ref, pad_ref, nv_ref, c_ref, w_ref, xs_ref, wbf_ref,
                     zbuf, sem, zsem, *, tm, seg_tile, n_row_steps):
    i = pl.program_id(0)
    base = i * tm
    zr = zbuf.shape[0]

    wbf_ref[...] = w_ref[...].astype(BF16)

    @pl.when(i == 0)
    def _():
        zbuf[...] = jnp.zeros_like(zbuf)

        def zero_tile(start):
            first = pl.multiple_of(start, zr)
            copies = [pltpu.make_async_copy(zbuf, xs_ref.at[pl.ds(first + k * zr, zr)], zsem)
                      for k in range(seg_tile // zr)]
            for cp in copies:
                cp.start()
            for cp in copies:
                cp.wait()

        def zero_padding(e, _):
            @pl.when(pad_ref[e] >= 0)
            def _():
                zero_tile(pad_ref[e])
            return 0

        def zero_unused(t, _):
            zero_tile(t * seg_tile)
            return 0

        lax.fori_loop(0, N_EXPERTS, zero_padding, 0)
        lax.fori_loop(nv_ref[0], xs_ref.shape[0] // seg_tile, zero_unused, 0)

    def row_copy(r, dst):
        return pltpu.make_async_copy(c_ref.at[pl.ds(r, 1)], xs_ref.at[pl.ds(dst, 1)], sem)

    @pl.when(i < n_row_steps)
    def _():
        def issue(g, _):
            for u in range(ROW_UNROLL):
                r = g * ROW_UNROLL + u
                row_copy(r, p1_ref[base + r]).start(priority=0)
                row_copy(r, p2_ref[base + r]).start(priority=1)
            return 0

        lax.fori_loop(0, tm // ROW_UNROLL, issue, 0)

        def drain(g, _):
            for _u in range(2 * ROW_UNROLL):
                row_copy(0, 0).wait()
            return 0

        lax.fori_loop(0, tm // ROW_UNROLL, drain, 0)


def _dispatch(c, pos1, pos2, pad_tile_row, n_valid, n_rows, seg_tile, w_down):
    T, width = c.shape
    tm = min(T, 256)
    n_row_steps = T // tm
    w2d = w_down.reshape(N_EXPERTS * FFN_DIM, D_MODEL)
    n_cast_steps = w2d.shape[0] // CAST_ROWS
    n_steps = max(n_row_steps, n_cast_steps)
    cast_block = lambda i, p1, p2, pt, nv: (jnp.minimum(i, n_cast_steps - 1), 0)
    xs, w_bf = pl.pallas_call(
        functools.partial(_dispatch_kernel, tm=tm, seg_tile=seg_tile, n_row_steps=n_row_steps),
        grid_spec=pltpu.PrefetchScalarGridSpec(
            num_scalar_prefetch=4,
            grid=(n_steps,),
            in_specs=[pl.BlockSpec((tm, width),
                                   lambda i, p1, p2, pt, nv: (jnp.minimum(i, n_row_steps - 1), 0)),
                      pl.BlockSpec((CAST_ROWS, D_MODEL), cast_block)],
            out_specs=[pl.BlockSpec(memory_space=pl.ANY),
                       pl.BlockSpec((CAST_ROWS, D_MODEL), cast_block)],
            scratch_shapes=[pltpu.VMEM((min(seg_tile, 64), width), c.dtype),
                            pltpu.SemaphoreType.DMA(()),
                            pltpu.SemaphoreType.DMA(())],
        ),
        out_shape=[jax.ShapeDtypeStruct((n_rows, width), c.dtype),
                   jax.ShapeDtypeStruct(w2d.shape, BF16)],
        compiler_params=_params(("arbitrary",)),
        name="moe_dispatch",
    )(pos1, pos2, pad_tile_row, n_valid, c, w2d)
    return xs, w_bf.reshape(w_down.shape)


def _combine_kernel(p1_ref, p2_ref, ys_ref, route_ref, h1_ref, p_ref, wg_ref, wp_ref,
                    g3_ref, gp_ref, o_ref, buf, sems, *, tm):
    i = pl.program_id(0)
    slot = i % 2

    def row_copy(src, s, k, r):
        return pltpu.make_async_copy(ys_ref.at[pl.ds(src, 1)], buf.at[s, k, pl.ds(r, 1)],
                                     sems.at[s])

    def gather_tile(tile, s):
        base = tile * tm

        def issue(g, _):
            for u in range(ROW_UNROLL):
                r = g * ROW_UNROLL + u
                row_copy(p1_ref[base + r], s, 0, r).start(priority=0)
                row_copy(p2_ref[base + r], s, 1, r).start(priority=1)
            return 0

        lax.fori_loop(0, tm // ROW_UNROLL, issue, 0)

    @pl.when(i == 0)
    def _():
        gather_tile(0, 0)

    @pl.when(i + 1 < pl.num_programs(0))
    def _():
        gather_tile(i + 1, 1 - slot)

    def drain(g, _):
        for _u in range(2 * ROW_UNROLL):
            row_copy(0, slot, 0, 0).wait()
        return 0

    lax.fori_loop(0, tm // ROW_UNROLL, drain, 0)
    route = route_ref[...]
    f = route[:, 2:3] * buf[slot, 0] + route[:, 3:4] * buf[slot, 1]
    o_ref[...] = _ffn_epilogue(f, h1_ref[...], p_ref[...], wg_ref[...], wp_ref[...],
                               g3_ref[...], gp_ref[...])


def _combine(ys, pos1, pos2, route, h1, p, layer, wg, wp, g3, gp):
    T = h1.shape[0]
    tm = min(T, 256)
    row = lambda i, p1, p2: (i, 0)
    const = lambda i, p1, p2: (0, 0)
    return pl.pallas_call(
        functools.partial(_combine_kernel, tm=tm),
        grid_spec=pltpu.PrefetchScalarGridSpec(
            num_scalar_prefetch=2,
            grid=(T // tm,),
            in_specs=[pl.BlockSpec(memory_space=pl.ANY),
                      pl.BlockSpec((tm, LANES), row),
                      pl.BlockSpec((tm, D_MODEL), row),
                      pl.BlockSpec((None, tm, PLE_DIM), lambda i, p1, p2: (layer, i, 0)),
                      pl.BlockSpec((D_MODEL, D_MODEL), const, pipeline_mode=pl.Buffered(1)),
                      pl.BlockSpec((PLE_DIM, D_MODEL), const, pipeline_mode=pl.Buffered(1)),
                      pl.BlockSpec((1, D_MODEL), const),
                      pl.BlockSpec((1, D_MODEL), const)],
            out_specs=pl.BlockSpec((tm, D_MODEL), row),
            scratch_shapes=[pltpu.VMEM((2, 2, tm, D_MODEL), F32),
                            pltpu.SemaphoreType.DMA((2,))],
        ),
        out_shape=jax.ShapeDtypeStruct((T, D_MODEL), F32),
        compiler_params=_params(("arbitrary",)),
        name="moe_combine",
    )(pos1, pos2, ys, route, h1, p, wg, wp, g3[None, :], gp[None, :])


def _swap_halves(w):
    half = w.shape[-1] // 2
    return jnp.concatenate([w[..., half:], w[..., :half]], axis=-1)


def _prep_w_in(w):
    w = w.astype(BF16)
    k_rope = w[:, 640:704]
    pad64 = jnp.zeros((D_MODEL, 64), w.dtype)
    cols = [w[:, :640], k_rope, pad64, _swap_halves(k_rope), pad64,
            jnp.zeros((D_MODEL, HGRN_COL0 - 896), w.dtype), w[:, 704:]]
    return jnp.concatenate(cols, axis=1).astype(BF16)


def _prep_w_uq(w):
    w = w.reshape(Q_RANK, N_HEADS, QK_DIM)
    nope = w[:, :, :NOPE_DIM]
    pe = w[:, :, NOPE_DIM:]
    pad = jnp.zeros((Q_RANK, N_HEADS, LANES - ROPE_DIM), w.dtype)
    pe_pad = jnp.concatenate([pe, pad], axis=-1)
    pe_swap = jnp.concatenate([_swap_halves(pe), pad], axis=-1)
    parts = [x.reshape(Q_RANK, GROUP_WIDTH) for x in (nope, pe_pad, pe_swap)]
    return jnp.concatenate(parts, axis=1).astype(BF16)


def _prep_w_ukv(w):
    w = w.reshape(KV_RANK, N_HEADS, NOPE_DIM + V_DIM)
    k = w[:, :, :NOPE_DIM].reshape(KV_RANK, GROUP_WIDTH)
    v = w[:, :, NOPE_DIM:].reshape(KV_RANK, GROUP_WIDTH)
    return jnp.concatenate([k, v], axis=1).astype(BF16)


def _moe_plan(route, counts, tm, n_tiles):
    e1 = route[:, 0].astype(jnp.int32)
    e2 = route[:, 1].astype(jnp.int32)
    rank1 = route[:, 4].astype(jnp.int32)
    rank2 = route[:, 5].astype(jnp.int32)
    cnt = counts[0, :N_EXPERTS].astype(jnp.int32)
    padded = ((cnt + tm - 1) // tm) * tm
    seg_end = jnp.cumsum(padded)
    seg_start = seg_end - padded
    pos1 = seg_start[e1] + rank1
    pos2 = seg_start[e2] + rank2
    tile_start = jnp.arange(n_tiles, dtype=jnp.int32) * tm
    tile_expert = jnp.sum(tile_start[:, None] >= seg_end[None, :], axis=1).astype(jnp.int32)
    tile_expert = jnp.minimum(tile_expert, N_EXPERTS - 1)
    n_valid = (seg_end[-1] // tm).astype(jnp.int32)[None]
    pad_tile_row = jnp.where(padded > 0, seg_end - tm, -1).astype(jnp.int32)
    return pos1, pos2, tile_expert, n_valid, pad_tile_row


def kernel(x, p, positions, sandwich_norms, w_in, mla_q_norm, mla_kv_norm, w_uq, w_ukv,
           hgrn_lb_logits, hgrn_out_norm, w_out, ffn_w_gu, ffn_w_down, moe_w_router,
           moe_w_gu, moe_w_down, ple_w_proj, ple_w_gate, ple_norm):
    B, S, _ = x.shape
    T = B * S
    h = x.reshape(T, D_MODEL)
    ct, st = _rope_tables(positions)
    hgrn_consts = _hgrn_consts(HGRN_CHUNK)
    tm = min(T, 512)
    for l in range(DEPTH):
        z = _norm_matmul(h, sandwich_norms[l, 0], _prep_w_in(w_in[l]))
        q, k, v = _mla_proj(z, ct, st, mla_q_norm[l], mla_kv_norm[l],
                            _prep_w_uq(w_uq[l]), _prep_w_ukv(w_ukv[l]), B, S)
        o_mla = _attention(q, k, v).reshape(T, GROUP_WIDTH)
        o_hgrn = _hgrn(z, hgrn_lb_logits, hgrn_out_norm[l], hgrn_consts, l, B, S)
        moe = l % 2 == 1
        h1, c = _out_proj(o_mla, o_hgrn, h, w_out[l].astype(BF16),
                          sandwich_norms[l, 1], sandwich_norms[l, 2],
                          jnp.uint32 if moe else BF16)
        wg = ple_w_gate[l].astype(BF16)
        wp = ple_w_proj[l].astype(BF16)
        p_all = p.reshape(DEPTH, T, PLE_DIM)
        if not moe:
            def one_expert(rows):
                return jnp.zeros((T // rows,), jnp.int32), jnp.full((1,), T // rows, jnp.int32)

            tm_dense = min(T, 1024)
            act = _gate_up(c, ffn_w_gu[l // 2][None], *one_expert(tm_dense), tm_dense, 512)
            f = _down(act, ffn_w_down[l // 2][None].astype(BF16), *one_expert(tm), tm)
            h = _post_ffn(f, h1, p_all, l, wg, wp, sandwich_norms[l, 3], ple_norm[l])
        else:
            n_tiles = (2 * T) // tm + N_EXPERTS
            route, counts = _router(c, moe_w_router[l // 2])
            pos1, pos2, tile_expert, n_valid, pad_tile_row = _moe_plan(route, counts, tm, n_tiles)
            xs, w_down_bf = _dispatch(c, pos1, pos2, pad_tile_row, n_valid, n_tiles * tm, tm,
                                      moe_w_down[l // 2])
            act = _gate_up(xs, moe_w_gu[l // 2], tile_expert, n_valid, tm, 1024)
            ys = _down(act, w_down_bf, tile_expert, n_valid, tm)
            h = _combine(ys, pos1, pos2, route, h1, p_all, l, wg, wp,
                         sandwich_norms[l, 3], ple_norm[l])
    return h.reshape(B, S, D_MODEL)
```

```python
import functools
import math

import numpy as np
import jax
import jax.numpy as jnp
from jax import lax
from jax.experimental import pallas as pl
from jax.experimental.pallas import tpu as pltpu

F32 = jnp.float32
BF16 = jnp.bfloat16

D_MODEL = 2048
DEPTH = 2
N_HEADS = 8
NOPE_DIM = 128
ROPE_DIM = 64
V_DIM = 128
QK_DIM = NOPE_DIM + ROPE_DIM
Q_RANK = 384
KV_RANK = 256
HGRN_K = 128
HGRN_V = 128
GROUP_WIDTH = N_HEADS * 128
FFN_DIM = 7168
N_EXPERTS = 8
PLE_DIM = 256
ROPE_THETA = 10000.0
LB_FLOOR = 1e-30
EPS = 1e-6

LANES = 128
QK_PAD = 256
Q_SCALE = QK_DIM ** -0.5 * math.log2(math.e)
Z_MLA = 1024
Z_WIDTH = 5120
HGRN_COL0 = 1024
HGRN_CHUNK = 64
HGRN_HEADS_PER_STEP = 4
ROW_UNROLL = 8
GU_SUB = 512
IN_PROJ_SUB = 1024
CAST_ROWS = 896
VMEM_LIMIT = 56 * 1024 * 1024


def _params(semantics, vmem=VMEM_LIMIT):
    return pltpu.CompilerParams(dimension_semantics=semantics, vmem_limit_bytes=vmem)


def _rms(x, gain_row):
    ms = jnp.mean(x * x, axis=-1, keepdims=True)
    return x * lax.rsqrt(ms + EPS) * gain_row


def _dot(a, b):
    return jnp.dot(a, b, preferred_element_type=F32)


def _dot_nt(a, b):
    return lax.dot_general(a, b, (((1,), (1,)), ((), ())), preferred_element_type=F32)


def _dot_tn(a, b):
    return lax.dot_general(a, b, (((0,), (0,)), ((), ())), preferred_element_type=F32)


def _sigmoid(x):
    return 1.0 / (1.0 + jnp.exp(-x))


def _fold_lanes(x, op):
    parts = [x[:, c:c + LANES] for c in range(0, x.shape[1], LANES)]
    while len(parts) > 1:
        parts = [op(a, b) for a, b in zip(parts[0::2], parts[1::2])] + parts[len(parts) & ~1:]
    return parts[0]


def _rope_kernel(pos_ref, inv_ref, sgn_ref, ct_ref, st_ref):
    ang = pos_ref[...].astype(F32) * inv_ref[...]
    keep = jnp.abs(sgn_ref[...])
    ct_ref[...] = jnp.cos(ang) * keep
    st_ref[...] = jnp.sin(ang) * sgn_ref[...]


def _rope_tables(positions):
    T = positions.size
    tt = min(T, 1024)
    half = ROPE_DIM // 2
    inv_freq = 1.0 / (ROPE_THETA ** (jnp.arange(0, ROPE_DIM, 2, dtype=F32) / ROPE_DIM))
    inv_row = jnp.concatenate([inv_freq, inv_freq, jnp.zeros((LANES - ROPE_DIM,), F32)])[None, :]
    sgn = np.zeros((1, LANES), np.float32)
    sgn[0, :half] = -1.0
    sgn[0, half:ROPE_DIM] = 1.0
    return pl.pallas_call(
        _rope_kernel,
        grid=(T // tt,),
        in_specs=[pl.BlockSpec((tt, 1), lambda i: (i, 0)),
                  pl.BlockSpec((1, LANES), lambda i: (0, 0)),
                  pl.BlockSpec((1, LANES), lambda i: (0, 0))],
        out_specs=[pl.BlockSpec((tt, LANES), lambda i: (i, 0))] * 2,
        out_shape=[jax.ShapeDtypeStruct((T, LANES), F32)] * 2,
        compiler_params=_params(("parallel",)),
        name="rope_tables",
    )(positions.reshape(T, 1), inv_row, jnp.asarray(sgn))


def _norm_matmul_kernel(x_ref, g_ref, w_ref, o_ref):
    a = _rms(x_ref[...], g_ref[...]).astype(BF16)
    for c in range(0, o_ref.shape[1], IN_PROJ_SUB):
        o_ref[:, c:c + IN_PROJ_SUB] = _dot(a, w_ref[:, c:c + IN_PROJ_SUB])


def _norm_matmul(x, gain, w):
    T, K = x.shape
    N = w.shape[1]
    tm = min(T, 256)
    return pl.pallas_call(
        _norm_matmul_kernel,
        grid=(T // tm,),
        in_specs=[pl.BlockSpec((tm, K), lambda i: (i, 0)),
                  pl.BlockSpec((1, K), lambda i: (0, 0)),
                  pl.BlockSpec((K, N), lambda i: (0, 0), pipeline_mode=pl.Buffered(1))],
        out_specs=pl.BlockSpec((tm, N), lambda i: (i, 0)),
        out_shape=jax.ShapeDtypeStruct((T, N), F32),
        compiler_params=_params(("parallel",)),
        name="in_proj",
    )(x, gain[None, :], w)


def _mla_proj_kernel(z_ref, ct_ref, st_ref, qn_ref, kvn_ref, wq_ref, wkv_ref,
                     q_ref, k_ref, v_ref):
    z = z_ref[...]
    ct = ct_ref[...]
    st = st_ref[...]
    aq = _rms(z[:, :Q_RANK], qn_ref[...]).astype(BF16)
    akv = _rms(z[:, Q_RANK:Q_RANK + KV_RANK], kvn_ref[...]).astype(BF16)
    k_a = z[:, 640:768]
    k_b = z[:, 768:896]
    k_pe = (k_a * ct + k_b * st).astype(BF16)
    q = _dot(aq, wq_ref[...])
    kv = _dot(akv, wkv_ref[...])
    for h in range(N_HEADS):
        lo, hi = h * LANES, (h + 1) * LANES
        q_ref[0, h, :, 0:LANES] = (q[:, lo:hi] * Q_SCALE).astype(BF16)
        q_pe = q[:, GROUP_WIDTH + lo:GROUP_WIDTH + hi] * ct \
            + q[:, 2 * GROUP_WIDTH + lo:2 * GROUP_WIDTH + hi] * st
        q_ref[0, h, :, LANES:QK_PAD] = (q_pe * Q_SCALE).astype(BF16)
        k_ref[0, h, :, 0:LANES] = kv[:, lo:hi].astype(BF16)
        k_ref[0, h, :, LANES:QK_PAD] = k_pe
        v_ref[0, h, :, :] = kv[:, GROUP_WIDTH + lo:GROUP_WIDTH + hi].astype(BF16)


def _mla_proj(z, ct, st, q_norm, kv_norm, wq, wkv, B, S):
    ts = min(S, 512)
    ns = S // ts
    tok = lambda b, i: (b * ns + i, 0)
    const = lambda b, i: (0, 0)
    head_out = lambda b, i: (b, 0, i, 0)
    return pl.pallas_call(
        _mla_proj_kernel,
        grid=(B, ns),
        in_specs=[pl.BlockSpec((ts, Z_MLA), tok),
                  pl.BlockSpec((ts, LANES), tok),
                  pl.BlockSpec((ts, LANES), tok),
                  pl.BlockSpec((1, Q_RANK), const),
                  pl.BlockSpec((1, KV_RANK), const),
                  pl.BlockSpec(wq.shape, const),
                  pl.BlockSpec(wkv.shape, const)],
        out_specs=[pl.BlockSpec((1, N_HEADS, ts, QK_PAD), head_out),
                   pl.BlockSpec((1, N_HEADS, ts, QK_PAD), head_out),
                   pl.BlockSpec((1, N_HEADS, ts, V_DIM), head_out)],
        out_shape=[jax.ShapeDtypeStruct((B, N_HEADS, S, QK_PAD), BF16),
                   jax.ShapeDtypeStruct((B, N_HEADS, S, QK_PAD), BF16),
                   jax.ShapeDtypeStruct((B, N_HEADS, S, V_DIM), BF16)],
        compiler_params=_params(("parallel", "parallel")),
        name="mla_proj",
    )(z, ct, st, q_norm[None, :], kv_norm[None, :], wq, wkv)


def _attn_kernel(q_ref, k_ref, v_ref, o_ref, s_a, s_b, *, tk):
    i = pl.program_id(2)
    top_rows = pl.ds(0, tk)
    bot_rows = pl.ds(tk, tk)

    def kv_rows(u):
        return pl.ds(pl.multiple_of(u * tk, tk), tk)

    def scores(u):
        return _dot_nt(q_ref[0, 0], k_ref[0, 0, kv_rows(u), :])

    lane = lax.broadcasted_iota(jnp.int32, (tk, LANES), 1)
    ones_col = jnp.where(lane == 0, 1.0, 0.0).astype(BF16)

    def values(u):
        return jnp.concatenate([v_ref[0, 0, kv_rows(u), :], ones_col], axis=1)

    def update(state, s, v, masked):
        m, acc = state
        if masked:
            row = lax.broadcasted_iota(jnp.int32, (tk, tk), 0)
            col = lax.broadcasted_iota(jnp.int32, (tk, tk), 1)
            s = jnp.where(col <= row, s, -jnp.inf)
        m_new = jnp.maximum(m, jnp.max(_fold_lanes(s, jnp.maximum), axis=-1, keepdims=True))
        alpha = jnp.exp2(m - m_new)
        p = jnp.exp2((s - m_new).astype(BF16))
        acc = alpha * acc + _dot(p, v)
        return m_new, acc

    def both(top, bot, s_ref, u):
        v = values(u)
        return (update(top, s_ref[top_rows, :], v, False),
                update(bot, s_ref[bot_rows, :], v, False))

    s_a[...] = scores(0)

    def body(t, state):
        top, bot = state
        s_b[...] = scores(2 * t + 1)
        top, bot = both(top, bot, s_a, 2 * t)
        s_a[...] = scores(2 * t + 2)
        top, bot = both(top, bot, s_b, 2 * t + 1)
        return top, bot

    init = (jnp.full((tk, 1), -jnp.inf, F32), jnp.zeros((tk, 2 * LANES), F32))
    top, bot = lax.fori_loop(0, i, body, (init, init))
    s_b[bot_rows, :] = _dot_nt(q_ref[0, 0, bot_rows, :], k_ref[0, 0, kv_rows(2 * i + 1), :])
    v0 = values(2 * i)
    top = update(top, s_a[top_rows, :], v0, True)
    bot = update(bot, s_a[bot_rows, :], v0, False)
    bot = update(bot, s_b[bot_rows, :], values(2 * i + 1), True)
    for rows, (_, acc) in ((top_rows, top), (bot_rows, bot)):
        o_ref[0, rows, :] = (acc[:, :V_DIM] / acc[:, V_DIM:V_DIM + 1]).astype(BF16)


def _attention(q, k, v):
    B, H, S, _ = q.shape
    tq = min(S, 1024)
    tk = tq // 2
    return pl.pallas_call(
        functools.partial(_attn_kernel, tk=tk),
        grid=(B, H, S // tq),
        in_specs=[pl.BlockSpec((1, 1, tq, QK_PAD), lambda b, h, i: (b, h, i, 0)),
                  pl.BlockSpec((1, 1, S, QK_PAD), lambda b, h, i: (b, h, 0, 0)),
                  pl.BlockSpec((1, 1, S, V_DIM), lambda b, h, i: (b, h, 0, 0))],
        out_specs=pl.BlockSpec((1, tq, V_DIM), lambda b, h, i: (b, i, h)),
        out_shape=jax.ShapeDtypeStruct((B, S, H * V_DIM), BF16),
        scratch_shapes=[pltpu.VMEM((tq, tk), F32), pltpu.VMEM((tq, tk), F32)],
        compiler_params=_params(("parallel", "parallel", "arbitrary")),
        name="mla_attention",
    )(q, k, v)


def _hgrn_consts(C):
    levels = [C >> (i + 1) for i in range(int(math.log2(C)))]
    t = np.arange(C)
    u = np.arange(C)
    mats = [(u[None, :] <= t[:, None]),
            (u[None, :] > t[:, None])]
    pair_masks = [np.eye(C, dtype=bool)]
    for m in levels:
        ref = (t // (2 * m)) * (2 * m) + m - 1
        hi = ((t // m) % 2) == 1
        rng_hi = (u[None, :] > ref[:, None]) & (u[None, :] <= t[:, None])
        rng_lo = (u[None, :] > t[:, None]) & (u[None, :] <= ref[:, None])
        mats.append(np.where(hi[:, None], rng_hi, rng_lo))
        same = (t[:, None] // (2 * m)) == (t[None, :] // (2 * m))
        pair_masks.append(hi[:, None] & (~hi[None, :]) & same)
    stack = np.concatenate(mats, axis=0).astype(np.float32)
    sums = jnp.asarray(np.concatenate([stack] * 3, axis=1), dtype=BF16)
    heads = np.eye(HGRN_HEADS_PER_STEP, dtype=bool)
    stacked = np.stack([np.kron(heads, m) for m in pair_masks])
    return sums, jnp.asarray(stacked.astype(np.float32))


def _hgrn_kernel(zq_ref, zf_ref, zi_ref, zg_ref, lbl_ref, on_ref,
                 sums_ref, pm_ref, o_ref, st_scr, *, layer, ts):
    C = HGRN_CHUNK
    n_levels = pm_ref.shape[0] - 1

    @pl.when(pl.program_id(2) == 0)
    def _():
        st_scr[...] = jnp.zeros_like(st_scr)

    lg = lbl_ref[...]
    ex = jnp.exp(lg - jnp.max(lg, axis=0, keepdims=True))
    pr = ex / jnp.sum(ex, axis=0, keepdims=True)
    cum = pr[0:1, :]
    for r in range(1, layer + 1):
        cum = cum + pr[r:r + 1, :]
    lb = cum - pr[0:1, :]
    lb_floor = jnp.maximum(lb, LB_FLOOR)
    one_m_lb = 1.0 - lb
    gain = on_ref[...]

    for c in range(ts // C):
        rows = pl.ds(c * C, C)
        q_raw = zq_ref[rows, :]
        zf = zf_ref[rows, :]
        g_raw = zg_ref[rows, :]
        v_all = zi_ref[rows, :].astype(BF16)
        q_all = q_raw * _sigmoid(q_raw)
        e = jnp.exp(-jnp.abs(zf))
        r = 1.0 / (1.0 + e)
        er = e * r
        pos = zf >= 0
        log_f = jnp.log(lb_floor + one_m_lb * jnp.where(pos, r, er))
        kk_all = one_m_lb * jnp.where(pos, er, r)
        gate_all = g_raw * _sigmoid(g_raw)
        p0 = log_f.astype(BF16)
        r1 = log_f - p0.astype(F32)
        p1 = r1.astype(BF16)
        p2 = (r1 - p1.astype(F32)).astype(BF16)
        sums_all = _dot(sums_ref[...], jnp.concatenate([p0, p1, p2], axis=0))

        def stack(x):
            return jnp.concatenate([x[:, h * LANES:(h + 1) * LANES]
                                    for h in range(HGRN_HEADS_PER_STEP)], axis=0)

        q_st = stack(q_all).astype(BF16)
        kk_st = stack(kk_all).astype(BF16)
        a = pm_ref[0] * _dot_nt(q_st, kk_st)
        for lv in range(n_levels):
            dec = jnp.exp(stack(sums_all[(2 + lv) * C:(3 + lv) * C])).astype(BF16)
            a = a + pm_ref[lv + 1] * _dot_nt(q_st * dec, kk_st * dec)
        o_intra = _dot(a.astype(BF16), stack(v_all))
        b_all = sums_all[0:C]
        st_all = st_scr[...]
        o_inter = _dot_nt(stack(q_all * jnp.exp(b_all)).astype(BF16), st_all.astype(BF16))
        k_end = (kk_all * jnp.exp(sums_all[C:2 * C])).astype(BF16)
        decay = jnp.exp(b_all[C - 1:C, :])
        for hh in range(HGRN_HEADS_PER_STEP):
            lanes = slice(hh * LANES, (hh + 1) * LANES)
            srows = slice(hh * HGRN_V, (hh + 1) * HGRN_V)
            o = o_intra[hh * C:(hh + 1) * C] + o_inter[hh * C:(hh + 1) * C, lanes]
            st_scr[srows, :] = st_all[srows] * decay[:, lanes] \
                + _dot_tn(v_all[:, lanes], k_end[:, lanes])
            o_ref[rows, pl.ds(hh * LANES, LANES)] = \
                (_rms(o, gain) * gate_all[:, lanes]).astype(BF16)


def _hgrn(z, lb_logits, out_norm, consts, layer, B, S):
    T = B * S
    ts = min(S, 256)
    ns = S // ts
    sums, pair_masks = consts
    hp = HGRN_HEADS_PER_STEP
    width = hp * LANES
    col0 = HGRN_COL0 // width
    groups = N_HEADS // hp

    def zspec(part):
        return pl.BlockSpec((ts, width),
                            lambda b, h, i: (b * ns + i, col0 + part * groups + h))

    full = lambda arr: pl.BlockSpec(arr.shape, lambda b, h, i: (0,) * arr.ndim)
    kern = functools.partial(_hgrn_kernel, layer=layer, ts=ts)
    return pl.pallas_call(
        kern,
        grid=(B, groups, ns),
        in_specs=[zspec(0), zspec(1), zspec(2), zspec(3),
                  pl.BlockSpec((DEPTH, width), lambda b, h, i: (0, h)),
                  pl.BlockSpec((1, HGRN_V), lambda b, h, i: (0, 0)),
                  full(sums), full(pair_masks)],
        out_specs=pl.BlockSpec((ts, width), lambda b, h, i: (b * ns + i, h)),
        out_shape=jax.ShapeDtypeStruct((T, GROUP_WIDTH), BF16),
        scratch_shapes=[pltpu.VMEM((hp * HGRN_V, HGRN_K), F32)],
        compiler_params=_params(("parallel", "parallel", "arbitrary")),
        name="hgrn2",
    )(z, z, z, z, lb_logits, out_norm[None, :], sums, pair_masks)


def _out_proj_kernel(om_ref, oh_ref, h_ref, w_ref, g1_ref, g2_ref, h1_ref, c_ref):
    y = _dot(om_ref[...], w_ref[0:GROUP_WIDTH, :]) + _dot(oh_ref[...], w_ref[GROUP_WIDTH:, :])
    h1 = h_ref[...] + _rms(y, g1_ref[...])
    h1_ref[...] = h1
    c_ref[...] = _rms(h1, g2_ref[...]).astype(c_ref.dtype)


def _out_proj(om, oh, h, w, g1, g2, c_dtype):
    T = h.shape[0]
    tm = min(T, 512)
    c_width = D_MODEL
    row = lambda i: (i, 0)
    const = lambda i: (0, 0)
    return pl.pallas_call(
        _out_proj_kernel,
        grid=(T // tm,),
        in_specs=[pl.BlockSpec((tm, GROUP_WIDTH), row),
                  pl.BlockSpec((tm, GROUP_WIDTH), row),
                  pl.BlockSpec((tm, D_MODEL), row),
                  pl.BlockSpec((D_MODEL, D_MODEL), const, pipeline_mode=pl.Buffered(1)),
                  pl.BlockSpec((1, D_MODEL), const),
                  pl.BlockSpec((1, D_MODEL), const)],
        out_specs=[pl.BlockSpec((tm, D_MODEL), row), pl.BlockSpec((tm, c_width), row)],
        out_shape=[jax.ShapeDtypeStruct((T, D_MODEL), F32),
                   jax.ShapeDtypeStruct((T, c_width), c_dtype)],
        compiler_params=_params(("parallel",)),
        name="out_proj",
    )(om, oh, h, w, g1[None, :], g2[None, :])


def _gate_up_kernel(te_ref, nv_ref, x_ref, wg_ref, wu_ref, o_ref, wg_bf, wu_bf):
    i = pl.program_id(1)
    valid = i < nv_ref[0]
    new_weights = jnp.logical_or(i == 0, te_ref[i] != te_ref[jnp.maximum(i - 1, 0)])

    def step(refresh):
        x = x_ref[...].astype(BF16)
        for c in range(0, o_ref.shape[1], GU_SUB):
            cols = slice(c, c + GU_SUB)
            if refresh:
                wg_bf[:, cols] = wg_ref[0, :, cols].astype(BF16)
                wu_bf[:, cols] = wu_ref[0, :, cols].astype(BF16)
            g = _dot(x, wg_bf[:, cols])
            u = _dot(x, wu_bf[:, cols])
            o_ref[:, cols] = (g * _sigmoid(g) * u).astype(BF16)

    pl.when(jnp.logical_and(valid, new_weights))(functools.partial(step, True))
    pl.when(jnp.logical_and(valid, jnp.logical_not(new_weights)))(functools.partial(step, False))

    @pl.when(jnp.logical_not(valid))
    def _():
        o_ref[...] = jnp.zeros_like(o_ref)


def _gate_up(x, w_gu, tile_expert, n_valid, tm, tf):
    R, xw = x.shape
    n_tiles = R // tm
    nf = FFN_DIM // tf

    def tile(i, nv):
        return jnp.minimum(i, nv[0] - 1)

    return pl.pallas_call(
        _gate_up_kernel,
        grid_spec=pltpu.PrefetchScalarGridSpec(
            num_scalar_prefetch=2,
            grid=(nf, n_tiles),
            in_specs=[pl.BlockSpec((tm, xw), lambda j, i, te, nv: (tile(i, nv), 0)),
                      pl.BlockSpec((1, D_MODEL, tf), lambda j, i, te, nv: (te[tile(i, nv)], 0, j)),
                      pl.BlockSpec((1, D_MODEL, tf),
                                   lambda j, i, te, nv: (te[tile(i, nv)], 0, j + nf))],
            out_specs=pl.BlockSpec((tm, tf), lambda j, i, te, nv: (i, j)),
            scratch_shapes=[pltpu.VMEM((D_MODEL, tf), BF16), pltpu.VMEM((D_MODEL, tf), BF16)],
        ),
        out_shape=jax.ShapeDtypeStruct((R, FFN_DIM), BF16),
        compiler_params=_params(("arbitrary", "arbitrary")),
        name="ffn_gate_up",
    )(tile_expert, n_valid, x, w_gu, w_gu)


def _down_kernel(te_ref, nv_ref, a_ref, w_ref, o_ref):
    valid = pl.program_id(1) < nv_ref[0]

    @pl.when(valid)
    def _():
        a = a_ref[...]
        for c in range(0, o_ref.shape[1], GU_SUB):
            o_ref[:, c:c + GU_SUB] = _dot(a, w_ref[0, :, c:c + GU_SUB])

    @pl.when(jnp.logical_not(valid))
    def _():
        o_ref[...] = jnp.zeros_like(o_ref)


def _down(act, w_down, tile_expert, n_valid, tm):
    R = act.shape[0]
    n_tiles = R // tm
    tn = 1024

    def tile(i, nv):
        return jnp.minimum(i, nv[0] - 1)

    return pl.pallas_call(
        _down_kernel,
        grid_spec=pltpu.PrefetchScalarGridSpec(
            num_scalar_prefetch=2,
            grid=(D_MODEL // tn, n_tiles),
            in_specs=[pl.BlockSpec((tm, FFN_DIM), lambda j, i, te, nv: (tile(i, nv), 0)),
                      pl.BlockSpec((1, FFN_DIM, tn), lambda j, i, te, nv: (te[tile(i, nv)], 0, j))],
            out_specs=pl.BlockSpec((tm, tn), lambda j, i, te, nv: (i, j)),
        ),
        out_shape=jax.ShapeDtypeStruct((R, D_MODEL), F32),
        compiler_params=_params(("arbitrary", "arbitrary")),
        name="ffn_down",
    )(tile_expert, n_valid, act, w_down)


def _ffn_epilogue(f, h1, p, wg, wp, g3, gp):
    h2 = h1 + _rms(f, g3)
    gate = _sigmoid(_dot(h2.astype(BF16), wg))
    proj = _dot(p.astype(BF16), wp)
    return h2 + _rms(gate * proj, gp)


def _post_ffn_kernel(f_ref, h1_ref, p_ref, wg_ref, wp_ref, g3_ref, gp_ref, o_ref):
    o_ref[...] = _ffn_epilogue(f_ref[...], h1_ref[...], p_ref[...], wg_ref[...], wp_ref[...],
                               g3_ref[...], gp_ref[...])


def _post_ffn(f, h1, p, layer, wg, wp, g3, gp):
    T = h1.shape[0]
    tm = min(T, 512)
    row = lambda i: (i, 0)
    const = lambda i: (0, 0)
    return pl.pallas_call(
        _post_ffn_kernel,
        grid=(T // tm,),
        in_specs=[pl.BlockSpec((tm, D_MODEL), row),
                  pl.BlockSpec((tm, D_MODEL), row),
                  pl.BlockSpec((None, tm, PLE_DIM), lambda i: (layer, i, 0)),
                  pl.BlockSpec((D_MODEL, D_MODEL), const, pipeline_mode=pl.Buffered(1)),
                  pl.BlockSpec((PLE_DIM, D_MODEL), const, pipeline_mode=pl.Buffered(1)),
                  pl.BlockSpec((1, D_MODEL), const),
                  pl.BlockSpec((1, D_MODEL), const)],
        out_specs=pl.BlockSpec((tm, D_MODEL), row),
        out_shape=jax.ShapeDtypeStruct((T, D_MODEL), F32),
        compiler_params=_params(("parallel",)),
        name="post_ffn",
    )(f, h1, p, wg, wp, g3[None, :], gp[None, :])


def _router_kernel(c_ref, wr_ref, tri_ref, route_ref, cnt_ref, carry):
    @pl.when(pl.program_id(0) == 0)
    def _():
        carry[...] = jnp.zeros_like(carry)

    logits = _dot(c_ref[...].astype(BF16), wr_ref[...])
    lane = lax.broadcasted_iota(jnp.int32, logits.shape, 1)
    lg = jnp.where(lane < N_EXPERTS, logits, -jnp.inf)
    m1 = jnp.max(lg, axis=-1, keepdims=True)
    i1 = jnp.min(jnp.where(lg == m1, lane, LANES), axis=-1, keepdims=True)
    lg2 = jnp.where(lane == i1, -jnp.inf, lg)
    m2 = jnp.max(lg2, axis=-1, keepdims=True)
    i2 = jnp.min(jnp.where(lg2 == m2, lane, LANES), axis=-1, keepdims=True)
    e = jnp.exp(m2 - m1)
    g1 = 1.0 / (1.0 + e)
    g2 = e / (1.0 + e)
    onehot = jnp.where((lane == i1) | (lane == i2), 1.0, 0.0)
    before = _dot(tri_ref[...], onehot.astype(BF16)) + carry[0:1, :]
    rank1 = jnp.sum(jnp.where(lane == i1, before, 0.0), axis=-1, keepdims=True)
    rank2 = jnp.sum(jnp.where(lane == i2, before, 0.0), axis=-1, keepdims=True)
    total = carry[0:1, :] + jnp.sum(onehot, axis=0, keepdims=True)
    carry[...] = jnp.broadcast_to(total, carry.shape)
    cnt_ref[...] = jnp.broadcast_to(total, cnt_ref.shape)
    out = jnp.where(lane == 0, i1.astype(F32), 0.0)
    out = jnp.where(lane == 1, i2.astype(F32), out)
    out = jnp.where(lane == 2, g1, out)
    out = jnp.where(lane == 3, g2, out)
    out = jnp.where(lane == 4, rank1, out)
    out = jnp.where(lane == 5, rank2, out)
    route_ref[...] = out


def _router(c, w_router):
    T = c.shape[0]
    tr = min(T, 512)
    wr = jnp.zeros((D_MODEL, LANES), BF16).at[:, :N_EXPERTS].set(w_router.astype(BF16))
    tri = jnp.asarray(np.tril(np.ones((tr, tr), np.float32), -1), dtype=BF16)
    return pl.pallas_call(
        _router_kernel,
        grid=(T // tr,),
        in_specs=[pl.BlockSpec((tr, c.shape[1]), lambda i: (i, 0)),
                  pl.BlockSpec((D_MODEL, LANES), lambda i: (0, 0)),
                  pl.BlockSpec((tr, tr), lambda i: (0, 0))],
        out_specs=[pl.BlockSpec((tr, LANES), lambda i: (i, 0)),
                   pl.BlockSpec((8, LANES), lambda i: (0, 0))],
        out_shape=[jax.ShapeDtypeStruct((T, LANES), F32),
                   jax.ShapeDtypeStruct((8, LANES), F32)],
        scratch_shapes=[pltpu.VMEM((8, LANES), F32)],
        compiler_params=_params(("arbitrary",)),
        name="moe_router",
    )(c, wr, tri)


def _dispatch_kernel(p1_ref, p2_ref, pad_ref, nv_ref, c_ref, w_ref, xs_ref, wbf_ref,
                     zbuf, sem, zsem, *, tm, seg_tile, n_row_steps):
    i = pl.program_id(0)
    base = i * tm
    zr = zbuf.shape[0]

    wbf_ref[...] = w_ref[...].astype(BF16)

    @pl.when(i == 0)
    def _():
        zbuf[...] = jnp.zeros_like(zbuf)

        def zero_tile(start):
            first = pl.multiple_of(start, zr)
            copies = [pltpu.make_async_copy(zbuf, xs_ref.at[pl.ds(first + k * zr, zr)], zsem)
                      for k in range(seg_tile // zr)]
            for cp in copies:
                cp.start()
            for cp in copies:
                cp.wait()

        def zero_padding(e, _):
            @pl.when(pad_ref[e] >= 0)
            def _():
                zero_tile(pad_ref[e])
            return 0

        def zero_unused(t, _):
            zero_tile(t * seg_tile)
            return 0

        lax.fori_loop(0, N_EXPERTS, zero_padding, 0)
        lax.fori_loop(nv_ref[0], xs_ref.shape[0] // seg_tile, zero_unused, 0)

    def row_copy(r, dst):
        return pltpu.make_async_copy(c_ref.at[pl.ds(r, 1)], xs_ref.at[pl.ds(dst, 1)], sem)

    @pl.when(i < n_row_steps)
    def _():
        def issue(g, _):
            for u in range(ROW_UNROLL):
                r = g * ROW_UNROLL + u
                row_copy(r, p1_ref[base + r]).start(priority=0)
                row_copy(r, p2_ref[base + r]).start(priority=1)
            return 0

        lax.fori_loop(0, tm // ROW_UNROLL, issue, 0)

        def drain(g, _):
            for _u in range(2 * ROW_UNROLL):
                row_copy(0, 0).wait()
            return 0

        lax.fori_loop(0, tm // ROW_UNROLL, drain, 0)


def _dispatch(c, pos1, pos2, pad_tile_row, n_valid, n_rows, seg_tile, w_down):
    T, width = c.shape
    tm = min(T, 256)
    n_row_steps = T // tm
    w2d = w_down.reshape(N_EXPERTS * FFN_DIM, D_MODEL)
    n_cast_steps = w2d.shape[0] // CAST_ROWS
    n_steps = max(n_row_steps, n_cast_steps)
    cast_block = lambda i, p1, p2, pt, nv: (jnp.minimum(i, n_cast_steps - 1), 0)
    xs, w_bf = pl.pallas_call(
        functools.partial(_dispatch_kernel, tm=tm, seg_tile=seg_tile, n_row_steps=n_row_steps),
        grid_spec=pltpu.PrefetchScalarGridSpec(
            num_scalar_prefetch=4,
            grid=(n_steps,),
            in_specs=[pl.BlockSpec((tm, width),
                                   lambda i, p1, p2, pt, nv: (jnp.minimum(i, n_row_steps - 1), 0)),
                      pl.BlockSpec((CAST_ROWS, D_MODEL), cast_block)],
            out_specs=[pl.BlockSpec(memory_space=pl.ANY),
                       pl.BlockSpec((CAST_ROWS, D_MODEL), cast_block)],
            scratch_shapes=[pltpu.VMEM((min(seg_tile, 64), width), c.dtype),
                            pltpu.SemaphoreType.DMA(()),
                            pltpu.SemaphoreType.DMA(())],
        ),
        out_shape=[jax.ShapeDtypeStruct((n_rows, width), c.dtype),
                   jax.ShapeDtypeStruct(w2d.shape, BF16)],
        compiler_params=_params(("arbitrary",)),
        name="moe_dispatch",
    )(pos1, pos2, pad_tile_row, n_valid, c, w2d)
    return xs, w_bf.reshape(w_down.shape)


def _combine_kernel(p1_ref, p2_ref, ys_ref, route_ref, h1_ref, p_ref, wg_ref, wp_ref,
                    g3_ref, gp_ref, o_ref, buf, sems, *, tm):
    i = pl.program_id(0)
    slot = i % 2

    def row_copy(src, s, k, r):
        return pltpu.make_async_copy(ys_ref.at[pl.ds(src, 1)], buf.at[s, k, pl.ds(r, 1)],
                                     sems.at[s])

    def gather_tile(tile, s):
        base = tile * tm

        def issue(g, _):
            for u in range(ROW_UNROLL):
                r = g * ROW_UNROLL + u
                row_copy(p1_ref[base + r], s, 0, r).start(priority=0)
                row_copy(p2_ref[base + r], s, 1, r).start(priority=1)
            return 0

        lax.fori_loop(0, tm // ROW_UNROLL, issue, 0)

    @pl.when(i == 0)
    def _():
        gather_tile(0, 0)

    @pl.when(i + 1 < pl.num_programs(0))
    def _():
        gather_tile(i + 1, 1 - slot)

    def drain(g, _):
        for _u in range(2 * ROW_UNROLL):
            row_copy(0, slot, 0, 0).wait()
        return 0

    lax.fori_loop(0, tm // ROW_UNROLL, drain, 0)
    route = route_ref[...]
    f = route[:, 2:3] * buf[slot, 0] + route[:, 3:4] * buf[slot, 1]
    o_ref[...] = _ffn_epilogue(f, h1_ref[...], p_ref[...], wg_ref[...], wp_ref[...],
                               g3_ref[...], gp_ref[...])


def _combine(ys, pos1, pos2, route, h1, p, layer, wg, wp, g3, gp):
    T = h1.shape[0]
    tm = min(T, 256)
    row = lambda i, p1, p2: (i, 0)
    const = lambda i, p1, p2: (0, 0)
    return pl.pallas_call(
        functools.partial(_combine_kernel, tm=tm),
        grid_spec=pltpu.PrefetchScalarGridSpec(
            num_scalar_prefetch=2,
            grid=(T // tm,),
            in_specs=[pl.BlockSpec(memory_space=pl.ANY),
                      pl.BlockSpec((tm, LANES), row),
                      pl.BlockSpec((tm, D_MODEL), row),
                      pl.BlockSpec((None, tm, PLE_DIM), lambda i, p1, p2: (layer, i, 0)),
                      pl.BlockSpec((D_MODEL, D_MODEL), const, pipeline_mode=pl.Buffered(1)),
                      pl.BlockSpec((PLE_DIM, D_MODEL), const, pipeline_mode=pl.Buffered(1)),
                      pl.BlockSpec((1, D_MODEL), const),
                      pl.BlockSpec((1, D_MODEL), const)],
            out_specs=pl.BlockSpec((tm, D_MODEL), row),
            scratch_shapes=[pltpu.VMEM((2, 2, tm, D_MODEL), F32),
                            pltpu.SemaphoreType.DMA((2,))],
        ),
        out_shape=jax.ShapeDtypeStruct((T, D_MODEL), F32),
        compiler_params=_params(("arbitrary",)),
        name="moe_combine",
    )(pos1, pos2, ys, route, h1, p, wg, wp, g3[None, :], gp[None, :])


def _swap_halves(w):
    half = w.shape[-1] // 2
    return jnp.concatenate([w[..., half:], w[..., :half]], axis=-1)


def _prep_w_in(w):
    w = w.astype(BF16)
    k_rope = w[:, 640:704]
    pad64 = jnp.zeros((D_MODEL, 64), w.dtype)
    cols = [w[:, :640], k_rope, pad64, _swap_halves(k_rope), pad64,
            jnp.zeros((D_MODEL, HGRN_COL0 - 896), w.dtype), w[:, 704:]]
    return jnp.concatenate(cols, axis=1).astype(BF16)


def _prep_w_uq(w):
    w = w.reshape(Q_RANK, N_HEADS, QK_DIM)
    nope = w[:, :, :NOPE_DIM]
    pe = w[:, :, NOPE_DIM:]
    pad = jnp.zeros((Q_RANK, N_HEADS, LANES - ROPE_DIM), w.dtype)
    pe_pad = jnp.concatenate([pe, pad], axis=-1)
    pe_swap = jnp.concatenate([_swap_halves(pe), pad], axis=-1)
    parts = [x.reshape(Q_RANK, GROUP_WIDTH) for x in (nope, pe_pad, pe_swap)]
    return jnp.concatenate(parts, axis=1).astype(BF16)


def _prep_w_ukv(w):
    w = w.reshape(KV_RANK, N_HEADS, NOPE_DIM + V_DIM)
    k = w[:, :, :NOPE_DIM].reshape(KV_RANK, GROUP_WIDTH)
    v = w[:, :, NOPE_DIM:].reshape(KV_RANK, GROUP_WIDTH)
    return jnp.concatenate([k, v], axis=1).astype(BF16)


def _moe_plan(route, counts, tm, n_tiles):
    e1 = route[:, 0].astype(jnp.int32)
    e2 = route[:, 1].astype(jnp.int32)
    rank1 = route[:, 4].astype(jnp.int32)
    rank2 = route[:, 5].astype(jnp.int32)
    cnt = counts[0, :N_EXPERTS].astype(jnp.int32)
    padded = ((cnt + tm - 1) // tm) * tm
    seg_end = jnp.cumsum(padded)
    seg_start = seg_end - padded
    pos1 = seg_start[e1] + rank1
    pos2 = seg_start[e2] + rank2
    tile_start = jnp.arange(n_tiles, dtype=jnp.int32) * tm
    tile_expert = jnp.sum(tile_start[:, None] >= seg_end[None, :], axis=1).astype(jnp.int32)
    tile_expert = jnp.minimum(tile_expert, N_EXPERTS - 1)
    n_valid = (seg_end[-1] // tm).astype(jnp.int32)[None]
    pad_tile_row = jnp.where(padded > 0, seg_end - tm, -1).astype(jnp.int32)
    return pos1, pos2, tile_expert, n_valid, pad_tile_row


def kernel(x, p, positions, sandwich_norms, w_in, mla_q_norm, mla_kv_norm, w_uq, w_ukv,
           hgrn_lb_logits, hgrn_out_norm, w_out, ffn_w_gu, ffn_w_down, moe_w_router,
           moe_w_gu, moe_w_down, ple_w_proj, ple_w_gate, ple_norm):
    B, S, _ = x.shape
    T = B * S
    h = x.reshape(T, D_MODEL)
    ct, st = _rope_tables(positions)
    hgrn_consts = _hgrn_consts(HGRN_CHUNK)
    tm = min(T, 512)
    for l in range(DEPTH):
        z = _norm_matmul(h, sandwich_norms[l, 0], _prep_w_in(w_in[l]))
        q, k, v = _mla_proj(z, ct, st, mla_q_norm[l], mla_kv_norm[l],
                            _prep_w_uq(w_uq[l]), _prep_w_ukv(w_ukv[l]), B, S)
        o_mla = _attention(q, k, v).reshape(T, GROUP_WIDTH)
        o_hgrn = _hgrn(z, hgrn_lb_logits, hgrn_out_norm[l], hgrn_consts, l, B, S)
        moe = l % 2 == 1
        h1, c = _out_proj(o_mla, o_hgrn, h, w_out[l].astype(BF16),
                          sandwich_norms[l, 1], sandwich_norms[l, 2],
                          F32 if moe else BF16)
        wg = ple_w_gate[l].astype(BF16)
        wp = ple_w_proj[l].astype(BF16)
        p_all = p.reshape(DEPTH, T, PLE_DIM)
        if not moe:
            def one_expert(rows):
                return jnp.zeros((T // rows,), jnp.int32), jnp.full((1,), T // rows, jnp.int32)

            tm_dense = min(T, 1024)
            act = _gate_up(c, ffn_w_gu[l // 2][None], *one_expert(tm_dense), tm_dense, 512)
            f = _down(act, ffn_w_down[l // 2][None].astype(BF16), *one_expert(tm), tm)
            h = _post_ffn(f, h1, p_all, l, wg, wp, sandwich_norms[l, 3], ple_norm[l])
        else:
            n_tiles = (2 * T) // tm + N_EXPERTS
            route, counts = _router(c, moe_w_router[l // 2])
            pos1, pos2, tile_expert, n_valid, pad_tile_row = _moe_plan(route, counts, tm, n_tiles)
            xs, w_down_bf = _dispatch(c, pos1, pos2, pad_tile_row, n_valid, n_tiles * tm, tm,
                                      moe_w_down[l // 2])
            act = _gate_up(xs, moe_w_gu[l // 2], tile_expert, n_valid, tm, 1024)
            ys = _down(act, w_down_bf, tile_expert, n_valid, tm)
            h = _combine(ys, pos1, pos2, route, h1, p_all, l, wg, wp,
                         sandwich_norms[l, 3], ple_norm[l])
    return h.reshape(B, S, D_MODEL)
```

```python
import functools
import math

import numpy as np
import jax
import jax.numpy as jnp
from jax import lax
from jax.experimental import pallas as pl
from jax.experimental.pallas import tpu as pltpu

F32 = jnp.float32
BF16 = jnp.bfloat16

D_MODEL = 2048
DEPTH = 2
N_HEADS = 8
NOPE_DIM = 128
ROPE_DIM = 64
V_DIM = 128
QK_DIM = NOPE_DIM + ROPE_DIM
Q_RANK = 384
KV_RANK = 256
HGRN_K = 128
HGRN_V = 128
GROUP_WIDTH = N_HEADS * 128
FFN_DIM = 7168
N_EXPERTS = 8
PLE_DIM = 256
ROPE_THETA = 10000.0
LB_FLOOR = 1e-30
EPS = 1e-6

LANES = 128
QK_PAD = 256
Q_SCALE = QK_DIM ** -0.5 * math.log2(math.e)
Z_MLA = 1024
Z_WIDTH = 5120
HGRN_COL0 = 1024
HGRN_CHUNK = 64
HGRN_HEADS_PER_STEP = 4
GU_SUB = 512
IN_PROJ_SUB = 1024
CAST_ROWS = 896
VMEM_LIMIT = 56 * 1024 * 1024


def _params(semantics, vmem=VMEM_LIMIT):
    return pltpu.CompilerParams(dimension_semantics=semantics, vmem_limit_bytes=vmem)


def _rms(x, gain_row):
    ms = jnp.mean(x * x, axis=-1, keepdims=True)
    return x * lax.rsqrt(ms + EPS) * gain_row


def _dot(a, b):
    return jnp.dot(a, b, preferred_element_type=F32)


def _dot_nt(a, b):
    return lax.dot_general(a, b, (((1,), (1,)), ((), ())), preferred_element_type=F32)


def _dot_tn(a, b):
    return lax.dot_general(a, b, (((0,), (0,)), ((), ())), preferred_element_type=F32)


def _sigmoid(x):
    return 1.0 / (1.0 + jnp.exp(-x))


def _fold_lanes(x, op):
    parts = [x[:, c:c + LANES] for c in range(0, x.shape[1], LANES)]
    while len(parts) > 1:
        parts = [op(a, b) for a, b in zip(parts[0::2], parts[1::2])] + parts[len(parts) & ~1:]
    return parts[0]


def _rope_kernel(pos_ref, inv_ref, sgn_ref, ct_ref, st_ref):
    ang = pos_ref[...].astype(F32) * inv_ref[...]
    keep = jnp.abs(sgn_ref[...])
    ct_ref[...] = jnp.cos(ang) * keep
    st_ref[...] = jnp.sin(ang) * sgn_ref[...]


def _rope_tables(positions):
    T = positions.size
    tt = min(T, 1024)
    half = ROPE_DIM // 2
    inv_freq = 1.0 / (ROPE_THETA ** (jnp.arange(0, ROPE_DIM, 2, dtype=F32) / ROPE_DIM))
    inv_row = jnp.concatenate([inv_freq, inv_freq, jnp.zeros((LANES - ROPE_DIM,), F32)])[None, :]
    sgn = np.zeros((1, LANES), np.float32)
    sgn[0, :half] = -1.0
    sgn[0, half:ROPE_DIM] = 1.0
    return pl.pallas_call(
        _rope_kernel,
        grid=(T // tt,),
        in_specs=[pl.BlockSpec((tt, 1), lambda i: (i, 0)),
                  pl.BlockSpec((1, LANES), lambda i: (0, 0)),
                  pl.BlockSpec((1, LANES), lambda i: (0, 0))],
        out_specs=[pl.BlockSpec((tt, LANES), lambda i: (i, 0))] * 2,
        out_shape=[jax.ShapeDtypeStruct((T, LANES), F32)] * 2,
        compiler_params=_params(("parallel",)),
        name="rope_tables",
    )(positions.reshape(T, 1), inv_row, jnp.asarray(sgn))


def _norm_matmul_kernel(x_ref, g_ref, w_ref, o_ref):
    a = _rms(x_ref[...], g_ref[...]).astype(BF16)
    for c in range(0, o_ref.shape[1], IN_PROJ_SUB):
        o_ref[:, c:c + IN_PROJ_SUB] = _dot(a, w_ref[:, c:c + IN_PROJ_SUB])


def _norm_matmul(x, gain, w):
    T, K = x.shape
    N = w.shape[1]
    tm = min(T, 512)
    return pl.pallas_call(
        _norm_matmul_kernel,
        grid=(T // tm,),
        in_specs=[pl.BlockSpec((tm, K), lambda i: (i, 0)),
                  pl.BlockSpec((1, K), lambda i: (0, 0)),
                  pl.BlockSpec((K, N), lambda i: (0, 0), pipeline_mode=pl.Buffered(1))],
        out_specs=pl.BlockSpec((tm, N), lambda i: (i, 0)),
        out_shape=jax.ShapeDtypeStruct((T, N), F32),
        compiler_params=_params(("parallel",)),
        name="in_proj",
    )(x, gain[None, :], w)


def _mla_proj_kernel(z_ref, ct_ref, st_ref, qn_ref, kvn_ref, wq_ref, wkv_ref,
                     q_ref, k_ref, v_ref):
    z = z_ref[...]
    ct = ct_ref[...]
    st = st_ref[...]
    aq = _rms(z[:, :Q_RANK], qn_ref[...]).astype(BF16)
    akv = _rms(z[:, Q_RANK:Q_RANK + KV_RANK], kvn_ref[...]).astype(BF16)
    k_a = z[:, 640:768]
    k_b = z[:, 768:896]
    k_pe = (k_a * ct + k_b * st).astype(BF16)
    q = _dot(aq, wq_ref[...])
    kv = _dot(akv, wkv_ref[...])
    for h in range(N_HEADS):
        lo, hi = h * LANES, (h + 1) * LANES
        q_ref[0, h, :, 0:LANES] = (q[:, lo:hi] * Q_SCALE).astype(BF16)
        q_pe = q[:, GROUP_WIDTH + lo:GROUP_WIDTH + hi] * ct \
            + q[:, 2 * GROUP_WIDTH + lo:2 * GROUP_WIDTH + hi] * st
        q_ref[0, h, :, LANES:QK_PAD] = (q_pe * Q_SCALE).astype(BF16)
        k_ref[0, h, :, 0:LANES] = kv[:, lo:hi].astype(BF16)
        k_ref[0, h, :, LANES:QK_PAD] = k_pe
        v_ref[0, h, :, :] = kv[:, GROUP_WIDTH + lo:GROUP_WIDTH + hi].astype(BF16)


def _mla_proj(z, ct, st, q_norm, kv_norm, wq, wkv, B, S):
    ts = min(S, 512)
    ns = S // ts
    tok = lambda b, i: (b * ns + i, 0)
    const = lambda b, i: (0, 0)
    head_out = lambda b, i: (b, 0, i, 0)
    return pl.pallas_call(
        _mla_proj_kernel,
        grid=(B, ns),
        in_specs=[pl.BlockSpec((ts, Z_MLA), tok),
                  pl.BlockSpec((ts, LANES), tok),
                  pl.BlockSpec((ts, LANES), tok),
                  pl.BlockSpec((1, Q_RANK), const),
                  pl.BlockSpec((1, KV_RANK), const),
                  pl.BlockSpec(wq.shape, const),
                  pl.BlockSpec(wkv.shape, const)],
        out_specs=[pl.BlockSpec((1, N_HEADS, ts, QK_PAD), head_out),
                   pl.BlockSpec((1, N_HEADS, ts, QK_PAD), head_out),
                   pl.BlockSpec((1, N_HEADS, ts, V_DIM), head_out)],
        out_shape=[jax.ShapeDtypeStruct((B, N_HEADS, S, QK_PAD), BF16),
                   jax.ShapeDtypeStruct((B, N_HEADS, S, QK_PAD), BF16),
                   jax.ShapeDtypeStruct((B, N_HEADS, S, V_DIM), BF16)],
        compiler_params=_params(("parallel", "parallel")),
        name="mla_proj",
    )(z, ct, st, q_norm[None, :], kv_norm[None, :], wq, wkv)


def _attn_kernel(q_ref, k_ref, v_ref, o_ref, s_a, s_b, *, tk):
    i = pl.program_id(2)
    top_rows = pl.ds(0, tk)
    bot_rows = pl.ds(tk, tk)

    def kv_rows(u):
        return pl.ds(pl.multiple_of(u * tk, tk), tk)

    def scores(u):
        return _dot_nt(q_ref[0, 0], k_ref[0, 0, kv_rows(u), :])

    lane = lax.broadcasted_iota(jnp.int32, (tk, LANES), 1)
    ones_col = jnp.where(lane == 0, 1.0, 0.0).astype(BF16)

    def values(u):
        return jnp.concatenate([v_ref[0, 0, kv_rows(u), :], ones_col], axis=1)

    def update(state, s, v, masked):
        m, acc = state
        if masked:
            row = lax.broadcasted_iota(jnp.int32, (tk, tk), 0)
            col = lax.broadcasted_iota(jnp.int32, (tk, tk), 1)
            s = jnp.where(col <= row, s, -jnp.inf)
        m_new = jnp.maximum(m, jnp.max(_fold_lanes(s, jnp.maximum), axis=-1, keepdims=True))
        alpha = jnp.exp2(m - m_new)
        p = jnp.exp2((s - m_new).astype(BF16))
        acc = alpha * acc + _dot(p, v)
        return m_new, acc

    def both(top, bot, s_ref, u):
        v = values(u)
        return (update(top, s_ref[top_rows, :], v, False),
                update(bot, s_ref[bot_rows, :], v, False))

    s_a[...] = scores(0)

    def body(t, state):
        top, bot = state
        s_b[...] = scores(2 * t + 1)
        top, bot = both(top, bot, s_a, 2 * t)
        s_a[...] = scores(2 * t + 2)
        top, bot = both(top, bot, s_b, 2 * t + 1)
        return top, bot

    init = (jnp.full((tk, 1), -jnp.inf, F32), jnp.zeros((tk, 2 * LANES), F32))
    top, bot = lax.fori_loop(0, i, body, (init, init))
    s_b[bot_rows, :] = _dot_nt(q_ref[0, 0, bot_rows, :], k_ref[0, 0, kv_rows(2 * i + 1), :])
    v0 = values(2 * i)
    top = update(top, s_a[top_rows, :], v0, True)
    bot = update(bot, s_a[bot_rows, :], v0, False)
    bot = update(bot, s_b[bot_rows, :], values(2 * i + 1), True)
    for rows, (_, acc) in ((top_rows, top), (bot_rows, bot)):
        o_ref[0, rows, :] = (acc[:, :V_DIM] / acc[:, V_DIM:V_DIM + 1]).astype(BF16)


def _attention(q, k, v):
    B, H, S, _ = q.shape
    tq = min(S, 1024)
    tk = tq // 2
    return pl.pallas_call(
        functools.partial(_attn_kernel, tk=tk),
        grid=(B, H, S // tq),
        in_specs=[pl.BlockSpec((1, 1, tq, QK_PAD), lambda b, h, i: (b, h, i, 0)),
                  pl.BlockSpec((1, 1, S, QK_PAD), lambda b, h, i: (b, h, 0, 0)),
                  pl.BlockSpec((1, 1, S, V_DIM), lambda b, h, i: (b, h, 0, 0))],
        out_specs=pl.BlockSpec((1, tq, V_DIM), lambda b, h, i: (b, i, h)),
        out_shape=jax.ShapeDtypeStruct((B, S, H * V_DIM), BF16),
        scratch_shapes=[pltpu.VMEM((tq, tk), F32), pltpu.VMEM((tq, tk), F32)],
        compiler_params=_params(("parallel", "parallel", "arbitrary")),
        name="mla_attention",
    )(q, k, v)


def _hgrn_consts(C):
    levels = [C >> (i + 1) for i in range(int(math.log2(C)))]
    t = np.arange(C)
    u = np.arange(C)
    mats = [(u[None, :] <= t[:, None]),
            (u[None, :] > t[:, None])]
    pair_masks = [np.eye(C, dtype=bool)]
    for m in levels:
        ref = (t // (2 * m)) * (2 * m) + m - 1
        hi = ((t // m) % 2) == 1
        rng_hi = (u[None, :] > ref[:, None]) & (u[None, :] <= t[:, None])
        rng_lo = (u[None, :] > t[:, None]) & (u[None, :] <= ref[:, None])
        mats.append(np.where(hi[:, None], rng_hi, rng_lo))
        same = (t[:, None] // (2 * m)) == (t[None, :] // (2 * m))
        pair_masks.append(hi[:, None] & (~hi[None, :]) & same)
    stack = np.concatenate(mats, axis=0).astype(np.float32)
    sums = jnp.asarray(np.concatenate([stack] * 3, axis=1), dtype=BF16)
    heads = np.eye(HGRN_HEADS_PER_STEP, dtype=bool)
    stacked = np.stack([np.kron(heads, m) for m in pair_masks])
    return sums, jnp.asarray(stacked.astype(np.float32))


def _hgrn_kernel(zq_ref, zf_ref, zi_ref, zg_ref, lbl_ref, on_ref,
                 sums_ref, pm_ref, o_ref, st_scr, *, layer, ts):
    C = HGRN_CHUNK
    n_levels = pm_ref.shape[0] - 1

    @pl.when(pl.program_id(2) == 0)
    def _():
        st_scr[...] = jnp.zeros_like(st_scr)

    lg = lbl_ref[...]
    ex = jnp.exp(lg - jnp.max(lg, axis=0, keepdims=True))
    pr = ex / jnp.sum(ex, axis=0, keepdims=True)
    cum = pr[0:1, :]
    for r in range(1, layer + 1):
        cum = cum + pr[r:r + 1, :]
    lb = cum - pr[0:1, :]
    lb_floor = jnp.maximum(lb, LB_FLOOR)
    one_m_lb = 1.0 - lb
    gain = on_ref[...]

    for c in range(ts // C):
        rows = pl.ds(c * C, C)
        q_raw = zq_ref[rows, :]
        zf = zf_ref[rows, :]
        g_raw = zg_ref[rows, :]
        v_all = zi_ref[rows, :].astype(BF16)
        q_all = q_raw * _sigmoid(q_raw)
        e = jnp.exp(-jnp.abs(zf))
        r = 1.0 / (1.0 + e)
        er = e * r
        pos = zf >= 0
        log_f = jnp.log(lb_floor + one_m_lb * jnp.where(pos, r, er))
        kk_all = one_m_lb * jnp.where(pos, er, r)
        gate_all = g_raw * _sigmoid(g_raw)
        p0 = log_f.astype(BF16)
        r1 = log_f - p0.astype(F32)
        p1 = r1.astype(BF16)
        p2 = (r1 - p1.astype(F32)).astype(BF16)
        sums_all = _dot(sums_ref[...], jnp.concatenate([p0, p1, p2], axis=0))

        def stack(x):
            return jnp.concatenate([x[:, h * LANES:(h + 1) * LANES]
                                    for h in range(HGRN_HEADS_PER_STEP)], axis=0)

        q_st = stack(q_all).astype(BF16)
        kk_st = stack(kk_all).astype(BF16)
        a = pm_ref[0] * _dot_nt(q_st, kk_st)
        for lv in range(n_levels):
            dec = jnp.exp(stack(sums_all[(2 + lv) * C:(3 + lv) * C])).astype(BF16)
            a = a + pm_ref[lv + 1] * _dot_nt(q_st * dec, kk_st * dec)
        o_intra = _dot(a.astype(BF16), stack(v_all))
        b_all = sums_all[0:C]
        st_all = st_scr[...]
        o_inter = _dot_nt(stack(q_all * jnp.exp(b_all)).astype(BF16), st_all.astype(BF16))
        k_end = (kk_all * jnp.exp(sums_all[C:2 * C])).astype(BF16)
        decay = jnp.exp(b_all[C - 1:C, :])
        for hh in range(HGRN_HEADS_PER_STEP):
            lanes = slice(hh * LANES, (hh + 1) * LANES)
            srows = slice(hh * HGRN_V, (hh + 1) * HGRN_V)
            o = o_intra[hh * C:(hh + 1) * C] + o_inter[hh * C:(hh + 1) * C, lanes]
            st_scr[srows, :] = st_all[srows] * decay[:, lanes] \
                + _dot_tn(v_all[:, lanes], k_end[:, lanes])
            o_ref[rows, pl.ds(hh * LANES, LANES)] = \
                (_rms(o, gain) * gate_all[:, lanes]).astype(BF16)


def _hgrn(z, lb_logits, out_norm, consts, layer, B, S):
    T = B * S
    ts = min(S, 512)
    ns = S // ts
    sums, pair_masks = consts
    hp = HGRN_HEADS_PER_STEP
    width = hp * LANES
    col0 = HGRN_COL0 // width
    groups = N_HEADS // hp

    def zspec(part):
        return pl.BlockSpec((ts, width),
                            lambda b, h, i: (b * ns + i, col0 + part * groups + h))

    full = lambda arr: pl.BlockSpec(arr.shape, lambda b, h, i: (0,) * arr.ndim)
    kern = functools.partial(_hgrn_kernel, layer=layer, ts=ts)
    return pl.pallas_call(
        kern,
        grid=(B, groups, ns),
        in_specs=[zspec(0), zspec(1), zspec(2), zspec(3),
                  pl.BlockSpec((DEPTH, width), lambda b, h, i: (0, h)),
                  pl.BlockSpec((1, HGRN_V), lambda b, h, i: (0, 0)),
                  full(sums), full(pair_masks)],
        out_specs=pl.BlockSpec((ts, width), lambda b, h, i: (b * ns + i, h)),
        out_shape=jax.ShapeDtypeStruct((T, GROUP_WIDTH), BF16),
        scratch_shapes=[pltpu.VMEM((hp * HGRN_V, HGRN_K), F32)],
        compiler_params=_params(("parallel", "parallel", "arbitrary")),
        name="hgrn2",
    )(z, z, z, z, lb_logits, out_norm[None, :], sums, pair_masks)


def _out_proj_kernel(om_ref, oh_ref, h_ref, w_ref, g1_ref, g2_ref, h1_ref, c_ref):
    y = _dot(om_ref[...], w_ref[0:GROUP_WIDTH, :]) + _dot(oh_ref[...], w_ref[GROUP_WIDTH:, :])
    h1 = h_ref[...] + _rms(y, g1_ref[...])
    h1_ref[...] = h1
    c_ref[...] = _rms(h1, g2_ref[...]).astype(c_ref.dtype)


def _out_proj(om, oh, h, w, g1, g2, c_dtype):
    T = h.shape[0]
    tm = min(T, 512)
    c_width = D_MODEL
    row = lambda i: (i, 0)
    const = lambda i: (0, 0)
    return pl.pallas_call(
        _out_proj_kernel,
        grid=(T // tm,),
        in_specs=[pl.BlockSpec((tm, GROUP_WIDTH), row),
                  pl.BlockSpec((tm, GROUP_WIDTH), row),
                  pl.BlockSpec((tm, D_MODEL), row),
                  pl.BlockSpec((D_MODEL, D_MODEL), const, pipeline_mode=pl.Buffered(1)),
                  pl.BlockSpec((1, D_MODEL), const),
                  pl.BlockSpec((1, D_MODEL), const)],
        out_specs=[pl.BlockSpec((tm, D_MODEL), row), pl.BlockSpec((tm, c_width), row)],
        out_shape=[jax.ShapeDtypeStruct((T, D_MODEL), F32),
                   jax.ShapeDtypeStruct((T, c_width), c_dtype)],
        compiler_params=_params(("parallel",)),
        name="out_proj",
    )(om, oh, h, w, g1[None, :], g2[None, :])


def _gate_up_kernel(te_ref, nv_ref, x_ref, wg_ref, wu_ref, o_ref, wg_bf, wu_bf):
    i = pl.program_id(1)
    valid = i < nv_ref[0]
    new_weights = jnp.logical_or(i == 0, te_ref[i] != te_ref[jnp.maximum(i - 1, 0)])

    def step(refresh):
        x = x_ref[...].astype(BF16)
        for c in range(0, o_ref.shape[1], GU_SUB):
            cols = slice(c, c + GU_SUB)
            if refresh:
                wg_bf[:, cols] = wg_ref[0, :, cols].astype(BF16)
                wu_bf[:, cols] = wu_ref[0, :, cols].astype(BF16)
            g = _dot(x, wg_bf[:, cols])
            u = _dot(x, wu_bf[:, cols])
            o_ref[:, cols] = (g * _sigmoid(g) * u).astype(BF16)

    pl.when(jnp.logical_and(valid, new_weights))(functools.partial(step, True))
    pl.when(jnp.logical_and(valid, jnp.logical_not(new_weights)))(functools.partial(step, False))

    @pl.when(jnp.logical_not(valid))
    def _():
        o_ref[...] = jnp.zeros_like(o_ref)


def _gate_up(x, w_gu, tile_expert, n_valid, tm, tf):
    R, xw = x.shape
    n_tiles = R // tm
    nf = FFN_DIM // tf

    def tile(i, nv):
        return jnp.minimum(i, nv[0] - 1)

    return pl.pallas_call(
        _gate_up_kernel,
        grid_spec=pltpu.PrefetchScalarGridSpec(
            num_scalar_prefetch=2,
            grid=(nf, n_tiles),
            in_specs=[pl.BlockSpec((tm, xw), lambda j, i, te, nv: (tile(i, nv), 0)),
                      pl.BlockSpec((1, D_MODEL, tf), lambda j, i, te, nv: (te[tile(i, nv)], 0, j)),
                      pl.BlockSpec((1, D_MODEL, tf),
                                   lambda j, i, te, nv: (te[tile(i, nv)], 0, j + nf))],
            out_specs=pl.BlockSpec((tm, tf), lambda j, i, te, nv: (i, j)),
            scratch_shapes=[pltpu.VMEM((D_MODEL, tf), BF16), pltpu.VMEM((D_MODEL, tf), BF16)],
        ),
        out_shape=jax.ShapeDtypeStruct((R, FFN_DIM), BF16),
        compiler_params=_params(("arbitrary", "arbitrary")),
        name="ffn_gate_up",
    )(tile_expert, n_valid, x, w_gu, w_gu)


def _down_kernel(te_ref, nv_ref, a_ref, w_ref, o_ref):
    valid = pl.program_id(1) < nv_ref[0]

    @pl.when(valid)
    def _():
        a = a_ref[...]
        for c in range(0, o_ref.shape[1], GU_SUB):
            o_ref[:, c:c + GU_SUB] = _dot(a, w_ref[0, :, c:c + GU_SUB])

    @pl.when(jnp.logical_not(valid))
    def _():
        o_ref[...] = jnp.zeros_like(o_ref)


def _down(act, w_down, tile_expert, n_valid, tm):
    R = act.shape[0]
    n_tiles = R // tm
    tn = 1024

    def tile(i, nv):
        return jnp.minimum(i, nv[0] - 1)

    return pl.pallas_call(
        _down_kernel,
        grid_spec=pltpu.PrefetchScalarGridSpec(
            num_scalar_prefetch=2,
            grid=(D_MODEL // tn, n_tiles),
            in_specs=[pl.BlockSpec((tm, FFN_DIM), lambda j, i, te, nv: (tile(i, nv), 0)),
                      pl.BlockSpec((1, FFN_DIM, tn), lambda j, i, te, nv: (te[tile(i, nv)], 0, j))],
            out_specs=pl.BlockSpec((tm, tn), lambda j, i, te, nv: (i, j)),
        ),
        out_shape=jax.ShapeDtypeStruct((R, D_MODEL), F32),
        compiler_params=_params(("arbitrary", "arbitrary")),
        name="ffn_down",
    )(tile_expert, n_valid, act, w_down)


def _ffn_epilogue(f, h1, p, wg, wp, g3, gp):
    h2 = h1 + _rms(f, g3)
    gate = _sigmoid(_dot(h2.astype(BF16), wg))
    proj = _dot(p.astype(BF16), wp)
    return h2 + _rms(gate * proj, gp)


def _post_ffn_kernel(f_ref, h1_ref, p_ref, wg_ref, wp_ref, g3_ref, gp_ref, o_ref):
    o_ref[...] = _ffn_epilogue(f_ref[...], h1_ref[...], p_ref[...], wg_ref[...], wp_ref[...],
                               g3_ref[...], gp_ref[...])


def _post_ffn(f, h1, p, layer, wg, wp, g3, gp):
    T = h1.shape[0]
    tm = min(T, 512)
    row = lambda i: (i, 0)
    const = lambda i: (0, 0)
    return pl.pallas_call(
        _post_ffn_kernel,
        grid=(T // tm,),
        in_specs=[pl.BlockSpec((tm, D_MODEL), row),
                  pl.BlockSpec((tm, D_MODEL), row),
                  pl.BlockSpec((None, tm, PLE_DIM), lambda i: (layer, i, 0)),
                  pl.BlockSpec((D_MODEL, D_MODEL), const, pipeline_mode=pl.Buffered(1)),
                  pl.BlockSpec((PLE_DIM, D_MODEL), const, pipeline_mode=pl.Buffered(1)),
                  pl.BlockSpec((1, D_MODEL), const),
                  pl.BlockSpec((1, D_MODEL), const)],
        out_specs=pl.BlockSpec((tm, D_MODEL), row),
        out_shape=jax.ShapeDtypeStruct((T, D_MODEL), F32),
        compiler_params=_params(("parallel",)),
        name="post_ffn",
    )(f, h1, p, wg, wp, g3[None, :], gp[None, :])


def _router_kernel(c_ref, wr_ref, tri_ref, route_ref, cnt_ref, carry):
    @pl.when(pl.program_id(0) == 0)
    def _():
        carry[...] = jnp.zeros_like(carry)

    logits = _dot(c_ref[...].astype(BF16), wr_ref[...])
    lane = lax.broadcasted_iota(jnp.int32, logits.shape, 1)
    lg = jnp.where(lane < N_EXPERTS, logits, -jnp.inf)
    m1 = jnp.max(lg, axis=-1, keepdims=True)
    i1 = jnp.min(jnp.where(lg == m1, lane, LANES), axis=-1, keepdims=True)
    lg2 = jnp.where(lane == i1, -jnp.inf, lg)
    m2 = jnp.max(lg2, axis=-1, keepdims=True)
    i2 = jnp.min(jnp.where(lg2 == m2, lane, LANES), axis=-1, keepdims=True)
    e = jnp.exp(m2 - m1)
    g1 = 1.0 / (1.0 + e)
    g2 = e / (1.0 + e)
    onehot = jnp.where((lane == i1) | (lane == i2), 1.0, 0.0)
    before = _dot(tri_ref[...], onehot.astype(BF16)) + carry[0:1, :]
    rank1 = jnp.sum(jnp.where(lane == i1, before, 0.0), axis=-1, keepdims=True)
    rank2 = jnp.sum(jnp.where(lane == i2, before, 0.0), axis=-1, keepdims=True)
    total = carry[0:1, :] + jnp.sum(onehot, axis=0, keepdims=True)
    carry[...] = jnp.broadcast_to(total, carry.shape)
    cnt_ref[...] = jnp.broadcast_to(total, cnt_ref.shape)
    out = jnp.where(lane == 0, i1.astype(F32), 0.0)
    out = jnp.where(lane == 1, i2.astype(F32), out)
    out = jnp.where(lane == 2, g1, out)
    out = jnp.where(lane == 3, g2, out)
    out = jnp.where(lane == 4, rank1, out)
    out = jnp.where(lane == 5, rank2, out)
    route_ref[...] = out


def _router(c, w_router):
    T = c.shape[0]
    tr = min(T, 512)
    wr = jnp.zeros((D_MODEL, LANES), BF16).at[:, :N_EXPERTS].set(w_router.astype(BF16))
    tri = jnp.asarray(np.tril(np.ones((tr, tr), np.float32), -1), dtype=BF16)
    return pl.pallas_call(
        _router_kernel,
        grid=(T // tr,),
        in_specs=[pl.BlockSpec((tr, c.shape[1]), lambda i: (i, 0)),
                  pl.BlockSpec((D_MODEL, LANES), lambda i: (0, 0)),
                  pl.BlockSpec((tr, tr), lambda i: (0, 0))],
        out_specs=[pl.BlockSpec((tr, LANES), lambda i: (i, 0)),
                   pl.BlockSpec((8, LANES), lambda i: (0, 0))],
        out_shape=[jax.ShapeDtypeStruct((T, LANES), F32),
                   jax.ShapeDtypeStruct((8, LANES), F32)],
        scratch_shapes=[pltpu.VMEM((8, LANES), F32)],
        compiler_params=_params(("arbitrary",)),
        name="moe_router",
    )(c, wr, tri)


def _dispatch_kernel(p1_ref, p2_ref, pad_ref, nv_ref, c_ref, w_ref, xs_ref, wbf_ref,
                     zbuf, sem, zsem, *, tm, seg_tile, n_row_steps):
    i = pl.program_id(0)
    base = i * tm
    zr = zbuf.shape[0]

    wbf_ref[...] = w_ref[...].astype(BF16)

    @pl.when(i == 0)
    def _():
        zbuf[...] = jnp.zeros_like(zbuf)

        def zero_tile(start):
            first = pl.multiple_of(start, zr)
            copies = [pltpu.make_async_copy(zbuf, xs_ref.at[pl.ds(first + k * zr, zr)], zsem)
                      for k in range(seg_tile // zr)]
            for cp in copies:
                cp.start()
            for cp in copies:
                cp.wait()

        def zero_padding(e, _):
            @pl.when(pad_ref[e] >= 0)
            def _():
                zero_tile(pad_ref[e])
            return 0

        def zero_unused(t, _):
            zero_tile(t * seg_tile)
            return 0

        lax.fori_loop(0, N_EXPERTS, zero_padding, 0)
        lax.fori_loop(nv_ref[0], xs_ref.shape[0] // seg_tile, zero_unused, 0)

    def row_copy(r, dst):
        return pltpu.make_async_copy(c_ref.at[pl.ds(r, 1)], xs_ref.at[pl.ds(dst, 1)], sem)

    @pl.when(i < n_row_steps)
    def _():
        for r in range(tm):
            row_copy(r, p1_ref[base + r]).start(priority=0)
            row_copy(r, p2_ref[base + r]).start(priority=1)
        for _r in range(2 * tm):
            row_copy(0, 0).wait()


def _dispatch(c, pos1, pos2, pad_tile_row, n_valid, n_rows, seg_tile, w_down):
    T, width = c.shape
    tm = min(T, 256)
    n_row_steps = T // tm
    w2d = w_down.reshape(N_EXPERTS * FFN_DIM, D_MODEL)
    n_cast_steps = w2d.shape[0] // CAST_ROWS
    n_steps = max(n_row_steps, n_cast_steps)
    cast_block = lambda i, p1, p2, pt, nv: (jnp.minimum(i, n_cast_steps - 1), 0)
    xs, w_bf = pl.pallas_call(
        functools.partial(_dispatch_kernel, tm=tm, seg_tile=seg_tile, n_row_steps=n_row_steps),
        grid_spec=pltpu.PrefetchScalarGridSpec(
            num_scalar_prefetch=4,
            grid=(n_steps,),
            in_specs=[pl.BlockSpec((tm, width),
                                   lambda i, p1, p2, pt, nv: (jnp.minimum(i, n_row_steps - 1), 0)),
                      pl.BlockSpec((CAST_ROWS, D_MODEL), cast_block)],
            out_specs=[pl.BlockSpec(memory_space=pl.ANY),
                       pl.BlockSpec((CAST_ROWS, D_MODEL), cast_block)],
            scratch_shapes=[pltpu.VMEM((min(seg_tile, 64), width), c.dtype),
                            pltpu.SemaphoreType.DMA(()),
                            pltpu.SemaphoreType.DMA(())],
        ),
        out_shape=[jax.ShapeDtypeStruct((n_rows, width), c.dtype),
                   jax.ShapeDtypeStruct(w2d.shape, BF16)],
        compiler_params=_params(("arbitrary",)),
        name="moe_dispatch",
    )(pos1, pos2, pad_tile_row, n_valid, c, w2d)
    return xs, w_bf.reshape(w_down.shape)


def _combine_kernel(p1_ref, p2_ref, ys_ref, route_ref, h1_ref, p_ref, wg_ref, wp_ref,
                    g3_ref, gp_ref, o_ref, buf, sems, *, tm):
    i = pl.program_id(0)
    slot = i % 2

    def row_copy(src, s, k, r):
        return pltpu.make_async_copy(ys_ref.at[pl.ds(src, 1)], buf.at[s, k, pl.ds(r, 1)],
                                     sems.at[s])

    def gather_tile(tile, s):
        base = tile * tm
        for r in range(tm):
            row_copy(p1_ref[base + r], s, 0, r).start(priority=0)
            row_copy(p2_ref[base + r], s, 1, r).start(priority=1)

    @pl.when(i == 0)
    def _():
        gather_tile(0, 0)

    @pl.when(i + 1 < pl.num_programs(0))
    def _():
        gather_tile(i + 1, 1 - slot)

    for _r in range(2 * tm):
        row_copy(0, slot, 0, 0).wait()
    route = route_ref[...]
    f = route[:, 2:3] * buf[slot, 0] + route[:, 3:4] * buf[slot, 1]
    o_ref[...] = _ffn_epilogue(f, h1_ref[...], p_ref[...], wg_ref[...], wp_ref[...],
                               g3_ref[...], gp_ref[...])


def _combine(ys, pos1, pos2, route, h1, p, layer, wg, wp, g3, gp):
    T = h1.shape[0]
    tm = min(T, 256)
    row = lambda i, p1, p2: (i, 0)
    const = lambda i, p1, p2: (0, 0)
    return pl.pallas_call(
        functools.partial(_combine_kernel, tm=tm),
        grid_spec=pltpu.PrefetchScalarGridSpec(
            num_scalar_prefetch=2,
            grid=(T // tm,),
            in_specs=[pl.BlockSpec(memory_space=pl.ANY),
                      pl.BlockSpec((tm, LANES), row),
                      pl.BlockSpec((tm, D_MODEL), row),
                      pl.BlockSpec((None, tm, PLE_DIM), lambda i, p1, p2: (layer, i, 0)),
                      pl.BlockSpec((D_MODEL, D_MODEL), const, pipeline_mode=pl.Buffered(1)),
                      pl.BlockSpec((PLE_DIM, D_MODEL), const, pipeline_mode=pl.Buffered(1)),
                      pl.BlockSpec((1, D_MODEL), const),
                      pl.BlockSpec((1, D_MODEL), const)],
            out_specs=pl.BlockSpec((tm, D_MODEL), row),
            scratch_shapes=[pltpu.VMEM((2, 2, tm, D_MODEL), F32),
                            pltpu.SemaphoreType.DMA((2,))],
        ),
        out_shape=jax.ShapeDtypeStruct((T, D_MODEL), F32),
        compiler_params=_params(("arbitrary",)),
        name="moe_combine",
    )(pos1, pos2, ys, route, h1, p, wg, wp, g3[None, :], gp[None, :])


def _swap_halves(w):
    half = w.shape[-1] // 2
    return jnp.concatenate([w[..., half:], w[..., :half]], axis=-1)


def _prep_w_in(w):
    w = w.astype(BF16)
    k_rope = w[:, 640:704]
    pad64 = jnp.zeros((D_MODEL, 64), w.dtype)
    cols = [w[:, :640], k_rope, pad64, _swap_halves(k_rope), pad64,
            jnp.zeros((D_MODEL, HGRN_COL0 - 896), w.dtype), w[:, 704:]]
    return jnp.concatenate(cols, axis=1).astype(BF16)


def _prep_w_uq(w):
    w = w.reshape(Q_RANK, N_HEADS, QK_DIM)
    nope = w[:, :, :NOPE_DIM]
    pe = w[:, :, NOPE_DIM:]
    pad = jnp.zeros((Q_RANK, N_HEADS, LANES - ROPE_DIM), w.dtype)
    pe_pad = jnp.concatenate([pe, pad], axis=-1)
    pe_swap = jnp.concatenate([_swap_halves(pe), pad], axis=-1)
    parts = [x.reshape(Q_RANK, GROUP_WIDTH) for x in (nope, pe_pad, pe_swap)]
    return jnp.concatenate(parts, axis=1).astype(BF16)


def _prep_w_ukv(w):
    w = w.reshape(KV_RANK, N_HEADS, NOPE_DIM + V_DIM)
    k = w[:, :, :NOPE_DIM].reshape(KV_RANK, GROUP_WIDTH)
    v = w[:, :, NOPE_DIM:].reshape(KV_RANK, GROUP_WIDTH)
    return jnp.concatenate([k, v], axis=1).astype(BF16)


def _moe_plan(route, counts, tm, n_tiles):
    e1 = route[:, 0].astype(jnp.int32)
    e2 = route[:, 1].astype(jnp.int32)
    rank1 = route[:, 4].astype(jnp.int32)
    rank2 = route[:, 5].astype(jnp.int32)
    cnt = counts[0, :N_EXPERTS].astype(jnp.int32)
    padded = ((cnt + tm - 1) // tm) * tm
    seg_end = jnp.cumsum(padded)
    seg_start = seg_end - padded
    pos1 = seg_start[e1] + rank1
    pos2 = seg_start[e2] + rank2
    tile_start = jnp.arange(n_tiles, dtype=jnp.int32) * tm
    tile_expert = jnp.sum(tile_start[:, None] >= seg_end[None, :], axis=1).astype(jnp.int32)
    tile_expert = jnp.minimum(tile_expert, N_EXPERTS - 1)
    n_valid = (seg_end[-1] // tm).astype(jnp.int32)[None]
    pad_tile_row = jnp.where(padded > 0, seg_end - tm, -1).astype(jnp.int32)
    return pos1, pos2, tile_expert, n_valid, pad_tile_row


def kernel(x, p, positions, sandwich_norms, w_in, mla_q_norm, mla_kv_norm, w_uq, w_ukv,
           hgrn_lb_logits, hgrn_out_norm, w_out, ffn_w_gu, ffn_w_down, moe_w_router,
           moe_w_gu, moe_w_down, ple_w_proj, ple_w_gate, ple_norm):
    B, S, _ = x.shape
    T = B * S
    h = x.reshape(T, D_MODEL)
    ct, st = _rope_tables(positions)
    hgrn_consts = _hgrn_consts(HGRN_CHUNK)
    tm = min(T, 512)
    for l in range(DEPTH):
        z = _norm_matmul(h, sandwich_norms[l, 0], _prep_w_in(w_in[l]))
        q, k, v = _mla_proj(z, ct, st, mla_q_norm[l], mla_kv_norm[l],
                            _prep_w_uq(w_uq[l]), _prep_w_ukv(w_ukv[l]), B, S)
        o_mla = _attention(q, k, v).reshape(T, GROUP_WIDTH)
        o_hgrn = _hgrn(z, hgrn_lb_logits, hgrn_out_norm[l], hgrn_consts, l, B, S)
        moe = l % 2 == 1
        h1, c = _out_proj(o_mla, o_hgrn, h, w_out[l].astype(BF16),
                          sandwich_norms[l, 1], sandwich_norms[l, 2],
                          F32 if moe else BF16)
        wg = ple_w_gate[l].astype(BF16)
        wp = ple_w_proj[l].astype(BF16)
        p_all = p.reshape(DEPTH, T, PLE_DIM)
        if not moe:
            def one_expert(rows):
                return jnp.zeros((T // rows,), jnp.int32), jnp.full((1,), T // rows, jnp.int32)

            tm_dense = min(T, 1024)
            act = _gate_up(c, ffn_w_gu[l // 2][None], *one_expert(tm_dense), tm_dense, 512)
            f = _down(act, ffn_w_down[l // 2][None].astype(BF16), *one_expert(tm), tm)
            h = _post_ffn(f, h1, p_all, l, wg, wp, sandwich_norms[l, 3], ple_norm[l])
        else:
            n_tiles = (2 * T) // tm + N_EXPERTS
            route, counts = _router(c, moe_w_router[l // 2])
            pos1, pos2, tile_expert, n_valid, pad_tile_row = _moe_plan(route, counts, tm, n_tiles)
            xs, w_down_bf = _dispatch(c, pos1, pos2, pad_tile_row, n_valid, n_tiles * tm, tm,
                                      moe_w_down[l // 2])
            act = _gate_up(xs, moe_w_gu[l // 2], tile_expert, n_valid, tm, 1024)
            ys = _down(act, w_down_bf, tile_expert, n_valid, tm)
            h = _combine(ys, pos1, pos2, route, h1, p_all, l, wg, wp,
                         sandwich_norms[l, 3], ple_norm[l])
    return h.reshape(B, S, D_MODEL)
```

```python
import functools
import math

import numpy as np
import jax
import jax.numpy as jnp
from jax import lax
from jax.experimental import pallas as pl
from jax.experimental.pallas import tpu as pltpu

F32 = jnp.float32
BF16 = jnp.bfloat16

D_MODEL = 2048
DEPTH = 2
N_HEADS = 8
NOPE_DIM = 128
ROPE_DIM = 64
V_DIM = 128
QK_DIM = NOPE_DIM + ROPE_DIM
Q_RANK = 384
KV_RANK = 256
HGRN_K = 128
HGRN_V = 128
GROUP_WIDTH = N_HEADS * 128
FFN_DIM = 7168
N_EXPERTS = 8
PLE_DIM = 256
ROPE_THETA = 10000.0
LB_FLOOR = 1e-30
EPS = 1e-6

LANES = 128
QK_PAD = 256
Q_SCALE = QK_DIM ** -0.5 * math.log2(math.e)
Z_MLA = 1024
Z_WIDTH = 5120
HGRN_COL0 = 1024
HGRN_CHUNK = 64
HGRN_HEADS_PER_STEP = 4
GU_SUB = 512
IN_PROJ_SUB = 1024
CAST_SLAB_BYTES = 4 * 1024 * 1024
VMEM_LIMIT = 56 * 1024 * 1024


def _params(semantics, vmem=VMEM_LIMIT):
    return pltpu.CompilerParams(dimension_semantics=semantics, vmem_limit_bytes=vmem)


def _rms(x, gain_row):
    ms = jnp.mean(x * x, axis=-1, keepdims=True)
    return x * lax.rsqrt(ms + EPS) * gain_row


def _dot(a, b):
    return jnp.dot(a, b, preferred_element_type=F32)


def _dot_nt(a, b):
    return lax.dot_general(a, b, (((1,), (1,)), ((), ())), preferred_element_type=F32)


def _dot_tn(a, b):
    return lax.dot_general(a, b, (((0,), (0,)), ((), ())), preferred_element_type=F32)


def _sigmoid(x):
    return 1.0 / (1.0 + jnp.exp(-x))


def _fold_lanes(x, op):
    parts = [x[:, c:c + LANES] for c in range(0, x.shape[1], LANES)]
    while len(parts) > 1:
        parts = [op(a, b) for a, b in zip(parts[0::2], parts[1::2])] + parts[len(parts) & ~1:]
    return parts[0]


def _rope_kernel(pos_ref, inv_ref, sgn_ref, ct_ref, st_ref):
    ang = pos_ref[...].astype(F32) * inv_ref[...]
    keep = jnp.abs(sgn_ref[...])
    ct_ref[...] = jnp.cos(ang) * keep
    st_ref[...] = jnp.sin(ang) * sgn_ref[...]


def _rope_tables(positions):
    T = positions.size
    tt = min(T, 1024)
    half = ROPE_DIM // 2
    inv_freq = 1.0 / (ROPE_THETA ** (jnp.arange(0, ROPE_DIM, 2, dtype=F32) / ROPE_DIM))
    inv_row = jnp.concatenate([inv_freq, inv_freq, jnp.zeros((LANES - ROPE_DIM,), F32)])[None, :]
    sgn = np.zeros((1, LANES), np.float32)
    sgn[0, :half] = -1.0
    sgn[0, half:ROPE_DIM] = 1.0
    return pl.pallas_call(
        _rope_kernel,
        grid=(T // tt,),
        in_specs=[pl.BlockSpec((tt, 1), lambda i: (i, 0)),
                  pl.BlockSpec((1, LANES), lambda i: (0, 0)),
                  pl.BlockSpec((1, LANES), lambda i: (0, 0))],
        out_specs=[pl.BlockSpec((tt, LANES), lambda i: (i, 0))] * 2,
        out_shape=[jax.ShapeDtypeStruct((T, LANES), F32)] * 2,
        compiler_params=_params(("parallel",)),
        name="rope_tables",
    )(positions.reshape(T, 1), inv_row, jnp.asarray(sgn))


def _norm_matmul_kernel(x_ref, g_ref, w_ref, o_ref):
    a = _rms(x_ref[...], g_ref[...]).astype(BF16)
    for c in range(0, o_ref.shape[1], IN_PROJ_SUB):
        o_ref[:, c:c + IN_PROJ_SUB] = _dot(a, w_ref[:, c:c + IN_PROJ_SUB])


def _norm_matmul(x, gain, w):
    T, K = x.shape
    N = w.shape[1]
    tm = min(T, 512)
    return pl.pallas_call(
        _norm_matmul_kernel,
        grid=(T // tm,),
        in_specs=[pl.BlockSpec((tm, K), lambda i: (i, 0)),
                  pl.BlockSpec((1, K), lambda i: (0, 0)),
                  pl.BlockSpec((K, N), lambda i: (0, 0), pipeline_mode=pl.Buffered(1))],
        out_specs=pl.BlockSpec((tm, N), lambda i: (i, 0)),
        out_shape=jax.ShapeDtypeStruct((T, N), F32),
        compiler_params=_params(("parallel",)),
        name="in_proj",
    )(x, gain[None, :], w)


def _mla_proj_kernel(z_ref, ct_ref, st_ref, qn_ref, kvn_ref, wq_ref, wkv_ref,
                     q_ref, k_ref, v_ref):
    z = z_ref[...]
    ct = ct_ref[...]
    st = st_ref[...]
    aq = _rms(z[:, :Q_RANK], qn_ref[...]).astype(BF16)
    akv = _rms(z[:, Q_RANK:Q_RANK + KV_RANK], kvn_ref[...]).astype(BF16)
    k_a = z[:, 640:768]
    k_b = z[:, 768:896]
    k_pe = (k_a * ct + k_b * st).astype(BF16)
    q = _dot(aq, wq_ref[...])
    kv = _dot(akv, wkv_ref[...])
    for h in range(N_HEADS):
        lo, hi = h * LANES, (h + 1) * LANES
        q_ref[0, h, :, 0:LANES] = (q[:, lo:hi] * Q_SCALE).astype(BF16)
        q_pe = q[:, GROUP_WIDTH + lo:GROUP_WIDTH + hi] * ct \
            + q[:, 2 * GROUP_WIDTH + lo:2 * GROUP_WIDTH + hi] * st
        q_ref[0, h, :, LANES:QK_PAD] = (q_pe * Q_SCALE).astype(BF16)
        k_ref[0, h, :, 0:LANES] = kv[:, lo:hi].astype(BF16)
        k_ref[0, h, :, LANES:QK_PAD] = k_pe
        v_ref[0, h, :, :] = kv[:, GROUP_WIDTH + lo:GROUP_WIDTH + hi].astype(BF16)


def _mla_proj(z, ct, st, q_norm, kv_norm, wq, wkv, B, S):
    ts = min(S, 512)
    ns = S // ts
    tok = lambda b, i: (b * ns + i, 0)
    const = lambda b, i: (0, 0)
    head_out = lambda b, i: (b, 0, i, 0)
    return pl.pallas_call(
        _mla_proj_kernel,
        grid=(B, ns),
        in_specs=[pl.BlockSpec((ts, Z_MLA), tok),
                  pl.BlockSpec((ts, LANES), tok),
                  pl.BlockSpec((ts, LANES), tok),
                  pl.BlockSpec((1, Q_RANK), const),
                  pl.BlockSpec((1, KV_RANK), const),
                  pl.BlockSpec(wq.shape, const),
                  pl.BlockSpec(wkv.shape, const)],
        out_specs=[pl.BlockSpec((1, N_HEADS, ts, QK_PAD), head_out),
                   pl.BlockSpec((1, N_HEADS, ts, QK_PAD), head_out),
                   pl.BlockSpec((1, N_HEADS, ts, V_DIM), head_out)],
        out_shape=[jax.ShapeDtypeStruct((B, N_HEADS, S, QK_PAD), BF16),
                   jax.ShapeDtypeStruct((B, N_HEADS, S, QK_PAD), BF16),
                   jax.ShapeDtypeStruct((B, N_HEADS, S, V_DIM), BF16)],
        compiler_params=_params(("parallel", "parallel")),
        name="mla_proj",
    )(z, ct, st, q_norm[None, :], kv_norm[None, :], wq, wkv)


def _attn_kernel(*refs, tk, with_cast):
    if with_cast:
        q_ref, k_ref, v_ref, w_ref, o_ref, wbf_ref, s_a, s_b = refs
    else:
        q_ref, k_ref, v_ref, o_ref, s_a, s_b = refs
    i = pl.program_id(2)
    top_rows = pl.ds(0, tk)
    bot_rows = pl.ds(tk, tk)

    def kv_rows(u):
        return pl.ds(pl.multiple_of(u * tk, tk), tk)

    def scores(u):
        return _dot_nt(q_ref[0, 0], k_ref[0, 0, kv_rows(u), :])

    lane = lax.broadcasted_iota(jnp.int32, (tk, LANES), 1)
    ones_col = jnp.where(lane == 0, 1.0, 0.0).astype(BF16)

    def values(u):
        return jnp.concatenate([v_ref[0, 0, kv_rows(u), :], ones_col], axis=1)

    def update(state, s, v, masked):
        m, acc = state
        if masked:
            row = lax.broadcasted_iota(jnp.int32, (tk, tk), 0)
            col = lax.broadcasted_iota(jnp.int32, (tk, tk), 1)
            s = jnp.where(col <= row, s, -jnp.inf)
        m_new = jnp.maximum(m, jnp.max(_fold_lanes(s, jnp.maximum), axis=-1, keepdims=True))
        alpha = jnp.exp2(m - m_new)
        p = jnp.exp2((s - m_new).astype(BF16))
        acc = alpha * acc + _dot(p, v)
        return m_new, acc

    def both(top, bot, s_ref, u):
        v = values(u)
        return (update(top, s_ref[top_rows, :], v, False),
                update(bot, s_ref[bot_rows, :], v, False))

    s_a[...] = scores(0)
    if with_cast:
        wbf_ref[...] = w_ref[...].astype(BF16)

    def body(t, state):
        top, bot = state
        s_b[...] = scores(2 * t + 1)
        top, bot = both(top, bot, s_a, 2 * t)
        s_a[...] = scores(2 * t + 2)
        top, bot = both(top, bot, s_b, 2 * t + 1)
        return top, bot

    init = (jnp.full((tk, 1), -jnp.inf, F32), jnp.zeros((tk, 2 * LANES), F32))
    top, bot = lax.fori_loop(0, i, body, (init, init))
    s_b[bot_rows, :] = _dot_nt(q_ref[0, 0, bot_rows, :], k_ref[0, 0, kv_rows(2 * i + 1), :])
    v0 = values(2 * i)
    top = update(top, s_a[top_rows, :], v0, True)
    bot = update(bot, s_a[bot_rows, :], v0, False)
    bot = update(bot, s_b[bot_rows, :], values(2 * i + 1), True)
    for rows, (_, acc) in ((top_rows, top), (bot_rows, bot)):
        o_ref[0, rows, :] = (acc[:, :V_DIM] / acc[:, V_DIM:V_DIM + 1]).astype(BF16)


def _attention(q, k, v, w_cast=None):
    B, H, S, _ = q.shape
    tq = min(S, 1024)
    tk = tq // 2
    nq = S // tq
    in_specs = [pl.BlockSpec((1, 1, tq, QK_PAD), lambda b, h, i: (b, h, i, 0)),
                pl.BlockSpec((1, 1, S, QK_PAD), lambda b, h, i: (b, h, 0, 0)),
                pl.BlockSpec((1, 1, S, V_DIM), lambda b, h, i: (b, h, 0, 0))]
    out_specs = [pl.BlockSpec((1, tq, V_DIM), lambda b, h, i: (b, i, h))]
    out_shape = [jax.ShapeDtypeStruct((B, S, H * V_DIM), BF16)]
    args = [q, k, v]
    if w_cast is not None:
        slab = w_cast.shape[0] // (B * H * nq)
        spec = pl.BlockSpec((slab, w_cast.shape[1]), lambda b, h, i: ((b * H + h) * nq + i, 0))
        in_specs.append(spec)
        out_specs.append(spec)
        out_shape.append(jax.ShapeDtypeStruct(w_cast.shape, BF16))
        args.append(w_cast)
    outs = pl.pallas_call(
        functools.partial(_attn_kernel, tk=tk, with_cast=w_cast is not None),
        grid=(B, H, nq),
        in_specs=in_specs,
        out_specs=out_specs,
        out_shape=out_shape,
        scratch_shapes=[pltpu.VMEM((tq, tk), F32), pltpu.VMEM((tq, tk), F32)],
        compiler_params=_params(("parallel", "parallel", "arbitrary")),
        name="mla_attention",
    )(*args)
    return outs if w_cast is not None else outs[0]


def _attention_and_cast(q, k, v, w):
    B, H, S, _ = q.shape
    steps = B * H * (S // min(S, 1024))
    rows, cols = w.shape
    slab = rows // steps
    if rows % steps == 0 and slab % 16 == 0 and slab * cols * 4 <= CAST_SLAB_BYTES:
        return _attention(q, k, v, w)
    return _attention(q, k, v), w.astype(BF16)


def _hgrn_consts(C):
    levels = [C >> (i + 1) for i in range(int(math.log2(C)))]
    t = np.arange(C)
    u = np.arange(C)
    mats = [(u[None, :] <= t[:, None]),
            (u[None, :] > t[:, None])]
    pair_masks = [np.eye(C, dtype=bool)]
    for m in levels:
        ref = (t // (2 * m)) * (2 * m) + m - 1
        hi = ((t // m) % 2) == 1
        rng_hi = (u[None, :] > ref[:, None]) & (u[None, :] <= t[:, None])
        rng_lo = (u[None, :] > t[:, None]) & (u[None, :] <= ref[:, None])
        mats.append(np.where(hi[:, None], rng_hi, rng_lo))
        same = (t[:, None] // (2 * m)) == (t[None, :] // (2 * m))
        pair_masks.append(hi[:, None] & (~hi[None, :]) & same)
    stack = np.concatenate(mats, axis=0).astype(np.float32)
    sums = jnp.asarray(np.concatenate([stack] * 3, axis=1), dtype=BF16)
    heads = np.eye(HGRN_HEADS_PER_STEP, dtype=bool)
    stacked = np.stack([np.kron(heads, m) for m in pair_masks])
    return sums, jnp.asarray(stacked.astype(np.float32))


def _hgrn_kernel(zq_ref, zf_ref, zi_ref, zg_ref, lbl_ref, on_ref,
                 sums_ref, pm_ref, o_ref, st_scr, *, layer, ts):
    C = HGRN_CHUNK
    n_levels = pm_ref.shape[0] - 1

    @pl.when(pl.program_id(2) == 0)
    def _():
        st_scr[...] = jnp.zeros_like(st_scr)

    lg = lbl_ref[...]
    ex = jnp.exp(lg - jnp.max(lg, axis=0, keepdims=True))
    pr = ex / jnp.sum(ex, axis=0, keepdims=True)
    cum = pr[0:1, :]
    for r in range(1, layer + 1):
        cum = cum + pr[r:r + 1, :]
    lb = cum - pr[0:1, :]
    lb_floor = jnp.maximum(lb, LB_FLOOR)
    one_m_lb = 1.0 - lb
    gain = on_ref[...]

    for c in range(ts // C):
        rows = pl.ds(c * C, C)
        q_raw = zq_ref[rows, :]
        zf = zf_ref[rows, :]
        g_raw = zg_ref[rows, :]
        v_all = zi_ref[rows, :].astype(BF16)
        q_all = q_raw * _sigmoid(q_raw)
        e = jnp.exp(-jnp.abs(zf))
        r = 1.0 / (1.0 + e)
        er = e * r
        pos = zf >= 0
        log_f = jnp.log(lb_floor + one_m_lb * jnp.where(pos, r, er))
        kk_all = one_m_lb * jnp.where(pos, er, r)
        gate_all = g_raw * _sigmoid(g_raw)
        p0 = log_f.astype(BF16)
        r1 = log_f - p0.astype(F32)
        p1 = r1.astype(BF16)
        p2 = (r1 - p1.astype(F32)).astype(BF16)
        sums_all = _dot(sums_ref[...], jnp.concatenate([p0, p1, p2], axis=0))

        def stack(x):
            return jnp.concatenate([x[:, h * LANES:(h + 1) * LANES]
                                    for h in range(HGRN_HEADS_PER_STEP)], axis=0)

        q_st = stack(q_all).astype(BF16)
        kk_st = stack(kk_all).astype(BF16)
        a = pm_ref[0] * _dot_nt(q_st, kk_st)
        for lv in range(n_levels):
            dec = jnp.exp(stack(sums_all[(2 + lv) * C:(3 + lv) * C])).astype(BF16)
            a = a + pm_ref[lv + 1] * _dot_nt(q_st * dec, kk_st * dec)
        o_intra = _dot(a.astype(BF16), stack(v_all))
        b_all = sums_all[0:C]
        st_all = st_scr[...]
        o_inter = _dot_nt(stack(q_all * jnp.exp(b_all)).astype(BF16), st_all.astype(BF16))
        k_end = (kk_all * jnp.exp(sums_all[C:2 * C])).astype(BF16)
        decay = jnp.exp(b_all[C - 1:C, :])
        for hh in range(HGRN_HEADS_PER_STEP):
            lanes = slice(hh * LANES, (hh + 1) * LANES)
            srows = slice(hh * HGRN_V, (hh + 1) * HGRN_V)
            o = o_intra[hh * C:(hh + 1) * C] + o_inter[hh * C:(hh + 1) * C, lanes]
            st_scr[srows, :] = st_all[srows] * decay[:, lanes] \
                + _dot_tn(v_all[:, lanes], k_end[:, lanes])
            o_ref[rows, pl.ds(hh * LANES, LANES)] = \
                (_rms(o, gain) * gate_all[:, lanes]).astype(BF16)


def _hgrn(z, lb_logits, out_norm, consts, layer, B, S):
    T = B * S
    ts = min(S, 512)
    ns = S // ts
    sums, pair_masks = consts
    hp = HGRN_HEADS_PER_STEP
    width = hp * LANES
    col0 = HGRN_COL0 // width
    groups = N_HEADS // hp

    def zspec(part):
        return pl.BlockSpec((ts, width),
                            lambda b, h, i: (b * ns + i, col0 + part * groups + h))

    full = lambda arr: pl.BlockSpec(arr.shape, lambda b, h, i: (0,) * arr.ndim)
    kern = functools.partial(_hgrn_kernel, layer=layer, ts=ts)
    return pl.pallas_call(
        kern,
        grid=(B, groups, ns),
        in_specs=[zspec(0), zspec(1), zspec(2), zspec(3),
                  pl.BlockSpec((DEPTH, width), lambda b, h, i: (0, h)),
                  pl.BlockSpec((1, HGRN_V), lambda b, h, i: (0, 0)),
                  full(sums), full(pair_masks)],
        out_specs=pl.BlockSpec((ts, width), lambda b, h, i: (b * ns + i, h)),
        out_shape=jax.ShapeDtypeStruct((T, GROUP_WIDTH), BF16),
        scratch_shapes=[pltpu.VMEM((hp * HGRN_V, HGRN_K), F32)],
        compiler_params=_params(("parallel", "parallel", "arbitrary")),
        name="hgrn2",
    )(z, z, z, z, lb_logits, out_norm[None, :], sums, pair_masks)


def _out_proj_kernel(om_ref, oh_ref, h_ref, w_ref, g1_ref, g2_ref, h1_ref, c_ref):
    y = _dot(om_ref[...], w_ref[0:GROUP_WIDTH, :]) + _dot(oh_ref[...], w_ref[GROUP_WIDTH:, :])
    h1 = h_ref[...] + _rms(y, g1_ref[...])
    h1_ref[...] = h1
    c_ref[...] = _rms(h1, g2_ref[...]).astype(c_ref.dtype)


def _out_proj(om, oh, h, w, g1, g2, c_dtype):
    T = h.shape[0]
    tm = min(T, 512)
    c_width = D_MODEL
    row = lambda i: (i, 0)
    const = lambda i: (0, 0)
    return pl.pallas_call(
        _out_proj_kernel,
        grid=(T // tm,),
        in_specs=[pl.BlockSpec((tm, GROUP_WIDTH), row),
                  pl.BlockSpec((tm, GROUP_WIDTH), row),
                  pl.BlockSpec((tm, D_MODEL), row),
                  pl.BlockSpec((D_MODEL, D_MODEL), const, pipeline_mode=pl.Buffered(1)),
                  pl.BlockSpec((1, D_MODEL), const),
                  pl.BlockSpec((1, D_MODEL), const)],
        out_specs=[pl.BlockSpec((tm, D_MODEL), row), pl.BlockSpec((tm, c_width), row)],
        out_shape=[jax.ShapeDtypeStruct((T, D_MODEL), F32),
                   jax.ShapeDtypeStruct((T, c_width), c_dtype)],
        compiler_params=_params(("parallel",)),
        name="out_proj",
    )(om, oh, h, w, g1[None, :], g2[None, :])


def _gate_up_kernel(te_ref, nv_ref, x_ref, wg_ref, wu_ref, o_ref, wg_bf, wu_bf):
    i = pl.program_id(1)
    valid = i < nv_ref[0]
    new_weights = jnp.logical_or(i == 0, te_ref[i] != te_ref[jnp.maximum(i - 1, 0)])

    def step(refresh):
        x = x_ref[...].astype(BF16)
        for c in range(0, o_ref.shape[1], GU_SUB):
            cols = slice(c, c + GU_SUB)
            if refresh:
                wg_bf[:, cols] = wg_ref[0, :, cols].astype(BF16)
                wu_bf[:, cols] = wu_ref[0, :, cols].astype(BF16)
            g = _dot(x, wg_bf[:, cols])
            u = _dot(x, wu_bf[:, cols])
            o_ref[:, cols] = (g * _sigmoid(g) * u).astype(BF16)

    pl.when(jnp.logical_and(valid, new_weights))(functools.partial(step, True))
    pl.when(jnp.logical_and(valid, jnp.logical_not(new_weights)))(functools.partial(step, False))

    @pl.when(jnp.logical_not(valid))
    def _():
        o_ref[...] = jnp.zeros_like(o_ref)


def _gate_up(x, w_gu, tile_expert, n_valid, tm, tf):
    R, xw = x.shape
    n_tiles = R // tm
    nf = FFN_DIM // tf

    def tile(i, nv):
        return jnp.minimum(i, nv[0] - 1)

    return pl.pallas_call(
        _gate_up_kernel,
        grid_spec=pltpu.PrefetchScalarGridSpec(
            num_scalar_prefetch=2,
            grid=(nf, n_tiles),
            in_specs=[pl.BlockSpec((tm, xw), lambda j, i, te, nv: (tile(i, nv), 0)),
                      pl.BlockSpec((1, D_MODEL, tf), lambda j, i, te, nv: (te[tile(i, nv)], 0, j)),
                      pl.BlockSpec((1, D_MODEL, tf),
                                   lambda j, i, te, nv: (te[tile(i, nv)], 0, j + nf))],
            out_specs=pl.BlockSpec((tm, tf), lambda j, i, te, nv: (i, j)),
            scratch_shapes=[pltpu.VMEM((D_MODEL, tf), BF16), pltpu.VMEM((D_MODEL, tf), BF16)],
        ),
        out_shape=jax.ShapeDtypeStruct((R, FFN_DIM), BF16),
        compiler_params=_params(("arbitrary", "arbitrary")),
        name="ffn_gate_up",
    )(tile_expert, n_valid, x, w_gu, w_gu)


def _down_kernel(te_ref, nv_ref, a_ref, w_ref, o_ref):
    valid = pl.program_id(1) < nv_ref[0]

    @pl.when(valid)
    def _():
        a = a_ref[...]
        for c in range(0, o_ref.shape[1], GU_SUB):
            o_ref[:, c:c + GU_SUB] = _dot(a, w_ref[0, :, c:c + GU_SUB])

    @pl.when(jnp.logical_not(valid))
    def _():
        o_ref[...] = jnp.zeros_like(o_ref)


def _down(act, w_down, tile_expert, n_valid, tm):
    R = act.shape[0]
    n_tiles = R // tm
    tn = 1024

    def tile(i, nv):
        return jnp.minimum(i, nv[0] - 1)

    return pl.pallas_call(
        _down_kernel,
        grid_spec=pltpu.PrefetchScalarGridSpec(
            num_scalar_prefetch=2,
            grid=(D_MODEL // tn, n_tiles),
            in_specs=[pl.BlockSpec((tm, FFN_DIM), lambda j, i, te, nv: (tile(i, nv), 0)),
                      pl.BlockSpec((1, FFN_DIM, tn), lambda j, i, te, nv: (te[tile(i, nv)], 0, j))],
            out_specs=pl.BlockSpec((tm, tn), lambda j, i, te, nv: (i, j)),
        ),
        out_shape=jax.ShapeDtypeStruct((R, D_MODEL), F32),
        compiler_params=_params(("arbitrary", "arbitrary")),
        name="ffn_down",
    )(tile_expert, n_valid, act, w_down)


def _ffn_epilogue(f, h1, p, wg, wp, g3, gp):
    h2 = h1 + _rms(f, g3)
    gate = _sigmoid(_dot(h2.astype(BF16), wg))
    proj = _dot(p.astype(BF16), wp)
    return h2 + _rms(gate * proj, gp)


def _post_ffn_kernel(f_ref, h1_ref, p_ref, wg_ref, wp_ref, g3_ref, gp_ref, o_ref):
    o_ref[...] = _ffn_epilogue(f_ref[...], h1_ref[...], p_ref[...], wg_ref[...], wp_ref[...],
                               g3_ref[...], gp_ref[...])


def _post_ffn(f, h1, p, layer, wg, wp, g3, gp):
    T = h1.shape[0]
    tm = min(T, 512)
    row = lambda i: (i, 0)
    const = lambda i: (0, 0)
    return pl.pallas_call(
        _post_ffn_kernel,
        grid=(T // tm,),
        in_specs=[pl.BlockSpec((tm, D_MODEL), row),
                  pl.BlockSpec((tm, D_MODEL), row),
                  pl.BlockSpec((None, tm, PLE_DIM), lambda i: (layer, i, 0)),
                  pl.BlockSpec((D_MODEL, D_MODEL), const, pipeline_mode=pl.Buffered(1)),
                  pl.BlockSpec((PLE_DIM, D_MODEL), const, pipeline_mode=pl.Buffered(1)),
                  pl.BlockSpec((1, D_MODEL), const),
                  pl.BlockSpec((1, D_MODEL), const)],
        out_specs=pl.BlockSpec((tm, D_MODEL), row),
        out_shape=jax.ShapeDtypeStruct((T, D_MODEL), F32),
        compiler_params=_params(("parallel",)),
        name="post_ffn",
    )(f, h1, p, wg, wp, g3[None, :], gp[None, :])


def _router_kernel(c_ref, wr_ref, tri_ref, route_ref, cnt_ref, carry):
    @pl.when(pl.program_id(0) == 0)
    def _():
        carry[...] = jnp.zeros_like(carry)

    logits = _dot(c_ref[...].astype(BF16), wr_ref[...])
    lane = lax.broadcasted_iota(jnp.int32, logits.shape, 1)
    lg = jnp.where(lane < N_EXPERTS, logits, -jnp.inf)
    m1 = jnp.max(lg, axis=-1, keepdims=True)
    i1 = jnp.min(jnp.where(lg == m1, lane, LANES), axis=-1, keepdims=True)
    lg2 = jnp.where(lane == i1, -jnp.inf, lg)
    m2 = jnp.max(lg2, axis=-1, keepdims=True)
    i2 = jnp.min(jnp.where(lg2 == m2, lane, LANES), axis=-1, keepdims=True)
    e = jnp.exp(m2 - m1)
    g1 = 1.0 / (1.0 + e)
    g2 = e / (1.0 + e)
    onehot = jnp.where((lane == i1) | (lane == i2), 1.0, 0.0)
    before = _dot(tri_ref[...], onehot.astype(BF16)) + carry[0:1, :]
    rank1 = jnp.sum(jnp.where(lane == i1, before, 0.0), axis=-1, keepdims=True)
    rank2 = jnp.sum(jnp.where(lane == i2, before, 0.0), axis=-1, keepdims=True)
    total = carry[0:1, :] + jnp.sum(onehot, axis=0, keepdims=True)
    carry[...] = jnp.broadcast_to(total, carry.shape)
    cnt_ref[...] = jnp.broadcast_to(total, cnt_ref.shape)
    out = jnp.where(lane == 0, i1.astype(F32), 0.0)
    out = jnp.where(lane == 1, i2.astype(F32), out)
    out = jnp.where(lane == 2, g1, out)
    out = jnp.where(lane == 3, g2, out)
    out = jnp.where(lane == 4, rank1, out)
    out = jnp.where(lane == 5, rank2, out)
    route_ref[...] = out


def _router(c, w_router):
    T = c.shape[0]
    tr = min(T, 512)
    wr = jnp.zeros((D_MODEL, LANES), BF16).at[:, :N_EXPERTS].set(w_router.astype(BF16))
    tri = jnp.asarray(np.tril(np.ones((tr, tr), np.float32), -1), dtype=BF16)
    return pl.pallas_call(
        _router_kernel,
        grid=(T // tr,),
        in_specs=[pl.BlockSpec((tr, c.shape[1]), lambda i: (i, 0)),
                  pl.BlockSpec((D_MODEL, LANES), lambda i: (0, 0)),
                  pl.BlockSpec((tr, tr), lambda i: (0, 0))],
        out_specs=[pl.BlockSpec((tr, LANES), lambda i: (i, 0)),
                   pl.BlockSpec((8, LANES), lambda i: (0, 0))],
        out_shape=[jax.ShapeDtypeStruct((T, LANES), F32),
                   jax.ShapeDtypeStruct((8, LANES), F32)],
        scratch_shapes=[pltpu.VMEM((8, LANES), F32)],
        compiler_params=_params(("arbitrary",)),
        name="moe_router",
    )(c, wr, tri)


def _dispatch_kernel(p1_ref, p2_ref, pad_ref, nv_ref, c_ref, xs_ref, zbuf, sem, zsem, *,
                     tm, seg_tile):
    i = pl.program_id(0)
    base = i * tm
    zr = zbuf.shape[0]

    @pl.when(i == 0)
    def _():
        zbuf[...] = jnp.zeros_like(zbuf)

        def zero_tile(start):
            first = pl.multiple_of(start, zr)
            copies = [pltpu.make_async_copy(zbuf, xs_ref.at[pl.ds(first + k * zr, zr)], zsem)
                      for k in range(seg_tile // zr)]
            for cp in copies:
                cp.start()
            for cp in copies:
                cp.wait()

        def zero_padding(e, _):
            @pl.when(pad_ref[e] >= 0)
            def _():
                zero_tile(pad_ref[e])
            return 0

        def zero_unused(t, _):
            zero_tile(t * seg_tile)
            return 0

        lax.fori_loop(0, N_EXPERTS, zero_padding, 0)
        lax.fori_loop(nv_ref[0], xs_ref.shape[0] // seg_tile, zero_unused, 0)

    def row_copy(r, dst):
        return pltpu.make_async_copy(c_ref.at[pl.ds(r, 1)], xs_ref.at[pl.ds(dst, 1)], sem)

    for r in range(tm):
        row_copy(r, p1_ref[base + r]).start(priority=0)
        row_copy(r, p2_ref[base + r]).start(priority=1)
    for _r in range(2 * tm):
        row_copy(0, 0).wait()


def _dispatch(c, pos1, pos2, pad_tile_row, n_valid, n_rows, seg_tile):
    T, width = c.shape
    tm = min(T, 256)
    return pl.pallas_call(
        functools.partial(_dispatch_kernel, tm=tm, seg_tile=seg_tile),
        grid_spec=pltpu.PrefetchScalarGridSpec(
            num_scalar_prefetch=4,
            grid=(T // tm,),
            in_specs=[pl.BlockSpec((tm, width), lambda i, p1, p2, pt, nv: (i, 0))],
            out_specs=pl.BlockSpec(memory_space=pl.ANY),
            scratch_shapes=[pltpu.VMEM((min(seg_tile, 64), width), c.dtype),
                            pltpu.SemaphoreType.DMA(()),
                            pltpu.SemaphoreType.DMA(())],
        ),
        out_shape=jax.ShapeDtypeStruct((n_rows, width), c.dtype),
        compiler_params=_params(("arbitrary",)),
        name="moe_dispatch",
    )(pos1, pos2, pad_tile_row, n_valid, c)


def _combine_kernel(p1_ref, p2_ref, ys_ref, route_ref, h1_ref, p_ref, wg_ref, wp_ref,
                    g3_ref, gp_ref, o_ref, buf, sems, *, tm):
    i = pl.program_id(0)
    slot = i % 2

    def row_copy(src, s, k, r):
        return pltpu.make_async_copy(ys_ref.at[pl.ds(src, 1)], buf.at[s, k, pl.ds(r, 1)],
                                     sems.at[s])

    def gather_tile(tile, s):
        base = tile * tm
        for r in range(tm):
            row_copy(p1_ref[base + r], s, 0, r).start(priority=0)
            row_copy(p2_ref[base + r], s, 1, r).start(priority=1)

    @pl.when(i == 0)
    def _():
        gather_tile(0, 0)

    @pl.when(i + 1 < pl.num_programs(0))
    def _():
        gather_tile(i + 1, 1 - slot)

    for _r in range(2 * tm):
        row_copy(0, slot, 0, 0).wait()
    route = route_ref[...]
    f = route[:, 2:3] * buf[slot, 0] + route[:, 3:4] * buf[slot, 1]
    o_ref[...] = _ffn_epilogue(f, h1_ref[...], p_ref[...], wg_ref[...], wp_ref[...],
                               g3_ref[...], gp_ref[...])


def _combine(ys, pos1, pos2, route, h1, p, layer, wg, wp, g3, gp):
    T = h1.shape[0]
    tm = min(T, 256)
    row = lambda i, p1, p2: (i, 0)
    const = lambda i, p1, p2: (0, 0)
    return pl.pallas_call(
        functools.partial(_combine_kernel, tm=tm),
        grid_spec=pltpu.PrefetchScalarGridSpec(
            num_scalar_prefetch=2,
            grid=(T // tm,),
            in_specs=[pl.BlockSpec(memory_space=pl.ANY),
                      pl.BlockSpec((tm, LANES), row),
                      pl.BlockSpec((tm, D_MODEL), row),
                      pl.BlockSpec((None, tm, PLE_DIM), lambda i, p1, p2: (layer, i, 0)),
                      pl.BlockSpec((D_MODEL, D_MODEL), const, pipeline_mode=pl.Buffered(1)),
                      pl.BlockSpec((PLE_DIM, D_MODEL), const, pipeline_mode=pl.Buffered(1)),
                      pl.BlockSpec((1, D_MODEL), const),
                      pl.BlockSpec((1, D_MODEL), const)],
            out_specs=pl.BlockSpec((tm, D_MODEL), row),
            scratch_shapes=[pltpu.VMEM((2, 2, tm, D_MODEL), F32),
                            pltpu.SemaphoreType.DMA((2,))],
        ),
        out_shape=jax.ShapeDtypeStruct((T, D_MODEL), F32),
        compiler_params=_params(("arbitrary",)),
        name="moe_combine",
    )(pos1, pos2, ys, route, h1, p, wg, wp, g3[None, :], gp[None, :])


def _swap_halves(w):
    half = w.shape[-1] // 2
    return jnp.concatenate([w[..., half:], w[..., :half]], axis=-1)


def _prep_w_in(w):
    w = w.astype(BF16)
    k_rope = w[:, 640:704]
    pad64 = jnp.zeros((D_MODEL, 64), w.dtype)
    cols = [w[:, :640], k_rope, pad64, _swap_halves(k_rope), pad64,
            jnp.zeros((D_MODEL, HGRN_COL0 - 896), w.dtype), w[:, 704:]]
    return jnp.concatenate(cols, axis=1).astype(BF16)


def _prep_w_uq(w):
    w = w.reshape(Q_RANK, N_HEADS, QK_DIM)
    nope = w[:, :, :NOPE_DIM]
    pe = w[:, :, NOPE_DIM:]
    pad = jnp.zeros((Q_RANK, N_HEADS, LANES - ROPE_DIM), w.dtype)
    pe_pad = jnp.concatenate([pe, pad], axis=-1)
    pe_swap = jnp.concatenate([_swap_halves(pe), pad], axis=-1)
    parts = [x.reshape(Q_RANK, GROUP_WIDTH) for x in (nope, pe_pad, pe_swap)]
    return jnp.concatenate(parts, axis=1).astype(BF16)


def _prep_w_ukv(w):
    w = w.reshape(KV_RANK, N_HEADS, NOPE_DIM + V_DIM)
    k = w[:, :, :NOPE_DIM].reshape(KV_RANK, GROUP_WIDTH)
    v = w[:, :, NOPE_DIM:].reshape(KV_RANK, GROUP_WIDTH)
    return jnp.concatenate([k, v], axis=1).astype(BF16)


def _moe_plan(route, counts, tm, n_tiles):
    e1 = route[:, 0].astype(jnp.int32)
    e2 = route[:, 1].astype(jnp.int32)
    rank1 = route[:, 4].astype(jnp.int32)
    rank2 = route[:, 5].astype(jnp.int32)
    cnt = counts[0, :N_EXPERTS].astype(jnp.int32)
    padded = ((cnt + tm - 1) // tm) * tm
    seg_end = jnp.cumsum(padded)
    seg_start = seg_end - padded
    pos1 = seg_start[e1] + rank1
    pos2 = seg_start[e2] + rank2
    tile_start = jnp.arange(n_tiles, dtype=jnp.int32) * tm
    tile_expert = jnp.sum(tile_start[:, None] >= seg_end[None, :], axis=1).astype(jnp.int32)
    tile_expert = jnp.minimum(tile_expert, N_EXPERTS - 1)
    n_valid = (seg_end[-1] // tm).astype(jnp.int32)[None]
    pad_tile_row = jnp.where(padded > 0, seg_end - tm, -1).astype(jnp.int32)
    return pos1, pos2, tile_expert, n_valid, pad_tile_row


def kernel(x, p, positions, sandwich_norms, w_in, mla_q_norm, mla_kv_norm, w_uq, w_ukv,
           hgrn_lb_logits, hgrn_out_norm, w_out, ffn_w_gu, ffn_w_down, moe_w_router,
           moe_w_gu, moe_w_down, ple_w_proj, ple_w_gate, ple_norm):
    B, S, _ = x.shape
    T = B * S
    h = x.reshape(T, D_MODEL)
    ct, st = _rope_tables(positions)
    hgrn_consts = _hgrn_consts(HGRN_CHUNK)
    tm = min(T, 512)
    for l in range(DEPTH):
        z = _norm_matmul(h, sandwich_norms[l, 0], _prep_w_in(w_in[l]))
        q, k, v = _mla_proj(z, ct, st, mla_q_norm[l], mla_kv_norm[l],
                            _prep_w_uq(w_uq[l]), _prep_w_ukv(w_ukv[l]), B, S)
        moe = l % 2 == 1
        if moe:
            w_down = moe_w_down[l // 2]
            o_mla, w_down_bf = _attention_and_cast(q, k, v, w_down.reshape(-1, D_MODEL))
            w_down_bf = w_down_bf.reshape(w_down.shape)
        else:
            o_mla = _attention(q, k, v)
        o_mla = o_mla.reshape(T, GROUP_WIDTH)
        o_hgrn = _hgrn(z, hgrn_lb_logits, hgrn_out_norm[l], hgrn_consts, l, B, S)
        h1, c = _out_proj(o_mla, o_hgrn, h, w_out[l].astype(BF16),
                          sandwich_norms[l, 1], sandwich_norms[l, 2],
                          F32 if moe else BF16)
        wg = ple_w_gate[l].astype(BF16)
        wp = ple_w_proj[l].astype(BF16)
        p_all = p.reshape(DEPTH, T, PLE_DIM)
        if not moe:
            def one_expert(rows):
                return jnp.zeros((T // rows,), jnp.int32), jnp.full((1,), T // rows, jnp.int32)

            tm_dense = min(T, 1024)
            act = _gate_up(c, ffn_w_gu[l // 2][None], *one_expert(tm_dense), tm_dense, 512)
            f = _down(act, ffn_w_down[l // 2][None].astype(BF16), *one_expert(tm), tm)
            h = _post_ffn(f, h1, p_all, l, wg, wp, sandwich_norms[l, 3], ple_norm[l])
        else:
            n_tiles = (2 * T) // tm + N_EXPERTS
            route, counts = _router(c, moe_w_router[l // 2])
            pos1, pos2, tile_expert, n_valid, pad_tile_row = _moe_plan(route, counts, tm, n_tiles)
            xs = _dispatch(c, pos1, pos2, pad_tile_row, n_valid, n_tiles * tm, tm)
            act = _gate_up(xs, moe_w_gu[l // 2], tile_expert, n_valid, tm, 1024)
            ys = _down(act, w_down_bf, tile_expert, n_valid, tm)
            h = _combine(ys, pos1, pos2, route, h1, p_all, l, wg, wp,
                         sandwich_norms[l, 3], ple_norm[l])
    return h.reshape(B, S, D_MODEL)
```

```python
import functools
import math

import numpy as np
import jax
import jax.numpy as jnp
from jax import lax
from jax.experimental import pallas as pl
from jax.experimental.pallas import tpu as pltpu

F32 = jnp.float32
BF16 = jnp.bfloat16

D_MODEL = 2048
DEPTH = 2
N_HEADS = 8
NOPE_DIM = 128
ROPE_DIM = 64
V_DIM = 128
QK_DIM = NOPE_DIM + ROPE_DIM
Q_RANK = 384
KV_RANK = 256
HGRN_K = 128
HGRN_V = 128
GROUP_WIDTH = N_HEADS * 128
FFN_DIM = 7168
N_EXPERTS = 8
PLE_DIM = 256
ROPE_THETA = 10000.0
LB_FLOOR = 1e-30
EPS = 1e-6

LANES = 128
QK_PAD = 256
Q_SCALE = QK_DIM ** -0.5 * math.log2(math.e)
Z_MLA = 1024
Z_WIDTH = 5120
HGRN_COL0 = 1024
HGRN_CHUNK = 64
HGRN_HEADS_PER_STEP = 4
GU_SUB = 512
IN_PROJ_SUB = 1024
CAST_SLAB_BYTES = 4 * 1024 * 1024
VMEM_LIMIT = 56 * 1024 * 1024


def _params(semantics, vmem=VMEM_LIMIT):
    return pltpu.CompilerParams(dimension_semantics=semantics, vmem_limit_bytes=vmem)


def _rms(x, gain_row):
    ms = jnp.mean(x * x, axis=-1, keepdims=True)
    return x * lax.rsqrt(ms + EPS) * gain_row


def _dot(a, b):
    return jnp.dot(a, b, preferred_element_type=F32)


def _dot_nt(a, b):
    return lax.dot_general(a, b, (((1,), (1,)), ((), ())), preferred_element_type=F32)


def _dot_tn(a, b):
    return lax.dot_general(a, b, (((0,), (0,)), ((), ())), preferred_element_type=F32)


def _sigmoid(x):
    return 1.0 / (1.0 + jnp.exp(-x))


def _fold_lanes(x, op):
    parts = [x[:, c:c + LANES] for c in range(0, x.shape[1], LANES)]
    while len(parts) > 1:
        parts = [op(a, b) for a, b in zip(parts[0::2], parts[1::2])] + parts[len(parts) & ~1:]
    return parts[0]


def _rope_kernel(pos_ref, inv_ref, sgn_ref, ct_ref, st_ref):
    ang = pos_ref[...].astype(F32) * inv_ref[...]
    keep = jnp.abs(sgn_ref[...])
    ct_ref[...] = jnp.cos(ang) * keep
    st_ref[...] = jnp.sin(ang) * sgn_ref[...]


def _rope_tables(positions):
    T = positions.size
    tt = min(T, 1024)
    half = ROPE_DIM // 2
    inv_freq = 1.0 / (ROPE_THETA ** (jnp.arange(0, ROPE_DIM, 2, dtype=F32) / ROPE_DIM))
    inv_row = jnp.concatenate([inv_freq, inv_freq, jnp.zeros((LANES - ROPE_DIM,), F32)])[None, :]
    sgn = np.zeros((1, LANES), np.float32)
    sgn[0, :half] = -1.0
    sgn[0, half:ROPE_DIM] = 1.0
    return pl.pallas_call(
        _rope_kernel,
        grid=(T // tt,),
        in_specs=[pl.BlockSpec((tt, 1), lambda i: (i, 0)),
                  pl.BlockSpec((1, LANES), lambda i: (0, 0)),
                  pl.BlockSpec((1, LANES), lambda i: (0, 0))],
        out_specs=[pl.BlockSpec((tt, LANES), lambda i: (i, 0))] * 2,
        out_shape=[jax.ShapeDtypeStruct((T, LANES), F32)] * 2,
        compiler_params=_params(("parallel",)),
        name="rope_tables",
    )(positions.reshape(T, 1), inv_row, jnp.asarray(sgn))


def _norm_matmul_kernel(x_ref, g_ref, w_ref, o_ref):
    a = _rms(x_ref[...], g_ref[...]).astype(BF16)
    for c in range(0, o_ref.shape[1], IN_PROJ_SUB):
        o_ref[:, c:c + IN_PROJ_SUB] = _dot(a, w_ref[:, c:c + IN_PROJ_SUB])


def _norm_matmul(x, gain, w):
    T, K = x.shape
    N = w.shape[1]
    tm = min(T, 512)
    return pl.pallas_call(
        _norm_matmul_kernel,
        grid=(T // tm,),
        in_specs=[pl.BlockSpec((tm, K), lambda i: (i, 0)),
                  pl.BlockSpec((1, K), lambda i: (0, 0)),
                  pl.BlockSpec((K, N), lambda i: (0, 0), pipeline_mode=pl.Buffered(1))],
        out_specs=pl.BlockSpec((tm, N), lambda i: (i, 0)),
        out_shape=jax.ShapeDtypeStruct((T, N), F32),
        compiler_params=_params(("parallel",)),
        name="in_proj",
    )(x, gain[None, :], w)


def _mla_proj_kernel(z_ref, ct_ref, st_ref, qn_ref, kvn_ref, wq_ref, wkv_ref,
                     q_ref, k_ref, v_ref):
    z = z_ref[...]
    ct = ct_ref[...]
    st = st_ref[...]
    aq = _rms(z[:, :Q_RANK], qn_ref[...]).astype(BF16)
    akv = _rms(z[:, Q_RANK:Q_RANK + KV_RANK], kvn_ref[...]).astype(BF16)
    k_a = z[:, 640:768]
    k_b = z[:, 768:896]
    k_pe = (k_a * ct + k_b * st).astype(BF16)
    q = _dot(aq, wq_ref[...])
    kv = _dot(akv, wkv_ref[...])
    for h in range(N_HEADS):
        lo, hi = h * LANES, (h + 1) * LANES
        q_ref[0, h, :, 0:LANES] = (q[:, lo:hi] * Q_SCALE).astype(BF16)
        q_pe = q[:, GROUP_WIDTH + lo:GROUP_WIDTH + hi] * ct \
            + q[:, 2 * GROUP_WIDTH + lo:2 * GROUP_WIDTH + hi] * st
        q_ref[0, h, :, LANES:QK_PAD] = (q_pe * Q_SCALE).astype(BF16)
        k_ref[0, h, :, 0:LANES] = kv[:, lo:hi].astype(BF16)
        k_ref[0, h, :, LANES:QK_PAD] = k_pe
        v_ref[0, h, :, :] = kv[:, GROUP_WIDTH + lo:GROUP_WIDTH + hi].astype(BF16)


def _mla_proj(z, ct, st, q_norm, kv_norm, wq, wkv, B, S):
    ts = min(S, 512)
    ns = S // ts
    tok = lambda b, i: (b * ns + i, 0)
    const = lambda b, i: (0, 0)
    head_out = lambda b, i: (b, 0, i, 0)
    return pl.pallas_call(
        _mla_proj_kernel,
        grid=(B, ns),
        in_specs=[pl.BlockSpec((ts, Z_MLA), tok),
                  pl.BlockSpec((ts, LANES), tok),
                  pl.BlockSpec((ts, LANES), tok),
                  pl.BlockSpec((1, Q_RANK), const),
                  pl.BlockSpec((1, KV_RANK), const),
                  pl.BlockSpec(wq.shape, const),
                  pl.BlockSpec(wkv.shape, const)],
        out_specs=[pl.BlockSpec((1, N_HEADS, ts, QK_PAD), head_out),
                   pl.BlockSpec((1, N_HEADS, ts, QK_PAD), head_out),
                   pl.BlockSpec((1, N_HEADS, ts, V_DIM), head_out)],
        out_shape=[jax.ShapeDtypeStruct((B, N_HEADS, S, QK_PAD), BF16),
                   jax.ShapeDtypeStruct((B, N_HEADS, S, QK_PAD), BF16),
                   jax.ShapeDtypeStruct((B, N_HEADS, S, V_DIM), BF16)],
        compiler_params=_params(("parallel", "parallel")),
        name="mla_proj",
    )(z, ct, st, q_norm[None, :], kv_norm[None, :], wq, wkv)


def _attn_kernel(*refs, tk, with_cast):
    if with_cast:
        q_ref, k_ref, v_ref, w_ref, o_ref, wbf_ref, s_a, s_b = refs
    else:
        q_ref, k_ref, v_ref, o_ref, s_a, s_b = refs
    i = pl.program_id(2)
    top_rows = pl.ds(0, tk)
    bot_rows = pl.ds(tk, tk)

    def kv_rows(u):
        return pl.ds(pl.multiple_of(u * tk, tk), tk)

    def scores(u):
        return _dot_nt(q_ref[0, 0], k_ref[0, 0, kv_rows(u), :])

    lane = lax.broadcasted_iota(jnp.int32, (tk, LANES), 1)
    ones_col = jnp.where(lane == 0, 1.0, 0.0).astype(BF16)

    def values(u):
        return jnp.concatenate([v_ref[0, 0, kv_rows(u), :], ones_col], axis=1)

    def update(state, s, v, masked):
        m, acc = state
        if masked:
            row = lax.broadcasted_iota(jnp.int32, (tk, tk), 0)
            col = lax.broadcasted_iota(jnp.int32, (tk, tk), 1)
            s = jnp.where(col <= row, s, -jnp.inf)
        m_new = jnp.maximum(m, jnp.max(_fold_lanes(s, jnp.maximum), axis=-1, keepdims=True))
        alpha = jnp.exp2(m - m_new)
        p = jnp.exp2((s - m_new).astype(BF16))
        acc = alpha * acc + _dot(p, v)
        return m_new, acc

    def both(top, bot, s_ref, u):
        v = values(u)
        return (update(top, s_ref[top_rows, :], v, False),
                update(bot, s_ref[bot_rows, :], v, False))

    s_a[...] = scores(0)
    if with_cast:
        wbf_ref[...] = w_ref[...].astype(BF16)

    def body(t, state):
        top, bot = state
        s_b[...] = scores(2 * t + 1)
        top, bot = both(top, bot, s_a, 2 * t)
        s_a[...] = scores(2 * t + 2)
        top, bot = both(top, bot, s_b, 2 * t + 1)
        return top, bot

    init = (jnp.full((tk, 1), -jnp.inf, F32), jnp.zeros((tk, 2 * LANES), F32))
    top, bot = lax.fori_loop(0, i, body, (init, init))
    s_b[bot_rows, :] = _dot_nt(q_ref[0, 0, bot_rows, :], k_ref[0, 0, kv_rows(2 * i + 1), :])
    v0 = values(2 * i)
    top = update(top, s_a[top_rows, :], v0, True)
    bot = update(bot, s_a[bot_rows, :], v0, False)
    bot = update(bot, s_b[bot_rows, :], values(2 * i + 1), True)
    for rows, (_, acc) in ((top_rows, top), (bot_rows, bot)):
        o_ref[0, rows, :] = (acc[:, :V_DIM] / acc[:, V_DIM:V_DIM + 1]).astype(BF16)


def _attention(q, k, v, w_cast=None):
    B, H, S, _ = q.shape
    tq = min(S, 1024)
    tk = tq // 2
    nq = S // tq
    in_specs = [pl.BlockSpec((1, 1, tq, QK_PAD), lambda b, h, i: (b, h, i, 0)),
                pl.BlockSpec((1, 1, S, QK_PAD), lambda b, h, i: (b, h, 0, 0)),
                pl.BlockSpec((1, 1, S, V_DIM), lambda b, h, i: (b, h, 0, 0))]
    out_specs = [pl.BlockSpec((1, tq, V_DIM), lambda b, h, i: (b, i, h))]
    out_shape = [jax.ShapeDtypeStruct((B, S, H * V_DIM), BF16)]
    args = [q, k, v]
    if w_cast is not None:
        slab = w_cast.shape[0] // (B * H * nq)
        spec = pl.BlockSpec((slab, w_cast.shape[1]), lambda b, h, i: ((b * H + h) * nq + i, 0))
        in_specs.append(spec)
        out_specs.append(spec)
        out_shape.append(jax.ShapeDtypeStruct(w_cast.shape, BF16))
        args.append(w_cast)
    outs = pl.pallas_call(
        functools.partial(_attn_kernel, tk=tk, with_cast=w_cast is not None),
        grid=(B, H, nq),
        in_specs=in_specs,
        out_specs=out_specs,
        out_shape=out_shape,
        scratch_shapes=[pltpu.VMEM((tq, tk), F32), pltpu.VMEM((tq, tk), F32)],
        compiler_params=_params(("parallel", "parallel", "arbitrary")),
        name="mla_attention",
    )(*args)
    return outs if w_cast is not None else outs[0]


def _cast_slab(shape, steps):
    rows, cols = shape
    slab = rows // steps
    fits = rows % steps == 0 and slab % 16 == 0 and slab * cols * 4 <= CAST_SLAB_BYTES
    return slab if fits else 0


def _attention_and_cast(q, k, v, w):
    B, H, S, _ = q.shape
    if _cast_slab(w.shape, B * H * (S // min(S, 1024))):
        return _attention(q, k, v, w)
    return _attention(q, k, v), w.astype(BF16)


def _hgrn_consts(C):
    levels = [C >> (i + 1) for i in range(int(math.log2(C)))]
    t = np.arange(C)
    u = np.arange(C)
    mats = [(u[None, :] <= t[:, None]),
            (u[None, :] > t[:, None])]
    pair_masks = [np.eye(C, dtype=bool)]
    for m in levels:
        ref = (t // (2 * m)) * (2 * m) + m - 1
        hi = ((t // m) % 2) == 1
        rng_hi = (u[None, :] > ref[:, None]) & (u[None, :] <= t[:, None])
        rng_lo = (u[None, :] > t[:, None]) & (u[None, :] <= ref[:, None])
        mats.append(np.where(hi[:, None], rng_hi, rng_lo))
        same = (t[:, None] // (2 * m)) == (t[None, :] // (2 * m))
        pair_masks.append(hi[:, None] & (~hi[None, :]) & same)
    stack = np.concatenate(mats, axis=0).astype(np.float32)
    sums = jnp.asarray(np.concatenate([stack] * 3, axis=1), dtype=BF16)
    heads = np.eye(HGRN_HEADS_PER_STEP, dtype=bool)
    stacked = np.stack([np.kron(heads, m) for m in pair_masks])
    return sums, jnp.asarray(stacked.astype(np.float32))


def _hgrn_kernel(*refs, layer, ts, n_cast):
    (zq_ref, zf_ref, zi_ref, zg_ref, lbl_ref, on_ref, sums_ref, pm_ref) = refs[:8]
    cast_src = refs[8:8 + n_cast]
    o_ref = refs[8 + n_cast]
    cast_dst = refs[9 + n_cast:9 + 2 * n_cast]
    st_scr = refs[9 + 2 * n_cast]
    C = HGRN_CHUNK
    n_levels = pm_ref.shape[0] - 1
    for src, dst in zip(cast_src, cast_dst):
        dst[...] = src[...].astype(BF16)

    @pl.when(pl.program_id(2) == 0)
    def _():
        st_scr[...] = jnp.zeros_like(st_scr)

    lg = lbl_ref[...]
    ex = jnp.exp(lg - jnp.max(lg, axis=0, keepdims=True))
    pr = ex / jnp.sum(ex, axis=0, keepdims=True)
    cum = pr[0:1, :]
    for r in range(1, layer + 1):
        cum = cum + pr[r:r + 1, :]
    lb = cum - pr[0:1, :]
    lb_floor = jnp.maximum(lb, LB_FLOOR)
    one_m_lb = 1.0 - lb
    gain = on_ref[...]

    for c in range(ts // C):
        rows = pl.ds(c * C, C)
        q_raw = zq_ref[rows, :]
        zf = zf_ref[rows, :]
        g_raw = zg_ref[rows, :]
        v_all = zi_ref[rows, :].astype(BF16)
        q_all = q_raw * _sigmoid(q_raw)
        e = jnp.exp(-jnp.abs(zf))
        r = 1.0 / (1.0 + e)
        er = e * r
        pos = zf >= 0
        log_f = jnp.log(lb_floor + one_m_lb * jnp.where(pos, r, er))
        kk_all = one_m_lb * jnp.where(pos, er, r)
        gate_all = g_raw * _sigmoid(g_raw)
        p0 = log_f.astype(BF16)
        r1 = log_f - p0.astype(F32)
        p1 = r1.astype(BF16)
        p2 = (r1 - p1.astype(F32)).astype(BF16)
        sums_all = _dot(sums_ref[...], jnp.concatenate([p0, p1, p2], axis=0))

        def stack(x):
            return jnp.concatenate([x[:, h * LANES:(h + 1) * LANES]
                                    for h in range(HGRN_HEADS_PER_STEP)], axis=0)

        q_st = stack(q_all).astype(BF16)
        kk_st = stack(kk_all).astype(BF16)
        a = pm_ref[0] * _dot_nt(q_st, kk_st)
        for lv in range(n_levels):
            dec = jnp.exp(stack(sums_all[(2 + lv) * C:(3 + lv) * C])).astype(BF16)
            a = a + pm_ref[lv + 1] * _dot_nt(q_st * dec, kk_st * dec)
        o_intra = _dot(a.astype(BF16), stack(v_all))
        b_all = sums_all[0:C]
        st_all = st_scr[...]
        o_inter = _dot_nt(stack(q_all * jnp.exp(b_all)).astype(BF16), st_all.astype(BF16))
        k_end = (kk_all * jnp.exp(sums_all[C:2 * C])).astype(BF16)
        decay = jnp.exp(b_all[C - 1:C, :])
        for hh in range(HGRN_HEADS_PER_STEP):
            lanes = slice(hh * LANES, (hh + 1) * LANES)
            srows = slice(hh * HGRN_V, (hh + 1) * HGRN_V)
            o = o_intra[hh * C:(hh + 1) * C] + o_inter[hh * C:(hh + 1) * C, lanes]
            st_scr[srows, :] = st_all[srows] * decay[:, lanes] \
                + _dot_tn(v_all[:, lanes], k_end[:, lanes])
            o_ref[rows, pl.ds(hh * LANES, LANES)] = \
                (_rms(o, gain) * gate_all[:, lanes]).astype(BF16)


def _hgrn(z, lb_logits, out_norm, consts, layer, B, S, casts=()):
    T = B * S
    ts = min(S, 512)
    ns = S // ts
    sums, pair_masks = consts
    hp = HGRN_HEADS_PER_STEP
    width = hp * LANES
    col0 = HGRN_COL0 // width
    groups = N_HEADS // hp
    steps = B * groups * ns
    riding = [w for w in casts if _cast_slab(w.shape, steps)]
    cast_specs = [pl.BlockSpec((_cast_slab(w.shape, steps), w.shape[1]),
                               lambda b, h, i: ((b * groups + h) * ns + i, 0)) for w in riding]

    def zspec(part):
        return pl.BlockSpec((ts, width),
                            lambda b, h, i: (b * ns + i, col0 + part * groups + h))

    full = lambda arr: pl.BlockSpec(arr.shape, lambda b, h, i: (0,) * arr.ndim)
    kern = functools.partial(_hgrn_kernel, layer=layer, ts=ts, n_cast=len(riding))
    outs = pl.pallas_call(
        kern,
        grid=(B, groups, ns),
        in_specs=[zspec(0), zspec(1), zspec(2), zspec(3),
                  pl.BlockSpec((DEPTH, width), lambda b, h, i: (0, h)),
                  pl.BlockSpec((1, HGRN_V), lambda b, h, i: (0, 0)),
                  full(sums), full(pair_masks)] + cast_specs,
        out_specs=[pl.BlockSpec((ts, width), lambda b, h, i: (b * ns + i, h))] + cast_specs,
        out_shape=[jax.ShapeDtypeStruct((T, GROUP_WIDTH), BF16)]
        + [jax.ShapeDtypeStruct(w.shape, BF16) for w in riding],
        scratch_shapes=[pltpu.VMEM((hp * HGRN_V, HGRN_K), F32)],
        compiler_params=_params(("parallel", "parallel", "arbitrary")),
        name="hgrn2",
    )(z, z, z, z, lb_logits, out_norm[None, :], sums, pair_masks, *riding)
    copies = iter(outs[1:])
    return [outs[0]] + [next(copies) if _cast_slab(w.shape, steps) else w.astype(BF16)
                        for w in casts]


def _out_proj_kernel(om_ref, oh_ref, h_ref, w_ref, g1_ref, g2_ref, h1_ref, c_ref):
    y = _dot(om_ref[...], w_ref[0:GROUP_WIDTH, :]) + _dot(oh_ref[...], w_ref[GROUP_WIDTH:, :])
    h1 = h_ref[...] + _rms(y, g1_ref[...])
    h1_ref[...] = h1
    c_ref[...] = _rms(h1, g2_ref[...]).astype(c_ref.dtype)


def _out_proj(om, oh, h, w, g1, g2, c_dtype):
    T = h.shape[0]
    tm = min(T, 512)
    c_width = D_MODEL
    row = lambda i: (i, 0)
    const = lambda i: (0, 0)
    return pl.pallas_call(
        _out_proj_kernel,
        grid=(T // tm,),
        in_specs=[pl.BlockSpec((tm, GROUP_WIDTH), row),
                  pl.BlockSpec((tm, GROUP_WIDTH), row),
                  pl.BlockSpec((tm, D_MODEL), row),
                  pl.BlockSpec((D_MODEL, D_MODEL), const, pipeline_mode=pl.Buffered(1)),
                  pl.BlockSpec((1, D_MODEL), const),
                  pl.BlockSpec((1, D_MODEL), const)],
        out_specs=[pl.BlockSpec((tm, D_MODEL), row), pl.BlockSpec((tm, c_width), row)],
        out_shape=[jax.ShapeDtypeStruct((T, D_MODEL), F32),
                   jax.ShapeDtypeStruct((T, c_width), c_dtype)],
        compiler_params=_params(("parallel",)),
        name="out_proj",
    )(om, oh, h, w, g1[None, :], g2[None, :])


def _gate_up_kernel(te_ref, nv_ref, x_ref, wg_ref, wu_ref, o_ref, wg_bf, wu_bf):
    i = pl.program_id(1)
    valid = i < nv_ref[0]
    new_weights = jnp.logical_or(i == 0, te_ref[i] != te_ref[jnp.maximum(i - 1, 0)])

    def step(refresh):
        x = x_ref[...].astype(BF16)
        for c in range(0, o_ref.shape[1], GU_SUB):
            cols = slice(c, c + GU_SUB)
            if refresh:
                wg_bf[:, cols] = wg_ref[0, :, cols].astype(BF16)
                wu_bf[:, cols] = wu_ref[0, :, cols].astype(BF16)
            g = _dot(x, wg_bf[:, cols])
            u = _dot(x, wu_bf[:, cols])
            o_ref[:, cols] = (g * _sigmoid(g) * u).astype(BF16)

    pl.when(jnp.logical_and(valid, new_weights))(functools.partial(step, True))
    pl.when(jnp.logical_and(valid, jnp.logical_not(new_weights)))(functools.partial(step, False))

    @pl.when(jnp.logical_not(valid))
    def _():
        o_ref[...] = jnp.zeros_like(o_ref)


def _gate_up(x, w_gu, tile_expert, n_valid, tm, tf):
    R, xw = x.shape
    n_tiles = R // tm
    nf = FFN_DIM // tf

    def tile(i, nv):
        return jnp.minimum(i, nv[0] - 1)

    return pl.pallas_call(
        _gate_up_kernel,
        grid_spec=pltpu.PrefetchScalarGridSpec(
            num_scalar_prefetch=2,
            grid=(nf, n_tiles),
            in_specs=[pl.BlockSpec((tm, xw), lambda j, i, te, nv: (tile(i, nv), 0)),
                      pl.BlockSpec((1, D_MODEL, tf), lambda j, i, te, nv: (te[tile(i, nv)], 0, j)),
                      pl.BlockSpec((1, D_MODEL, tf),
                                   lambda j, i, te, nv: (te[tile(i, nv)], 0, j + nf))],
            out_specs=pl.BlockSpec((tm, tf), lambda j, i, te, nv: (i, j)),
            scratch_shapes=[pltpu.VMEM((D_MODEL, tf), BF16), pltpu.VMEM((D_MODEL, tf), BF16)],
        ),
        out_shape=jax.ShapeDtypeStruct((R, FFN_DIM), BF16),
        compiler_params=_params(("arbitrary", "arbitrary")),
        name="ffn_gate_up",
    )(tile_expert, n_valid, x, w_gu, w_gu)


def _down_kernel(te_ref, nv_ref, a_ref, w_ref, o_ref):
    valid = pl.program_id(1) < nv_ref[0]

    @pl.when(valid)
    def _():
        a = a_ref[...]
        for c in range(0, o_ref.shape[1], GU_SUB):
            o_ref[:, c:c + GU_SUB] = _dot(a, w_ref[0, :, c:c + GU_SUB])

    @pl.when(jnp.logical_not(valid))
    def _():
        o_ref[...] = jnp.zeros_like(o_ref)


def _down(act, w_down, tile_expert, n_valid, tm):
    R = act.shape[0]
    n_tiles = R // tm
    tn = 1024

    def tile(i, nv):
        return jnp.minimum(i, nv[0] - 1)

    return pl.pallas_call(
        _down_kernel,
        grid_spec=pltpu.PrefetchScalarGridSpec(
            num_scalar_prefetch=2,
            grid=(D_MODEL // tn, n_tiles),
            in_specs=[pl.BlockSpec((tm, FFN_DIM), lambda j, i, te, nv: (tile(i, nv), 0)),
                      pl.BlockSpec((1, FFN_DIM, tn), lambda j, i, te, nv: (te[tile(i, nv)], 0, j))],
            out_specs=pl.BlockSpec((tm, tn), lambda j, i, te, nv: (i, j)),
        ),
        out_shape=jax.ShapeDtypeStruct((R, D_MODEL), F32),
        compiler_params=_params(("arbitrary", "arbitrary")),
        name="ffn_down",
    )(tile_expert, n_valid, act, w_down)


def _ffn_epilogue(f, h1, p, wg, wp, g3, gp):
    h2 = h1 + _rms(f, g3)
    gate = _sigmoid(_dot(h2.astype(BF16), wg))
    proj = _dot(p.astype(BF16), wp)
    return h2 + _rms(gate * proj, gp)


def _post_ffn_kernel(f_ref, h1_ref, p_ref, wg_ref, wp_ref, g3_ref, gp_ref, o_ref):
    o_ref[...] = _ffn_epilogue(f_ref[...], h1_ref[...], p_ref[...], wg_ref[...], wp_ref[...],
                               g3_ref[...], gp_ref[...])


def _post_ffn(f, h1, p, layer, wg, wp, g3, gp):
    T = h1.shape[0]
    tm = min(T, 512)
    row = lambda i: (i, 0)
    const = lambda i: (0, 0)
    return pl.pallas_call(
        _post_ffn_kernel,
        grid=(T // tm,),
        in_specs=[pl.BlockSpec((tm, D_MODEL), row),
                  pl.BlockSpec((tm, D_MODEL), row),
                  pl.BlockSpec((None, tm, PLE_DIM), lambda i: (layer, i, 0)),
                  pl.BlockSpec((D_MODEL, D_MODEL), const, pipeline_mode=pl.Buffered(1)),
                  pl.BlockSpec((PLE_DIM, D_MODEL), const, pipeline_mode=pl.Buffered(1)),
                  pl.BlockSpec((1, D_MODEL), const),
                  pl.BlockSpec((1, D_MODEL), const)],
        out_specs=pl.BlockSpec((tm, D_MODEL), row),
        out_shape=jax.ShapeDtypeStruct((T, D_MODEL), F32),
        compiler_params=_params(("parallel",)),
        name="post_ffn",
    )(f, h1, p, wg, wp, g3[None, :], gp[None, :])


def _router_kernel(c_ref, wr_ref, tri_ref, route_ref, cnt_ref, carry):
    @pl.when(pl.program_id(0) == 0)
    def _():
        carry[...] = jnp.zeros_like(carry)

    logits = _dot(c_ref[...].astype(BF16), wr_ref[...])
    lane = lax.broadcasted_iota(jnp.int32, logits.shape, 1)
    lg = jnp.where(lane < N_EXPERTS, logits, -jnp.inf)
    m1 = jnp.max(lg, axis=-1, keepdims=True)
    i1 = jnp.min(jnp.where(lg == m1, lane, LANES), axis=-1, keepdims=True)
    lg2 = jnp.where(lane == i1, -jnp.inf, lg)
    m2 = jnp.max(lg2, axis=-1, keepdims=True)
    i2 = jnp.min(jnp.where(lg2 == m2, lane, LANES), axis=-1, keepdims=True)
    e = jnp.exp(m2 - m1)
    g1 = 1.0 / (1.0 + e)
    g2 = e / (1.0 + e)
    onehot = jnp.where((lane == i1) | (lane == i2), 1.0, 0.0)
    before = _dot(tri_ref[...], onehot.astype(BF16)) + carry[0:1, :]
    rank1 = jnp.sum(jnp.where(lane == i1, before, 0.0), axis=-1, keepdims=True)
    rank2 = jnp.sum(jnp.where(lane == i2, before, 0.0), axis=-1, keepdims=True)
    total = carry[0:1, :] + jnp.sum(onehot, axis=0, keepdims=True)
    carry[...] = jnp.broadcast_to(total, carry.shape)
    cnt_ref[...] = jnp.broadcast_to(total, cnt_ref.shape)
    out = jnp.where(lane == 0, i1.astype(F32), 0.0)
    out = jnp.where(lane == 1, i2.astype(F32), out)
    out = jnp.where(lane == 2, g1, out)
    out = jnp.where(lane == 3, g2, out)
    out = jnp.where(lane == 4, rank1, out)
    out = jnp.where(lane == 5, rank2, out)
    route_ref[...] = out


def _router(c, w_router):
    T = c.shape[0]
    tr = min(T, 512)
    wr = jnp.zeros((D_MODEL, LANES), BF16).at[:, :N_EXPERTS].set(w_router.astype(BF16))
    tri = jnp.asarray(np.tril(np.ones((tr, tr), np.float32), -1), dtype=BF16)
    return pl.pallas_call(
        _router_kernel,
        grid=(T // tr,),
        in_specs=[pl.BlockSpec((tr, c.shape[1]), lambda i: (i, 0)),
                  pl.BlockSpec((D_MODEL, LANES), lambda i: (0, 0)),
                  pl.BlockSpec((tr, tr), lambda i: (0, 0))],
        out_specs=[pl.BlockSpec((tr, LANES), lambda i: (i, 0)),
                   pl.BlockSpec((8, LANES), lambda i: (0, 0))],
        out_shape=[jax.ShapeDtypeStruct((T, LANES), F32),
                   jax.ShapeDtypeStruct((8, LANES), F32)],
        scratch_shapes=[pltpu.VMEM((8, LANES), F32)],
        compiler_params=_params(("arbitrary",)),
        name="moe_router",
    )(c, wr, tri)


def _dispatch_kernel(p1_ref, p2_ref, pad_ref, nv_ref, c_ref, xs_ref, zbuf, sem, zsem, *,
                     tm, seg_tile):
    i = pl.program_id(0)
    base = i * tm
    zr = zbuf.shape[0]

    @pl.when(i == 0)
    def _():
        zbuf[...] = jnp.zeros_like(zbuf)

        def zero_tile(start):
            first = pl.multiple_of(start, zr)
            copies = [pltpu.make_async_copy(zbuf, xs_ref.at[pl.ds(first + k * zr, zr)], zsem)
                      for k in range(seg_tile // zr)]
            for cp in copies:
                cp.start()
            for cp in copies:
                cp.wait()

        def zero_padding(e, _):
            @pl.when(pad_ref[e] >= 0)
            def _():
                zero_tile(pad_ref[e])
            return 0

        def zero_unused(t, _):
            zero_tile(t * seg_tile)
            return 0

        lax.fori_loop(0, N_EXPERTS, zero_padding, 0)
        lax.fori_loop(nv_ref[0], xs_ref.shape[0] // seg_tile, zero_unused, 0)

    def row_copy(r, dst):
        return pltpu.make_async_copy(c_ref.at[pl.ds(r, 1)], xs_ref.at[pl.ds(dst, 1)], sem)

    for r in range(tm):
        row_copy(r, p1_ref[base + r]).start(priority=0)
        row_copy(r, p2_ref[base + r]).start(priority=1)
    for _r in range(2 * tm):
        row_copy(0, 0).wait()


def _dispatch(c, pos1, pos2, pad_tile_row, n_valid, n_rows, seg_tile):
    T, width = c.shape
    tm = min(T, 256)
    return pl.pallas_call(
        functools.partial(_dispatch_kernel, tm=tm, seg_tile=seg_tile),
        grid_spec=pltpu.PrefetchScalarGridSpec(
            num_scalar_prefetch=4,
            grid=(T // tm,),
            in_specs=[pl.BlockSpec((tm, width), lambda i, p1, p2, pt, nv: (i, 0))],
            out_specs=pl.BlockSpec(memory_space=pl.ANY),
            scratch_shapes=[pltpu.VMEM((min(seg_tile, 64), width), c.dtype),
                            pltpu.SemaphoreType.DMA(()),
                            pltpu.SemaphoreType.DMA(())],
        ),
        out_shape=jax.ShapeDtypeStruct((n_rows, width), c.dtype),
        compiler_params=_params(("arbitrary",)),
        name="moe_dispatch",
    )(pos1, pos2, pad_tile_row, n_valid, c)


def _combine_kernel(p1_ref, p2_ref, ys_ref, route_ref, h1_ref, p_ref, wg_ref, wp_ref,
                    g3_ref, gp_ref, o_ref, buf, sems, *, tm):
    i = pl.program_id(0)
    slot = i % 2

    def row_copy(src, s, k, r):
        return pltpu.make_async_copy(ys_ref.at[pl.ds(src, 1)], buf.at[s, k, pl.ds(r, 1)],
                                     sems.at[s])

    def gather_tile(tile, s):
        base = tile * tm
        for r in range(tm):
            row_copy(p1_ref[base + r], s, 0, r).start(priority=0)
            row_copy(p2_ref[base + r], s, 1, r).start(priority=1)

    @pl.when(i == 0)
    def _():
        gather_tile(0, 0)

    @pl.when(i + 1 < pl.num_programs(0))
    def _():
        gather_tile(i + 1, 1 - slot)

    for _r in range(2 * tm):
        row_copy(0, slot, 0, 0).wait()
    route = route_ref[...]
    f = route[:, 2:3] * buf[slot, 0] + route[:, 3:4] * buf[slot, 1]
    o_ref[...] = _ffn_epilogue(f, h1_ref[...], p_ref[...], wg_ref[...], wp_ref[...],
                               g3_ref[...], gp_ref[...])


def _combine(ys, pos1, pos2, route, h1, p, layer, wg, wp, g3, gp):
    T = h1.shape[0]
    tm = min(T, 256)
    row = lambda i, p1, p2: (i, 0)
    const = lambda i, p1, p2: (0, 0)
    return pl.pallas_call(
        functools.partial(_combine_kernel, tm=tm),
        grid_spec=pltpu.PrefetchScalarGridSpec(
            num_scalar_prefetch=2,
            grid=(T // tm,),
            in_specs=[pl.BlockSpec(memory_space=pl.ANY),
                      pl.BlockSpec((tm, LANES), row),
                      pl.BlockSpec((tm, D_MODEL), row),
                      pl.BlockSpec((None, tm, PLE_DIM), lambda i, p1, p2: (layer, i, 0)),
                      pl.BlockSpec((D_MODEL, D_MODEL), const, pipeline_mode=pl.Buffered(1)),
                      pl.BlockSpec((PLE_DIM, D_MODEL), const, pipeline_mode=pl.Buffered(1)),
                      pl.BlockSpec((1, D_MODEL), const),
                      pl.BlockSpec((1, D_MODEL), const)],
            out_specs=pl.BlockSpec((tm, D_MODEL), row),
            scratch_shapes=[pltpu.VMEM((2, 2, tm, D_MODEL), F32),
                            pltpu.SemaphoreType.DMA((2,))],
        ),
        out_shape=jax.ShapeDtypeStruct((T, D_MODEL), F32),
        compiler_params=_params(("arbitrary",)),
        name="moe_combine",
    )(pos1, pos2, ys, route, h1, p, wg, wp, g3[None, :], gp[None, :])


def _swap_halves(w):
    half = w.shape[-1] // 2
    return jnp.concatenate([w[..., half:], w[..., :half]], axis=-1)


def _prep_w_in(w):
    w = w.astype(BF16)
    k_rope = w[:, 640:704]
    pad64 = jnp.zeros((D_MODEL, 64), w.dtype)
    cols = [w[:, :640], k_rope, pad64, _swap_halves(k_rope), pad64,
            jnp.zeros((D_MODEL, HGRN_COL0 - 896), w.dtype), w[:, 704:]]
    return jnp.concatenate(cols, axis=1).astype(BF16)


def _prep_w_uq(w):
    w = w.reshape(Q_RANK, N_HEADS, QK_DIM)
    nope = w[:, :, :NOPE_DIM]
    pe = w[:, :, NOPE_DIM:]
    pad = jnp.zeros((Q_RANK, N_HEADS, LANES - ROPE_DIM), w.dtype)
    pe_pad = jnp.concatenate([pe, pad], axis=-1)
    pe_swap = jnp.concatenate([_swap_halves(pe), pad], axis=-1)
    parts = [x.reshape(Q_RANK, GROUP_WIDTH) for x in (nope, pe_pad, pe_swap)]
    return jnp.concatenate(parts, axis=1).astype(BF16)


def _prep_w_ukv(w):
    w = w.reshape(KV_RANK, N_HEADS, NOPE_DIM + V_DIM)
    k = w[:, :, :NOPE_DIM].reshape(KV_RANK, GROUP_WIDTH)
    v = w[:, :, NOPE_DIM:].reshape(KV_RANK, GROUP_WIDTH)
    return jnp.concatenate([k, v], axis=1).astype(BF16)


def _moe_plan(route, counts, tm, n_tiles):
    e1 = route[:, 0].astype(jnp.int32)
    e2 = route[:, 1].astype(jnp.int32)
    rank1 = route[:, 4].astype(jnp.int32)
    rank2 = route[:, 5].astype(jnp.int32)
    cnt = counts[0, :N_EXPERTS].astype(jnp.int32)
    padded = ((cnt + tm - 1) // tm) * tm
    seg_end = jnp.cumsum(padded)
    seg_start = seg_end - padded
    pos1 = seg_start[e1] + rank1
    pos2 = seg_start[e2] + rank2
    tile_start = jnp.arange(n_tiles, dtype=jnp.int32) * tm
    tile_expert = jnp.sum(tile_start[:, None] >= seg_end[None, :], axis=1).astype(jnp.int32)
    tile_expert = jnp.minimum(tile_expert, N_EXPERTS - 1)
    n_valid = (seg_end[-1] // tm).astype(jnp.int32)[None]
    pad_tile_row = jnp.where(padded > 0, seg_end - tm, -1).astype(jnp.int32)
    return pos1, pos2, tile_expert, n_valid, pad_tile_row


def kernel(x, p, positions, sandwich_norms, w_in, mla_q_norm, mla_kv_norm, w_uq, w_ukv,
           hgrn_lb_logits, hgrn_out_norm, w_out, ffn_w_gu, ffn_w_down, moe_w_router,
           moe_w_gu, moe_w_down, ple_w_proj, ple_w_gate, ple_norm):
    B, S, _ = x.shape
    T = B * S
    h = x.reshape(T, D_MODEL)
    ct, st = _rope_tables(positions)
    hgrn_consts = _hgrn_consts(HGRN_CHUNK)
    tm = min(T, 512)
    for l in range(DEPTH):
        z = _norm_matmul(h, sandwich_norms[l, 0], _prep_w_in(w_in[l]))
        q, k, v = _mla_proj(z, ct, st, mla_q_norm[l], mla_kv_norm[l],
                            _prep_w_uq(w_uq[l]), _prep_w_ukv(w_ukv[l]), B, S)
        moe = l % 2 == 1
        if moe:
            w_down = moe_w_down[l // 2]
            o_mla, w_down_bf = _attention_and_cast(q, k, v, w_down.reshape(-1, D_MODEL))
            w_down_bf = w_down_bf.reshape(w_down.shape)
        else:
            o_mla = _attention(q, k, v)
        o_mla = o_mla.reshape(T, GROUP_WIDTH)
        later = [w_out[l], ple_w_gate[l]] + ([] if moe else [ffn_w_down[l // 2]])
        o_hgrn, w_out_bf, wg, *rest = _hgrn(z, hgrn_lb_logits, hgrn_out_norm[l], hgrn_consts,
                                            l, B, S, casts=later)
        h1, c = _out_proj(o_mla, o_hgrn, h, w_out_bf,
                          sandwich_norms[l, 1], sandwich_norms[l, 2],
                          F32 if moe else BF16)
        wp = ple_w_proj[l].astype(BF16)
        p_all = p.reshape(DEPTH, T, PLE_DIM)
        if not moe:
            def one_expert(rows):
                return jnp.zeros((T // rows,), jnp.int32), jnp.full((1,), T // rows, jnp.int32)

            tm_dense = min(T, 1024)
            act = _gate_up(c, ffn_w_gu[l // 2][None], *one_expert(tm_dense), tm_dense, 512)
            f = _down(act, rest[0][None], *one_expert(tm), tm)
            h = _post_ffn(f, h1, p_all, l, wg, wp, sandwich_norms[l, 3], ple_norm[l])
        else:
            n_tiles = (2 * T) // tm + N_EXPERTS
            route, counts = _router(c, moe_w_router[l // 2])
            pos1, pos2, tile_expert, n_valid, pad_tile_row = _moe_plan(route, counts, tm, n_tiles)
            xs = _dispatch(c, pos1, pos2, pad_tile_row, n_valid, n_tiles * tm, tm)
            act = _gate_up(xs, moe_w_gu[l // 2], tile_expert, n_valid, tm, 1024)
            ys = _down(act, w_down_bf, tile_expert, n_valid, tm)
            h = _combine(ys, pos1, pos2, route, h1, p_all, l, wg, wp,
                         sandwich_norms[l, 3], ple_norm[l])
    return h.reshape(B, S, D_MODEL)
```

```python
import functools
import math

import numpy as np
import jax
import jax.numpy as jnp
from jax import lax
from jax.experimental import pallas as pl
from jax.experimental.pallas import tpu as pltpu

F32 = jnp.float32
BF16 = jnp.bfloat16

D_MODEL = 2048
DEPTH = 2
N_HEADS = 8
NOPE_DIM = 128
ROPE_DIM = 64
V_DIM = 128
QK_DIM = NOPE_DIM + ROPE_DIM
Q_RANK = 384
KV_RANK = 256
HGRN_K = 128
HGRN_V = 128
GROUP_WIDTH = N_HEADS * 128
FFN_DIM = 7168
N_EXPERTS = 8
PLE_DIM = 256
ROPE_THETA = 10000.0
LB_FLOOR = 1e-30
EPS = 1e-6

LANES = 128
QK_PAD = 256
Q_SCALE = QK_DIM ** -0.5 * math.log2(math.e)
Z_MLA = 1024
Z_WIDTH = 5120
HGRN_COL0 = 1024
HGRN_CHUNK = 64
HGRN_HEADS_PER_STEP = 4
GU_SUB = 512
IN_PROJ_SUB = 1024
CAST_SLAB_BYTES = 4 * 1024 * 1024
VMEM_LIMIT = 56 * 1024 * 1024


def _params(semantics, vmem=VMEM_LIMIT):
    return pltpu.CompilerParams(dimension_semantics=semantics, vmem_limit_bytes=vmem)


def _rms(x, gain_row):
    ms = jnp.mean(x * x, axis=-1, keepdims=True)
    return x * lax.rsqrt(ms + EPS) * gain_row


def _dot(a, b):
    return jnp.dot(a, b, preferred_element_type=F32)


def _dot_nt(a, b):
    return lax.dot_general(a, b, (((1,), (1,)), ((), ())), preferred_element_type=F32)


def _dot_tn(a, b):
    return lax.dot_general(a, b, (((0,), (0,)), ((), ())), preferred_element_type=F32)


def _sigmoid(x):
    return 1.0 / (1.0 + jnp.exp(-x))


def _fold_lanes(x, op):
    parts = [x[:, c:c + LANES] for c in range(0, x.shape[1], LANES)]
    while len(parts) > 1:
        parts = [op(a, b) for a, b in zip(parts[0::2], parts[1::2])] + parts[len(parts) & ~1:]
    return parts[0]


def _rope_kernel(pos_ref, inv_ref, sgn_ref, ct_ref, st_ref):
    ang = pos_ref[...].astype(F32) * inv_ref[...]
    keep = jnp.abs(sgn_ref[...])
    ct_ref[...] = jnp.cos(ang) * keep
    st_ref[...] = jnp.sin(ang) * sgn_ref[...]


def _rope_tables(positions):
    T = positions.size
    tt = min(T, 1024)
    half = ROPE_DIM // 2
    inv_freq = 1.0 / (ROPE_THETA ** (jnp.arange(0, ROPE_DIM, 2, dtype=F32) / ROPE_DIM))
    inv_row = jnp.concatenate([inv_freq, inv_freq, jnp.zeros((LANES - ROPE_DIM,), F32)])[None, :]
    sgn = np.zeros((1, LANES), np.float32)
    sgn[0, :half] = -1.0
    sgn[0, half:ROPE_DIM] = 1.0
    return pl.pallas_call(
        _rope_kernel,
        grid=(T // tt,),
        in_specs=[pl.BlockSpec((tt, 1), lambda i: (i, 0)),
                  pl.BlockSpec((1, LANES), lambda i: (0, 0)),
                  pl.BlockSpec((1, LANES), lambda i: (0, 0))],
        out_specs=[pl.BlockSpec((tt, LANES), lambda i: (i, 0))] * 2,
        out_shape=[jax.ShapeDtypeStruct((T, LANES), F32)] * 2,
        compiler_params=_params(("parallel",)),
        name="rope_tables",
    )(positions.reshape(T, 1), inv_row, jnp.asarray(sgn))


def _norm_matmul_kernel(x_ref, g_ref, w_ref, o_ref):
    a = _rms(x_ref[...], g_ref[...]).astype(BF16)
    for c in range(0, o_ref.shape[1], IN_PROJ_SUB):
        o_ref[:, c:c + IN_PROJ_SUB] = _dot(a, w_ref[:, c:c + IN_PROJ_SUB])


def _norm_matmul(x, gain, w):
    T, K = x.shape
    N = w.shape[1]
    tm = min(T, 512)
    return pl.pallas_call(
        _norm_matmul_kernel,
        grid=(T // tm,),
        in_specs=[pl.BlockSpec((tm, K), lambda i: (i, 0)),
                  pl.BlockSpec((1, K), lambda i: (0, 0)),
                  pl.BlockSpec((K, N), lambda i: (0, 0), pipeline_mode=pl.Buffered(1))],
        out_specs=pl.BlockSpec((tm, N), lambda i: (i, 0)),
        out_shape=jax.ShapeDtypeStruct((T, N), F32),
        compiler_params=_params(("parallel",)),
        name="in_proj",
    )(x, gain[None, :], w)


def _mla_proj_kernel(z_ref, ct_ref, st_ref, qn_ref, kvn_ref, wq_ref, wkv_ref,
                     q_ref, k_ref, v_ref):
    z = z_ref[...]
    ct = ct_ref[...]
    st = st_ref[...]
    aq = _rms(z[:, :Q_RANK], qn_ref[...]).astype(BF16)
    akv = _rms(z[:, Q_RANK:Q_RANK + KV_RANK], kvn_ref[...]).astype(BF16)
    k_a = z[:, 640:768]
    k_b = z[:, 768:896]
    k_pe = (k_a * ct + k_b * st).astype(BF16)
    q = _dot(aq, wq_ref[...])
    kv = _dot(akv, wkv_ref[...])
    for h in range(N_HEADS):
        lo, hi = h * LANES, (h + 1) * LANES
        q_ref[0, h, :, 0:LANES] = (q[:, lo:hi] * Q_SCALE).astype(BF16)
        q_pe = q[:, GROUP_WIDTH + lo:GROUP_WIDTH + hi] * ct \
            + q[:, 2 * GROUP_WIDTH + lo:2 * GROUP_WIDTH + hi] * st
        q_ref[0, h, :, LANES:QK_PAD] = (q_pe * Q_SCALE).astype(BF16)
        k_ref[0, h, :, 0:LANES] = kv[:, lo:hi].astype(BF16)
        k_ref[0, h, :, LANES:QK_PAD] = k_pe
        v_ref[0, h, :, :] = kv[:, GROUP_WIDTH + lo:GROUP_WIDTH + hi].astype(BF16)


def _mla_proj(z, ct, st, q_norm, kv_norm, wq, wkv, B, S):
    ts = min(S, 512)
    ns = S // ts
    tok = lambda b, i: (b * ns + i, 0)
    const = lambda b, i: (0, 0)
    head_out = lambda b, i: (b, 0, i, 0)
    return pl.pallas_call(
        _mla_proj_kernel,
        grid=(B, ns),
        in_specs=[pl.BlockSpec((ts, Z_MLA), tok),
                  pl.BlockSpec((ts, LANES), tok),
                  pl.BlockSpec((ts, LANES), tok),
                  pl.BlockSpec((1, Q_RANK), const),
                  pl.BlockSpec((1, KV_RANK), const),
                  pl.BlockSpec(wq.shape, const),
                  pl.BlockSpec(wkv.shape, const)],
        out_specs=[pl.BlockSpec((1, N_HEADS, ts, QK_PAD), head_out),
                   pl.BlockSpec((1, N_HEADS, ts, QK_PAD), head_out),
                   pl.BlockSpec((1, N_HEADS, ts, V_DIM), head_out)],
        out_shape=[jax.ShapeDtypeStruct((B, N_HEADS, S, QK_PAD), BF16),
                   jax.ShapeDtypeStruct((B, N_HEADS, S, QK_PAD), BF16),
                   jax.ShapeDtypeStruct((B, N_HEADS, S, V_DIM), BF16)],
        compiler_params=_params(("parallel", "parallel")),
        name="mla_proj",
    )(z, ct, st, q_norm[None, :], kv_norm[None, :], wq, wkv)


def _attn_kernel(*refs, tk, with_cast):
    if with_cast:
        q_ref, k_ref, v_ref, w_ref, o_ref, wbf_ref, s_a, s_b = refs
    else:
        q_ref, k_ref, v_ref, o_ref, s_a, s_b = refs
    i = pl.program_id(2)
    top_rows = pl.ds(0, tk)
    bot_rows = pl.ds(tk, tk)

    def kv_rows(u):
        return pl.ds(pl.multiple_of(u * tk, tk), tk)

    def scores(u):
        return _dot_nt(q_ref[0, 0], k_ref[0, 0, kv_rows(u), :])

    lane = lax.broadcasted_iota(jnp.int32, (tk, LANES), 1)
    ones_col = jnp.where(lane == 0, 1.0, 0.0).astype(BF16)

    def values(u):
        return jnp.concatenate([v_ref[0, 0, kv_rows(u), :], ones_col], axis=1)

    def update(state, s, v, masked):
        m, acc = state
        if masked:
            row = lax.broadcasted_iota(jnp.int32, (tk, tk), 0)
            col = lax.broadcasted_iota(jnp.int32, (tk, tk), 1)
            s = jnp.where(col <= row, s, -jnp.inf)
        m_new = jnp.maximum(m, jnp.max(_fold_lanes(s, jnp.maximum), axis=-1, keepdims=True))
        alpha = jnp.exp2(m - m_new)
        p = jnp.exp2((s - m_new).astype(BF16))
        acc = alpha * acc + _dot(p, v)
        return m_new, acc

    def both(top, bot, s_ref, u):
        v = values(u)
        return (update(top, s_ref[top_rows, :], v, False),
                update(bot, s_ref[bot_rows, :], v, False))

    s_a[...] = scores(0)
    if with_cast:
        wbf_ref[...] = w_ref[...].astype(BF16)

    def body(t, state):
        top, bot = state
        s_b[...] = scores(2 * t + 1)
        top, bot = both(top, bot, s_a, 2 * t)
        s_a[...] = scores(2 * t + 2)
        top, bot = both(top, bot, s_b, 2 * t + 1)
        return top, bot

    init = (jnp.full((tk, 1), -jnp.inf, F32), jnp.zeros((tk, 2 * LANES), F32))
    top, bot = lax.fori_loop(0, i, body, (init, init))
    s_b[bot_rows, :] = _dot_nt(q_ref[0, 0, bot_rows, :], k_ref[0, 0, kv_rows(2 * i + 1), :])
    v0 = values(2 * i)
    top = update(top, s_a[top_rows, :], v0, True)
    bot = update(bot, s_a[bot_rows, :], v0, False)
    bot = update(bot, s_b[bot_rows, :], values(2 * i + 1), True)
    for rows, (_, acc) in ((top_rows, top), (bot_rows, bot)):
        o_ref[0, rows, :] = (acc[:, :V_DIM] / acc[:, V_DIM:V_DIM + 1]).astype(BF16)


def _attention(q, k, v, w_cast=None):
    B, H, S, _ = q.shape
    tq = min(S, 1024)
    tk = tq // 2
    nq = S // tq
    in_specs = [pl.BlockSpec((1, 1, tq, QK_PAD), lambda b, h, i: (b, h, i, 0)),
                pl.BlockSpec((1, 1, S, QK_PAD), lambda b, h, i: (b, h, 0, 0)),
                pl.BlockSpec((1, 1, S, V_DIM), lambda b, h, i: (b, h, 0, 0))]
    out_specs = [pl.BlockSpec((1, tq, V_DIM), lambda b, h, i: (b, i, h))]
    out_shape = [jax.ShapeDtypeStruct((B, S, H * V_DIM), BF16)]
    args = [q, k, v]
    if w_cast is not None:
        slab = w_cast.shape[0] // (B * H * nq)
        spec = pl.BlockSpec((slab, w_cast.shape[1]), lambda b, h, i: ((b * H + h) * nq + i, 0))
        in_specs.append(spec)
        out_specs.append(spec)
        out_shape.append(jax.ShapeDtypeStruct(w_cast.shape, BF16))
        args.append(w_cast)
    outs = pl.pallas_call(
        functools.partial(_attn_kernel, tk=tk, with_cast=w_cast is not None),
        grid=(B, H, nq),
        in_specs=in_specs,
        out_specs=out_specs,
        out_shape=out_shape,
        scratch_shapes=[pltpu.VMEM((tq, tk), F32), pltpu.VMEM((tq, tk), F32)],
        compiler_params=_params(("parallel", "parallel", "arbitrary")),
        name="mla_attention",
    )(*args)
    return outs if w_cast is not None else outs[0]


def _cast_slab(shape, steps):
    rows, cols = shape
    slab = rows // steps
    fits = rows % steps == 0 and slab % 16 == 0 and slab * cols * 4 <= CAST_SLAB_BYTES
    return slab if fits else 0


def _attention_and_cast(q, k, v, w):
    B, H, S, _ = q.shape
    if _cast_slab(w.shape, B * H * (S // min(S, 1024))):
        return _attention(q, k, v, w)
    return _attention(q, k, v), w.astype(BF16)


def _hgrn_consts(C):
    levels = [C >> (i + 1) for i in range(int(math.log2(C)))]
    t = np.arange(C)
    u = np.arange(C)
    mats = [(u[None, :] <= t[:, None]),
            (u[None, :] > t[:, None])]
    pair_masks = [np.eye(C, dtype=bool)]
    for m in levels:
        ref = (t // (2 * m)) * (2 * m) + m - 1
        hi = ((t // m) % 2) == 1
        rng_hi = (u[None, :] > ref[:, None]) & (u[None, :] <= t[:, None])
        rng_lo = (u[None, :] > t[:, None]) & (u[None, :] <= ref[:, None])
        mats.append(np.where(hi[:, None], rng_hi, rng_lo))
        same = (t[:, None] // (2 * m)) == (t[None, :] // (2 * m))
        pair_masks.append(hi[:, None] & (~hi[None, :]) & same)
    stack = np.concatenate(mats, axis=0).astype(np.float32)
    sums = jnp.asarray(np.concatenate([stack] * 3, axis=1), dtype=BF16)
    heads = np.eye(HGRN_HEADS_PER_STEP, dtype=bool)
    stacked = np.stack([np.kron(heads, m) for m in pair_masks])
    return sums, jnp.asarray(stacked.astype(np.float32))


def _hgrn_kernel(*refs, layer, ts, n_cast):
    (zq_ref, zf_ref, zi_ref, zg_ref, lbl_ref, on_ref, sums_ref, pm_ref) = refs[:8]
    cast_src = refs[8:8 + n_cast]
    o_ref = refs[8 + n_cast]
    cast_dst = refs[9 + n_cast:9 + 2 * n_cast]
    st_scr = refs[9 + 2 * n_cast]
    C = HGRN_CHUNK
    n_levels = pm_ref.shape[0] - 1
    for src, dst in zip(cast_src, cast_dst):
        dst[...] = src[...].astype(BF16)

    @pl.when(pl.program_id(2) == 0)
    def _():
        st_scr[...] = jnp.zeros_like(st_scr)

    lg = lbl_ref[...]
    ex = jnp.exp(lg - jnp.max(lg, axis=0, keepdims=True))
    pr = ex / jnp.sum(ex, axis=0, keepdims=True)
    cum = pr[0:1, :]
    for r in range(1, layer + 1):
        cum = cum + pr[r:r + 1, :]
    lb = cum - pr[0:1, :]
    lb_floor = jnp.maximum(lb, LB_FLOOR)
    one_m_lb = 1.0 - lb
    gain = on_ref[...]

    for c in range(ts // C):
        rows = pl.ds(c * C, C)
        q_raw = zq_ref[rows, :]
        zf = zf_ref[rows, :]
        g_raw = zg_ref[rows, :]
        v_all = zi_ref[rows, :].astype(BF16)
        q_all = q_raw * _sigmoid(q_raw)
        e = jnp.exp(-jnp.abs(zf))
        r = 1.0 / (1.0 + e)
        er = e * r
        pos = zf >= 0
        log_f = jnp.log(lb_floor + one_m_lb * jnp.where(pos, r, er))
        kk_all = one_m_lb * jnp.where(pos, er, r)
        gate_all = g_raw * _sigmoid(g_raw)
        p0 = log_f.astype(BF16)
        r1 = log_f - p0.astype(F32)
        p1 = r1.astype(BF16)
        p2 = (r1 - p1.astype(F32)).astype(BF16)
        sums_all = _dot(sums_ref[...], jnp.concatenate([p0, p1, p2], axis=0))

        def stack(x):
            return jnp.concatenate([x[:, h * LANES:(h + 1) * LANES]
                                    for h in range(HGRN_HEADS_PER_STEP)], axis=0)

        q_st = stack(q_all).astype(BF16)
        kk_st = stack(kk_all).astype(BF16)
        a = pm_ref[0] * _dot_nt(q_st, kk_st)
        for lv in range(n_levels):
            dec = jnp.exp(stack(sums_all[(2 + lv) * C:(3 + lv) * C])).astype(BF16)
            a = a + pm_ref[lv + 1] * _dot_nt(q_st * dec, kk_st * dec)
        o_intra = _dot(a.astype(BF16), stack(v_all))
        b_all = sums_all[0:C]
        st_all = st_scr[...]
        o_inter = _dot_nt(stack(q_all * jnp.exp(b_all)).astype(BF16), st_all.astype(BF16))
        k_end = (kk_all * jnp.exp(sums_all[C:2 * C])).astype(BF16)
        decay = jnp.exp(b_all[C - 1:C, :])
        for hh in range(HGRN_HEADS_PER_STEP):
            lanes = slice(hh * LANES, (hh + 1) * LANES)
            srows = slice(hh * HGRN_V, (hh + 1) * HGRN_V)
            o = o_intra[hh * C:(hh + 1) * C] + o_inter[hh * C:(hh + 1) * C, lanes]
            st_scr[srows, :] = st_all[srows] * decay[:, lanes] \
                + _dot_tn(v_all[:, lanes], k_end[:, lanes])
            o_ref[rows, pl.ds(hh * LANES, LANES)] = \
                (_rms(o, gain) * gate_all[:, lanes]).astype(BF16)


def _hgrn(z, lb_logits, out_norm, consts, layer, B, S, casts=()):
    T = B * S
    ts = min(S, 512)
    ns = S // ts
    sums, pair_masks = consts
    hp = HGRN_HEADS_PER_STEP
    width = hp * LANES
    col0 = HGRN_COL0 // width
    groups = N_HEADS // hp
    steps = B * groups * ns
    step = lambda b, h, i: (b * groups + h) * ns + i
    riding = [(w, idx) for w, idx in casts if _cast_slab(w.shape[1:], steps)]
    cast_in = [pl.BlockSpec((None, _cast_slab(w.shape[1:], steps), w.shape[2]),
                            lambda b, h, i, idx=idx: (idx, step(b, h, i), 0)) for w, idx in riding]
    cast_out = [pl.BlockSpec((_cast_slab(w.shape[1:], steps), w.shape[2]),
                             lambda b, h, i: (step(b, h, i), 0)) for w, _ in riding]

    def zspec(part):
        return pl.BlockSpec((ts, width),
                            lambda b, h, i: (b * ns + i, col0 + part * groups + h))

    full = lambda arr: pl.BlockSpec(arr.shape, lambda b, h, i: (0,) * arr.ndim)
    kern = functools.partial(_hgrn_kernel, layer=layer, ts=ts, n_cast=len(riding))
    outs = pl.pallas_call(
        kern,
        grid=(B, groups, ns),
        in_specs=[zspec(0), zspec(1), zspec(2), zspec(3),
                  pl.BlockSpec((DEPTH, width), lambda b, h, i: (0, h)),
                  pl.BlockSpec((1, HGRN_V), lambda b, h, i: (0, 0)),
                  full(sums), full(pair_masks)] + cast_in,
        out_specs=[pl.BlockSpec((ts, width), lambda b, h, i: (b * ns + i, h))] + cast_out,
        out_shape=[jax.ShapeDtypeStruct((T, GROUP_WIDTH), BF16)]
        + [jax.ShapeDtypeStruct(w.shape[1:], BF16) for w, _ in riding],
        scratch_shapes=[pltpu.VMEM((hp * HGRN_V, HGRN_K), F32)],
        compiler_params=_params(("parallel", "parallel", "arbitrary")),
        name="hgrn2",
    )(z, z, z, z, lb_logits, out_norm[None, :], sums, pair_masks, *[w for w, _ in riding])
    copies = iter(outs[1:])
    return [outs[0]] + [next(copies) if _cast_slab(w.shape[1:], steps) else w[idx].astype(BF16)
                        for w, idx in casts]


def _out_proj_kernel(om_ref, oh_ref, h_ref, w_ref, g1_ref, g2_ref, h1_ref, c_ref):
    y = _dot(om_ref[...], w_ref[0:GROUP_WIDTH, :]) + _dot(oh_ref[...], w_ref[GROUP_WIDTH:, :])
    h1 = h_ref[...] + _rms(y, g1_ref[...])
    h1_ref[...] = h1
    c_ref[...] = _rms(h1, g2_ref[...]).astype(c_ref.dtype)


def _out_proj(om, oh, h, w, g1, g2, c_dtype):
    T = h.shape[0]
    tm = min(T, 512)
    c_width = D_MODEL
    row = lambda i: (i, 0)
    const = lambda i: (0, 0)
    return pl.pallas_call(
        _out_proj_kernel,
        grid=(T // tm,),
        in_specs=[pl.BlockSpec((tm, GROUP_WIDTH), row),
                  pl.BlockSpec((tm, GROUP_WIDTH), row),
                  pl.BlockSpec((tm, D_MODEL), row),
                  pl.BlockSpec((D_MODEL, D_MODEL), const, pipeline_mode=pl.Buffered(1)),
                  pl.BlockSpec((1, D_MODEL), const),
                  pl.BlockSpec((1, D_MODEL), const)],
        out_specs=[pl.BlockSpec((tm, D_MODEL), row), pl.BlockSpec((tm, c_width), row)],
        out_shape=[jax.ShapeDtypeStruct((T, D_MODEL), F32),
                   jax.ShapeDtypeStruct((T, c_width), c_dtype)],
        compiler_params=_params(("parallel",)),
        name="out_proj",
    )(om, oh, h, w, g1[None, :], g2[None, :])


def _gate_up_kernel(te_ref, nv_ref, x_ref, wg_ref, wu_ref, o_ref, wg_bf, wu_bf):
    i = pl.program_id(1)
    valid = i < nv_ref[0]
    new_weights = jnp.logical_or(i == 0, te_ref[i] != te_ref[jnp.maximum(i - 1, 0)])

    def step(refresh):
        x = x_ref[...].astype(BF16)
        for c in range(0, o_ref.shape[1], GU_SUB):
            cols = slice(c, c + GU_SUB)
            if refresh:
                wg_bf[:, cols] = wg_ref[0, :, cols].astype(BF16)
                wu_bf[:, cols] = wu_ref[0, :, cols].astype(BF16)
            g = _dot(x, wg_bf[:, cols])
            u = _dot(x, wu_bf[:, cols])
            o_ref[:, cols] = (g * _sigmoid(g) * u).astype(BF16)

    pl.when(jnp.logical_and(valid, new_weights))(functools.partial(step, True))
    pl.when(jnp.logical_and(valid, jnp.logical_not(new_weights)))(functools.partial(step, False))

    @pl.when(jnp.logical_not(valid))
    def _():
        o_ref[...] = jnp.zeros_like(o_ref)


def _gate_up(x, w_gu, tile_expert, n_valid, tm, tf):
    R, xw = x.shape
    n_tiles = R // tm
    nf = FFN_DIM // tf

    def tile(i, nv):
        return jnp.minimum(i, nv[0] - 1)

    return pl.pallas_call(
        _gate_up_kernel,
        grid_spec=pltpu.PrefetchScalarGridSpec(
            num_scalar_prefetch=2,
            grid=(nf, n_tiles),
            in_specs=[pl.BlockSpec((tm, xw), lambda j, i, te, nv: (tile(i, nv), 0)),
                      pl.BlockSpec((1, D_MODEL, tf), lambda j, i, te, nv: (te[tile(i, nv)], 0, j)),
                      pl.BlockSpec((1, D_MODEL, tf),
                                   lambda j, i, te, nv: (te[tile(i, nv)], 0, j + nf))],
            out_specs=pl.BlockSpec((tm, tf), lambda j, i, te, nv: (i, j)),
            scratch_shapes=[pltpu.VMEM((D_MODEL, tf), BF16), pltpu.VMEM((D_MODEL, tf), BF16)],
        ),
        out_shape=jax.ShapeDtypeStruct((R, FFN_DIM), BF16),
        compiler_params=_params(("arbitrary", "arbitrary")),
        name="ffn_gate_up",
    )(tile_expert, n_valid, x, w_gu, w_gu)


def _down_kernel(te_ref, nv_ref, a_ref, w_ref, o_ref):
    valid = pl.program_id(1) < nv_ref[0]

    @pl.when(valid)
    def _():
        a = a_ref[...]
        for c in range(0, o_ref.shape[1], GU_SUB):
            o_ref[:, c:c + GU_SUB] = _dot(a, w_ref[0, :, c:c + GU_SUB])

    @pl.when(jnp.logical_not(valid))
    def _():
        o_ref[...] = jnp.zeros_like(o_ref)


def _down(act, w_down, tile_expert, n_valid, tm):
    R = act.shape[0]
    n_tiles = R // tm
    tn = 1024

    def tile(i, nv):
        return jnp.minimum(i, nv[0] - 1)

    return pl.pallas_call(
        _down_kernel,
        grid_spec=pltpu.PrefetchScalarGridSpec(
            num_scalar_prefetch=2,
            grid=(D_MODEL // tn, n_tiles),
            in_specs=[pl.BlockSpec((tm, FFN_DIM), lambda j, i, te, nv: (tile(i, nv), 0)),
                      pl.BlockSpec((1, FFN_DIM, tn), lambda j, i, te, nv: (te[tile(i, nv)], 0, j))],
            out_specs=pl.BlockSpec((tm, tn), lambda j, i, te, nv: (i, j)),
        ),
        out_shape=jax.ShapeDtypeStruct((R, D_MODEL), F32),
        compiler_params=_params(("arbitrary", "arbitrary")),
        name="ffn_down",
    )(tile_expert, n_valid, act, w_down)


def _ffn_epilogue(f, h1, p, wg, wp, g3, gp):
    h2 = h1 + _rms(f, g3)
    gate = _sigmoid(_dot(h2.astype(BF16), wg))
    proj = _dot(p.astype(BF16), wp)
    return h2 + _rms(gate * proj, gp)


def _post_ffn_kernel(f_ref, h1_ref, p_ref, wg_ref, wp_ref, g3_ref, gp_ref, o_ref):
    o_ref[...] = _ffn_epilogue(f_ref[...], h1_ref[...], p_ref[...], wg_ref[...], wp_ref[...],
                               g3_ref[...], gp_ref[...])


def _post_ffn(f, h1, p, layer, wg, wp, g3, gp):
    T = h1.shape[0]
    tm = min(T, 512)
    row = lambda i: (i, 0)
    const = lambda i: (0, 0)
    return pl.pallas_call(
        _post_ffn_kernel,
        grid=(T // tm,),
        in_specs=[pl.BlockSpec((tm, D_MODEL), row),
                  pl.BlockSpec((tm, D_MODEL), row),
                  pl.BlockSpec((None, tm, PLE_DIM), lambda i: (layer, i, 0)),
                  pl.BlockSpec((D_MODEL, D_MODEL), const, pipeline_mode=pl.Buffered(1)),
                  pl.BlockSpec((PLE_DIM, D_MODEL), const, pipeline_mode=pl.Buffered(1)),
                  pl.BlockSpec((1, D_MODEL), const),
                  pl.BlockSpec((1, D_MODEL), const)],
        out_specs=pl.BlockSpec((tm, D_MODEL), row),
        out_shape=jax.ShapeDtypeStruct((T, D_MODEL), F32),
        compiler_params=_params(("parallel",)),
        name="post_ffn",
    )(f, h1, p, wg, wp, g3[None, :], gp[None, :])


def _router_kernel(c_ref, wr_ref, tri_ref, route_ref, cnt_ref, carry):
    @pl.when(pl.program_id(0) == 0)
    def _():
        carry[...] = jnp.zeros_like(carry)

    logits = _dot(c_ref[...].astype(BF16), wr_ref[...])
    lane = lax.broadcasted_iota(jnp.int32, logits.shape, 1)
    lg = jnp.where(lane < N_EXPERTS, logits, -jnp.inf)
    m1 = jnp.max(lg, axis=-1, keepdims=True)
    i1 = jnp.min(jnp.where(lg == m1, lane, LANES), axis=-1, keepdims=True)
    lg2 = jnp.where(lane == i1, -jnp.inf, lg)
    m2 = jnp.max(lg2, axis=-1, keepdims=True)
    i2 = jnp.min(jnp.where(lg2 == m2, lane, LANES), axis=-1, keepdims=True)
    e = jnp.exp(m2 - m1)
    g1 = 1.0 / (1.0 + e)
    g2 = e / (1.0 + e)
    onehot = jnp.where((lane == i1) | (lane == i2), 1.0, 0.0)
    before = _dot(tri_ref[...], onehot.astype(BF16)) + carry[0:1, :]
    rank1 = jnp.sum(jnp.where(lane == i1, before, 0.0), axis=-1, keepdims=True)
    rank2 = jnp.sum(jnp.where(lane == i2, before, 0.0), axis=-1, keepdims=True)
    total = carry[0:1, :] + jnp.sum(onehot, axis=0, keepdims=True)
    carry[...] = jnp.broadcast_to(total, carry.shape)
    cnt_ref[...] = jnp.broadcast_to(total, cnt_ref.shape)
    out = jnp.where(lane == 0, i1.astype(F32), 0.0)
    out = jnp.where(lane == 1, i2.astype(F32), out)
    out = jnp.where(lane == 2, g1, out)
    out = jnp.where(lane == 3, g2, out)
    out = jnp.where(lane == 4, rank1, out)
    out = jnp.where(lane == 5, rank2, out)
    route_ref[...] = out


def _router(c, w_router):
    T = c.shape[0]
    tr = min(T, 512)
    wr = jnp.zeros((D_MODEL, LANES), BF16).at[:, :N_EXPERTS].set(w_router.astype(BF16))
    tri = jnp.asarray(np.tril(np.ones((tr, tr), np.float32), -1), dtype=BF16)
    return pl.pallas_call(
        _router_kernel,
        grid=(T // tr,),
        in_specs=[pl.BlockSpec((tr, c.shape[1]), lambda i: (i, 0)),
                  pl.BlockSpec((D_MODEL, LANES), lambda i: (0, 0)),
                  pl.BlockSpec((tr, tr), lambda i: (0, 0))],
        out_specs=[pl.BlockSpec((tr, LANES), lambda i: (i, 0)),
                   pl.BlockSpec((8, LANES), lambda i: (0, 0))],
        out_shape=[jax.ShapeDtypeStruct((T, LANES), F32),
                   jax.ShapeDtypeStruct((8, LANES), F32)],
        scratch_shapes=[pltpu.VMEM((8, LANES), F32)],
        compiler_params=_params(("arbitrary",)),
        name="moe_router",
    )(c, wr, tri)


def _dispatch_kernel(p1_ref, p2_ref, pad_ref, nv_ref, c_ref, xs_ref, zbuf, sem, zsem, *,
                     tm, seg_tile):
    i = pl.program_id(0)
    base = i * tm
    zr = zbuf.shape[0]

    @pl.when(i == 0)
    def _():
        zbuf[...] = jnp.zeros_like(zbuf)

        def zero_tile(start):
            first = pl.multiple_of(start, zr)
            copies = [pltpu.make_async_copy(zbuf, xs_ref.at[pl.ds(first + k * zr, zr)], zsem)
                      for k in range(seg_tile // zr)]
            for cp in copies:
                cp.start()
            for cp in copies:
                cp.wait()

        def zero_padding(e, _):
            @pl.when(pad_ref[e] >= 0)
            def _():
                zero_tile(pad_ref[e])
            return 0

        def zero_unused(t, _):
            zero_tile(t * seg_tile)
            return 0

        lax.fori_loop(0, N_EXPERTS, zero_padding, 0)
        lax.fori_loop(nv_ref[0], xs_ref.shape[0] // seg_tile, zero_unused, 0)

    def row_copy(r, dst):
        return pltpu.make_async_copy(c_ref.at[pl.ds(r, 1)], xs_ref.at[pl.ds(dst, 1)], sem)

    for r in range(tm):
        row_copy(r, p1_ref[base + r]).start(priority=0)
        row_copy(r, p2_ref[base + r]).start(priority=1)
    for _r in range(2 * tm):
        row_copy(0, 0).wait()


def _dispatch(c, pos1, pos2, pad_tile_row, n_valid, n_rows, seg_tile):
    T, width = c.shape
    tm = min(T, 256)
    return pl.pallas_call(
        functools.partial(_dispatch_kernel, tm=tm, seg_tile=seg_tile),
        grid_spec=pltpu.PrefetchScalarGridSpec(
            num_scalar_prefetch=4,
            grid=(T // tm,),
            in_specs=[pl.BlockSpec((tm, width), lambda i, p1, p2, pt, nv: (i, 0))],
            out_specs=pl.BlockSpec(memory_space=pl.ANY),
            scratch_shapes=[pltpu.VMEM((min(seg_tile, 64), width), c.dtype),
                            pltpu.SemaphoreType.DMA(()),
                            pltpu.SemaphoreType.DMA(())],
        ),
        out_shape=jax.ShapeDtypeStruct((n_rows, width), c.dtype),
        compiler_params=_params(("arbitrary",)),
        name="moe_dispatch",
    )(pos1, pos2, pad_tile_row, n_valid, c)


def _combine_kernel(p1_ref, p2_ref, ys_ref, route_ref, h1_ref, p_ref, wg_ref, wp_ref,
                    g3_ref, gp_ref, o_ref, buf, sems, *, tm):
    i = pl.program_id(0)
    slot = i % 2

    def row_copy(src, s, k, r):
        return pltpu.make_async_copy(ys_ref.at[pl.ds(src, 1)], buf.at[s, k, pl.ds(r, 1)],
                                     sems.at[s])

    def gather_tile(tile, s):
        base = tile * tm
        for r in range(tm):
            row_copy(p1_ref[base + r], s, 0, r).start(priority=0)
            row_copy(p2_ref[base + r], s, 1, r).start(priority=1)

    @pl.when(i == 0)
    def _():
        gather_tile(0, 0)

    @pl.when(i + 1 < pl.num_programs(0))
    def _():
        gather_tile(i + 1, 1 - slot)

    for _r in range(2 * tm):
        row_copy(0, slot, 0, 0).wait()
    route = route_ref[...]
    f = route[:, 2:3] * buf[slot, 0] + route[:, 3:4] * buf[slot, 1]
    o_ref[...] = _ffn_epilogue(f, h1_ref[...], p_ref[...], wg_ref[...], wp_ref[...],
                               g3_ref[...], gp_ref[...])


def _combine(ys, pos1, pos2, route, h1, p, layer, wg, wp, g3, gp):
    T = h1.shape[0]
    tm = min(T, 256)
    row = lambda i, p1, p2: (i, 0)
    const = lambda i, p1, p2: (0, 0)
    return pl.pallas_call(
        functools.partial(_combine_kernel, tm=tm),
        grid_spec=pltpu.PrefetchScalarGridSpec(
            num_scalar_prefetch=2,
            grid=(T // tm,),
            in_specs=[pl.BlockSpec(memory_space=pl.ANY),
                      pl.BlockSpec((tm, LANES), row),
                      pl.BlockSpec((tm, D_MODEL), row),
                      pl.BlockSpec((None, tm, PLE_DIM), lambda i, p1, p2: (layer, i, 0)),
                      pl.BlockSpec((D_MODEL, D_MODEL), const, pipeline_mode=pl.Buffered(1)),
                      pl.BlockSpec((PLE_DIM, D_MODEL), const, pipeline_mode=pl.Buffered(1)),
                      pl.BlockSpec((1, D_MODEL), const),
                      pl.BlockSpec((1, D_MODEL), const)],
            out_specs=pl.BlockSpec((tm, D_MODEL), row),
            scratch_shapes=[pltpu.VMEM((2, 2, tm, D_MODEL), F32),
                            pltpu.SemaphoreType.DMA((2,))],
        ),
        out_shape=jax.ShapeDtypeStruct((T, D_MODEL), F32),
        compiler_params=_params(("arbitrary",)),
        name="moe_combine",
    )(pos1, pos2, ys, route, h1, p, wg, wp, g3[None, :], gp[None, :])


def _swap_halves(w):
    half = w.shape[-1] // 2
    return jnp.concatenate([w[..., half:], w[..., :half]], axis=-1)


def _prep_w_in(w):
    w = w.astype(BF16)
    k_rope = w[:, 640:704]
    pad64 = jnp.zeros((D_MODEL, 64), w.dtype)
    cols = [w[:, :640], k_rope, pad64, _swap_halves(k_rope), pad64,
            jnp.zeros((D_MODEL, HGRN_COL0 - 896), w.dtype), w[:, 704:]]
    return jnp.concatenate(cols, axis=1).astype(BF16)


def _prep_w_uq(w):
    w = w.reshape(Q_RANK, N_HEADS, QK_DIM)
    nope = w[:, :, :NOPE_DIM]
    pe = w[:, :, NOPE_DIM:]
    pad = jnp.zeros((Q_RANK, N_HEADS, LANES - ROPE_DIM), w.dtype)
    pe_pad = jnp.concatenate([pe, pad], axis=-1)
    pe_swap = jnp.concatenate([_swap_halves(pe), pad], axis=-1)
    parts = [x.reshape(Q_RANK, GROUP_WIDTH) for x in (nope, pe_pad, pe_swap)]
    return jnp.concatenate(parts, axis=1).astype(BF16)


def _prep_w_ukv(w):
    w = w.reshape(KV_RANK, N_HEADS, NOPE_DIM + V_DIM)
    k = w[:, :, :NOPE_DIM].reshape(KV_RANK, GROUP_WIDTH)
    v = w[:, :, NOPE_DIM:].reshape(KV_RANK, GROUP_WIDTH)
    return jnp.concatenate([k, v], axis=1).astype(BF16)


def _moe_plan(route, counts, tm, n_tiles):
    e1 = route[:, 0].astype(jnp.int32)
    e2 = route[:, 1].astype(jnp.int32)
    rank1 = route[:, 4].astype(jnp.int32)
    rank2 = route[:, 5].astype(jnp.int32)
    cnt = counts[0, :N_EXPERTS].astype(jnp.int32)
    padded = ((cnt + tm - 1) // tm) * tm
    seg_end = jnp.cumsum(padded)
    seg_start = seg_end - padded
    pos1 = seg_start[e1] + rank1
    pos2 = seg_start[e2] + rank2
    tile_start = jnp.arange(n_tiles, dtype=jnp.int32) * tm
    tile_expert = jnp.sum(tile_start[:, None] >= seg_end[None, :], axis=1).astype(jnp.int32)
    tile_expert = jnp.minimum(tile_expert, N_EXPERTS - 1)
    n_valid = (seg_end[-1] // tm).astype(jnp.int32)[None]
    pad_tile_row = jnp.where(padded > 0, seg_end - tm, -1).astype(jnp.int32)
    return pos1, pos2, tile_expert, n_valid, pad_tile_row


def kernel(x, p, positions, sandwich_norms, w_in, mla_q_norm, mla_kv_norm, w_uq, w_ukv,
           hgrn_lb_logits, hgrn_out_norm, w_out, ffn_w_gu, ffn_w_down, moe_w_router,
           moe_w_gu, moe_w_down, ple_w_proj, ple_w_gate, ple_norm):
    B, S, _ = x.shape
    T = B * S
    h = x.reshape(T, D_MODEL)
    ct, st = _rope_tables(positions)
    hgrn_consts = _hgrn_consts(HGRN_CHUNK)
    tm = min(T, 512)
    for l in range(DEPTH):
        z = _norm_matmul(h, sandwich_norms[l, 0], _prep_w_in(w_in[l]))
        q, k, v = _mla_proj(z, ct, st, mla_q_norm[l], mla_kv_norm[l],
                            _prep_w_uq(w_uq[l]), _prep_w_ukv(w_ukv[l]), B, S)
        moe = l % 2 == 1
        if moe:
            w_down = moe_w_down[l // 2]
            o_mla, w_down_bf = _attention_and_cast(q, k, v, w_down.reshape(-1, D_MODEL))
            w_down_bf = w_down_bf.reshape(w_down.shape)
        else:
            o_mla = _attention(q, k, v)
        o_mla = o_mla.reshape(T, GROUP_WIDTH)
        later = [(w_out, l), (ple_w_gate, l)] + ([] if moe else [(ffn_w_down, l // 2)])
        o_hgrn, w_out_bf, wg, *rest = _hgrn(z, hgrn_lb_logits, hgrn_out_norm[l], hgrn_consts,
                                            l, B, S, casts=later)
        h1, c = _out_proj(o_mla, o_hgrn, h, w_out_bf,
                          sandwich_norms[l, 1], sandwich_norms[l, 2],
                          F32 if moe else BF16)
        wp = ple_w_proj[l].astype(BF16)
        p_all = p.reshape(DEPTH, T, PLE_DIM)
        if not moe:
            def one_expert(rows):
                return jnp.zeros((T // rows,), jnp.int32), jnp.full((1,), T // rows, jnp.int32)

            tm_dense = min(T, 1024)
            act = _gate_up(c, ffn_w_gu[l // 2][None], *one_expert(tm_dense), tm_dense, 512)
            f = _down(act, rest[0][None], *one_expert(tm), tm)
            h = _post_ffn(f, h1, p_all, l, wg, wp, sandwich_norms[l, 3], ple_norm[l])
        else:
            n_tiles = (2 * T) // tm + N_EXPERTS
            route, counts = _router(c, moe_w_router[l // 2])
            pos1, pos2, tile_expert, n_valid, pad_tile_row = _moe_plan(route, counts, tm, n_tiles)
            xs = _dispatch(c, pos1, pos2, pad_tile_row, n_valid, n_tiles * tm, tm)
            act = _gate_up(xs, moe_w_gu[l // 2], tile_expert, n_valid, tm, 1024)
            ys = _down(act, w_down_bf, tile_expert, n_valid, tm)
            h = _combine(ys, pos1, pos2, route, h1, p_all, l, wg, wp,
                         sandwich_norms[l, 3], ple_norm[l])
    return h.reshape(B, S, D_MODEL)
```

```python
import functools
import math

import numpy as np
import jax
import jax.numpy as jnp
from jax import lax
from jax.experimental import pallas as pl
from jax.experimental.pallas import tpu as pltpu

F32 = jnp.float32
BF16 = jnp.bfloat16

D_MODEL = 2048
DEPTH = 2
N_HEADS = 8
NOPE_DIM = 128
ROPE_DIM = 64
V_DIM = 128
QK_DIM = NOPE_DIM + ROPE_DIM
Q_RANK = 384
KV_RANK = 256
HGRN_K = 128
HGRN_V = 128
GROUP_WIDTH = N_HEADS * 128
FFN_DIM = 7168
N_EXPERTS = 8
PLE_DIM = 256
ROPE_THETA = 10000.0
LB_FLOOR = 1e-30
EPS = 1e-6

LANES = 128
QK_PAD = 256
Q_SCALE = QK_DIM ** -0.5 * math.log2(math.e)
Z_MLA = 1024
Z_WIDTH = 5120
HGRN_COL0 = 1024
HGRN_CHUNK = 64
HGRN_HEADS_PER_STEP = 4
GU_SUB = 512
IN_PROJ_SUB = 1024
CAST_SLAB_BYTES = 4 * 1024 * 1024
VMEM_LIMIT = 56 * 1024 * 1024


def _params(semantics, vmem=VMEM_LIMIT):
    return pltpu.CompilerParams(dimension_semantics=semantics, vmem_limit_bytes=vmem)


def _rms(x, gain_row):
    ms = jnp.mean(x * x, axis=-1, keepdims=True)
    return x * lax.rsqrt(ms + EPS) * gain_row


def _dot(a, b):
    return jnp.dot(a, b, preferred_element_type=F32)


def _dot_nt(a, b):
    return lax.dot_general(a, b, (((1,), (1,)), ((), ())), preferred_element_type=F32)


def _dot_tn(a, b):
    return lax.dot_general(a, b, (((0,), (0,)), ((), ())), preferred_element_type=F32)


def _sigmoid(x):
    return 1.0 / (1.0 + jnp.exp(-x))


def _fold_lanes(x, op):
    parts = [x[:, c:c + LANES] for c in range(0, x.shape[1], LANES)]
    while len(parts) > 1:
        parts = [op(a, b) for a, b in zip(parts[0::2], parts[1::2])] + parts[len(parts) & ~1:]
    return parts[0]


def _rope_kernel(pos_ref, inv_ref, sgn_ref, ct_ref, st_ref):
    ang = pos_ref[...].astype(F32) * inv_ref[...]
    keep = jnp.abs(sgn_ref[...])
    ct_ref[...] = jnp.cos(ang) * keep
    st_ref[...] = jnp.sin(ang) * sgn_ref[...]


def _rope_tables(positions):
    T = positions.size
    tt = min(T, 1024)
    half = ROPE_DIM // 2
    inv_freq = 1.0 / (ROPE_THETA ** (jnp.arange(0, ROPE_DIM, 2, dtype=F32) / ROPE_DIM))
    inv_row = jnp.concatenate([inv_freq, inv_freq, jnp.zeros((LANES - ROPE_DIM,), F32)])[None, :]
    sgn = np.zeros((1, LANES), np.float32)
    sgn[0, :half] = -1.0
    sgn[0, half:ROPE_DIM] = 1.0
    return pl.pallas_call(
        _rope_kernel,
        grid=(T // tt,),
        in_specs=[pl.BlockSpec((tt, 1), lambda i: (i, 0)),
                  pl.BlockSpec((1, LANES), lambda i: (0, 0)),
                  pl.BlockSpec((1, LANES), lambda i: (0, 0))],
        out_specs=[pl.BlockSpec((tt, LANES), lambda i: (i, 0))] * 2,
        out_shape=[jax.ShapeDtypeStruct((T, LANES), F32)] * 2,
        compiler_params=_params(("parallel",)),
        name="rope_tables",
    )(positions.reshape(T, 1), inv_row, jnp.asarray(sgn))


def _norm_matmul_kernel(x_ref, g_ref, w_ref, o_ref):
    a = _rms(x_ref[...], g_ref[...]).astype(BF16)
    for c in range(0, o_ref.shape[1], IN_PROJ_SUB):
        o_ref[:, c:c + IN_PROJ_SUB] = _dot(a, w_ref[:, c:c + IN_PROJ_SUB])


def _norm_matmul(x, gain, w):
    T, K = x.shape
    N = w.shape[1]
    tm = min(T, 512)
    return pl.pallas_call(
        _norm_matmul_kernel,
        grid=(T // tm,),
        in_specs=[pl.BlockSpec((tm, K), lambda i: (i, 0)),
                  pl.BlockSpec((1, K), lambda i: (0, 0)),
                  pl.BlockSpec((K, N), lambda i: (0, 0), pipeline_mode=pl.Buffered(1))],
        out_specs=pl.BlockSpec((tm, N), lambda i: (i, 0)),
        out_shape=jax.ShapeDtypeStruct((T, N), F32),
        compiler_params=_params(("parallel",)),
        name="in_proj",
    )(x, gain[None, :], w)


def _mla_proj_kernel(z_ref, ct_ref, st_ref, qn_ref, kvn_ref, wq_ref, wkv_ref,
                     q_ref, k_ref, v_ref):
    z = z_ref[...]
    ct = ct_ref[...]
    st = st_ref[...]
    aq = _rms(z[:, :Q_RANK], qn_ref[...]).astype(BF16)
    akv = _rms(z[:, Q_RANK:Q_RANK + KV_RANK], kvn_ref[...]).astype(BF16)
    k_a = z[:, 640:768]
    k_b = z[:, 768:896]
    k_pe = (k_a * ct + k_b * st).astype(BF16)
    q = _dot(aq, wq_ref[...])
    kv = _dot(akv, wkv_ref[...])
    for h in range(N_HEADS):
        lo, hi = h * LANES, (h + 1) * LANES
        q_ref[0, h, :, 0:LANES] = (q[:, lo:hi] * Q_SCALE).astype(BF16)
        q_pe = q[:, GROUP_WIDTH + lo:GROUP_WIDTH + hi] * ct \
            + q[:, 2 * GROUP_WIDTH + lo:2 * GROUP_WIDTH + hi] * st
        q_ref[0, h, :, LANES:QK_PAD] = (q_pe * Q_SCALE).astype(BF16)
        k_ref[0, h, :, 0:LANES] = kv[:, lo:hi].astype(BF16)
        k_ref[0, h, :, LANES:QK_PAD] = k_pe
        v_ref[0, h, :, :] = kv[:, GROUP_WIDTH + lo:GROUP_WIDTH + hi].astype(BF16)


def _mla_proj(z, ct, st, q_norm, kv_norm, wq, wkv, B, S):
    ts = min(S, 512)
    ns = S // ts
    tok = lambda b, i: (b * ns + i, 0)
    const = lambda b, i: (0, 0)
    head_out = lambda b, i: (b, 0, i, 0)
    return pl.pallas_call(
        _mla_proj_kernel,
        grid=(B, ns),
        in_specs=[pl.BlockSpec((ts, Z_MLA), tok),
                  pl.BlockSpec((ts, LANES), tok),
                  pl.BlockSpec((ts, LANES), tok),
                  pl.BlockSpec((1, Q_RANK), const),
                  pl.BlockSpec((1, KV_RANK), const),
                  pl.BlockSpec(wq.shape, const),
                  pl.BlockSpec(wkv.shape, const)],
        out_specs=[pl.BlockSpec((1, N_HEADS, ts, QK_PAD), head_out),
                   pl.BlockSpec((1, N_HEADS, ts, QK_PAD), head_out),
                   pl.BlockSpec((1, N_HEADS, ts, V_DIM), head_out)],
        out_shape=[jax.ShapeDtypeStruct((B, N_HEADS, S, QK_PAD), BF16),
                   jax.ShapeDtypeStruct((B, N_HEADS, S, QK_PAD), BF16),
                   jax.ShapeDtypeStruct((B, N_HEADS, S, V_DIM), BF16)],
        compiler_params=_params(("parallel", "parallel")),
        name="mla_proj",
    )(z, ct, st, q_norm[None, :], kv_norm[None, :], wq, wkv)


def _attn_kernel(*refs, tk, with_cast):
    if with_cast:
        q_ref, k_ref, v_ref, w_ref, o_ref, wbf_ref, s_a, s_b = refs
    else:
        q_ref, k_ref, v_ref, o_ref, s_a, s_b = refs
    i = pl.program_id(2)
    top_rows = pl.ds(0, tk)
    bot_rows = pl.ds(tk, tk)

    def kv_rows(u):
        return pl.ds(pl.multiple_of(u * tk, tk), tk)

    def scores(u):
        return _dot_nt(q_ref[0, 0], k_ref[0, 0, kv_rows(u), :])

    lane = lax.broadcasted_iota(jnp.int32, (tk, LANES), 1)
    ones_col = jnp.where(lane == 0, 1.0, 0.0).astype(BF16)

    def values(u):
        return jnp.concatenate([v_ref[0, 0, kv_rows(u), :], ones_col], axis=1)

    def update(state, s, v, masked):
        m, acc = state
        if masked:
            row = lax.broadcasted_iota(jnp.int32, (tk, tk), 0)
            col = lax.broadcasted_iota(jnp.int32, (tk, tk), 1)
            s = jnp.where(col <= row, s, -jnp.inf)
        m_new = jnp.maximum(m, jnp.max(_fold_lanes(s, jnp.maximum), axis=-1, keepdims=True))
        alpha = jnp.exp2(m - m_new)
        p = jnp.exp2((s - m_new).astype(BF16))
        acc = alpha * acc + _dot(p, v)
        return m_new, acc

    def both(top, bot, s_ref, u):
        v = values(u)
        return (update(top, s_ref[top_rows, :], v, False),
                update(bot, s_ref[bot_rows, :], v, False))

    s_a[...] = scores(0)
    if with_cast:
        wbf_ref[...] = w_ref[...].astype(BF16)

    def body(t, state):
        top, bot = state
        s_b[...] = scores(2 * t + 1)
        top, bot = both(top, bot, s_a, 2 * t)
        s_a[...] = scores(2 * t + 2)
        top, bot = both(top, bot, s_b, 2 * t + 1)
        return top, bot

    init = (jnp.full((tk, 1), -jnp.inf, F32), jnp.zeros((tk, 2 * LANES), F32))
    top, bot = lax.fori_loop(0, i, body, (init, init))
    s_b[bot_rows, :] = _dot_nt(q_ref[0, 0, bot_rows, :], k_ref[0, 0, kv_rows(2 * i + 1), :])
    v0 = values(2 * i)
    top = update(top, s_a[top_rows, :], v0, True)
    bot = update(bot, s_a[bot_rows, :], v0, False)
    bot = update(bot, s_b[bot_rows, :], values(2 * i + 1), True)
    for rows, (_, acc) in ((top_rows, top), (bot_rows, bot)):
        o_ref[0, rows, :] = (acc[:, :V_DIM] / acc[:, V_DIM:V_DIM + 1]).astype(BF16)


def _attention(q, k, v, w_cast=None):
    B, H, S, _ = q.shape
    tq = min(S, 1024)
    tk = tq // 2
    nq = S // tq
    in_specs = [pl.BlockSpec((1, 1, tq, QK_PAD), lambda b, h, i: (b, h, i, 0)),
                pl.BlockSpec((1, 1, S, QK_PAD), lambda b, h, i: (b, h, 0, 0)),
                pl.BlockSpec((1, 1, S, V_DIM), lambda b, h, i: (b, h, 0, 0))]
    out_specs = [pl.BlockSpec((1, tq, V_DIM), lambda b, h, i: (b, i, h))]
    out_shape = [jax.ShapeDtypeStruct((B, S, H * V_DIM), BF16)]
    args = [q, k, v]
    if w_cast is not None:
        slab = w_cast.shape[0] // (B * H * nq)
        spec = pl.BlockSpec((slab, w_cast.shape[1]), lambda b, h, i: ((b * H + h) * nq + i, 0))
        in_specs.append(spec)
        out_specs.append(spec)
        out_shape.append(jax.ShapeDtypeStruct(w_cast.shape, BF16))
        args.append(w_cast)
    outs = pl.pallas_call(
        functools.partial(_attn_kernel, tk=tk, with_cast=w_cast is not None),
        grid=(B, H, nq),
        in_specs=in_specs,
        out_specs=out_specs,
        out_shape=out_shape,
        scratch_shapes=[pltpu.VMEM((tq, tk), F32), pltpu.VMEM((tq, tk), F32)],
        compiler_params=_params(("parallel", "parallel", "arbitrary")),
        name="mla_attention",
    )(*args)
    return outs if w_cast is not None else outs[0]


def _cast_slab(shape, steps):
    rows, cols = shape
    slab = rows // steps
    fits = rows % steps == 0 and slab % 16 == 0 and slab * cols * 4 <= CAST_SLAB_BYTES
    return slab if fits else 0


def _attention_and_cast(q, k, v, w):
    B, H, S, _ = q.shape
    if _cast_slab(w.shape, B * H * (S // min(S, 1024))):
        return _attention(q, k, v, w)
    return _attention(q, k, v), w.astype(BF16)


def _hgrn_consts(C):
    levels = [C >> (i + 1) for i in range(int(math.log2(C)))]
    t = np.arange(C)
    u = np.arange(C)
    mats = [(u[None, :] <= t[:, None]),
            (u[None, :] > t[:, None])]
    pair_masks = [np.eye(C, dtype=bool)]
    for m in levels:
        ref = (t // (2 * m)) * (2 * m) + m - 1
        hi = ((t // m) % 2) == 1
        rng_hi = (u[None, :] > ref[:, None]) & (u[None, :] <= t[:, None])
        rng_lo = (u[None, :] > t[:, None]) & (u[None, :] <= ref[:, None])
        mats.append(np.where(hi[:, None], rng_hi, rng_lo))
        same = (t[:, None] // (2 * m)) == (t[None, :] // (2 * m))
        pair_masks.append(hi[:, None] & (~hi[None, :]) & same)
    stack = np.concatenate(mats, axis=0).astype(np.float32)
    sums = jnp.asarray(np.concatenate([stack] * 3, axis=1), dtype=BF16)
    heads = np.eye(HGRN_HEADS_PER_STEP, dtype=bool)
    stacked = np.stack([np.kron(heads, m) for m in pair_masks])
    return sums, jnp.asarray(stacked.astype(np.float32))


def _hgrn_kernel(*refs, layer, ts, n_cast):
    (zq_ref, zf_ref, zi_ref, zg_ref, lbl_ref, on_ref, sums_ref, pm_ref) = refs[:8]
    cast_src = refs[8:8 + n_cast]
    o_ref = refs[8 + n_cast]
    cast_dst = refs[9 + n_cast:9 + 2 * n_cast]
    st_scr = refs[9 + 2 * n_cast]
    C = HGRN_CHUNK
    n_levels = pm_ref.shape[0] - 1
    for src, dst in zip(cast_src, cast_dst):
        dst[...] = src[...].astype(BF16)

    @pl.when(pl.program_id(2) == 0)
    def _():
        st_scr[...] = jnp.zeros_like(st_scr)

    lg = lbl_ref[...]
    ex = jnp.exp(lg - jnp.max(lg, axis=0, keepdims=True))
    pr = ex / jnp.sum(ex, axis=0, keepdims=True)
    cum = pr[0:1, :]
    for r in range(1, layer + 1):
        cum = cum + pr[r:r + 1, :]
    lb = cum - pr[0:1, :]
    lb_floor = jnp.maximum(lb, LB_FLOOR)
    one_m_lb = 1.0 - lb
    gain = on_ref[...]

    for c in range(ts // C):
        rows = pl.ds(c * C, C)
        q_raw = zq_ref[rows, :]
        zf = zf_ref[rows, :]
        g_raw = zg_ref[rows, :]
        v_all = zi_ref[rows, :].astype(BF16)
        q_all = q_raw * _sigmoid(q_raw)
        e = jnp.exp(-jnp.abs(zf))
        r = 1.0 / (1.0 + e)
        er = e * r
        pos = zf >= 0
        log_f = jnp.log(lb_floor + one_m_lb * jnp.where(pos, r, er))
        kk_all = one_m_lb * jnp.where(pos, er, r)
        gate_all = g_raw * _sigmoid(g_raw)
        p0 = log_f.astype(BF16)
        r1 = log_f - p0.astype(F32)
        p1 = r1.astype(BF16)
        p2 = (r1 - p1.astype(F32)).astype(BF16)
        sums_all = _dot(sums_ref[...], jnp.concatenate([p0, p1, p2], axis=0))

        def stack(x):
            return jnp.concatenate([x[:, h * LANES:(h + 1) * LANES]
                                    for h in range(HGRN_HEADS_PER_STEP)], axis=0)

        q_st = stack(q_all).astype(BF16)
        kk_st = stack(kk_all).astype(BF16)
        a = pm_ref[0] * _dot_nt(q_st, kk_st)
        for lv in range(n_levels):
            dec = jnp.exp(stack(sums_all[(2 + lv) * C:(3 + lv) * C])).astype(BF16)
            a = a + pm_ref[lv + 1] * _dot_nt(q_st * dec, kk_st * dec)
        o_intra = _dot(a.astype(BF16), stack(v_all))
        b_all = sums_all[0:C]
        st_all = st_scr[...]
        o_inter = _dot_nt(stack(q_all * jnp.exp(b_all)).astype(BF16), st_all.astype(BF16))
        k_end = (kk_all * jnp.exp(sums_all[C:2 * C])).astype(BF16)
        decay = jnp.exp(b_all[C - 1:C, :])
        for hh in range(HGRN_HEADS_PER_STEP):
            lanes = slice(hh * LANES, (hh + 1) * LANES)
            srows = slice(hh * HGRN_V, (hh + 1) * HGRN_V)
            o = o_intra[hh * C:(hh + 1) * C] + o_inter[hh * C:(hh + 1) * C, lanes]
            st_scr[srows, :] = st_all[srows] * decay[:, lanes] \
                + _dot_tn(v_all[:, lanes], k_end[:, lanes])
            o_ref[rows, pl.ds(hh * LANES, LANES)] = \
                (_rms(o, gain) * gate_all[:, lanes]).astype(BF16)


def _hgrn(z, lb_logits, out_norm, consts, layer, B, S, casts=()):
    T = B * S
    ts = min(S, 512)
    ns = S // ts
    sums, pair_masks = consts
    hp = HGRN_HEADS_PER_STEP
    width = hp * LANES
    col0 = HGRN_COL0 // width
    groups = N_HEADS // hp
    steps = B * groups * ns
    step = lambda b, h, i: (b * groups + h) * ns + i
    riding = [(w, idx) for w, idx in casts if _cast_slab(w.shape[1:], steps)]
    cast_in = [pl.BlockSpec((None, _cast_slab(w.shape[1:], steps), w.shape[2]),
                            lambda b, h, i, idx=idx: (idx, step(b, h, i), 0)) for w, idx in riding]
    cast_out = [pl.BlockSpec((_cast_slab(w.shape[1:], steps), w.shape[2]),
                             lambda b, h, i: (step(b, h, i), 0)) for w, _ in riding]

    def zspec(part):
        return pl.BlockSpec((ts, width),
                            lambda b, h, i: (b * ns + i, col0 + part * groups + h))

    full = lambda arr: pl.BlockSpec(arr.shape, lambda b, h, i: (0,) * arr.ndim)
    kern = functools.partial(_hgrn_kernel, layer=layer, ts=ts, n_cast=len(riding))
    outs = pl.pallas_call(
        kern,
        grid=(B, groups, ns),
        in_specs=[zspec(0), zspec(1), zspec(2), zspec(3),
                  pl.BlockSpec((DEPTH, width), lambda b, h, i: (0, h)),
                  pl.BlockSpec((1, HGRN_V), lambda b, h, i: (0, 0)),
                  full(sums), full(pair_masks)] + cast_in,
        out_specs=[pl.BlockSpec((ts, width), lambda b, h, i: (b * ns + i, h))] + cast_out,
        out_shape=[jax.ShapeDtypeStruct((T, GROUP_WIDTH), BF16)]
        + [jax.ShapeDtypeStruct(w.shape[1:], BF16) for w, _ in riding],
        scratch_shapes=[pltpu.VMEM((hp * HGRN_V, HGRN_K), F32)],
        compiler_params=_params(("parallel", "parallel", "arbitrary")),
        name="hgrn2",
    )(z, z, z, z, lb_logits, out_norm[None, :], sums, pair_masks, *[w for w, _ in riding])
    copies = iter(outs[1:])
    return [outs[0]] + [next(copies) if _cast_slab(w.shape[1:], steps) else w[idx].astype(BF16)
                        for w, idx in casts]


def _route_tile(c_bf, wr_ref, tri_ref, route_ref, cnt_ref, carry):
    @pl.when(pl.program_id(0) == 0)
    def _():
        carry[...] = jnp.zeros_like(carry)

    logits = _dot(c_bf, wr_ref[...])
    lane = lax.broadcasted_iota(jnp.int32, logits.shape, 1)
    lg = jnp.where(lane < N_EXPERTS, logits, -jnp.inf)
    m1 = jnp.max(lg, axis=-1, keepdims=True)
    i1 = jnp.min(jnp.where(lg == m1, lane, LANES), axis=-1, keepdims=True)
    lg2 = jnp.where(lane == i1, -jnp.inf, lg)
    m2 = jnp.max(lg2, axis=-1, keepdims=True)
    i2 = jnp.min(jnp.where(lg2 == m2, lane, LANES), axis=-1, keepdims=True)
    e = jnp.exp(m2 - m1)
    g1 = 1.0 / (1.0 + e)
    g2 = e / (1.0 + e)
    onehot = jnp.where((lane == i1) | (lane == i2), 1.0, 0.0)
    before = _dot(tri_ref[...], onehot.astype(BF16)) + carry[0:1, :]
    rank1 = jnp.sum(jnp.where(lane == i1, before, 0.0), axis=-1, keepdims=True)
    rank2 = jnp.sum(jnp.where(lane == i2, before, 0.0), axis=-1, keepdims=True)
    total = carry[0:1, :] + jnp.sum(onehot, axis=0, keepdims=True)
    carry[...] = jnp.broadcast_to(total, carry.shape)
    cnt_ref[...] = jnp.broadcast_to(total, cnt_ref.shape)
    out = jnp.where(lane == 0, i1.astype(F32), 0.0)
    out = jnp.where(lane == 1, i2.astype(F32), out)
    out = jnp.where(lane == 2, g1, out)
    out = jnp.where(lane == 3, g2, out)
    out = jnp.where(lane == 4, rank1, out)
    out = jnp.where(lane == 5, rank2, out)
    route_ref[...] = out


def _out_proj_kernel(om_ref, oh_ref, h_ref, w_ref, g1_ref, g2_ref, *rest, route):
    if route:
        wr_ref, tri_ref, h1_ref, c_ref, route_ref, cnt_ref, carry = rest
    else:
        h1_ref, c_ref = rest
    y = _dot(om_ref[...], w_ref[0:GROUP_WIDTH, :]) + _dot(oh_ref[...], w_ref[GROUP_WIDTH:, :])
    h1 = h_ref[...] + _rms(y, g1_ref[...])
    h1_ref[...] = h1
    c = _rms(h1, g2_ref[...])
    c_ref[...] = c.astype(c_ref.dtype)
    if route:
        _route_tile(c.astype(BF16), wr_ref, tri_ref, route_ref, cnt_ref, carry)


def _out_proj(om, oh, h, w, g1, g2, c_dtype, w_router=None):
    T = h.shape[0]
    tm = min(T, 512)
    route = w_router is not None
    row = lambda i: (i, 0)
    const = lambda i: (0, 0)
    in_specs = [pl.BlockSpec((tm, GROUP_WIDTH), row),
                pl.BlockSpec((tm, GROUP_WIDTH), row),
                pl.BlockSpec((tm, D_MODEL), row),
                pl.BlockSpec((D_MODEL, D_MODEL), const, pipeline_mode=pl.Buffered(1)),
                pl.BlockSpec((1, D_MODEL), const),
                pl.BlockSpec((1, D_MODEL), const)]
    out_specs = [pl.BlockSpec((tm, D_MODEL), row), pl.BlockSpec((tm, D_MODEL), row)]
    out_shape = [jax.ShapeDtypeStruct((T, D_MODEL), F32),
                 jax.ShapeDtypeStruct((T, D_MODEL), c_dtype)]
    args = [om, oh, h, w, g1[None, :], g2[None, :]]
    scratch = []
    if route:
        wr = jnp.zeros((D_MODEL, LANES), BF16).at[:, :N_EXPERTS].set(w_router.astype(BF16))
        tri = jnp.asarray(np.tril(np.ones((tm, tm), np.float32), -1), dtype=BF16)
        in_specs += [pl.BlockSpec((D_MODEL, LANES), const), pl.BlockSpec((tm, tm), const)]
        out_specs += [pl.BlockSpec((tm, LANES), row), pl.BlockSpec((8, LANES), const)]
        out_shape += [jax.ShapeDtypeStruct((T, LANES), F32),
                      jax.ShapeDtypeStruct((8, LANES), F32)]
        args += [wr, tri]
        scratch = [pltpu.VMEM((8, LANES), F32)]
    return pl.pallas_call(
        functools.partial(_out_proj_kernel, route=route),
        grid=(T // tm,),
        in_specs=in_specs,
        out_specs=out_specs,
        out_shape=out_shape,
        scratch_shapes=scratch,
        compiler_params=_params(("arbitrary",) if route else ("parallel",)),
        name="out_proj",
    )(*args)


def _gate_up_kernel(te_ref, nv_ref, x_ref, wg_ref, wu_ref, o_ref, wg_bf, wu_bf):
    i = pl.program_id(1)
    valid = i < nv_ref[0]
    new_weights = jnp.logical_or(i == 0, te_ref[i] != te_ref[jnp.maximum(i - 1, 0)])

    def step(refresh):
        x = x_ref[...].astype(BF16)
        for c in range(0, o_ref.shape[1], GU_SUB):
            cols = slice(c, c + GU_SUB)
            if refresh:
                wg_bf[:, cols] = wg_ref[0, :, cols].astype(BF16)
                wu_bf[:, cols] = wu_ref[0, :, cols].astype(BF16)
            g = _dot(x, wg_bf[:, cols])
            u = _dot(x, wu_bf[:, cols])
            o_ref[:, cols] = (g * _sigmoid(g) * u).astype(BF16)

    pl.when(jnp.logical_and(valid, new_weights))(functools.partial(step, True))
    pl.when(jnp.logical_and(valid, jnp.logical_not(new_weights)))(functools.partial(step, False))

    @pl.when(jnp.logical_not(valid))
    def _():
        o_ref[...] = jnp.zeros_like(o_ref)


def _gate_up(x, w_gu, tile_expert, n_valid, tm, tf):
    R, xw = x.shape
    n_tiles = R // tm
    nf = FFN_DIM // tf

    def tile(i, nv):
        return jnp.minimum(i, nv[0] - 1)

    return pl.pallas_call(
        _gate_up_kernel,
        grid_spec=pltpu.PrefetchScalarGridSpec(
            num_scalar_prefetch=2,
            grid=(nf, n_tiles),
            in_specs=[pl.BlockSpec((tm, xw), lambda j, i, te, nv: (tile(i, nv), 0)),
                      pl.BlockSpec((1, D_MODEL, tf), lambda j, i, te, nv: (te[tile(i, nv)], 0, j)),
                      pl.BlockSpec((1, D_MODEL, tf),
                                   lambda j, i, te, nv: (te[tile(i, nv)], 0, j + nf))],
            out_specs=pl.BlockSpec((tm, tf), lambda j, i, te, nv: (i, j)),
            scratch_shapes=[pltpu.VMEM((D_MODEL, tf), BF16), pltpu.VMEM((D_MODEL, tf), BF16)],
        ),
        out_shape=jax.ShapeDtypeStruct((R, FFN_DIM), BF16),
        compiler_params=_params(("arbitrary", "arbitrary")),
        name="ffn_gate_up",
    )(tile_expert, n_valid, x, w_gu, w_gu)


def _down_kernel(te_ref, nv_ref, a_ref, w_ref, o_ref):
    valid = pl.program_id(1) < nv_ref[0]

    @pl.when(valid)
    def _():
        a = a_ref[...]
        for c in range(0, o_ref.shape[1], GU_SUB):
            o_ref[:, c:c + GU_SUB] = _dot(a, w_ref[0, :, c:c + GU_SUB])

    @pl.when(jnp.logical_not(valid))
    def _():
        o_ref[...] = jnp.zeros_like(o_ref)


def _down(act, w_down, tile_expert, n_valid, tm):
    R = act.shape[0]
    n_tiles = R // tm
    tn = 1024

    def tile(i, nv):
        return jnp.minimum(i, nv[0] - 1)

    return pl.pallas_call(
        _down_kernel,
        grid_spec=pltpu.PrefetchScalarGridSpec(
            num_scalar_prefetch=2,
            grid=(D_MODEL // tn, n_tiles),
            in_specs=[pl.BlockSpec((tm, FFN_DIM), lambda j, i, te, nv: (tile(i, nv), 0)),
                      pl.BlockSpec((1, FFN_DIM, tn), lambda j, i, te, nv: (te[tile(i, nv)], 0, j))],
            out_specs=pl.BlockSpec((tm, tn), lambda j, i, te, nv: (i, j)),
        ),
        out_shape=jax.ShapeDtypeStruct((R, D_MODEL), F32),
        compiler_params=_params(("arbitrary", "arbitrary")),
        name="ffn_down",
    )(tile_expert, n_valid, act, w_down)


def _ffn_epilogue(f, h1, p, wg, wp, g3, gp):
    h2 = h1 + _rms(f, g3)
    gate = _sigmoid(_dot(h2.astype(BF16), wg))
    proj = _dot(p.astype(BF16), wp)
    return h2 + _rms(gate * proj, gp)


def _post_ffn_kernel(f_ref, h1_ref, p_ref, wg_ref, wp_ref, g3_ref, gp_ref, o_ref):
    o_ref[...] = _ffn_epilogue(f_ref[...], h1_ref[...], p_ref[...], wg_ref[...], wp_ref[...],
                               g3_ref[...], gp_ref[...])


def _post_ffn(f, h1, p, layer, wg, wp, g3, gp):
    T = h1.shape[0]
    tm = min(T, 512)
    row = lambda i: (i, 0)
    const = lambda i: (0, 0)
    return pl.pallas_call(
        _post_ffn_kernel,
        grid=(T // tm,),
        in_specs=[pl.BlockSpec((tm, D_MODEL), row),
                  pl.BlockSpec((tm, D_MODEL), row),
                  pl.BlockSpec((None, tm, PLE_DIM), lambda i: (layer, i, 0)),
                  pl.BlockSpec((D_MODEL, D_MODEL), const, pipeline_mode=pl.Buffered(1)),
                  pl.BlockSpec((PLE_DIM, D_MODEL), const, pipeline_mode=pl.Buffered(1)),
                  pl.BlockSpec((1, D_MODEL), const),
                  pl.BlockSpec((1, D_MODEL), const)],
        out_specs=pl.BlockSpec((tm, D_MODEL), row),
        out_shape=jax.ShapeDtypeStruct((T, D_MODEL), F32),
        compiler_params=_params(("parallel",)),
        name="post_ffn",
    )(f, h1, p, wg, wp, g3[None, :], gp[None, :])


def _dispatch_kernel(p1_ref, p2_ref, pad_ref, nv_ref, c_ref, xs_ref, zbuf, sem, zsem, *,
                     tm, seg_tile):
    i = pl.program_id(0)
    base = i * tm
    zr = zbuf.shape[0]

    @pl.when(i == 0)
    def _():
        zbuf[...] = jnp.zeros_like(zbuf)

        def zero_tile(start):
            first = pl.multiple_of(start, zr)
            copies = [pltpu.make_async_copy(zbuf, xs_ref.at[pl.ds(first + k * zr, zr)], zsem)
                      for k in range(seg_tile // zr)]
            for cp in copies:
                cp.start()
            for cp in copies:
                cp.wait()

        def zero_padding(e, _):
            @pl.when(pad_ref[e] >= 0)
            def _():
                zero_tile(pad_ref[e])
            return 0

        def zero_unused(t, _):
            zero_tile(t * seg_tile)
            return 0

        lax.fori_loop(0, N_EXPERTS, zero_padding, 0)
        lax.fori_loop(nv_ref[0], xs_ref.shape[0] // seg_tile, zero_unused, 0)

    def row_copy(r, dst):
        return pltpu.make_async_copy(c_ref.at[pl.ds(r, 1)], xs_ref.at[pl.ds(dst, 1)], sem)

    for r in range(tm):
        row_copy(r, p1_ref[base + r]).start(priority=0)
        row_copy(r, p2_ref[base + r]).start(priority=1)
    for _r in range(2 * tm):
        row_copy(0, 0).wait()


def _dispatch(c, pos1, pos2, pad_tile_row, n_valid, n_rows, seg_tile):
    T, width = c.shape
    tm = min(T, 256)
    return pl.pallas_call(
        functools.partial(_dispatch_kernel, tm=tm, seg_tile=seg_tile),
        grid_spec=pltpu.PrefetchScalarGridSpec(
            num_scalar_prefetch=4,
            grid=(T // tm,),
            in_specs=[pl.BlockSpec((tm, width), lambda i, p1, p2, pt, nv: (i, 0))],
            out_specs=pl.BlockSpec(memory_space=pl.ANY),
            scratch_shapes=[pltpu.VMEM((min(seg_tile, 64), width), c.dtype),
                            pltpu.SemaphoreType.DMA(()),
                            pltpu.SemaphoreType.DMA(())],
        ),
        out_shape=jax.ShapeDtypeStruct((n_rows, width), c.dtype),
        compiler_params=_params(("arbitrary",)),
        name="moe_dispatch",
    )(pos1, pos2, pad_tile_row, n_valid, c)


def _combine_kernel(p1_ref, p2_ref, ys_ref, route_ref, h1_ref, p_ref, wg_ref, wp_ref,
                    g3_ref, gp_ref, o_ref, buf, sems, *, tm):
    i = pl.program_id(0)
    slot = i % 2

    def row_copy(src, s, k, r):
        return pltpu.make_async_copy(ys_ref.at[pl.ds(src, 1)], buf.at[s, k, pl.ds(r, 1)],
                                     sems.at[s])

    def gather_tile(tile, s):
        base = tile * tm
        for r in range(tm):
            row_copy(p1_ref[base + r], s, 0, r).start(priority=0)
            row_copy(p2_ref[base + r], s, 1, r).start(priority=1)

    @pl.when(i == 0)
    def _():
        gather_tile(0, 0)

    @pl.when(i + 1 < pl.num_programs(0))
    def _():
        gather_tile(i + 1, 1 - slot)

    for _r in range(2 * tm):
        row_copy(0, slot, 0, 0).wait()
    route = route_ref[...]
    f = route[:, 2:3] * buf[slot, 0] + route[:, 3:4] * buf[slot, 1]
    o_ref[...] = _ffn_epilogue(f, h1_ref[...], p_ref[...], wg_ref[...], wp_ref[...],
                               g3_ref[...], gp_ref[...])


def _combine(ys, pos1, pos2, route, h1, p, layer, wg, wp, g3, gp):
    T = h1.shape[0]
    tm = min(T, 256)
    row = lambda i, p1, p2: (i, 0)
    const = lambda i, p1, p2: (0, 0)
    return pl.pallas_call(
        functools.partial(_combine_kernel, tm=tm),
        grid_spec=pltpu.PrefetchScalarGridSpec(
            num_scalar_prefetch=2,
            grid=(T // tm,),
            in_specs=[pl.BlockSpec(memory_space=pl.ANY),
                      pl.BlockSpec((tm, LANES), row),
                      pl.BlockSpec((tm, D_MODEL), row),
                      pl.BlockSpec((None, tm, PLE_DIM), lambda i, p1, p2: (layer, i, 0)),
                      pl.BlockSpec((D_MODEL, D_MODEL), const, pipeline_mode=pl.Buffered(1)),
                      pl.BlockSpec((PLE_DIM, D_MODEL), const, pipeline_mode=pl.Buffered(1)),
                      pl.BlockSpec((1, D_MODEL), const),
                      pl.BlockSpec((1, D_MODEL), const)],
            out_specs=pl.BlockSpec((tm, D_MODEL), row),
            scratch_shapes=[pltpu.VMEM((2, 2, tm, D_MODEL), F32),
                            pltpu.SemaphoreType.DMA((2,))],
        ),
        out_shape=jax.ShapeDtypeStruct((T, D_MODEL), F32),
        compiler_params=_params(("arbitrary",)),
        name="moe_combine",
    )(pos1, pos2, ys, route, h1, p, wg, wp, g3[None, :], gp[None, :])


def _swap_halves(w):
    half = w.shape[-1] // 2
    return jnp.concatenate([w[..., half:], w[..., :half]], axis=-1)


def _prep_w_in(w):
    w = w.astype(BF16)
    k_rope = w[:, 640:704]
    pad64 = jnp.zeros((D_MODEL, 64), w.dtype)
    cols = [w[:, :640], k_rope, pad64, _swap_halves(k_rope), pad64,
            jnp.zeros((D_MODEL, HGRN_COL0 - 896), w.dtype), w[:, 704:]]
    return jnp.concatenate(cols, axis=1).astype(BF16)


def _prep_w_uq(w):
    w = w.reshape(Q_RANK, N_HEADS, QK_DIM)
    nope = w[:, :, :NOPE_DIM]
    pe = w[:, :, NOPE_DIM:]
    pad = jnp.zeros((Q_RANK, N_HEADS, LANES - ROPE_DIM), w.dtype)
    pe_pad = jnp.concatenate([pe, pad], axis=-1)
    pe_swap = jnp.concatenate([_swap_halves(pe), pad], axis=-1)
    parts = [x.reshape(Q_RANK, GROUP_WIDTH) for x in (nope, pe_pad, pe_swap)]
    return jnp.concatenate(parts, axis=1).astype(BF16)


def _prep_w_ukv(w):
    w = w.reshape(KV_RANK, N_HEADS, NOPE_DIM + V_DIM)
    k = w[:, :, :NOPE_DIM].reshape(KV_RANK, GROUP_WIDTH)
    v = w[:, :, NOPE_DIM:].reshape(KV_RANK, GROUP_WIDTH)
    return jnp.concatenate([k, v], axis=1).astype(BF16)


def _moe_plan(route, counts, tm, n_tiles):
    e1 = route[:, 0].astype(jnp.int32)
    e2 = route[:, 1].astype(jnp.int32)
    rank1 = route[:, 4].astype(jnp.int32)
    rank2 = route[:, 5].astype(jnp.int32)
    cnt = counts[0, :N_EXPERTS].astype(jnp.int32)
    padded = ((cnt + tm - 1) // tm) * tm
    seg_end = jnp.cumsum(padded)
    seg_start = seg_end - padded
    pos1 = seg_start[e1] + rank1
    pos2 = seg_start[e2] + rank2
    tile_start = jnp.arange(n_tiles, dtype=jnp.int32) * tm
    tile_expert = jnp.sum(tile_start[:, None] >= seg_end[None, :], axis=1).astype(jnp.int32)
    tile_expert = jnp.minimum(tile_expert, N_EXPERTS - 1)
    n_valid = (seg_end[-1] // tm).astype(jnp.int32)[None]
    pad_tile_row = jnp.where(padded > 0, seg_end - tm, -1).astype(jnp.int32)
    return pos1, pos2, tile_expert, n_valid, pad_tile_row


def kernel(x, p, positions, sandwich_norms, w_in, mla_q_norm, mla_kv_norm, w_uq, w_ukv,
           hgrn_lb_logits, hgrn_out_norm, w_out, ffn_w_gu, ffn_w_down, moe_w_router,
           moe_w_gu, moe_w_down, ple_w_proj, ple_w_gate, ple_norm):
    B, S, _ = x.shape
    T = B * S
    h = x.reshape(T, D_MODEL)
    ct, st = _rope_tables(positions)
    hgrn_consts = _hgrn_consts(HGRN_CHUNK)
    tm = min(T, 512)
    for l in range(DEPTH):
        z = _norm_matmul(h, sandwich_norms[l, 0], _prep_w_in(w_in[l]))
        q, k, v = _mla_proj(z, ct, st, mla_q_norm[l], mla_kv_norm[l],
                            _prep_w_uq(w_uq[l]), _prep_w_ukv(w_ukv[l]), B, S)
        moe = l % 2 == 1
        if moe:
            w_down = moe_w_down[l // 2]
            o_mla, w_down_bf = _attention_and_cast(q, k, v, w_down.reshape(-1, D_MODEL))
            w_down_bf = w_down_bf.reshape(w_down.shape)
        else:
            o_mla = _attention(q, k, v)
        o_mla = o_mla.reshape(T, GROUP_WIDTH)
        later = [(w_out, l), (ple_w_gate, l)] + ([] if moe else [(ffn_w_down, l // 2)])
        o_hgrn, w_out_bf, wg, *rest = _hgrn(z, hgrn_lb_logits, hgrn_out_norm[l], hgrn_consts,
                                            l, B, S, casts=later)
        h1, c, *routing = _out_proj(o_mla, o_hgrn, h, w_out_bf,
                                    sandwich_norms[l, 1], sandwich_norms[l, 2],
                                    F32 if moe else BF16,
                                    moe_w_router[l // 2] if moe else None)
        wp = ple_w_proj[l].astype(BF16)
        p_all = p.reshape(DEPTH, T, PLE_DIM)
        if not moe:
            def one_expert(rows):
                return jnp.zeros((T // rows,), jnp.int32), jnp.full((1,), T // rows, jnp.int32)

            tm_dense = min(T, 1024)
            act = _gate_up(c, ffn_w_gu[l // 2][None], *one_expert(tm_dense), tm_dense, 512)
            f = _down(act, rest[0][None], *one_expert(tm), tm)
            h = _post_ffn(f, h1, p_all, l, wg, wp, sandwich_norms[l, 3], ple_norm[l])
        else:
            n_tiles = (2 * T) // tm + N_EXPERTS
            route, counts = routing
            pos1, pos2, tile_expert, n_valid, pad_tile_row = _moe_plan(route, counts, tm, n_tiles)
            xs = _dispatch(c, pos1, pos2, pad_tile_row, n_valid, n_tiles * tm, tm)
            act = _gate_up(xs, moe_w_gu[l // 2], tile_expert, n_valid, tm, 1024)
            ys = _down(act, w_down_bf, tile_expert, n_valid, tm)
            h = _combine(ys, pos1, pos2, route, h1, p_all, l, wg, wp,
                         sandwich_norms[l, 3], ple_norm[l])
    return h.reshape(B, S, D_MODEL)
```

```python
import functools
import math

import numpy as np
import jax
import jax.numpy as jnp
from jax import lax
from jax.experimental import pallas as pl
from jax.experimental.pallas import tpu as pltpu

F32 = jnp.float32
BF16 = jnp.bfloat16

D_MODEL = 2048
DEPTH = 2
N_HEADS = 8
NOPE_DIM = 128
ROPE_DIM = 64
V_DIM = 128
QK_DIM = NOPE_DIM + ROPE_DIM
Q_RANK = 384
KV_RANK = 256
HGRN_K = 128
HGRN_V = 128
GROUP_WIDTH = N_HEADS * 128
FFN_DIM = 7168
N_EXPERTS = 8
PLE_DIM = 256
ROPE_THETA = 10000.0
LB_FLOOR = 1e-30
EPS = 1e-6

LANES = 128
QK_PAD = 256
Q_SCALE = QK_DIM ** -0.5 * math.log2(math.e)
Z_MLA = 1024
Z_WIDTH = 5120
HGRN_COL0 = 1024
HGRN_CHUNK = 64
HGRN_HEADS_PER_STEP = 4
GU_SUB = 512
IN_PROJ_SUB = 1024
CAST_SLAB_BYTES = 4 * 1024 * 1024
VMEM_LIMIT = 56 * 1024 * 1024


def _params(semantics, vmem=VMEM_LIMIT):
    return pltpu.CompilerParams(dimension_semantics=semantics, vmem_limit_bytes=vmem)


def _rms(x, gain_row):
    ms = jnp.mean(x * x, axis=-1, keepdims=True)
    return x * lax.rsqrt(ms + EPS) * gain_row


def _dot(a, b):
    return jnp.dot(a, b, preferred_element_type=F32)


def _dot_nt(a, b):
    return lax.dot_general(a, b, (((1,), (1,)), ((), ())), preferred_element_type=F32)


def _dot_tn(a, b):
    return lax.dot_general(a, b, (((0,), (0,)), ((), ())), preferred_element_type=F32)


def _sigmoid(x):
    return 1.0 / (1.0 + jnp.exp(-x))


def _fold_lanes(x, op):
    parts = [x[:, c:c + LANES] for c in range(0, x.shape[1], LANES)]
    while len(parts) > 1:
        parts = [op(a, b) for a, b in zip(parts[0::2], parts[1::2])] + parts[len(parts) & ~1:]
    return parts[0]


def _rope_kernel(pos_ref, inv_ref, keep_ref, sgn_ref, ct_ref, st_ref):
    ang = pos_ref[...].astype(F32) * inv_ref[...]
    ct_ref[...] = jnp.cos(ang) * keep_ref[...]
    st_ref[...] = jnp.sin(ang) * sgn_ref[...]


def _rope_tables(positions):
    T = positions.size
    tt = min(T, 1024)
    half = ROPE_DIM // 2
    inv_freq = 1.0 / (ROPE_THETA ** (jnp.arange(0, ROPE_DIM, 2, dtype=F32) / ROPE_DIM))
    inv_row = jnp.concatenate([inv_freq] * (LANES // half))[None, :]
    keep = np.zeros((1, LANES), np.float32)
    keep[0, :ROPE_DIM] = 1.0
    sgn = np.zeros((1, LANES), np.float32)
    sgn[0, ROPE_DIM:ROPE_DIM + half] = -1.0
    sgn[0, ROPE_DIM + half:] = 1.0
    return pl.pallas_call(
        _rope_kernel,
        grid=(T // tt,),
        in_specs=[pl.BlockSpec((tt, 1), lambda i: (i, 0)),
                  pl.BlockSpec((1, LANES), lambda i: (0, 0)),
                  pl.BlockSpec((1, LANES), lambda i: (0, 0)),
                  pl.BlockSpec((1, LANES), lambda i: (0, 0))],
        out_specs=[pl.BlockSpec((tt, LANES), lambda i: (i, 0))] * 2,
        out_shape=[jax.ShapeDtypeStruct((T, LANES), F32)] * 2,
        compiler_params=_params(("parallel",)),
        name="rope_tables",
    )(positions.reshape(T, 1), inv_row, jnp.asarray(keep), jnp.asarray(sgn))


def _norm_matmul_kernel(x_ref, g_ref, w_ref, o_ref):
    a = _rms(x_ref[...], g_ref[...]).astype(BF16)
    for c in range(0, o_ref.shape[1], IN_PROJ_SUB):
        o_ref[:, c:c + IN_PROJ_SUB] = _dot(a, w_ref[:, c:c + IN_PROJ_SUB])


def _norm_matmul(x, gain, w):
    T, K = x.shape
    N = w.shape[1]
    tm = min(T, 512)
    return pl.pallas_call(
        _norm_matmul_kernel,
        grid=(T // tm,),
        in_specs=[pl.BlockSpec((tm, K), lambda i: (i, 0)),
                  pl.BlockSpec((1, K), lambda i: (0, 0)),
                  pl.BlockSpec((K, N), lambda i: (0, 0), pipeline_mode=pl.Buffered(1))],
        out_specs=pl.BlockSpec((tm, N), lambda i: (i, 0)),
        out_shape=jax.ShapeDtypeStruct((T, N), F32),
        compiler_params=_params(("parallel",)),
        name="in_proj",
    )(x, gain[None, :], w)


def _mla_proj_kernel(z_ref, ct_ref, st_ref, qn_ref, kvn_ref, wq_ref, wkv_ref,
                     q_ref, k_ref, v_ref):
    z = z_ref[...]
    cs = ct_ref[...] + st_ref[...]

    def rope(x_and_swapped):
        y = x_and_swapped * cs
        return y + pltpu.roll(y, ROPE_DIM, 1)

    aq = _rms(z[:, :Q_RANK], qn_ref[...]).astype(BF16)
    akv = _rms(z[:, Q_RANK:Q_RANK + KV_RANK], kvn_ref[...]).astype(BF16)
    lane = lax.broadcasted_iota(jnp.int32, cs.shape, 1)
    k_pe = jnp.where(lane < ROPE_DIM, rope(z[:, 640:768]), 0.0).astype(BF16)
    q = _dot(aq, wq_ref[...])
    kv = _dot(akv, wkv_ref[...])
    for h in range(N_HEADS):
        lo, hi = h * LANES, (h + 1) * LANES
        q_ref[0, h, :, 0:LANES] = (q[:, 2 * lo:2 * lo + LANES] * Q_SCALE).astype(BF16)
        q_pe = rope(q[:, 2 * lo + LANES:2 * hi])
        q_ref[0, h, :, LANES:QK_PAD] = (q_pe * Q_SCALE).astype(BF16)
        k_ref[0, h, :, 0:LANES] = kv[:, lo:hi].astype(BF16)
        k_ref[0, h, :, LANES:QK_PAD] = k_pe
        v_ref[0, h, :, :] = kv[:, GROUP_WIDTH + lo:GROUP_WIDTH + hi].astype(BF16)


def _mla_proj(z, ct, st, q_norm, kv_norm, wq, wkv, B, S):
    ts = min(S, 512)
    ns = S // ts
    tok = lambda b, i: (b * ns + i, 0)
    const = lambda b, i: (0, 0)
    head_out = lambda b, i: (b, 0, i, 0)
    return pl.pallas_call(
        _mla_proj_kernel,
        grid=(B, ns),
        in_specs=[pl.BlockSpec((ts, Z_MLA), tok),
                  pl.BlockSpec((ts, LANES), tok),
                  pl.BlockSpec((ts, LANES), tok),
                  pl.BlockSpec((1, Q_RANK), const),
                  pl.BlockSpec((1, KV_RANK), const),
                  pl.BlockSpec(wq.shape, const),
                  pl.BlockSpec(wkv.shape, const)],
        out_specs=[pl.BlockSpec((1, N_HEADS, ts, QK_PAD), head_out),
                   pl.BlockSpec((1, N_HEADS, ts, QK_PAD), head_out),
                   pl.BlockSpec((1, N_HEADS, ts, V_DIM), head_out)],
        out_shape=[jax.ShapeDtypeStruct((B, N_HEADS, S, QK_PAD), BF16),
                   jax.ShapeDtypeStruct((B, N_HEADS, S, QK_PAD), BF16),
                   jax.ShapeDtypeStruct((B, N_HEADS, S, V_DIM), BF16)],
        compiler_params=_params(("parallel", "parallel")),
        name="mla_proj",
    )(z, ct, st, q_norm[None, :], kv_norm[None, :], wq, wkv)


def _attn_kernel(*refs, tk, with_cast):
    if with_cast:
        q_ref, k_ref, v_ref, w_ref, o_ref, wbf_ref, s_a, s_b = refs
    else:
        q_ref, k_ref, v_ref, o_ref, s_a, s_b = refs
    i = pl.program_id(2)
    top_rows = pl.ds(0, tk)
    bot_rows = pl.ds(tk, tk)

    def kv_rows(u):
        return pl.ds(pl.multiple_of(u * tk, tk), tk)

    def scores(u):
        return _dot_nt(q_ref[0, 0], k_ref[0, 0, kv_rows(u), :])

    lane = lax.broadcasted_iota(jnp.int32, (tk, LANES), 1)
    ones_col = jnp.where(lane == 0, 1.0, 0.0).astype(BF16)

    def values(u):
        return jnp.concatenate([v_ref[0, 0, kv_rows(u), :], ones_col], axis=1)

    def update(state, s, v, masked):
        m, acc = state
        if masked:
            row = lax.broadcasted_iota(jnp.int32, (tk, tk), 0)
            col = lax.broadcasted_iota(jnp.int32, (tk, tk), 1)
            s = jnp.where(col <= row, s, -jnp.inf)
        m_new = jnp.maximum(m, jnp.max(_fold_lanes(s, jnp.maximum), axis=-1, keepdims=True))
        alpha = jnp.exp2(m - m_new)
        p = jnp.exp2((s - m_new).astype(BF16))
        acc = alpha * acc + _dot(p, v)
        return m_new, acc

    def both(top, bot, s_ref, u):
        v = values(u)
        return (update(top, s_ref[top_rows, :], v, False),
                update(bot, s_ref[bot_rows, :], v, False))

    s_a[...] = scores(0)
    if with_cast:
        wbf_ref[...] = w_ref[...].astype(BF16)

    def body(t, state):
        top, bot = state
        s_b[...] = scores(2 * t + 1)
        top, bot = both(top, bot, s_a, 2 * t)
        s_a[...] = scores(2 * t + 2)
        top, bot = both(top, bot, s_b, 2 * t + 1)
        return top, bot

    init = (jnp.full((tk, 1), -jnp.inf, F32), jnp.zeros((tk, 2 * LANES), F32))
    top, bot = lax.fori_loop(0, i, body, (init, init))
    s_b[bot_rows, :] = _dot_nt(q_ref[0, 0, bot_rows, :], k_ref[0, 0, kv_rows(2 * i + 1), :])
    v0 = values(2 * i)
    top = update(top, s_a[top_rows, :], v0, True)
    bot = update(bot, s_a[bot_rows, :], v0, False)
    bot = update(bot, s_b[bot_rows, :], values(2 * i + 1), True)
    for rows, (_, acc) in ((top_rows, top), (bot_rows, bot)):
        o_ref[0, rows, :] = (acc[:, :V_DIM] / acc[:, V_DIM:V_DIM + 1]).astype(BF16)


def _attention(q, k, v, w_cast=None):
    B, H, S, _ = q.shape
    tq = min(S, 1024)
    tk = tq // 2
    nq = S // tq
    in_specs = [pl.BlockSpec((1, 1, tq, QK_PAD), lambda b, h, i: (b, h, i, 0)),
                pl.BlockSpec((1, 1, S, QK_PAD), lambda b, h, i: (b, h, 0, 0)),
                pl.BlockSpec((1, 1, S, V_DIM), lambda b, h, i: (b, h, 0, 0))]
    out_specs = [pl.BlockSpec((1, tq, V_DIM), lambda b, h, i: (b, i, h))]
    out_shape = [jax.ShapeDtypeStruct((B, S, H * V_DIM), BF16)]
    args = [q, k, v]
    if w_cast is not None:
        slab = w_cast.shape[0] // (B * H * nq)
        spec = pl.BlockSpec((slab, w_cast.shape[1]), lambda b, h, i: ((b * H + h) * nq + i, 0))
        in_specs.append(spec)
        out_specs.append(spec)
        out_shape.append(jax.ShapeDtypeStruct(w_cast.shape, BF16))
        args.append(w_cast)
    outs = pl.pallas_call(
        functools.partial(_attn_kernel, tk=tk, with_cast=w_cast is not None),
        grid=(B, H, nq),
        in_specs=in_specs,
        out_specs=out_specs,
        out_shape=out_shape,
        scratch_shapes=[pltpu.VMEM((tq, tk), F32), pltpu.VMEM((tq, tk), F32)],
        compiler_params=_params(("parallel", "parallel", "arbitrary")),
        name="mla_attention",
    )(*args)
    return outs if w_cast is not None else outs[0]


def _cast_slab(shape, steps):
    rows, cols = shape
    slab = rows // steps
    fits = rows % steps == 0 and slab % 16 == 0 and slab * cols * 4 <= CAST_SLAB_BYTES
    return slab if fits else 0


def _attention_and_cast(q, k, v, w):
    B, H, S, _ = q.shape
    if _cast_slab(w.shape, B * H * (S // min(S, 1024))):
        return _attention(q, k, v, w)
    return _attention(q, k, v), w.astype(BF16)


def _hgrn_consts(C):
    levels = [C >> (i + 1) for i in range(int(math.log2(C)))]
    t = np.arange(C)
    u = np.arange(C)
    mats = [(u[None, :] <= t[:, None]),
            (u[None, :] > t[:, None])]
    pair_masks = [np.eye(C, dtype=bool)]
    for m in levels:
        ref = (t // (2 * m)) * (2 * m) + m - 1
        hi = ((t // m) % 2) == 1
        rng_hi = (u[None, :] > ref[:, None]) & (u[None, :] <= t[:, None])
        rng_lo = (u[None, :] > t[:, None]) & (u[None, :] <= ref[:, None])
        mats.append(np.where(hi[:, None], rng_hi, rng_lo))
        same = (t[:, None] // (2 * m)) == (t[None, :] // (2 * m))
        pair_masks.append(hi[:, None] & (~hi[None, :]) & same)
    stack = np.concatenate(mats, axis=0).astype(np.float32)
    sums = jnp.asarray(np.concatenate([stack] * 3, axis=1), dtype=BF16)
    heads = np.eye(HGRN_HEADS_PER_STEP, dtype=bool)
    stacked = np.stack([np.kron(heads, m) for m in pair_masks])
    return sums, jnp.asarray(stacked.astype(np.float32))


def _hgrn_kernel(*refs, layer, ts, n_cast):
    (zq_ref, zf_ref, zi_ref, zg_ref, lbl_ref, on_ref, sums_ref, pm_ref) = refs[:8]
    cast_src = refs[8:8 + n_cast]
    o_ref = refs[8 + n_cast]
    cast_dst = refs[9 + n_cast:9 + 2 * n_cast]
    st_scr = refs[9 + 2 * n_cast]
    C = HGRN_CHUNK
    n_levels = pm_ref.shape[0] - 1
    for src, dst in zip(cast_src, cast_dst):
        dst[...] = src[...].astype(BF16)

    @pl.when(pl.program_id(2) == 0)
    def _():
        st_scr[...] = jnp.zeros_like(st_scr)

    lg = lbl_ref[...]
    ex = jnp.exp(lg - jnp.max(lg, axis=0, keepdims=True))
    pr = ex / jnp.sum(ex, axis=0, keepdims=True)
    cum = pr[0:1, :]
    for r in range(1, layer + 1):
        cum = cum + pr[r:r + 1, :]
    lb = cum - pr[0:1, :]
    lb_floor = jnp.maximum(lb, LB_FLOOR)
    one_m_lb = 1.0 - lb
    gain = on_ref[...]

    for c in range(ts // C):
        rows = pl.ds(c * C, C)
        q_raw = zq_ref[rows, :]
        zf = zf_ref[rows, :]
        g_raw = zg_ref[rows, :]
        v_all = zi_ref[rows, :].astype(BF16)
        q_all = q_raw * _sigmoid(q_raw)
        e = jnp.exp(-jnp.abs(zf))
        r = 1.0 / (1.0 + e)
        er = e * r
        pos = zf >= 0
        log_f = jnp.log(lb_floor + one_m_lb * jnp.where(pos, r, er))
        kk_all = one_m_lb * jnp.where(pos, er, r)
        gate_all = g_raw * _sigmoid(g_raw)
        p0 = log_f.astype(BF16)
        r1 = log_f - p0.astype(F32)
        p1 = r1.astype(BF16)
        p2 = (r1 - p1.astype(F32)).astype(BF16)
        sums_all = _dot(sums_ref[...], jnp.concatenate([p0, p1, p2], axis=0))

        def stack(x):
            return jnp.concatenate([x[:, h * LANES:(h + 1) * LANES]
                                    for h in range(HGRN_HEADS_PER_STEP)], axis=0)

        q_st = stack(q_all).astype(BF16)
        kk_st = stack(kk_all).astype(BF16)
        a = pm_ref[0] * _dot_nt(q_st, kk_st)
        for lv in range(n_levels):
            dec = jnp.exp(stack(sums_all[(2 + lv) * C:(3 + lv) * C])).astype(BF16)
            a = a + pm_ref[lv + 1] * _dot_nt(q_st * dec, kk_st * dec)
        o_intra = _dot(a.astype(BF16), stack(v_all))
        b_all = sums_all[0:C]
        st_all = st_scr[...]
        o_inter = _dot_nt(stack(q_all * jnp.exp(b_all)).astype(BF16), st_all.astype(BF16))
        k_end = (kk_all * jnp.exp(sums_all[C:2 * C])).astype(BF16)
        decay = jnp.exp(b_all[C - 1:C, :])
        for hh in range(HGRN_HEADS_PER_STEP):
            lanes = slice(hh * LANES, (hh + 1) * LANES)
            srows = slice(hh * HGRN_V, (hh + 1) * HGRN_V)
            o = o_intra[hh * C:(hh + 1) * C] + o_inter[hh * C:(hh + 1) * C, lanes]
            st_scr[srows, :] = st_all[srows] * decay[:, lanes] \
                + _dot_tn(v_all[:, lanes], k_end[:, lanes])
            o_ref[rows, pl.ds(hh * LANES, LANES)] = \
                (_rms(o, gain) * gate_all[:, lanes]).astype(BF16)


def _hgrn(z, lb_logits, out_norm, consts, layer, B, S, casts=()):
    T = B * S
    ts = min(S, 512)
    ns = S // ts
    sums, pair_masks = consts
    hp = HGRN_HEADS_PER_STEP
    width = hp * LANES
    col0 = HGRN_COL0 // width
    groups = N_HEADS // hp
    steps = B * groups * ns
    step = lambda b, h, i: (b * groups + h) * ns + i
    riding = [(w, idx) for w, idx in casts if _cast_slab(w.shape[1:], steps)]
    cast_in = [pl.BlockSpec((None, _cast_slab(w.shape[1:], steps), w.shape[2]),
                            lambda b, h, i, idx=idx: (idx, step(b, h, i), 0)) for w, idx in riding]
    cast_out = [pl.BlockSpec((_cast_slab(w.shape[1:], steps), w.shape[2]),
                             lambda b, h, i: (step(b, h, i), 0)) for w, _ in riding]

    def zspec(part):
        return pl.BlockSpec((ts, width),
                            lambda b, h, i: (b * ns + i, col0 + part * groups + h))

    full = lambda arr: pl.BlockSpec(arr.shape, lambda b, h, i: (0,) * arr.ndim)
    kern = functools.partial(_hgrn_kernel, layer=layer, ts=ts, n_cast=len(riding))
    outs = pl.pallas_call(
        kern,
        grid=(B, groups, ns),
        in_specs=[zspec(0), zspec(1), zspec(2), zspec(3),
                  pl.BlockSpec((DEPTH, width), lambda b, h, i: (0, h)),
                  pl.BlockSpec((1, HGRN_V), lambda b, h, i: (0, 0)),
                  full(sums), full(pair_masks)] + cast_in,
        out_specs=[pl.BlockSpec((ts, width), lambda b, h, i: (b * ns + i, h))] + cast_out,
        out_shape=[jax.ShapeDtypeStruct((T, GROUP_WIDTH), BF16)]
        + [jax.ShapeDtypeStruct(w.shape[1:], BF16) for w, _ in riding],
        scratch_shapes=[pltpu.VMEM((hp * HGRN_V, HGRN_K), F32)],
        compiler_params=_params(("parallel", "parallel", "arbitrary")),
        name="hgrn2",
    )(z, z, z, z, lb_logits, out_norm[None, :], sums, pair_masks, *[w for w, _ in riding])
    copies = iter(outs[1:])
    return [outs[0]] + [next(copies) if _cast_slab(w.shape[1:], steps) else w[idx].astype(BF16)
                        for w, idx in casts]


def _out_proj_kernel(om_ref, oh_ref, h_ref, w_ref, g1_ref, g2_ref, h1_ref, c_ref):
    y = _dot(om_ref[...], w_ref[0:GROUP_WIDTH, :]) + _dot(oh_ref[...], w_ref[GROUP_WIDTH:, :])
    h1 = h_ref[...] + _rms(y, g1_ref[...])
    h1_ref[...] = h1
    c_ref[...] = _rms(h1, g2_ref[...]).astype(c_ref.dtype)


def _out_proj(om, oh, h, w, g1, g2, c_dtype):
    T = h.shape[0]
    tm = min(T, 512)
    c_width = D_MODEL
    row = lambda i: (i, 0)
    const = lambda i: (0, 0)
    return pl.pallas_call(
        _out_proj_kernel,
        grid=(T // tm,),
        in_specs=[pl.BlockSpec((tm, GROUP_WIDTH), row),
                  pl.BlockSpec((tm, GROUP_WIDTH), row),
                  pl.BlockSpec((tm, D_MODEL), row),
                  pl.BlockSpec((D_MODEL, D_MODEL), const, pipeline_mode=pl.Buffered(1)),
                  pl.BlockSpec((1, D_MODEL), const),
                  pl.BlockSpec((1, D_MODEL), const)],
        out_specs=[pl.BlockSpec((tm, D_MODEL), row), pl.BlockSpec((tm, c_width), row)],
        out_shape=[jax.ShapeDtypeStruct((T, D_MODEL), F32),
                   jax.ShapeDtypeStruct((T, c_width), c_dtype)],
        compiler_params=_params(("parallel",)),
        name="out_proj",
    )(om, oh, h, w, g1[None, :], g2[None, :])


def _gate_up_kernel(te_ref, nv_ref, x_ref, wg_ref, wu_ref, o_ref, wg_bf, wu_bf):
    i = pl.program_id(1)
    valid = i < nv_ref[0]
    new_weights = jnp.logical_or(i == 0, te_ref[i] != te_ref[jnp.maximum(i - 1, 0)])

    def step(refresh):
        x = x_ref[...].astype(BF16)
        for c in range(0, o_ref.shape[1], GU_SUB):
            cols = slice(c, c + GU_SUB)
            if refresh:
                wg_bf[:, cols] = wg_ref[0, :, cols].astype(BF16)
                wu_bf[:, cols] = wu_ref[0, :, cols].astype(BF16)
            g = _dot(x, wg_bf[:, cols])
            u = _dot(x, wu_bf[:, cols])
            o_ref[:, cols] = (g * _sigmoid(g) * u).astype(BF16)

    pl.when(jnp.logical_and(valid, new_weights))(functools.partial(step, True))
    pl.when(jnp.logical_and(valid, jnp.logical_not(new_weights)))(functools.partial(step, False))

    @pl.when(jnp.logical_not(valid))
    def _():
        o_ref[...] = jnp.zeros_like(o_ref)


def _gate_up(x, w_gu, tile_expert, n_valid, tm, tf):
    R, xw = x.shape
    n_tiles = R // tm
    nf = FFN_DIM // tf

    def tile(i, nv):
        return jnp.minimum(i, nv[0] - 1)

    return pl.pallas_call(
        _gate_up_kernel,
        grid_spec=pltpu.PrefetchScalarGridSpec(
            num_scalar_prefetch=2,
            grid=(nf, n_tiles),
            in_specs=[pl.BlockSpec((tm, xw), lambda j, i, te, nv: (tile(i, nv), 0)),
                      pl.BlockSpec((1, D_MODEL, tf), lambda j, i, te, nv: (te[tile(i, nv)], 0, j)),
                      pl.BlockSpec((1, D_MODEL, tf),
                                   lambda j, i, te, nv: (te[tile(i, nv)], 0, j + nf))],
            out_specs=pl.BlockSpec((tm, tf), lambda j, i, te, nv: (i, j)),
            scratch_shapes=[pltpu.VMEM((D_MODEL, tf), BF16), pltpu.VMEM((D_MODEL, tf), BF16)],
        ),
        out_shape=jax.ShapeDtypeStruct((R, FFN_DIM), BF16),
        compiler_params=_params(("arbitrary", "arbitrary")),
        name="ffn_gate_up",
    )(tile_expert, n_valid, x, w_gu, w_gu)


def _down_kernel(te_ref, nv_ref, a_ref, w_ref, o_ref):
    valid = pl.program_id(1) < nv_ref[0]

    @pl.when(valid)
    def _():
        a = a_ref[...]
        for c in range(0, o_ref.shape[1], GU_SUB):
            o_ref[:, c:c + GU_SUB] = _dot(a, w_ref[0, :, c:c + GU_SUB])

    @pl.when(jnp.logical_not(valid))
    def _():
        o_ref[...] = jnp.zeros_like(o_ref)


def _down(act, w_down, tile_expert, n_valid, tm):
    R = act.shape[0]
    n_tiles = R // tm
    tn = 1024

    def tile(i, nv):
        return jnp.minimum(i, nv[0] - 1)

    return pl.pallas_call(
        _down_kernel,
        grid_spec=pltpu.PrefetchScalarGridSpec(
            num_scalar_prefetch=2,
            grid=(D_MODEL // tn, n_tiles),
            in_specs=[pl.BlockSpec((tm, FFN_DIM), lambda j, i, te, nv: (tile(i, nv), 0)),
                      pl.BlockSpec((1, FFN_DIM, tn), lambda j, i, te, nv: (te[tile(i, nv)], 0, j))],
            out_specs=pl.BlockSpec((tm, tn), lambda j, i, te, nv: (i, j)),
        ),
        out_shape=jax.ShapeDtypeStruct((R, D_MODEL), F32),
        compiler_params=_params(("arbitrary", "arbitrary")),
        name="ffn_down",
    )(tile_expert, n_valid, act, w_down)


def _ffn_epilogue(f, h1, p, wg, wp, g3, gp):
    h2 = h1 + _rms(f, g3)
    gate = _sigmoid(_dot(h2.astype(BF16), wg))
    proj = _dot(p.astype(BF16), wp)
    return h2 + _rms(gate * proj, gp)


def _post_ffn_kernel(f_ref, h1_ref, p_ref, wg_ref, wp_ref, g3_ref, gp_ref, o_ref):
    o_ref[...] = _ffn_epilogue(f_ref[...], h1_ref[...], p_ref[...], wg_ref[...], wp_ref[...],
                               g3_ref[...], gp_ref[...])


def _post_ffn(f, h1, p, layer, wg, wp, g3, gp):
    T = h1.shape[0]
    tm = min(T, 512)
    row = lambda i: (i, 0)
    const = lambda i: (0, 0)
    return pl.pallas_call(
        _post_ffn_kernel,
        grid=(T // tm,),
        in_specs=[pl.BlockSpec((tm, D_MODEL), row),
                  pl.BlockSpec((tm, D_MODEL), row),
                  pl.BlockSpec((None, tm, PLE_DIM), lambda i: (layer, i, 0)),
                  pl.BlockSpec((D_MODEL, D_MODEL), const, pipeline_mode=pl.Buffered(1)),
                  pl.BlockSpec((PLE_DIM, D_MODEL), const, pipeline_mode=pl.Buffered(1)),
                  pl.BlockSpec((1, D_MODEL), const),
                  pl.BlockSpec((1, D_MODEL), const)],
        out_specs=pl.BlockSpec((tm, D_MODEL), row),
        out_shape=jax.ShapeDtypeStruct((T, D_MODEL), F32),
        compiler_params=_params(("parallel",)),
        name="post_ffn",
    )(f, h1, p, wg, wp, g3[None, :], gp[None, :])


def _router_kernel(c_ref, wr_ref, tri_ref, route_ref, cnt_ref, carry):
    @pl.when(pl.program_id(0) == 0)
    def _():
        carry[...] = jnp.zeros_like(carry)

    logits = _dot(c_ref[...].astype(BF16), wr_ref[...])
    lane = lax.broadcasted_iota(jnp.int32, logits.shape, 1)
    lg = jnp.where(lane < N_EXPERTS, logits, -jnp.inf)
    m1 = jnp.max(lg, axis=-1, keepdims=True)
    i1 = jnp.min(jnp.where(lg == m1, lane, LANES), axis=-1, keepdims=True)
    lg2 = jnp.where(lane == i1, -jnp.inf, lg)
    m2 = jnp.max(lg2, axis=-1, keepdims=True)
    i2 = jnp.min(jnp.where(lg2 == m2, lane, LANES), axis=-1, keepdims=True)
    e = jnp.exp(m2 - m1)
    g1 = 1.0 / (1.0 + e)
    g2 = e / (1.0 + e)
    onehot = jnp.where((lane == i1) | (lane == i2), 1.0, 0.0)
    before = _dot(tri_ref[...], onehot.astype(BF16)) + carry[0:1, :]
    rank1 = jnp.sum(jnp.where(lane == i1, before, 0.0), axis=-1, keepdims=True)
    rank2 = jnp.sum(jnp.where(lane == i2, before, 0.0), axis=-1, keepdims=True)
    total = carry[0:1, :] + jnp.sum(onehot, axis=0, keepdims=True)
    carry[...] = jnp.broadcast_to(total, carry.shape)
    cnt_ref[...] = jnp.broadcast_to(total, cnt_ref.shape)
    out = jnp.where(lane == 0, i1.astype(F32), 0.0)
    out = jnp.where(lane == 1, i2.astype(F32), out)
    out = jnp.where(lane == 2, g1, out)
    out = jnp.where(lane == 3, g2, out)
    out = jnp.where(lane == 4, rank1, out)
    out = jnp.where(lane == 5, rank2, out)
    route_ref[...] = out


def _router(c, w_router):
    T = c.shape[0]
    tr = min(T, 512)
    wr = jnp.zeros((D_MODEL, LANES), BF16).at[:, :N_EXPERTS].set(w_router.astype(BF16))
    tri = jnp.asarray(np.tril(np.ones((tr, tr), np.float32), -1), dtype=BF16)
    return pl.pallas_call(
        _router_kernel,
        grid=(T // tr,),
        in_specs=[pl.BlockSpec((tr, c.shape[1]), lambda i: (i, 0)),
                  pl.BlockSpec((D_MODEL, LANES), lambda i: (0, 0)),
                  pl.BlockSpec((tr, tr), lambda i: (0, 0))],
        out_specs=[pl.BlockSpec((tr, LANES), lambda i: (i, 0)),
                   pl.BlockSpec((8, LANES), lambda i: (0, 0))],
        out_shape=[jax.ShapeDtypeStruct((T, LANES), F32),
                   jax.ShapeDtypeStruct((8, LANES), F32)],
        scratch_shapes=[pltpu.VMEM((8, LANES), F32)],
        compiler_params=_params(("arbitrary",)),
        name="moe_router",
    )(c, wr, tri)


def _dispatch_kernel(p1_ref, p2_ref, pad_ref, nv_ref, c_ref, xs_ref, zbuf, sem, zsem, *,
                     tm, seg_tile):
    i = pl.program_id(0)
    base = i * tm
    zr = zbuf.shape[0]

    @pl.when(i == 0)
    def _():
        zbuf[...] = jnp.zeros_like(zbuf)

        def zero_tile(start):
            first = pl.multiple_of(start, zr)
            copies = [pltpu.make_async_copy(zbuf, xs_ref.at[pl.ds(first + k * zr, zr)], zsem)
                      for k in range(seg_tile // zr)]
            for cp in copies:
                cp.start()
            for cp in copies:
                cp.wait()

        def zero_padding(e, _):
            @pl.when(pad_ref[e] >= 0)
            def _():
                zero_tile(pad_ref[e])
            return 0

        def zero_unused(t, _):
            zero_tile(t * seg_tile)
            return 0

        lax.fori_loop(0, N_EXPERTS, zero_padding, 0)
        lax.fori_loop(nv_ref[0], xs_ref.shape[0] // seg_tile, zero_unused, 0)

    def row_copy(r, dst):
        return pltpu.make_async_copy(c_ref.at[pl.ds(r, 1)], xs_ref.at[pl.ds(dst, 1)], sem)

    for r in range(tm):
        row_copy(r, p1_ref[base + r]).start(priority=0)
        row_copy(r, p2_ref[base + r]).start(priority=1)
    for _r in range(2 * tm):
        row_copy(0, 0).wait()


def _dispatch(c, pos1, pos2, pad_tile_row, n_valid, n_rows, seg_tile):
    T, width = c.shape
    tm = min(T, 256)
    return pl.pallas_call(
        functools.partial(_dispatch_kernel, tm=tm, seg_tile=seg_tile),
        grid_spec=pltpu.PrefetchScalarGridSpec(
            num_scalar_prefetch=4,
            grid=(T // tm,),
            in_specs=[pl.BlockSpec((tm, width), lambda i, p1, p2, pt, nv: (i, 0))],
            out_specs=pl.BlockSpec(memory_space=pl.ANY),
            scratch_shapes=[pltpu.VMEM((min(seg_tile, 64), width), c.dtype),
                            pltpu.SemaphoreType.DMA(()),
                            pltpu.SemaphoreType.DMA(())],
        ),
        out_shape=jax.ShapeDtypeStruct((n_rows, width), c.dtype),
        compiler_params=_params(("arbitrary",)),
        name="moe_dispatch",
    )(pos1, pos2, pad_tile_row, n_valid, c)


def _combine_kernel(p1_ref, p2_ref, ys_ref, route_ref, h1_ref, p_ref, wg_ref, wp_ref,
                    g3_ref, gp_ref, o_ref, buf, sems, *, tm):
    i = pl.program_id(0)
    slot = i % 2

    def row_copy(src, s, k, r):
        return pltpu.make_async_copy(ys_ref.at[pl.ds(src, 1)], buf.at[s, k, pl.ds(r, 1)],
                                     sems.at[s])

    def gather_tile(tile, s):
        base = tile * tm
        for r in range(tm):
            row_copy(p1_ref[base + r], s, 0, r).start(priority=0)
            row_copy(p2_ref[base + r], s, 1, r).start(priority=1)

    @pl.when(i == 0)
    def _():
        gather_tile(0, 0)

    @pl.when(i + 1 < pl.num_programs(0))
    def _():
        gather_tile(i + 1, 1 - slot)

    for _r in range(2 * tm):
        row_copy(0, slot, 0, 0).wait()
    route = route_ref[...]
    f = route[:, 2:3] * buf[slot, 0] + route[:, 3:4] * buf[slot, 1]
    o_ref[...] = _ffn_epilogue(f, h1_ref[...], p_ref[...], wg_ref[...], wp_ref[...],
                               g3_ref[...], gp_ref[...])


def _combine(ys, pos1, pos2, route, h1, p, layer, wg, wp, g3, gp):
    T = h1.shape[0]
    tm = min(T, 256)
    row = lambda i, p1, p2: (i, 0)
    const = lambda i, p1, p2: (0, 0)
    return pl.pallas_call(
        functools.partial(_combine_kernel, tm=tm),
        grid_spec=pltpu.PrefetchScalarGridSpec(
            num_scalar_prefetch=2,
            grid=(T // tm,),
            in_specs=[pl.BlockSpec(memory_space=pl.ANY),
                      pl.BlockSpec((tm, LANES), row),
                      pl.BlockSpec((tm, D_MODEL), row),
                      pl.BlockSpec((None, tm, PLE_DIM), lambda i, p1, p2: (layer, i, 0)),
                      pl.BlockSpec((D_MODEL, D_MODEL), const, pipeline_mode=pl.Buffered(1)),
                      pl.BlockSpec((PLE_DIM, D_MODEL), const, pipeline_mode=pl.Buffered(1)),
                      pl.BlockSpec((1, D_MODEL), const),
                      pl.BlockSpec((1, D_MODEL), const)],
            out_specs=pl.BlockSpec((tm, D_MODEL), row),
            scratch_shapes=[pltpu.VMEM((2, 2, tm, D_MODEL), F32),
                            pltpu.SemaphoreType.DMA((2,))],
        ),
        out_shape=jax.ShapeDtypeStruct((T, D_MODEL), F32),
        compiler_params=_params(("arbitrary",)),
        name="moe_combine",
    )(pos1, pos2, ys, route, h1, p, wg, wp, g3[None, :], gp[None, :])


def _swap_halves(w):
    half = w.shape[-1] // 2
    return jnp.concatenate([w[..., half:], w[..., :half]], axis=-1)


def _prep_w_in(w):
    w = w.astype(BF16)
    k_rope = w[:, 640:704]
    cols = [w[:, :704], _swap_halves(k_rope),
            jnp.zeros((D_MODEL, HGRN_COL0 - 768), w.dtype), w[:, 704:]]
    return jnp.concatenate(cols, axis=1).astype(BF16)


def _prep_w_uq(w):
    w = w.reshape(Q_RANK, N_HEADS, QK_DIM)
    pe = w[:, :, NOPE_DIM:]
    per_head = jnp.concatenate([w, _swap_halves(pe)], axis=-1)
    return per_head.reshape(Q_RANK, N_HEADS * QK_PAD).astype(BF16)


def _prep_w_ukv(w):
    w = w.reshape(KV_RANK, N_HEADS, NOPE_DIM + V_DIM)
    k = w[:, :, :NOPE_DIM].reshape(KV_RANK, GROUP_WIDTH)
    v = w[:, :, NOPE_DIM:].reshape(KV_RANK, GROUP_WIDTH)
    return jnp.concatenate([k, v], axis=1).astype(BF16)


def _moe_plan(route, counts, tm, n_tiles):
    e1 = route[:, 0].astype(jnp.int32)
    e2 = route[:, 1].astype(jnp.int32)
    rank1 = route[:, 4].astype(jnp.int32)
    rank2 = route[:, 5].astype(jnp.int32)
    cnt = counts[0, :N_EXPERTS].astype(jnp.int32)
    padded = ((cnt + tm - 1) // tm) * tm
    seg_end = jnp.cumsum(padded)
    seg_start = seg_end - padded
    pos1 = seg_start[e1] + rank1
    pos2 = seg_start[e2] + rank2
    tile_start = jnp.arange(n_tiles, dtype=jnp.int32) * tm
    tile_expert = jnp.sum(tile_start[:, None] >= seg_end[None, :], axis=1).astype(jnp.int32)
    tile_expert = jnp.minimum(tile_expert, N_EXPERTS - 1)
    n_valid = (seg_end[-1] // tm).astype(jnp.int32)[None]
    pad_tile_row = jnp.where(padded > 0, seg_end - tm, -1).astype(jnp.int32)
    return pos1, pos2, tile_expert, n_valid, pad_tile_row


def kernel(x, p, positions, sandwich_norms, w_in, mla_q_norm, mla_kv_norm, w_uq, w_ukv,
           hgrn_lb_logits, hgrn_out_norm, w_out, ffn_w_gu, ffn_w_down, moe_w_router,
           moe_w_gu, moe_w_down, ple_w_proj, ple_w_gate, ple_norm):
    B, S, _ = x.shape
    T = B * S
    h = x.reshape(T, D_MODEL)
    ct, st = _rope_tables(positions)
    hgrn_consts = _hgrn_consts(HGRN_CHUNK)
    tm = min(T, 512)
    for l in range(DEPTH):
        z = _norm_matmul(h, sandwich_norms[l, 0], _prep_w_in(w_in[l]))
        q, k, v = _mla_proj(z, ct, st, mla_q_norm[l], mla_kv_norm[l],
                            _prep_w_uq(w_uq[l]), _prep_w_ukv(w_ukv[l]), B, S)
        moe = l % 2 == 1
        if moe:
            w_down = moe_w_down[l // 2]
            o_mla, w_down_bf = _attention_and_cast(q, k, v, w_down.reshape(-1, D_MODEL))
            w_down_bf = w_down_bf.reshape(w_down.shape)
        else:
            o_mla = _attention(q, k, v)
        o_mla = o_mla.reshape(T, GROUP_WIDTH)
        later = [(w_out, l), (ple_w_gate, l)] + ([] if moe else [(ffn_w_down, l // 2)])
        o_hgrn, w_out_bf, wg, *rest = _hgrn(z, hgrn_lb_logits, hgrn_out_norm[l], hgrn_consts,
                                            l, B, S, casts=later)
        h1, c = _out_proj(o_mla, o_hgrn, h, w_out_bf,
                          sandwich_norms[l, 1], sandwich_norms[l, 2],
                          F32 if moe else BF16)
        wp = ple_w_proj[l].astype(BF16)
        p_all = p.reshape(DEPTH, T, PLE_DIM)
        if not moe:
            def one_expert(rows):
                return jnp.zeros((T // rows,), jnp.int32), jnp.full((1,), T // rows, jnp.int32)

            tm_dense = min(T, 1024)
            act = _gate_up(c, ffn_w_gu[l // 2][None], *one_expert(tm_dense), tm_dense, 512)
            f = _down(act, rest[0][None], *one_expert(tm), tm)
            h = _post_ffn(f, h1, p_all, l, wg, wp, sandwich_norms[l, 3], ple_norm[l])
        else:
            n_tiles = (2 * T) // tm + N_EXPERTS
            route, counts = _router(c, moe_w_router[l // 2])
            pos1, pos2, tile_expert, n_valid, pad_tile_row = _moe_plan(route, counts, tm, n_tiles)
            xs = _dispatch(c, pos1, pos2, pad_tile_row, n_valid, n_tiles * tm, tm)
            act = _gate_up(xs, moe_w_gu[l // 2], tile_expert, n_valid, tm, 1024)
            ys = _down(act, w_down_bf, tile_expert, n_valid, tm)
            h = _combine(ys, pos1, pos2, route, h1, p_all, l, wg, wp,
                         sandwich_norms[l, 3], ple_norm[l])
    return h.reshape(B, S, D_MODEL)
```

```python
import functools
import math

import numpy as np
import jax
import jax.numpy as jnp
from jax import lax
from jax.experimental import pallas as pl
from jax.experimental.pallas import tpu as pltpu

F32 = jnp.float32
BF16 = jnp.bfloat16

D_MODEL = 2048
DEPTH = 2
N_HEADS = 8
NOPE_DIM = 128
ROPE_DIM = 64
V_DIM = 128
QK_DIM = NOPE_DIM + ROPE_DIM
Q_RANK = 384
KV_RANK = 256
HGRN_K = 128
HGRN_V = 128
GROUP_WIDTH = N_HEADS * 128
FFN_DIM = 7168
N_EXPERTS = 8
PLE_DIM = 256
ROPE_THETA = 10000.0
LB_FLOOR = 1e-30
EPS = 1e-6

LANES = 128
QK_PAD = 256
Q_SCALE = QK_DIM ** -0.5 * math.log2(math.e)
Z_MLA = 1024
HGRN_COL0 = 1024
HGRN_CHUNK = 64
HGRN_HEADS_PER_STEP = 4
GU_SUB = 512
IN_PROJ_SUB = 1024
CAST_SLAB_BYTES = 4 * 1024 * 1024
VMEM_LIMIT = 56 * 1024 * 1024


def _params(semantics, vmem=VMEM_LIMIT):
    return pltpu.CompilerParams(dimension_semantics=semantics, vmem_limit_bytes=vmem)


def _rms(x, gain_row):
    ms = jnp.mean(x * x, axis=-1, keepdims=True)
    return x * lax.rsqrt(ms + EPS) * gain_row


def _dot(a, b):
    return jnp.dot(a, b, preferred_element_type=F32)


def _dot_nt(a, b):
    return lax.dot_general(a, b, (((1,), (1,)), ((), ())), preferred_element_type=F32)


def _dot_tn(a, b):
    return lax.dot_general(a, b, (((0,), (0,)), ((), ())), preferred_element_type=F32)


def _sigmoid(x):
    return 1.0 / (1.0 + jnp.exp(-x))


def _fold_lanes(x, op):
    parts = [x[:, c:c + LANES] for c in range(0, x.shape[1], LANES)]
    while len(parts) > 1:
        parts = [op(a, b) for a, b in zip(parts[0::2], parts[1::2])] + parts[len(parts) & ~1:]
    return parts[0]


def _rope_kernel(pos_ref, inv_ref, keep_ref, sgn_ref, ct_ref, st_ref):
    ang = pos_ref[...].astype(F32) * inv_ref[...]
    ct_ref[...] = jnp.cos(ang) * keep_ref[...]
    st_ref[...] = jnp.sin(ang) * sgn_ref[...]


def _rope_tables(positions):
    T = positions.size
    tt = min(T, 1024)
    half = ROPE_DIM // 2
    inv_freq = 1.0 / (ROPE_THETA ** (jnp.arange(0, ROPE_DIM, 2, dtype=F32) / ROPE_DIM))
    inv_row = jnp.concatenate([inv_freq] * (LANES // half))[None, :]
    keep = np.zeros((1, LANES), np.float32)
    keep[0, :ROPE_DIM] = 1.0
    sgn = np.zeros((1, LANES), np.float32)
    sgn[0, ROPE_DIM:ROPE_DIM + half] = -1.0
    sgn[0, ROPE_DIM + half:] = 1.0
    return pl.pallas_call(
        _rope_kernel,
        grid=(T // tt,),
        in_specs=[pl.BlockSpec((tt, 1), lambda i: (i, 0)),
                  pl.BlockSpec((1, LANES), lambda i: (0, 0)),
                  pl.BlockSpec((1, LANES), lambda i: (0, 0)),
                  pl.BlockSpec((1, LANES), lambda i: (0, 0))],
        out_specs=[pl.BlockSpec((tt, LANES), lambda i: (i, 0))] * 2,
        out_shape=[jax.ShapeDtypeStruct((T, LANES), F32)] * 2,
        compiler_params=_params(("parallel",)),
        name="rope_tables",
    )(positions.reshape(T, 1), inv_row, jnp.asarray(keep), jnp.asarray(sgn))


def _norm_matmul_kernel(x_ref, g_ref, w_ref, o_ref):
    a = _rms(x_ref[...], g_ref[...]).astype(BF16)
    for c in range(0, o_ref.shape[1], IN_PROJ_SUB):
        o_ref[:, c:c + IN_PROJ_SUB] = _dot(a, w_ref[:, c:c + IN_PROJ_SUB])


def _norm_matmul(x, gain, w):
    T, K = x.shape
    N = w.shape[1]
    tm = min(T, 512)
    return pl.pallas_call(
        _norm_matmul_kernel,
        grid=(T // tm,),
        in_specs=[pl.BlockSpec((tm, K), lambda i: (i, 0)),
                  pl.BlockSpec((1, K), lambda i: (0, 0)),
                  pl.BlockSpec((K, N), lambda i: (0, 0), pipeline_mode=pl.Buffered(1))],
        out_specs=pl.BlockSpec((tm, N), lambda i: (i, 0)),
        out_shape=jax.ShapeDtypeStruct((T, N), F32),
        compiler_params=_params(("parallel",)),
        name="in_proj",
    )(x, gain[None, :], w)


def _mla_proj_kernel(z_ref, ct_ref, st_ref, qn_ref, kvn_ref, wq_ref, wkv_ref,
                     q_ref, k_ref, v_ref):
    z = z_ref[...]
    cs = ct_ref[...] + st_ref[...]

    def rope(x_and_swapped):
        y = x_and_swapped * cs
        return y + pltpu.roll(y, ROPE_DIM, 1)

    aq = _rms(z[:, :Q_RANK], qn_ref[...]).astype(BF16)
    akv = _rms(z[:, Q_RANK:Q_RANK + KV_RANK], kvn_ref[...]).astype(BF16)
    lane = lax.broadcasted_iota(jnp.int32, cs.shape, 1)
    k_pe = jnp.where(lane < ROPE_DIM, rope(z[:, 640:768]), 0.0).astype(BF16)
    q = _dot(aq, wq_ref[...])
    kv = _dot(akv, wkv_ref[...])
    for h in range(N_HEADS):
        lo, hi = h * LANES, (h + 1) * LANES
        q_ref[0, h, :, 0:LANES] = (q[:, 2 * lo:2 * lo + LANES] * Q_SCALE).astype(BF16)
        q_pe = rope(q[:, 2 * lo + LANES:2 * hi])
        q_ref[0, h, :, LANES:QK_PAD] = (q_pe * Q_SCALE).astype(BF16)
        k_ref[0, h, :, 0:LANES] = kv[:, lo:hi].astype(BF16)
        k_ref[0, h, :, LANES:QK_PAD] = k_pe
        v_ref[0, h, :, :] = kv[:, GROUP_WIDTH + lo:GROUP_WIDTH + hi].astype(BF16)


def _mla_proj(z, ct, st, q_norm, kv_norm, wq, wkv, B, S):
    ts = min(S, 512)
    ns = S // ts
    tok = lambda b, i: (b * ns + i, 0)
    const = lambda b, i: (0, 0)
    head_out = lambda b, i: (b, 0, i, 0)
    return pl.pallas_call(
        _mla_proj_kernel,
        grid=(B, ns),
        in_specs=[pl.BlockSpec((ts, Z_MLA), tok),
                  pl.BlockSpec((ts, LANES), tok),
                  pl.BlockSpec((ts, LANES), tok),
                  pl.BlockSpec((1, Q_RANK), const),
                  pl.BlockSpec((1, KV_RANK), const),
                  pl.BlockSpec(wq.shape, const),
                  pl.BlockSpec(wkv.shape, const)],
        out_specs=[pl.BlockSpec((1, N_HEADS, ts, QK_PAD), head_out),
                   pl.BlockSpec((1, N_HEADS, ts, QK_PAD), head_out),
                   pl.BlockSpec((1, N_HEADS, ts, V_DIM), head_out)],
        out_shape=[jax.ShapeDtypeStruct((B, N_HEADS, S, QK_PAD), BF16),
                   jax.ShapeDtypeStruct((B, N_HEADS, S, QK_PAD), BF16),
                   jax.ShapeDtypeStruct((B, N_HEADS, S, V_DIM), BF16)],
        compiler_params=_params(("parallel", "parallel")),
        name="mla_proj",
    )(z, ct, st, q_norm[None, :], kv_norm[None, :], wq, wkv)


def _attn_kernel(*refs, tk, with_cast):
    if with_cast:
        q_ref, k_ref, v_ref, w_ref, o_ref, wbf_ref, s_a, s_b = refs
    else:
        q_ref, k_ref, v_ref, o_ref, s_a, s_b = refs
    i = pl.program_id(2)
    top_rows = pl.ds(0, tk)
    bot_rows = pl.ds(tk, tk)

    def kv_rows(u):
        return pl.ds(pl.multiple_of(u * tk, tk), tk)

    def scores(u):
        return _dot_nt(q_ref[0, 0], k_ref[0, 0, kv_rows(u), :])

    lane = lax.broadcasted_iota(jnp.int32, (tk, LANES), 1)
    ones_col = jnp.where(lane == 0, 1.0, 0.0).astype(BF16)

    def values(u):
        return jnp.concatenate([v_ref[0, 0, kv_rows(u), :], ones_col], axis=1)

    def update(state, s, v, masked):
        m, acc = state
        if masked:
            row = lax.broadcasted_iota(jnp.int32, (tk, tk), 0)
            col = lax.broadcasted_iota(jnp.int32, (tk, tk), 1)
            s = jnp.where(col <= row, s, -jnp.inf)
        m_new = jnp.maximum(m, jnp.max(_fold_lanes(s, jnp.maximum), axis=-1, keepdims=True))
        alpha = jnp.exp2(m - m_new)
        p = jnp.exp2((s - m_new).astype(BF16))
        acc = alpha * acc + _dot(p, v)
        return m_new, acc

    def both(top, bot, s_ref, u):
        v = values(u)
        return (update(top, s_ref[top_rows, :], v, False),
                update(bot, s_ref[bot_rows, :], v, False))

    s_a[...] = scores(0)
    if with_cast:
        wbf_ref[...] = w_ref[...].astype(BF16)

    def body(t, state):
        top, bot = state
        s_b[...] = scores(2 * t + 1)
        top, bot = both(top, bot, s_a, 2 * t)
        s_a[...] = scores(2 * t + 2)
        top, bot = both(top, bot, s_b, 2 * t + 1)
        return top, bot

    init = (jnp.full((tk, 1), -jnp.inf, F32), jnp.zeros((tk, 2 * LANES), F32))
    top, bot = lax.fori_loop(0, i, body, (init, init))
    s_b[bot_rows, :] = _dot_nt(q_ref[0, 0, bot_rows, :], k_ref[0, 0, kv_rows(2 * i + 1), :])
    v0 = values(2 * i)
    top = update(top, s_a[top_rows, :], v0, True)
    bot = update(bot, s_a[bot_rows, :], v0, False)
    bot = update(bot, s_b[bot_rows, :], values(2 * i + 1), True)
    for rows, (_, acc) in ((top_rows, top), (bot_rows, bot)):
        o_ref[0, rows, :] = (acc[:, :V_DIM] / acc[:, V_DIM:V_DIM + 1]).astype(BF16)


def _attention(q, k, v, w_cast=None):
    B, H, S, _ = q.shape
    tq = min(S, 1024)
    tk = tq // 2
    nq = S // tq
    in_specs = [pl.BlockSpec((1, 1, tq, QK_PAD), lambda b, h, i: (b, h, i, 0)),
                pl.BlockSpec((1, 1, S, QK_PAD), lambda b, h, i: (b, h, 0, 0)),
                pl.BlockSpec((1, 1, S, V_DIM), lambda b, h, i: (b, h, 0, 0))]
    out_specs = [pl.BlockSpec((1, tq, V_DIM), lambda b, h, i: (b, i, h))]
    out_shape = [jax.ShapeDtypeStruct((B, S, H * V_DIM), BF16)]
    args = [q, k, v]
    if w_cast is not None:
        slab = w_cast.shape[0] // (B * H * nq)
        spec = pl.BlockSpec((slab, w_cast.shape[1]), lambda b, h, i: ((b * H + h) * nq + i, 0))
        in_specs.append(spec)
        out_specs.append(spec)
        out_shape.append(jax.ShapeDtypeStruct(w_cast.shape, BF16))
        args.append(w_cast)
    outs = pl.pallas_call(
        functools.partial(_attn_kernel, tk=tk, with_cast=w_cast is not None),
        grid=(B, H, nq),
        in_specs=in_specs,
        out_specs=out_specs,
        out_shape=out_shape,
        scratch_shapes=[pltpu.VMEM((tq, tk), F32), pltpu.VMEM((tq, tk), F32)],
        compiler_params=_params(("parallel", "parallel", "arbitrary")),
        name="mla_attention",
    )(*args)
    return outs if w_cast is not None else outs[0]


def _cast_slab(shape, steps):
    rows, cols = shape
    slab = rows // steps
    fits = rows % steps == 0 and slab % 16 == 0 and slab * cols * 4 <= CAST_SLAB_BYTES
    return slab if fits else 0


def _attention_and_cast(q, k, v, w):
    B, H, S, _ = q.shape
    if _cast_slab(w.shape, B * H * (S // min(S, 1024))):
        return _attention(q, k, v, w)
    return _attention(q, k, v), w.astype(BF16)


def _hgrn_consts(C):
    levels = [C >> (i + 1) for i in range(int(math.log2(C)))]
    t = np.arange(C)
    u = np.arange(C)
    mats = [(u[None, :] <= t[:, None]),
            (u[None, :] > t[:, None])]
    pair_masks = [np.eye(C, dtype=bool)]
    for m in levels:
        ref = (t // (2 * m)) * (2 * m) + m - 1
        hi = ((t // m) % 2) == 1
        rng_hi = (u[None, :] > ref[:, None]) & (u[None, :] <= t[:, None])
        rng_lo = (u[None, :] > t[:, None]) & (u[None, :] <= ref[:, None])
        mats.append(np.where(hi[:, None], rng_hi, rng_lo))
        same = (t[:, None] // (2 * m)) == (t[None, :] // (2 * m))
        pair_masks.append(hi[:, None] & (~hi[None, :]) & same)
    stack = np.concatenate(mats, axis=0).astype(np.float32)
    sums = jnp.asarray(np.concatenate([stack] * 3, axis=1), dtype=BF16)
    heads = np.eye(HGRN_HEADS_PER_STEP, dtype=bool)
    stacked = np.stack([np.kron(heads, m) for m in pair_masks])
    return sums, jnp.asarray(stacked.astype(np.float32))


def _hgrn_kernel(*refs, layer, ts, n_cast):
    (zq_ref, zf_ref, zi_ref, zg_ref, lbl_ref, on_ref, sums_ref, pm_ref) = refs[:8]
    cast_src = refs[8:8 + n_cast]
    o_ref = refs[8 + n_cast]
    cast_dst = refs[9 + n_cast:9 + 2 * n_cast]
    st_scr = refs[9 + 2 * n_cast]
    C = HGRN_CHUNK
    n_levels = pm_ref.shape[0] - 1
    for src, dst in zip(cast_src, cast_dst):
        dst[...] = src[...].astype(BF16)

    @pl.when(pl.program_id(2) == 0)
    def _():
        st_scr[...] = jnp.zeros_like(st_scr)

    lg = lbl_ref[...]
    ex = jnp.exp(lg - jnp.max(lg, axis=0, keepdims=True))
    pr = ex / jnp.sum(ex, axis=0, keepdims=True)
    cum = pr[0:1, :]
    for r in range(1, layer + 1):
        cum = cum + pr[r:r + 1, :]
    lb = cum - pr[0:1, :]
    lb_floor = jnp.maximum(lb, LB_FLOOR)
    one_m_lb = 1.0 - lb
    gain = on_ref[...]

    for c in range(ts // C):
        rows = pl.ds(c * C, C)
        q_raw = zq_ref[rows, :]
        zf = zf_ref[rows, :]
        g_raw = zg_ref[rows, :]
        v_all = zi_ref[rows, :].astype(BF16)
        q_all = q_raw * _sigmoid(q_raw)
        e = jnp.exp(-jnp.abs(zf))
        r = 1.0 / (1.0 + e)
        er = e * r
        pos = zf >= 0
        log_f = jnp.log(lb_floor + one_m_lb * jnp.where(pos, r, er))
        kk_all = one_m_lb * jnp.where(pos, er, r)
        gate_all = g_raw * _sigmoid(g_raw)
        p0 = log_f.astype(BF16)
        r1 = log_f - p0.astype(F32)
        p1 = r1.astype(BF16)
        p2 = (r1 - p1.astype(F32)).astype(BF16)
        sums_all = _dot(sums_ref[...], jnp.concatenate([p0, p1, p2], axis=0))

        def stack(x):
            return jnp.concatenate([x[:, h * LANES:(h + 1) * LANES]
                                    for h in range(HGRN_HEADS_PER_STEP)], axis=0)

        q_st = stack(q_all).astype(BF16)
        kk_st = stack(kk_all).astype(BF16)
        a = pm_ref[0] * _dot_nt(q_st, kk_st)
        for lv in range(n_levels):
            dec = jnp.exp(stack(sums_all[(2 + lv) * C:(3 + lv) * C])).astype(BF16)
            a = a + pm_ref[lv + 1] * _dot_nt(q_st * dec, kk_st * dec)
        o_intra = _dot(a.astype(BF16), stack(v_all))
        b_all = sums_all[0:C]
        st_all = st_scr[...]
        o_inter = _dot_nt(stack(q_all * jnp.exp(b_all)).astype(BF16), st_all.astype(BF16))
        k_end = (kk_all * jnp.exp(sums_all[C:2 * C])).astype(BF16)
        decay = jnp.exp(b_all[C - 1:C, :])
        for hh in range(HGRN_HEADS_PER_STEP):
            lanes = slice(hh * LANES, (hh + 1) * LANES)
            srows = slice(hh * HGRN_V, (hh + 1) * HGRN_V)
            o = o_intra[hh * C:(hh + 1) * C] + o_inter[hh * C:(hh + 1) * C, lanes]
            st_scr[srows, :] = st_all[srows] * decay[:, lanes] \
                + _dot_tn(v_all[:, lanes], k_end[:, lanes])
            o_ref[rows, pl.ds(hh * LANES, LANES)] = \
                (_rms(o, gain) * gate_all[:, lanes]).astype(BF16)


def _hgrn(z, lb_logits, out_norm, consts, layer, B, S, casts=()):
    T = B * S
    ts = min(S, 512)
    ns = S // ts
    sums, pair_masks = consts
    hp = HGRN_HEADS_PER_STEP
    width = hp * LANES
    col0 = HGRN_COL0 // width
    groups = N_HEADS // hp
    steps = B * groups * ns
    step = lambda b, h, i: (b * groups + h) * ns + i
    riding = [(w, idx) for w, idx in casts if _cast_slab(w.shape[1:], steps)]
    cast_in = [pl.BlockSpec((None, _cast_slab(w.shape[1:], steps), w.shape[2]),
                            lambda b, h, i, idx=idx: (idx, step(b, h, i), 0)) for w, idx in riding]
    cast_out = [pl.BlockSpec((_cast_slab(w.shape[1:], steps), w.shape[2]),
                             lambda b, h, i: (step(b, h, i), 0)) for w, _ in riding]

    def zspec(part):
        return pl.BlockSpec((ts, width),
                            lambda b, h, i: (b * ns + i, col0 + part * groups + h))

    full = lambda arr: pl.BlockSpec(arr.shape, lambda b, h, i: (0,) * arr.ndim)
    kern = functools.partial(_hgrn_kernel, layer=layer, ts=ts, n_cast=len(riding))
    outs = pl.pallas_call(
        kern,
        grid=(B, groups, ns),
        in_specs=[zspec(0), zspec(1), zspec(2), zspec(3),
                  pl.BlockSpec((DEPTH, width), lambda b, h, i: (0, h)),
                  pl.BlockSpec((1, HGRN_V), lambda b, h, i: (0, 0)),
                  full(sums), full(pair_masks)] + cast_in,
        out_specs=[pl.BlockSpec((ts, width), lambda b, h, i: (b * ns + i, h))] + cast_out,
        out_shape=[jax.ShapeDtypeStruct((T, GROUP_WIDTH), BF16)]
        + [jax.ShapeDtypeStruct(w.shape[1:], BF16) for w, _ in riding],
        scratch_shapes=[pltpu.VMEM((hp * HGRN_V, HGRN_K), F32)],
        compiler_params=_params(("parallel", "parallel", "arbitrary")),
        name="hgrn2",
    )(z, z, z, z, lb_logits, out_norm[None, :], sums, pair_masks, *[w for w, _ in riding])
    copies = iter(outs[1:])
    return [outs[0]] + [next(copies) if _cast_slab(w.shape[1:], steps) else w[idx].astype(BF16)
                        for w, idx in casts]


def _out_proj_kernel(om_ref, oh_ref, h_ref, w_ref, g1_ref, g2_ref, h1_ref, c_ref):
    y = _dot(om_ref[...], w_ref[0:GROUP_WIDTH, :]) + _dot(oh_ref[...], w_ref[GROUP_WIDTH:, :])
    h1 = h_ref[...] + _rms(y, g1_ref[...])
    h1_ref[...] = h1
    c_ref[...] = _rms(h1, g2_ref[...]).astype(c_ref.dtype)


def _out_proj(om, oh, h, w, g1, g2, c_dtype):
    T = h.shape[0]
    tm = min(T, 512)
    row = lambda i: (i, 0)
    const = lambda i: (0, 0)
    return pl.pallas_call(
        _out_proj_kernel,
        grid=(T // tm,),
        in_specs=[pl.BlockSpec((tm, GROUP_WIDTH), row),
                  pl.BlockSpec((tm, GROUP_WIDTH), row),
                  pl.BlockSpec((tm, D_MODEL), row),
                  pl.BlockSpec((D_MODEL, D_MODEL), const, pipeline_mode=pl.Buffered(1)),
                  pl.BlockSpec((1, D_MODEL), const),
                  pl.BlockSpec((1, D_MODEL), const)],
        out_specs=[pl.BlockSpec((tm, D_MODEL), row), pl.BlockSpec((tm, D_MODEL), row)],
        out_shape=[jax.ShapeDtypeStruct((T, D_MODEL), F32),
                   jax.ShapeDtypeStruct((T, D_MODEL), c_dtype)],
        compiler_params=_params(("parallel",)),
        name="out_proj",
    )(om, oh, h, w, g1[None, :], g2[None, :])


def _gate_up_kernel(te_ref, nv_ref, x_ref, wg_ref, wu_ref, o_ref, wg_bf, wu_bf):
    i = pl.program_id(1)
    valid = i < nv_ref[0]
    new_weights = jnp.logical_or(i == 0, te_ref[i] != te_ref[jnp.maximum(i - 1, 0)])

    def step(refresh):
        x = x_ref[...].astype(BF16)
        for c in range(0, o_ref.shape[1], GU_SUB):
            cols = slice(c, c + GU_SUB)
            if refresh:
                wg_bf[:, cols] = wg_ref[0, :, cols].astype(BF16)
                wu_bf[:, cols] = wu_ref[0, :, cols].astype(BF16)
            g = _dot(x, wg_bf[:, cols])
            u = _dot(x, wu_bf[:, cols])
            o_ref[:, cols] = (g * _sigmoid(g) * u).astype(BF16)

    pl.when(jnp.logical_and(valid, new_weights))(functools.partial(step, True))
    pl.when(jnp.logical_and(valid, jnp.logical_not(new_weights)))(functools.partial(step, False))

    @pl.when(jnp.logical_not(valid))
    def _():
        o_ref[...] = jnp.zeros_like(o_ref)


def _gate_up(x, w_gu, tile_expert, n_valid, tm, tf):
    R, xw = x.shape
    n_tiles = R // tm
    nf = FFN_DIM // tf

    def tile(i, nv):
        return jnp.minimum(i, nv[0] - 1)

    return pl.pallas_call(
        _gate_up_kernel,
        grid_spec=pltpu.PrefetchScalarGridSpec(
            num_scalar_prefetch=2,
            grid=(nf, n_tiles),
            in_specs=[pl.BlockSpec((tm, xw), lambda j, i, te, nv: (tile(i, nv), 0)),
                      pl.BlockSpec((1, D_MODEL, tf), lambda j, i, te, nv: (te[tile(i, nv)], 0, j)),
                      pl.BlockSpec((1, D_MODEL, tf),
                                   lambda j, i, te, nv: (te[tile(i, nv)], 0, j + nf))],
            out_specs=pl.BlockSpec((tm, tf), lambda j, i, te, nv: (i, j)),
            scratch_shapes=[pltpu.VMEM((D_MODEL, tf), BF16), pltpu.VMEM((D_MODEL, tf), BF16)],
        ),
        out_shape=jax.ShapeDtypeStruct((R, FFN_DIM), BF16),
        compiler_params=_params(("arbitrary", "arbitrary")),
        name="ffn_gate_up",
    )(tile_expert, n_valid, x, w_gu, w_gu)


def _down_kernel(te_ref, nv_ref, a_ref, w_ref, o_ref):
    valid = pl.program_id(1) < nv_ref[0]

    @pl.when(valid)
    def _():
        a = a_ref[...]
        for c in range(0, o_ref.shape[1], GU_SUB):
            o_ref[:, c:c + GU_SUB] = _dot(a, w_ref[0, :, c:c + GU_SUB])

    @pl.when(jnp.logical_not(valid))
    def _():
        o_ref[...] = jnp.zeros_like(o_ref)


def _down(act, w_down, tile_expert, n_valid, tm):
    R = act.shape[0]
    n_tiles = R // tm
    tn = 1024

    def tile(i, nv):
        return jnp.minimum(i, nv[0] - 1)

    return pl.pallas_call(
        _down_kernel,
        grid_spec=pltpu.PrefetchScalarGridSpec(
            num_scalar_prefetch=2,
            grid=(D_MODEL // tn, n_tiles),
            in_specs=[pl.BlockSpec((tm, FFN_DIM), lambda j, i, te, nv: (tile(i, nv), 0)),
                      pl.BlockSpec((1, FFN_DIM, tn), lambda j, i, te, nv: (te[tile(i, nv)], 0, j))],
            out_specs=pl.BlockSpec((tm, tn), lambda j, i, te, nv: (i, j)),
        ),
        out_shape=jax.ShapeDtypeStruct((R, D_MODEL), F32),
        compiler_params=_params(("arbitrary", "arbitrary")),
        name="ffn_down",
    )(tile_expert, n_valid, act, w_down)


def _ffn_epilogue(f, h1, p, wg, wp, g3, gp):
    h2 = h1 + _rms(f, g3)
    gate = _sigmoid(_dot(h2.astype(BF16), wg))
    proj = _dot(p.astype(BF16), wp)
    return h2 + _rms(gate * proj, gp)


def _post_ffn_kernel(f_ref, h1_ref, p_ref, wg_ref, wp_ref, g3_ref, gp_ref, o_ref):
    o_ref[...] = _ffn_epilogue(f_ref[...], h1_ref[...], p_ref[...], wg_ref[...], wp_ref[...],
                               g3_ref[...], gp_ref[...])


def _post_ffn(f, h1, p, layer, wg, wp, g3, gp):
    T = h1.shape[0]
    tm = min(T, 512)
    row = lambda i: (i, 0)
    const = lambda i: (0, 0)
    return pl.pallas_call(
        _post_ffn_kernel,
        grid=(T // tm,),
        in_specs=[pl.BlockSpec((tm, D_MODEL), row),
                  pl.BlockSpec((tm, D_MODEL), row),
                  pl.BlockSpec((None, tm, PLE_DIM), lambda i: (layer, i, 0)),
                  pl.BlockSpec((D_MODEL, D_MODEL), const, pipeline_mode=pl.Buffered(1)),
                  pl.BlockSpec((PLE_DIM, D_MODEL), const, pipeline_mode=pl.Buffered(1)),
                  pl.BlockSpec((1, D_MODEL), const),
                  pl.BlockSpec((1, D_MODEL), const)],
        out_specs=pl.BlockSpec((tm, D_MODEL), row),
        out_shape=jax.ShapeDtypeStruct((T, D_MODEL), F32),
        compiler_params=_params(("parallel",)),
        name="post_ffn",
    )(f, h1, p, wg, wp, g3[None, :], gp[None, :])


def _router_kernel(c_ref, wr_ref, tri_ref, route_ref, cnt_ref, carry):
    @pl.when(pl.program_id(0) == 0)
    def _():
        carry[...] = jnp.zeros_like(carry)

    logits = _dot(c_ref[...].astype(BF16), wr_ref[...])
    lane = lax.broadcasted_iota(jnp.int32, logits.shape, 1)
    lg = jnp.where(lane < N_EXPERTS, logits, -jnp.inf)
    m1 = jnp.max(lg, axis=-1, keepdims=True)
    i1 = jnp.min(jnp.where(lg == m1, lane, LANES), axis=-1, keepdims=True)
    lg2 = jnp.where(lane == i1, -jnp.inf, lg)
    m2 = jnp.max(lg2, axis=-1, keepdims=True)
    i2 = jnp.min(jnp.where(lg2 == m2, lane, LANES), axis=-1, keepdims=True)
    e = jnp.exp(m2 - m1)
    g1 = 1.0 / (1.0 + e)
    g2 = e / (1.0 + e)
    onehot = jnp.where((lane == i1) | (lane == i2), 1.0, 0.0)
    before = _dot(tri_ref[...], onehot.astype(BF16)) + carry[0:1, :]
    rank1 = jnp.sum(jnp.where(lane == i1, before, 0.0), axis=-1, keepdims=True)
    rank2 = jnp.sum(jnp.where(lane == i2, before, 0.0), axis=-1, keepdims=True)
    total = carry[0:1, :] + jnp.sum(onehot, axis=0, keepdims=True)
    carry[...] = jnp.broadcast_to(total, carry.shape)
    cnt_ref[...] = jnp.broadcast_to(total, cnt_ref.shape)
    out = jnp.where(lane == 0, i1.astype(F32), 0.0)
    out = jnp.where(lane == 1, i2.astype(F32), out)
    out = jnp.where(lane == 2, g1, out)
    out = jnp.where(lane == 3, g2, out)
    out = jnp.where(lane == 4, rank1, out)
    out = jnp.where(lane == 5, rank2, out)
    route_ref[...] = out


def _router(c, w_router):
    T = c.shape[0]
    tr = min(T, 512)
    wr = jnp.zeros((D_MODEL, LANES), BF16).at[:, :N_EXPERTS].set(w_router.astype(BF16))
    tri = jnp.asarray(np.tril(np.ones((tr, tr), np.float32), -1), dtype=BF16)
    return pl.pallas_call(
        _router_kernel,
        grid=(T // tr,),
        in_specs=[pl.BlockSpec((tr, c.shape[1]), lambda i: (i, 0)),
                  pl.BlockSpec((D_MODEL, LANES), lambda i: (0, 0)),
                  pl.BlockSpec((tr, tr), lambda i: (0, 0))],
        out_specs=[pl.BlockSpec((tr, LANES), lambda i: (i, 0)),
                   pl.BlockSpec((8, LANES), lambda i: (0, 0))],
        out_shape=[jax.ShapeDtypeStruct((T, LANES), F32),
                   jax.ShapeDtypeStruct((8, LANES), F32)],
        scratch_shapes=[pltpu.VMEM((8, LANES), F32)],
        compiler_params=_params(("arbitrary",)),
        name="moe_router",
    )(c, wr, tri)


def _dispatch_kernel(p1_ref, p2_ref, pad_ref, nv_ref, c_ref, xs_ref, zbuf, sem, zsem, *,
                     tm, seg_tile):
    i = pl.program_id(0)
    base = i * tm
    zr = zbuf.shape[0]

    @pl.when(i == 0)
    def _():
        zbuf[...] = jnp.zeros_like(zbuf)

        def zero_tile(start):
            first = pl.multiple_of(start, zr)
            copies = [pltpu.make_async_copy(zbuf, xs_ref.at[pl.ds(first + k * zr, zr)], zsem)
                      for k in range(seg_tile // zr)]
            for cp in copies:
                cp.start()
            for cp in copies:
                cp.wait()

        def zero_padding(e, _):
            @pl.when(pad_ref[e] >= 0)
            def _():
                zero_tile(pad_ref[e])
            return 0

        def zero_unused(t, _):
            zero_tile(t * seg_tile)
            return 0

        lax.fori_loop(0, N_EXPERTS, zero_padding, 0)
        lax.fori_loop(nv_ref[0], xs_ref.shape[0] // seg_tile, zero_unused, 0)

    def row_copy(r, dst):
        return pltpu.make_async_copy(c_ref.at[pl.ds(r, 1)], xs_ref.at[pl.ds(dst, 1)], sem)

    for r in range(tm):
        row_copy(r, p1_ref[base + r]).start(priority=0)
        row_copy(r, p2_ref[base + r]).start(priority=1)
    for _r in range(2 * tm):
        row_copy(0, 0).wait()


def _dispatch(c, pos1, pos2, pad_tile_row, n_valid, n_rows, seg_tile):
    T, width = c.shape
    tm = min(T, 256)
    return pl.pallas_call(
        functools.partial(_dispatch_kernel, tm=tm, seg_tile=seg_tile),
        grid_spec=pltpu.PrefetchScalarGridSpec(
            num_scalar_prefetch=4,
            grid=(T // tm,),
            in_specs=[pl.BlockSpec((tm, width), lambda i, p1, p2, pt, nv: (i, 0))],
            out_specs=pl.BlockSpec(memory_space=pl.ANY),
            scratch_shapes=[pltpu.VMEM((min(seg_tile, 64), width), c.dtype),
                            pltpu.SemaphoreType.DMA(()),
                            pltpu.SemaphoreType.DMA(())],
        ),
        out_shape=jax.ShapeDtypeStruct((n_rows, width), c.dtype),
        compiler_params=_params(("arbitrary",)),
        name="moe_dispatch",
    )(pos1, pos2, pad_tile_row, n_valid, c)


def _combine_kernel(p1_ref, p2_ref, ys_ref, route_ref, h1_ref, p_ref, wg_ref, wp_ref,
                    g3_ref, gp_ref, o_ref, buf, sems, *, tm):
    i = pl.program_id(0)
    slot = i % 2

    def row_copy(src, s, k, r):
        return pltpu.make_async_copy(ys_ref.at[pl.ds(src, 1)], buf.at[s, k, pl.ds(r, 1)],
                                     sems.at[s])

    def gather_tile(tile, s):
        base = tile * tm
        for r in range(tm):
            row_copy(p1_ref[base + r], s, 0, r).start(priority=0)
            row_copy(p2_ref[base + r], s, 1, r).start(priority=1)

    @pl.when(i == 0)
    def _():
        gather_tile(0, 0)

    @pl.when(i + 1 < pl.num_programs(0))
    def _():
        gather_tile(i + 1, 1 - slot)

    for _r in range(2 * tm):
        row_copy(0, slot, 0, 0).wait()
    route = route_ref[...]
    f = route[:, 2:3] * buf[slot, 0] + route[:, 3:4] * buf[slot, 1]
    o_ref[...] = _ffn_epilogue(f, h1_ref[...], p_ref[...], wg_ref[...], wp_ref[...],
                               g3_ref[...], gp_ref[...])


def _combine(ys, pos1, pos2, route, h1, p, layer, wg, wp, g3, gp):
    T = h1.shape[0]
    tm = min(T, 256)
    row = lambda i, p1, p2: (i, 0)
    const = lambda i, p1, p2: (0, 0)
    return pl.pallas_call(
        functools.partial(_combine_kernel, tm=tm),
        grid_spec=pltpu.PrefetchScalarGridSpec(
            num_scalar_prefetch=2,
            grid=(T // tm,),
            in_specs=[pl.BlockSpec(memory_space=pl.ANY),
                      pl.BlockSpec((tm, LANES), row),
                      pl.BlockSpec((tm, D_MODEL), row),
                      pl.BlockSpec((None, tm, PLE_DIM), lambda i, p1, p2: (layer, i, 0)),
                      pl.BlockSpec((D_MODEL, D_MODEL), const, pipeline_mode=pl.Buffered(1)),
                      pl.BlockSpec((PLE_DIM, D_MODEL), const, pipeline_mode=pl.Buffered(1)),
                      pl.BlockSpec((1, D_MODEL), const),
                      pl.BlockSpec((1, D_MODEL), const)],
            out_specs=pl.BlockSpec((tm, D_MODEL), row),
            scratch_shapes=[pltpu.VMEM((2, 2, tm, D_MODEL), F32),
                            pltpu.SemaphoreType.DMA((2,))],
        ),
        out_shape=jax.ShapeDtypeStruct((T, D_MODEL), F32),
        compiler_params=_params(("arbitrary",)),
        name="moe_combine",
    )(pos1, pos2, ys, route, h1, p, wg, wp, g3[None, :], gp[None, :])


def _swap_halves(w):
    half = w.shape[-1] // 2
    return jnp.concatenate([w[..., half:], w[..., :half]], axis=-1)


def _prep_w_in(w):
    w = w.astype(BF16)
    k_rope = w[:, 640:704]
    cols = [w[:, :704], _swap_halves(k_rope),
            jnp.zeros((D_MODEL, HGRN_COL0 - 768), w.dtype), w[:, 704:]]
    return jnp.concatenate(cols, axis=1).astype(BF16)


def _prep_w_uq(w):
    w = w.reshape(Q_RANK, N_HEADS, QK_DIM)
    pe = w[:, :, NOPE_DIM:]
    per_head = jnp.concatenate([w, _swap_halves(pe)], axis=-1)
    return per_head.reshape(Q_RANK, N_HEADS * QK_PAD).astype(BF16)


def _prep_w_ukv(w):
    w = w.reshape(KV_RANK, N_HEADS, NOPE_DIM + V_DIM)
    k = w[:, :, :NOPE_DIM].reshape(KV_RANK, GROUP_WIDTH)
    v = w[:, :, NOPE_DIM:].reshape(KV_RANK, GROUP_WIDTH)
    return jnp.concatenate([k, v], axis=1).astype(BF16)


def _moe_plan(route, counts, tm, n_tiles):
    e1 = route[:, 0].astype(jnp.int32)
    e2 = route[:, 1].astype(jnp.int32)
    rank1 = route[:, 4].astype(jnp.int32)
    rank2 = route[:, 5].astype(jnp.int32)
    cnt = counts[0, :N_EXPERTS].astype(jnp.int32)
    padded = ((cnt + tm - 1) // tm) * tm
    seg_end = jnp.cumsum(padded)
    seg_start = seg_end - padded
    pos1 = seg_start[e1] + rank1
    pos2 = seg_start[e2] + rank2
    tile_start = jnp.arange(n_tiles, dtype=jnp.int32) * tm
    tile_expert = jnp.sum(tile_start[:, None] >= seg_end[None, :], axis=1).astype(jnp.int32)
    tile_expert = jnp.minimum(tile_expert, N_EXPERTS - 1)
    n_valid = (seg_end[-1] // tm).astype(jnp.int32)[None]
    pad_tile_row = jnp.where(padded > 0, seg_end - tm, -1).astype(jnp.int32)
    return pos1, pos2, tile_expert, n_valid, pad_tile_row


def kernel(x, p, positions, sandwich_norms, w_in, mla_q_norm, mla_kv_norm, w_uq, w_ukv,
           hgrn_lb_logits, hgrn_out_norm, w_out, ffn_w_gu, ffn_w_down, moe_w_router,
           moe_w_gu, moe_w_down, ple_w_proj, ple_w_gate, ple_norm):
    B, S, _ = x.shape
    T = B * S
    h = x.reshape(T, D_MODEL)
    ct, st = _rope_tables(positions)
    hgrn_consts = _hgrn_consts(HGRN_CHUNK)
    tm = min(T, 512)
    for l in range(DEPTH):
        z = _norm_matmul(h, sandwich_norms[l, 0], _prep_w_in(w_in[l]))
        q, k, v = _mla_proj(z, ct, st, mla_q_norm[l], mla_kv_norm[l],
                            _prep_w_uq(w_uq[l]), _prep_w_ukv(w_ukv[l]), B, S)
        moe = l % 2 == 1
        if moe:
            w_down = moe_w_down[l // 2]
            o_mla, w_down_bf = _attention_and_cast(q, k, v, w_down.reshape(-1, D_MODEL))
            w_down_bf = w_down_bf.reshape(w_down.shape)
        else:
            o_mla = _attention(q, k, v)
        o_mla = o_mla.reshape(T, GROUP_WIDTH)
        later = [(w_out, l), (ple_w_gate, l)] + ([] if moe else [(ffn_w_down, l // 2)])
        o_hgrn, w_out_bf, wg, *rest = _hgrn(z, hgrn_lb_logits, hgrn_out_norm[l], hgrn_consts,
                                            l, B, S, casts=later)
        h1, c = _out_proj(o_mla, o_hgrn, h, w_out_bf,
                          sandwich_norms[l, 1], sandwich_norms[l, 2],
                          F32 if moe else BF16)
        wp = ple_w_proj[l].astype(BF16)
        p_all = p.reshape(DEPTH, T, PLE_DIM)
        if not moe:
            def one_expert(rows):
                return jnp.zeros((T // rows,), jnp.int32), jnp.full((1,), T // rows, jnp.int32)

            tm_dense = min(T, 1024)
            act = _gate_up(c, ffn_w_gu[l // 2][None], *one_expert(tm_dense), tm_dense, 512)
            f = _down(act, rest[0][None], *one_expert(tm), tm)
            h = _post_ffn(f, h1, p_all, l, wg, wp, sandwich_norms[l, 3], ple_norm[l])
        else:
            n_tiles = (2 * T) // tm + N_EXPERTS
            route, counts = _router(c, moe_w_router[l // 2])
            pos1, pos2, tile_expert, n_valid, pad_tile_row = _moe_plan(route, counts, tm, n_tiles)
            xs = _dispatch(c, pos1, pos2, pad_tile_row, n_valid, n_tiles * tm, tm)
            act = _gate_up(xs, moe_w_gu[l // 2], tile_expert, n_valid, tm, 1024)
            ys = _down(act, w_down_bf, tile_expert, n_valid, tm)
            h = _combine(ys, pos1, pos2, route, h1, p_all, l, wg, wp,
                         sandwich_norms[l, 3], ple_norm[l])
    return h.reshape(B, S, D_MODEL)
```
